```python
import math
import jax, jax.numpy as jnp
from jax import lax
import numpy as np

D_MODEL = 1024
BATCH = 4
SEQ = 4096
DEPTH = 4
DEC_BATCH = 8
DEC_SEQ = 2048
PAST_LEN = 128

N_MEM = 256
EPS = 1e-6
S5_WIDTH = D_MODEL // 2
S5_GROUP = 16
S5_GROUPS = S5_WIDTH // S5_GROUP
S5_STATE = 64
GLA_HEADS = 4
GLA_DV = (D_MODEL // 2) // GLA_HEADS
GLA_DK = GLA_DV // 2
GLA_RANK = 16
GLA_TAU = 16.0
GLA_CHUNK = 64
GLA_QK = GLA_HEADS * GLA_DK
GLA_V = GLA_HEADS * GLA_DV
EV_COLS = S5_WIDTH + 2 * GLA_QK + 2 * GLA_V + 2 * GLA_RANK
EV_SPLITS = (S5_WIDTH, S5_WIDTH + GLA_QK, S5_WIDTH + 2 * GLA_QK,
             S5_WIDTH + 2 * GLA_QK + GLA_V, S5_WIDTH + 2 * GLA_QK + 2 * GLA_V)
DIFF_HEADS = 8
DIFF_DK = 64
DIFF_DV = 2 * DIFF_DK
Q_BLOCK = 128
X_HEADS = 4
X_DH = D_MODEL // X_HEADS
D_FF = 4 * D_MODEL
N_EVEN = (DEPTH + 1) // 2
N_ODD = DEPTH // 2

kernel_name = "hybrid_s5_gla_diffattn_encoder"


def rms_norm(x, g):
    xf = x.astype(jnp.float32)
    y = xf * lax.rsqrt(jnp.mean(xf * xf, axis=-1, keepdims=True) + EPS)
    return (y * g.astype(jnp.float32)).astype(x.dtype)


def alibi_slopes(n_heads):
    return jnp.asarray(2.0 ** (-8.0 * np.arange(1, n_heads + 1, dtype=np.float32) / n_heads), jnp.float32)


def s5_direction(u, lam_re, lam_im, log_step, b_re, b_im, c_re, c_im, reverse):
    f32 = jnp.float32
    lam = lax.complex(lam_re.astype(f32), lam_im.astype(f32))
    step = jnp.exp(log_step.astype(f32))[:, None]
    lam_bar = jnp.exp(lam * step)
    b = lax.complex(b_re.astype(f32), b_im.astype(f32))
    b_bar = ((lam_bar - 1.0) / lam)[..., None] * b
    c = lax.complex(c_re.astype(f32), c_im.astype(f32))
    bu = jnp.einsum('gpc,blgc->blgp', b_bar, u.astype(jnp.complex64))
    a = jnp.broadcast_to(lam_bar, bu.shape)

    def combine(left, right):
        a_l, b_l = left
        a_r, b_r = right
        return a_r * a_l, a_r * b_l + b_r

    _, states = lax.associative_scan(combine, (a, bu), axis=1, reverse=reverse)
    return jnp.einsum('gcp,blgp->blgc', c, states).real


def s5_mixer(u, lam_re, lam_im, log_step, b_re, b_im, c_re, c_im, d, w_glu):
    bsz, L, _ = u.shape
    ug = u.astype(jnp.float32).reshape(bsz, L, S5_GROUPS, S5_GROUP)
    y = (s5_direction(ug, lam_re[0], lam_im[0], log_step[0], b_re[0], b_im[0], c_re[0], c_im[0], False)
         + s5_direction(ug, lam_re[1], lam_im[1], log_step[1], b_re[1], b_im[1], c_re[1], c_im[1], True)
         + d.astype(jnp.float32) * ug)
    y = jax.nn.gelu(y.reshape(bsz, L, S5_WIDTH))
    y = y * jax.nn.sigmoid(y @ w_glu.astype(jnp.float32))
    return y.astype(u.dtype)


def gla_direction(q, k, v, g):
    bsz, L, H, dk = q.shape
    dv = v.shape[-1]
    C = GLA_CHUNK
    n = L // C
    q = q.reshape(bsz, n, C, H, dk)
    k = k.reshape(bsz, n, C, H, dk)
    v = v.reshape(bsz, n, C, H, dv)
    bcum = jnp.cumsum(g.reshape(bsz, n, C, H, dk), axis=2)
    blast = bcum[:, :, -1:]
    bref = bcum[:, :, C // 2 - 1:C // 2]
    q_rel = q * jnp.exp(bcum - bref)
    k_rel = k * jnp.exp(bref - bcum)
    scores = jnp.einsum('bnihd,bnjhd->bnhij', q_rel, k_rel)
    causal_in_chunk = jnp.tril(jnp.ones((C, C), dtype=bool))
    scores = jnp.where(causal_in_chunk, scores, 0.0)
    o_intra = jnp.einsum('bnhij,bnjhe->bnihe', scores, v)
    k_out = k * jnp.exp(blast - bcum)
    chunk_kv = jnp.einsum('bnjhd,bnjhe->bnhde', k_out, v)
    decay = jnp.exp(blast[:, :, 0])

    def step(S, inp):
        dec, kv = inp
        return dec[..., None] * S + kv, S

    S0 = jnp.zeros((bsz, H, dk, dv), jnp.float32)
    _, S_prev = lax.scan(step, S0, (jnp.moveaxis(decay, 1, 0), jnp.moveaxis(chunk_kv, 1, 0)))
    S_prev = jnp.moveaxis(S_prev, 0, 1)
    o_inter = jnp.einsum('bnihd,bnhde->bnihe', q * jnp.exp(bcum), S_prev)
    return (o_intra + o_inter).reshape(bsz, L, H, dv)


def gla_mixer(q, k, v, og, glr, w_gate, b_gate, norm_g):
    f32 = jnp.float32
    bsz, L, _ = q.shape
    qh = q.astype(f32).reshape(bsz, L, GLA_HEADS, GLA_DK) * (GLA_DK ** -0.5)
    kh = k.astype(f32).reshape(bsz, L, GLA_HEADS, GLA_DK)
    vh = v.astype(f32).reshape(bsz, L, GLA_HEADS, GLA_DV)
    lr = glr.astype(f32).reshape(bsz, L, 2, GLA_RANK)
    logit = jnp.einsum('blzr,zrk->blzk', lr, w_gate.astype(f32)) + b_gate.astype(f32)
    g = (jax.nn.log_sigmoid(logit) / GLA_TAU).reshape(bsz, L, 2, GLA_HEADS, GLA_DK)
    flip = lambda t: jnp.flip(t, axis=1)
    o = (gla_direction(qh, kh, vh, g[:, :, 0])
         + flip(gla_direction(flip(qh), flip(kh), flip(vh), flip(g[:, :, 1]))))
    o = rms_norm(o, norm_g).reshape(bsz, L, GLA_V)
    return (o * jax.nn.silu(og.astype(f32))).astype(q.dtype)


def diff_attention(h, w_in, q_norm, k_norm, lq1, lk1, lq2, lk2, sub_norm, lambda_init):
    f32 = jnp.float32
    bsz, L, _ = h.shape
    proj = (h @ w_in).astype(f32)
    q, k, v = jnp.split(proj, 3, axis=-1)
    q = rms_norm(q.reshape(bsz, L, DIFF_HEADS, 2, DIFF_DK), q_norm) * (DIFF_DK ** -0.5)
    k = rms_norm(k.reshape(bsz, L, DIFF_HEADS, 2, DIFF_DK), k_norm)
    v = v.reshape(bsz, L, DIFF_HEADS, DIFF_DV)
    lam = (jnp.exp(jnp.sum(lq1.astype(f32) * lk1.astype(f32)))
           - jnp.exp(jnp.sum(lq2.astype(f32) * lk2.astype(f32))) + lambda_init)
    slopes = alibi_slopes(DIFF_HEADS)[:, None, None]
    nblk = L // Q_BLOCK
    qb = jnp.moveaxis(q.reshape(bsz, nblk, Q_BLOCK, DIFF_HEADS, 2, DIFF_DK), 1, 0)
    kpos = jnp.arange(L, dtype=f32)

    def block(args):
        qblk, i = args
        qpos = (i * Q_BLOCK + jnp.arange(Q_BLOCK)).astype(f32)
        bias = -slopes * jnp.abs(qpos[:, None] - kpos[None, :])
        s = jnp.einsum('bqhzd,bkhzd->bzhqk', qblk, k) + bias[None, None]
        p = jax.nn.softmax(s, axis=-1)
        attn = p[:, 0] - lam * p[:, 1]
        return jnp.einsum('bhqk,bkhe->bqhe', attn, v)

    o = lax.map(block, (qb, jnp.arange(nblk)))
    o = jnp.moveaxis(o, 0, 1).reshape(bsz, L, DIFF_HEADS, DIFF_DV)
    o = rms_norm(o, sub_norm) * (1.0 - lambda_init)
    return o.reshape(bsz, L, DIFF_HEADS * DIFF_DV).astype(h.dtype)


def cross_attention(h, m, w_q, w_kv, w_o, q_norm, k_norm):
    f32 = jnp.float32
    bsz, L, _ = h.shape
    q = rms_norm((h @ w_q).astype(f32).reshape(bsz, L, X_HEADS, X_DH), q_norm) * (X_DH ** -0.5)
    kv = (m @ w_kv).astype(f32).reshape(bsz, m.shape[1], 2, X_HEADS, X_DH)
    k = rms_norm(kv[:, :, 0], k_norm)
    v = kv[:, :, 1]
    p = jax.nn.softmax(jnp.einsum('bqhd,bmhd->bhqm', q, k), axis=-1)
    o = jnp.einsum('bhqm,bmhd->bqhd', p, v).reshape(bsz, L, D_MODEL)
    return o.astype(h.dtype) @ w_o


def trunk(x, mem, p):
    for layer in range(DEPTH):
        h = rms_norm(x, p['norm_mix'][layer])
        if layer % 2 == 0:
            e = layer // 2
            proj = h @ p['ev_w_in'][e]
            u, q, k, v, og, glr = jnp.split(proj, EV_SPLITS, axis=-1)
            y_s5 = s5_mixer(u, p['s5_lambda_re'][e], p['s5_lambda_im'][e], p['s5_log_step'][e],
                            p['s5_b_re'][e], p['s5_b_im'][e], p['s5_c_re'][e], p['s5_c_im'][e],
                            p['s5_d'][e], p['s5_w_glu'][e])
            y_gla = gla_mixer(q, k, v, og, glr, p['gla_w_gate'][e], p['gla_b_gate'][e], p['gla_norm'][e])
            y = jnp.concatenate([y_s5, y_gla], axis=-1).astype(x.dtype) @ p['ev_w_out'][e]
        else:
            o = layer // 2
            lambda_init = 0.8 - 0.6 * math.exp(-0.3 * layer)
            y = diff_attention(h, p['od_w_in'][o], p['diff_q_norm'][o], p['diff_k_norm'][o],
                               p['diff_lambda_q1'][o], p['diff_lambda_k1'][o],
                               p['diff_lambda_q2'][o], p['diff_lambda_k2'][o],
                               p['diff_norm'][o], lambda_init) @ p['od_w_out'][o]
        x = x + y.astype(x.dtype)
        h = rms_norm(x, p['norm_cross'][layer])
        m = rms_norm(mem, p['norm_mem'][layer])
        x = x + cross_attention(h, m, p['x_w_q'][layer], p['x_w_kv'][layer], p['x_w_o'][layer],
                                p['x_q_norm'][layer], p['x_k_norm'][layer]).astype(x.dtype)
        h = rms_norm(x, p['norm_mlp'][layer])
        x = x + (jnp.square(jax.nn.relu(h @ p['mlp_w1'][layer])) @ p['mlp_w2'][layer]).astype(x.dtype)
    return x


def setup_inputs(seed: int = 0) -> dict:
    key = jax.random.key(seed)
    ks = iter(jax.random.split(key, 64))
    f32 = jnp.float32
    nrm = lambda shape, scale: scale * jax.random.normal(next(ks), shape, f32)
    gain = lambda shape: 1.0 + 0.02 * jax.random.normal(next(ks), shape, f32)
    E, O, G, P = N_EVEN, N_ODD, S5_GROUPS, S5_STATE
    n_idx = jnp.arange(P, dtype=f32)
    return {
        'x_prompt': nrm((BATCH, SEQ, D_MODEL), 1.0),
        'x_sample': nrm((DEC_BATCH, DEC_SEQ, D_MODEL), 1.0),
        'mem_prompt': nrm((BATCH, N_MEM, D_MODEL), 1.0),
        'mem_sample': nrm((DEC_BATCH, N_MEM, D_MODEL), 1.0),
        'norm_mix': gain((DEPTH, D_MODEL)),
        'norm_cross': gain((DEPTH, D_MODEL)),
        'norm_mem': gain((DEPTH, D_MODEL)),
        'norm_mlp': gain((DEPTH, D_MODEL)),
        'ev_w_in': nrm((E, D_MODEL, EV_COLS), D_MODEL ** -0.5),
        'ev_w_out': nrm((E, D_MODEL, D_MODEL), D_MODEL ** -0.5),
        's5_lambda_re': -0.5 + nrm((E, 2, G, P), 0.01),
        's5_lambda_im': math.pi * n_idx + nrm((E, 2, G, P), 0.01),
        's5_log_step': jax.random.uniform(next(ks), (E, 2, G), f32, math.log(1e-3), math.log(1e-1)),
        's5_b_re': nrm((E, 2, G, P, S5_GROUP), (2.0 * S5_GROUP) ** -0.5),
        's5_b_im': nrm((E, 2, G, P, S5_GROUP), (2.0 * S5_GROUP) ** -0.5),
        's5_c_re': nrm((E, 2, G, S5_GROUP, P), (2.0 * P) ** -0.5),
        's5_c_im': nrm((E, 2, G, S5_GROUP, P), (2.0 * P) ** -0.5),
        's5_d': nrm((E, G, S5_GROUP), 1.0),
        's5_w_glu': nrm((E, S5_WIDTH, S5_WIDTH), S5_WIDTH ** -0.5),
        'gla_w_gate': nrm((E, 2, GLA_RANK, GLA_QK), GLA_RANK ** -0.5),
        'gla_b_gate': nrm((E, 2, GLA_QK), 0.1),
        'gla_norm': gain((E, GLA_DV)),
        'od_w_in': nrm((O, D_MODEL, 3 * D_MODEL), D_MODEL ** -0.5),
        'od_w_out': nrm((O, D_MODEL, D_MODEL), D_MODEL ** -0.5),
        'diff_q_norm': gain((O, DIFF_DK)),
        'diff_k_norm': gain((O, DIFF_DK)),
        'diff_lambda_q1': nrm((O, DIFF_DK), 0.1),
        'diff_lambda_k1': nrm((O, DIFF_DK), 0.1),
        'diff_lambda_q2': nrm((O, DIFF_DK), 0.1),
        'diff_lambda_k2': nrm((O, DIFF_DK), 0.1),
        'diff_norm': gain((O, DIFF_DV)),
        'x_w_q': nrm((DEPTH, D_MODEL, D_MODEL), D_MODEL ** -0.5),
        'x_w_kv': nrm((DEPTH, D_MODEL, 2 * D_MODEL), D_MODEL ** -0.5),
        'x_w_o': nrm((DEPTH, D_MODEL, D_MODEL), D_MODEL ** -0.5),
        'x_q_norm': gain((DEPTH, X_DH)),
        'x_k_norm': gain((DEPTH, X_DH)),
        'mlp_w1': nrm((DEPTH, D_MODEL, D_FF), D_MODEL ** -0.5),
        'mlp_w2': nrm((DEPTH, D_FF, D_MODEL), D_FF ** -0.5),
    }


def reference(x_prompt, x_sample, mem_prompt, mem_sample, norm_mix, norm_cross, norm_mem, norm_mlp,
              ev_w_in, ev_w_out, s5_lambda_re, s5_lambda_im, s5_log_step, s5_b_re, s5_b_im,
              s5_c_re, s5_c_im, s5_d, s5_w_glu, gla_w_gate, gla_b_gate, gla_norm,
              od_w_in, od_w_out, diff_q_norm, diff_k_norm, diff_lambda_q1, diff_lambda_k1,
              diff_lambda_q2, diff_lambda_k2, diff_norm, x_w_q, x_w_kv, x_w_o, x_q_norm, x_k_norm,
              mlp_w1, mlp_w2):
    params = dict(norm_mix=norm_mix, norm_cross=norm_cross, norm_mem=norm_mem, norm_mlp=norm_mlp,
                  ev_w_in=ev_w_in, ev_w_out=ev_w_out, s5_lambda_re=s5_lambda_re,
                  s5_lambda_im=s5_lambda_im, s5_log_step=s5_log_step, s5_b_re=s5_b_re, s5_b_im=s5_b_im,
                  s5_c_re=s5_c_re, s5_c_im=s5_c_im, s5_d=s5_d, s5_w_glu=s5_w_glu,
                  gla_w_gate=gla_w_gate, gla_b_gate=gla_b_gate, gla_norm=gla_norm,
                  od_w_in=od_w_in, od_w_out=od_w_out, diff_q_norm=diff_q_norm, diff_k_norm=diff_k_norm,
                  diff_lambda_q1=diff_lambda_q1, diff_lambda_k1=diff_lambda_k1,
                  diff_lambda_q2=diff_lambda_q2, diff_lambda_k2=diff_lambda_k2, diff_norm=diff_norm,
                  x_w_q=x_w_q, x_w_kv=x_w_kv, x_w_o=x_w_o, x_q_norm=x_q_norm, x_k_norm=x_k_norm,
                  mlp_w1=mlp_w1, mlp_w2=mlp_w2)
    y_prompt = trunk(x_prompt, mem_prompt, params)
    y_sample = trunk(x_sample, mem_sample, params)
    return (y_prompt, y_sample)
```

```python
import functools
import math

import numpy as np
import jax
import jax.numpy as jnp
from jax import lax
from jax.experimental import pallas as pl
from jax.experimental.pallas import tpu as pltpu

F32 = jnp.float32
BF16 = jnp.bfloat16
HIGHEST = lax.Precision.HIGHEST

D_MODEL = 1024
DEPTH = 4
EPS = 1e-6
S5_WIDTH = 512
S5_GROUP = 16
S5_GROUPS = 32
S5_STATE = 64
S5_CHUNK = 64
GLA_HEADS = 4
GLA_DV = 128
GLA_DK = 64
GLA_RANK = 16
GLA_TAU = 16.0
GLA_CHUNK = 64
GLA_QK = GLA_HEADS * GLA_DK
GLA_V = GLA_HEADS * GLA_DV
EV_PAD_COLS = 2304
EV_TN = 768
DIFF_HEADS = 8
DIFF_DK = 64
DIFF_DV = 128
X_HEADS = 4
X_DH = 256
D_FF = 4096

ROW_TILE = 512
VMEM_LIMIT = 48 * 1024 * 1024

NT_DIMS = (((1,), (1,)), ((), ()))
TN_DIMS = (((0,), (0,)), ((), ()))


def _params(*sem):
    return pltpu.CompilerParams(dimension_semantics=sem, vmem_limit_bytes=VMEM_LIMIT)


def _rms(x, gain):
    ms = jnp.mean(x * x, axis=-1, keepdims=True)
    return x * lax.rsqrt(ms + EPS) * gain


def _sigmoid(x):
    return 1.0 / (1.0 + jnp.exp(-x))


def _norm_matmul_body(x_ref, g_ref, w_ref, o_ref, h_ref):
    @pl.when(pl.program_id(1) == 0)
    def _():
        h_ref[...] = _rms(x_ref[...], g_ref[...]).astype(BF16)

    o_ref[...] = jnp.dot(h_ref[...], w_ref[...], preferred_element_type=F32).astype(o_ref.dtype)


def norm_matmul(x, gain, w, out_dtype, tn, tm=ROW_TILE):
    t, d = x.shape
    n = w.shape[1]
    return pl.pallas_call(
        _norm_matmul_body,
        out_shape=jax.ShapeDtypeStruct((t, n), out_dtype),
        grid=(t // tm, n // tn),
        in_specs=[pl.BlockSpec((tm, d), lambda i, j: (i, 0)),
                  pl.BlockSpec((1, d), lambda i, j: (0, 0)),
                  pl.BlockSpec((d, tn), lambda i, j: (0, j))],
        out_specs=pl.BlockSpec((tm, tn), lambda i, j: (i, j)),
        scratch_shapes=[pltpu.VMEM((tm, d), BF16)],
        compiler_params=_params("parallel", "arbitrary"),
        name="norm_matmul",
    )(x, gain.reshape(1, d), w)


def _matmul_res_body(a_ref, w_ref, r_ref, o_ref):
    o_ref[...] = r_ref[...] + jnp.dot(a_ref[...].astype(BF16), w_ref[...], preferred_element_type=F32)


def matmul_residual(a, w, res, tn=512, tm=ROW_TILE):
    t, k = a.shape
    n = w.shape[1]
    return pl.pallas_call(
        _matmul_res_body,
        out_shape=jax.ShapeDtypeStruct((t, n), F32),
        grid=(t // tm, n // tn),
        in_specs=[pl.BlockSpec((tm, k), lambda i, j: (i, 0)),
                  pl.BlockSpec((k, tn), lambda i, j: (0, j)),
                  pl.BlockSpec((tm, tn), lambda i, j: (i, j))],
        out_specs=pl.BlockSpec((tm, tn), lambda i, j: (i, j)),
        compiler_params=_params("parallel", "arbitrary"),
        name="matmul_residual",
    )(a, w, res)


def _mlp_body(x_ref, g_ref, w1_ref, w2_ref, o_ref, h_ref, acc_ref):
    f = pl.program_id(1)

    @pl.when(f == 0)
    def _():
        h_ref[...] = _rms(x_ref[...], g_ref[...]).astype(BF16)
        acc_ref[...] = jnp.zeros_like(acc_ref)

    hid = jnp.dot(h_ref[...], w1_ref[...], preferred_element_type=F32)
    hid = jnp.square(jnp.maximum(hid, 0.0)).astype(BF16)
    acc_ref[...] += jnp.dot(hid, w2_ref[...], preferred_element_type=F32)

    @pl.when(f == pl.num_programs(1) - 1)
    def _():
        o_ref[...] = x_ref[...] + acc_ref[...]


def mlp_block(x, gain, w1, w2, tf=512, tm=ROW_TILE):
    t, d = x.shape
    ff = w1.shape[1]
    return pl.pallas_call(
        _mlp_body,
        out_shape=jax.ShapeDtypeStruct((t, d), F32),
        grid=(t // tm, ff // tf),
        in_specs=[pl.BlockSpec((tm, d), lambda i, f: (i, 0)),
                  pl.BlockSpec((1, d), lambda i, f: (0, 0)),
                  pl.BlockSpec((d, tf), lambda i, f: (0, f)),
                  pl.BlockSpec((tf, d), lambda i, f: (f, 0))],
        out_specs=pl.BlockSpec((tm, d), lambda i, f: (i, 0)),
        scratch_shapes=[pltpu.VMEM((tm, d), BF16), pltpu.VMEM((tm, d), F32)],
        compiler_params=_params("parallel", "arbitrary"),
        name="mlp_block",
    )(x, gain.reshape(1, d), w1, w2)


def _mem_kv_body(m_ref, g_ref, w_ref, kg_ref, k_ref, v_ref):
    h = _rms(m_ref[0], g_ref[0]).astype(BF16)
    kv = jnp.dot(h, w_ref[0], preferred_element_type=F32)
    for hd in range(X_HEADS):
        sl = slice(hd * X_DH, (hd + 1) * X_DH)
        k_ref[0, 0, :, sl] = _rms(kv[:, sl], kg_ref[0]).astype(BF16)
    v_ref[0, 0] = kv[:, D_MODEL:].astype(BF16)


def mem_kv(mem, norm_mem, w_kv, k_norm):
    bm, nm, d = mem.shape
    out = jax.ShapeDtypeStruct((DEPTH, bm, nm, d), BF16)
    return pl.pallas_call(
        _mem_kv_body,
        out_shape=(out, out),
        grid=(DEPTH, bm),
        in_specs=[pl.BlockSpec((1, nm, d), lambda l, b: (b, 0, 0)),
                  pl.BlockSpec((1, 1, d), lambda l, b: (l, 0, 0)),
                  pl.BlockSpec((1, d, 2 * d), lambda l, b: (l, 0, 0)),
                  pl.BlockSpec((1, 1, X_DH), lambda l, b: (l, 0, 0))],
        out_specs=(pl.BlockSpec((1, 1, nm, d), lambda l, b: (l, b, 0, 0)),
                   pl.BlockSpec((1, 1, nm, d), lambda l, b: (l, b, 0, 0))),
        compiler_params=_params("arbitrary", "arbitrary"),
        name="mem_kv",
    )(mem, norm_mem.reshape(DEPTH, 1, d), w_kv, k_norm.reshape(DEPTH, 1, X_DH))


def _cross_body(x_ref, g_ref, wq_ref, qg_ref, k_ref, v_ref, wo_ref, o_ref):
    x = x_ref[...]
    h = _rms(x, g_ref[...]).astype(BF16)
    q = jnp.dot(h, wq_ref[...], preferred_element_type=F32)
    heads = []
    for hd in range(X_HEADS):
        sl = slice(hd * X_DH, (hd + 1) * X_DH)
        qn = _rms(q[:, sl], qg_ref[...]).astype(BF16)
        s = lax.dot_general(qn, k_ref[0, :, sl], NT_DIMS, preferred_element_type=F32)
        p = jnp.exp(s - jnp.max(s, axis=-1, keepdims=True))
        l = jnp.sum(p, axis=-1, keepdims=True)
        oh = jnp.dot(p.astype(BF16), v_ref[0, :, sl], preferred_element_type=F32) / l
        heads.append(oh.astype(BF16))
    o = jnp.concatenate(heads, axis=-1)
    o_ref[...] = x + jnp.dot(o, wo_ref[...], preferred_element_type=F32)


def cross_block(x, seq_len, gain, w_q, q_gain, kn, v, w_o, tm=ROW_TILE):
    t, d = x.shape
    nm = kn.shape[1]
    per_seq = seq_len // tm
    return pl.pallas_call(
        _cross_body,
        out_shape=jax.ShapeDtypeStruct((t, d), F32),
        grid=(t // tm,),
        in_specs=[pl.BlockSpec((tm, d), lambda i: (i, 0)),
                  pl.BlockSpec((1, d), lambda i: (0, 0)),
                  pl.BlockSpec((d, d), lambda i: (0, 0)),
                  pl.BlockSpec((1, X_DH), lambda i: (0, 0)),
                  pl.BlockSpec((1, nm, d), lambda i: (i // per_seq, 0, 0)),
                  pl.BlockSpec((1, nm, d), lambda i: (i // per_seq, 0, 0)),
                  pl.BlockSpec((d, d), lambda i: (0, 0))],
        out_specs=pl.BlockSpec((tm, d), lambda i: (i, 0)),
        compiler_params=_params("parallel"),
        name="cross_block",
    )(x, gain.reshape(1, d), w_q, (q_gain * X_DH ** -0.5).reshape(1, X_DH), kn, v, w_o)


def s5_operators(lam_re, lam_im, log_step, b_re, b_im, c_re, c_im, d):
    lc = S5_CHUNK
    lam = lax.complex(lam_re.astype(F32), lam_im.astype(F32))
    step = jnp.exp(log_step.astype(F32))[..., None]
    lam_bar = jnp.exp(lam * step)
    b_bar = ((lam_bar - 1.0) / lam)[..., None] * lax.complex(b_re.astype(F32), b_im.astype(F32))
    c = lax.complex(c_re.astype(F32), c_im.astype(F32))
    pw = jnp.cumprod(jnp.broadcast_to(lam_bar[..., None], lam_bar.shape + (lc,)), axis=-1)
    pw = jnp.concatenate([jnp.ones_like(pw[..., :1]), pw], axis=-1)
    kern = jnp.einsum('zgcp,zgpt,zgpd->zgtcd', c, pw[..., :lc], b_bar, precision=HIGHEST).real
    kf, kb = kern[0], kern[1]
    k0 = kf[:, :1] + kb[:, :1] + (d.astype(F32)[:, :, None] * jnp.eye(S5_GROUP, dtype=F32))[:, None]
    kern_full = jnp.concatenate([kb[:, :0:-1], k0, kf[:, 1:]], axis=1)
    lag = np.arange(lc)[None, :] - np.arange(lc)[:, None] + lc - 1
    toep = kern_full[:, lag]
    toep = toep.transpose(0, 1, 4, 2, 3).reshape(S5_GROUPS, lc * S5_GROUP, lc * S5_GROUP)

    pf = jnp.einsum('gps,gpd->gsdp', pw[0][..., lc - 1::-1], b_bar[0])
    pb = jnp.einsum('gps,gpd->gsdp', pw[1][..., :lc], b_bar[1])
    p_op = jnp.concatenate([pf.real, pf.imag, pb.real, pb.imag], axis=-1)
    p_op = p_op.reshape(S5_GROUPS, lc * S5_GROUP, 4 * S5_STATE)

    qf = jnp.einsum('gcp,gpt->gptc', c[0], pw[0][..., 1:])
    qb = jnp.einsum('gcp,gpt->gptc', c[1], pw[1][..., :0:-1])
    q_op = jnp.concatenate([qf.real, -qf.imag, qb.real, -qb.imag], axis=1)
    q_op = q_op.reshape(S5_GROUPS, 4 * S5_STATE, lc * S5_GROUP)

    a = pw[..., lc]
    coef = jnp.stack([jnp.concatenate([a[0].real, a[0].real], -1),
                      jnp.concatenate([-a[0].imag, a[0].imag], -1),
                      jnp.concatenate([a[1].real, a[1].real], -1),
                      jnp.concatenate([-a[1].imag, a[1].imag], -1)])
    return toep.astype(BF16), p_op, q_op, coef


def _s5_state_in_body(u_ref, p_ref, v_ref):
    v_ref[0] = jnp.dot(u_ref[0], p_ref[0], preferred_element_type=F32, precision=HIGHEST)


def _s5_scan_body(vf_ref, vb_ref, cf_ref, xf_ref, xb_ref):
    n = vf_ref.shape[0]
    c1f, c2f, c1b, c2b = cf_ref[0], cf_ref[1], cf_ref[2], cf_ref[3]

    def step(i, carry):
        xf, xb = carry
        xf_ref[i] = xf
        xf = xf * c1f + pltpu.roll(xf, S5_STATE, 1) * c2f + vf_ref[i]
        j = n - 1 - i
        xb_ref[j] = xb
        xb = xb * c1b + pltpu.roll(xb, S5_STATE, 1) * c2b + vb_ref[j]
        return xf, xb

    zero = jnp.zeros(vf_ref.shape[1:], F32)
    lax.fori_loop(0, n, step, (zero, zero))


def _s5_out_body(u_ref, t_ref, x_ref, q_ref, y_ref):
    y = jnp.dot(u_ref[0].astype(BF16), t_ref[0], preferred_element_type=F32)
    y_ref[0] = y + jnp.dot(x_ref[0], q_ref[0], preferred_element_type=F32, precision=HIGHEST)


def s5_scan(u, bsz, seq_len, ops):
    toep, p_op, q_op, coef = ops
    lc, g, w = S5_CHUNK, S5_GROUPS, S5_CHUNK * S5_GROUP
    n = seq_len // lc
    c = bsz * n
    ns = 4 * S5_STATE
    ug = u.reshape(bsz, n, lc, g, S5_GROUP).transpose(3, 0, 1, 2, 4).reshape(g, c, w)
    v = pl.pallas_call(
        _s5_state_in_body,
        out_shape=jax.ShapeDtypeStruct((g, c, ns), F32),
        grid=(g,),
        in_specs=[pl.BlockSpec((1, c, w), lambda i: (i, 0, 0)),
                  pl.BlockSpec((1, w, ns), lambda i: (i, 0, 0))],
        out_specs=pl.BlockSpec((1, c, ns), lambda i: (i, 0, 0)),
        compiler_params=_params("parallel"),
        name="s5_state_in",
    )(ug, p_op)
    v = v.reshape(g, bsz, n, ns).transpose(2, 1, 0, 3).reshape(n, bsz * g, ns)
    half = ns // 2
    cf = jnp.tile(coef, (1, bsz, 1))
    xf, xb = pl.pallas_call(
        _s5_scan_body,
        out_shape=(jax.ShapeDtypeStruct((n, bsz * g, half), F32),) * 2,
        compiler_params=pltpu.CompilerParams(vmem_limit_bytes=VMEM_LIMIT),
        name="s5_chunk_scan",
    )(v[..., :half], v[..., half:], cf)
    x = jnp.concatenate([xf, xb], axis=-1).reshape(n, bsz, g, ns).transpose(2, 1, 0, 3).reshape(g, c, ns)
    y = pl.pallas_call(
        _s5_out_body,
        out_shape=jax.ShapeDtypeStruct((g, c, w), F32),
        grid=(g,),
        in_specs=[pl.BlockSpec((1, c, w), lambda i: (i, 0, 0)),
                  pl.BlockSpec((1, w, w), lambda i: (i, 0, 0)),
                  pl.BlockSpec((1, c, ns), lambda i: (i, 0, 0)),
                  pl.BlockSpec((1, ns, w), lambda i: (i, 0, 0))],
        out_specs=pl.BlockSpec((1, c, w), lambda i: (i, 0, 0)),
        compiler_params=_params("parallel"),
        name="s5_out",
    )(ug, toep, x, q_op)
    return y.reshape(g, bsz, n, lc, S5_GROUP).transpose(1, 2, 3, 0, 4).reshape(bsz * seq_len, S5_WIDTH)


def _gla_chunk(q, k, v, glr, wg, bg, s_ref, forward):
    cs = GLA_CHUNK
    logit = jnp.dot(glr, wg, preferred_element_type=F32, precision=HIGHEST) + bg
    g = (jnp.minimum(logit, 0.0) - jnp.log1p(jnp.exp(-jnp.abs(logit)))) / GLA_TAU
    row = lax.broadcasted_iota(jnp.int32, (cs, cs), 0)
    col = lax.broadcasted_iota(jnp.int32, (cs, cs), 1)
    keep = (col <= row) if forward else (col >= row)
    bcum = jnp.dot(keep.astype(F32), g, preferred_element_type=F32, precision=HIGHEST)
    i_ref = cs // 2 - 1 if forward else cs // 2
    i_last = cs - 1 if forward else 0
    bref = bcum[i_ref:i_ref + 1]
    blast = bcum[i_last:i_last + 1]
    q_rel = q * jnp.exp(bcum - bref)
    k_rel = (k * jnp.exp(bref - bcum)).astype(BF16)
    k_out = (k * jnp.exp(blast - bcum)).astype(BF16)
    q_dec = q * jnp.exp(bcum)
    decay = jnp.exp(blast)
    lane = lax.broadcasted_iota(jnp.int32, (1, GLA_QK), 1)
    outs = []
    for h in range(GLA_HEADS):
        hm = ((lane >= h * GLA_DK) & (lane < (h + 1) * GLA_DK)).astype(F32)
        vh = v[:, h * GLA_DV:(h + 1) * GLA_DV].astype(BF16)
        s = lax.dot_general((q_rel * hm).astype(BF16), k_rel, NT_DIMS, preferred_element_type=F32)
        s = jnp.where(keep, s, 0.0)
        o = jnp.dot(s.astype(BF16), vh, preferred_element_type=F32)
        st = s_ref[h]
        o = o + lax.dot_general((q_dec * hm).astype(BF16), st.astype(BF16), NT_DIMS,
                                preferred_element_type=F32)
        kv = lax.dot_general(vh, k_out, TN_DIMS, preferred_element_type=F32)
        s_ref[h] = decay * st + kv * hm
        outs.append(o)
    return jnp.concatenate(outs, axis=-1)


def _gla_body(qf_ref, kf_ref, vf_ref, gf_ref, qb_ref, kb_ref, vb_ref, gb_ref, wg_ref, bg_ref,
              of_ref, ob_ref, sf_ref, sb_ref, *, chunks):
    @pl.when(pl.program_id(1) == 0)
    def _():
        sf_ref[...] = jnp.zeros_like(sf_ref)
        sb_ref[...] = jnp.zeros_like(sb_ref)

    cs = GLA_CHUNK
    scale = GLA_DK ** -0.5
    for ci in range(chunks):
        rows = pl.ds(ci * cs, cs)
        of_ref[rows, :] = _gla_chunk(qf_ref[rows, :] * scale, kf_ref[rows, :], vf_ref[rows, :],
                                     gf_ref[rows, :], wg_ref[0], bg_ref[0], sf_ref, True)
    for ci in reversed(range(chunks)):
        rows = pl.ds(ci * cs, cs)
        ob_ref[rows, :] = _gla_chunk(qb_ref[rows, :] * scale, kb_ref[rows, :], vb_ref[rows, :],
                                     gb_ref[rows, :], wg_ref[1], bg_ref[1], sb_ref, False)


def gla_scan(proj, bsz, seq_len, w_gate, b_gate, chunks=4):
    t = proj.shape[0]
    tb = chunks * GLA_CHUNK
    nb = seq_len // tb
    fwd = lambda cb: (lambda b, i: (b * nb + i, cb))
    bwd = lambda cb: (lambda b, i: (b * nb + nb - 1 - i, cb))
    qc, kc, vc, gc = 512 // GLA_QK, 768 // GLA_QK, 1024 // GLA_V, 2048 // 128
    wg = jnp.zeros((2, 128, GLA_QK), F32)
    wg = wg.at[0, :GLA_RANK].set(w_gate[0].astype(F32)).at[1, GLA_RANK:2 * GLA_RANK].set(w_gate[1].astype(F32))
    out = jax.ShapeDtypeStruct((t, GLA_V), F32)
    state = pltpu.VMEM((GLA_HEADS, GLA_DV, GLA_QK), F32)
    return pl.pallas_call(
        functools.partial(_gla_body, chunks=chunks),
        out_shape=(out, out),
        grid=(bsz, nb),
        in_specs=[pl.BlockSpec((tb, GLA_QK), fwd(qc)), pl.BlockSpec((tb, GLA_QK), fwd(kc)),
                  pl.BlockSpec((tb, GLA_V), fwd(vc)), pl.BlockSpec((tb, 128), fwd(gc)),
                  pl.BlockSpec((tb, GLA_QK), bwd(qc)), pl.BlockSpec((tb, GLA_QK), bwd(kc)),
                  pl.BlockSpec((tb, GLA_V), bwd(vc)), pl.BlockSpec((tb, 128), bwd(gc)),
                  pl.BlockSpec((2, 128, GLA_QK), lambda b, i: (0, 0, 0)),
                  pl.BlockSpec((2, 1, GLA_QK), lambda b, i: (0, 0, 0))],
        out_specs=(pl.BlockSpec((tb, GLA_V), fwd(0)), pl.BlockSpec((tb, GLA_V), bwd(0))),
        scratch_shapes=[state, state],
        compiler_params=_params("parallel", "arbitrary"),
        name="gla_scan",
    )(proj, proj, proj, proj, proj, proj, proj, proj, wg, b_gate.astype(F32).reshape(2, 1, GLA_QK))


def _even_out_body(x_ref, ys_ref, of_ref, ob_ref, og_ref, wglu_ref, gn_ref, wtop_ref, wbot_ref, o_ref):
    y = ys_ref[...]
    y = 0.5 * y * (1.0 + jnp.tanh(math.sqrt(2.0 / math.pi) * (y + 0.044715 * (y * y * y))))
    gate = jnp.dot(y.astype(BF16), wglu_ref[...], preferred_element_type=F32)
    y = y * _sigmoid(gate)
    o = of_ref[...] + ob_ref[...]
    og = og_ref[...]
    heads = []
    for h in range(GLA_HEADS):
        sl = slice(h * GLA_DV, (h + 1) * GLA_DV)
        heads.append(_rms(o[:, sl], gn_ref[...]))
    o = jnp.concatenate(heads, axis=-1) * (og * _sigmoid(og))
    o_ref[...] = (x_ref[...]
                  + jnp.dot(y.astype(BF16), wtop_ref[...], preferred_element_type=F32)
                  + jnp.dot(o.astype(BF16), wbot_ref[...], preferred_element_type=F32))


def even_out(x, ys, o_f, o_b, proj, w_glu, gla_norm, w_out, tm=ROW_TILE):
    t, d = x.shape
    row = lambda i: (i, 0)
    const = lambda i: (0, 0)
    return pl.pallas_call(
        _even_out_body,
        out_shape=jax.ShapeDtypeStruct((t, d), F32),
        grid=(t // tm,),
        in_specs=[pl.BlockSpec((tm, d), row), pl.BlockSpec((tm, S5_WIDTH), row),
                  pl.BlockSpec((tm, GLA_V), row), pl.BlockSpec((tm, GLA_V), row),
                  pl.BlockSpec((tm, GLA_V), lambda i: (i, 1536 // GLA_V)),
                  pl.BlockSpec((S5_WIDTH, S5_WIDTH), const), pl.BlockSpec((1, GLA_DV), const),
                  pl.BlockSpec((S5_WIDTH, d), const), pl.BlockSpec((GLA_V, d), const)],
        out_specs=pl.BlockSpec((tm, d), row),
        compiler_params=_params("parallel"),
        name="even_out",
    )(x, ys, o_f, o_b, proj, w_glu, gla_norm.astype(F32).reshape(1, GLA_DV),
      w_out[:S5_WIDTH], w_out[S5_WIDTH:])


def _seg_rms(a, gain):
    a2 = a * a
    lane = lax.broadcasted_iota(jnp.int32, a.shape, 1)
    lo = lane < DIFF_DK
    s_lo = jnp.sum(jnp.where(lo, a2, 0.0), axis=-1, keepdims=True)
    s_hi = jnp.sum(jnp.where(lo, 0.0, a2), axis=-1, keepdims=True)
    ms = jnp.where(lo, s_lo, s_hi) * (1.0 / DIFF_DK)
    return a * lax.rsqrt(ms + EPS) * gain


def _qkv_body(x_ref, g_ref, w_ref, qk_gain_ref, o_ref, h_ref):
    j = pl.program_id(1)

    @pl.when(j == 0)
    def _():
        h_ref[...] = _rms(x_ref[...], g_ref[...]).astype(BF16)

    acc = jnp.dot(h_ref[...], w_ref[...], preferred_element_type=F32)

    @pl.when(j < 2)
    def _():
        gain = qk_gain_ref[0]
        for c in range(DIFF_HEADS):
            sl = slice(c * 2 * DIFF_DK, (c + 1) * 2 * DIFF_DK)
            o_ref[:, sl] = _seg_rms(acc[:, sl], gain).astype(o_ref.dtype)

    @pl.when(j == 2)
    def _():
        o_ref[...] = acc.astype(o_ref.dtype)


def qkv_project(x, gain, w, q_norm, k_norm, tm=ROW_TILE):
    t, d = x.shape
    qg = jnp.tile(q_norm.astype(F32) * DIFF_DK ** -0.5, 2)
    kg = jnp.tile(k_norm.astype(F32), 2)
    qk_gain = jnp.stack([qg, kg]).reshape(2, 1, 2 * DIFF_DK)
    return pl.pallas_call(
        _qkv_body,
        out_shape=jax.ShapeDtypeStruct((t, 3 * d), BF16),
        grid=(t // tm, 3),
        in_specs=[pl.BlockSpec((tm, d), lambda i, j: (i, 0)),
                  pl.BlockSpec((1, d), lambda i, j: (0, 0)),
                  pl.BlockSpec((d, d), lambda i, j: (0, j)),
                  pl.BlockSpec((1, 1, 2 * DIFF_DK), lambda i, j: (jnp.minimum(j, 1), 0, 0))],
        out_specs=pl.BlockSpec((tm, d), lambda i, j: (i, j)),
        scratch_shapes=[pltpu.VMEM((tm, d), BF16)],
        compiler_params=_params("parallel", "arbitrary"),
        name="qkv_project",
    )(x, gain.reshape(1, d), w, qk_gain)


def _diff_attn_body(slope_ref, q_ref, k_ref, v_ref, lq1_ref, lk1_ref, lq2_ref, lk2_ref, sub_ref,
                    o_ref, m_ref, l_ref, acc_ref, *, tq, tk, seq_len, lambda_init):
    h = pl.program_id(1)
    qi = pl.program_id(2)
    slope = slope_ref[h]
    q = q_ref[...]
    lane = lax.broadcasted_iota(jnp.int32, q.shape, 1)
    zero = jnp.zeros_like(q)
    qz = (jnp.where(lane < DIFF_DK, q, zero), jnp.where(lane < DIFF_DK, zero, q))
    m_ref[...] = jnp.full(m_ref.shape, -jnp.inf, F32)
    l_ref[...] = jnp.zeros_like(l_ref)
    acc_ref[...] = jnp.zeros_like(acc_ref)
    rel = (lax.broadcasted_iota(jnp.int32, (tq, tk), 0)
           - lax.broadcasted_iota(jnp.int32, (tq, tk), 1))

    def kv_step(j, carry):
        rows = pl.ds(pl.multiple_of(j * tk, tk), tk)
        k = k_ref[rows, :]
        v = v_ref[rows, :]
        bias = -slope * jnp.abs(rel + (qi * tq - j * tk)).astype(F32)
        for z in range(2):
            s = lax.dot_general(qz[z], k, NT_DIMS, preferred_element_type=F32) + bias
            m_prev = m_ref[z]
            m_new = jnp.maximum(m_prev, jnp.max(s, axis=-1, keepdims=True))
            alpha = jnp.exp(m_prev - m_new)
            p = jnp.exp(s - m_new)
            l_ref[z] = alpha * l_ref[z] + jnp.sum(p, axis=-1, keepdims=True)
            acc_ref[z] = alpha * acc_ref[z] + jnp.dot(p.astype(BF16), v, preferred_element_type=F32)
            m_ref[z] = m_new
        return carry

    lax.fori_loop(0, seq_len // tk, kv_step, 0)
    lam = (jnp.exp(jnp.sum(lq1_ref[...] * lk1_ref[...], axis=-1, keepdims=True))
           - jnp.exp(jnp.sum(lq2_ref[...] * lk2_ref[...], axis=-1, keepdims=True)) + lambda_init)
    o = acc_ref[0] / l_ref[0] - lam * (acc_ref[1] / l_ref[1])
    o_ref[...] = (_rms(o, sub_ref[...]) * (1.0 - lambda_init)).astype(o_ref.dtype)


def diff_attention(qkv, bsz, seq_len, lq1, lk1, lq2, lk2, sub_norm, lambda_init, tq=256, tk=512):
    t = qkv.shape[0]
    nq = seq_len // tq
    slopes = jnp.asarray(2.0 ** (-8.0 * np.arange(1, DIFF_HEADS + 1, dtype=np.float32) / DIFF_HEADS), F32)
    vec = lambda a: a.astype(F32).reshape(1, DIFF_DK)
    const = lambda b, h, i: (0, 0)
    return pl.pallas_call(
        functools.partial(_diff_attn_body, tq=tq, tk=tk, seq_len=seq_len, lambda_init=lambda_init),
        out_shape=jax.ShapeDtypeStruct((t, DIFF_HEADS * DIFF_DV), BF16),
        grid=(bsz, DIFF_HEADS, nq),
        in_specs=[pl.BlockSpec(memory_space=pltpu.SMEM),
                  pl.BlockSpec((tq, 2 * DIFF_DK), lambda b, h, i: (b * nq + i, h)),
                  pl.BlockSpec((seq_len, 2 * DIFF_DK), lambda b, h, i: (b, DIFF_HEADS + h)),
                  pl.BlockSpec((seq_len, DIFF_DV), lambda b, h, i: (b, 2 * DIFF_HEADS + h)),
                  pl.BlockSpec((1, DIFF_DK), const), pl.BlockSpec((1, DIFF_DK), const),
                  pl.BlockSpec((1, DIFF_DK), const), pl.BlockSpec((1, DIFF_DK), const),
                  pl.BlockSpec((1, DIFF_DV), const)],
        out_specs=pl.BlockSpec((tq, DIFF_DV), lambda b, h, i: (b * nq + i, h)),
        scratch_shapes=[pltpu.VMEM((2, tq, 1), F32), pltpu.VMEM((2, tq, 1), F32),
                        pltpu.VMEM((2, tq, DIFF_DV), F32)],
        compiler_params=_params("parallel", "parallel", "arbitrary"),
        name="diff_attention",
    )(slopes, qkv, qkv, qkv, vec(lq1), vec(lk1), vec(lq2), vec(lk2),
      sub_norm.astype(F32).reshape(1, DIFF_DV))


def _pad_ev_w_in(w):
    return jnp.pad(w, ((0, 0), (0, EV_PAD_COLS - w.shape[1]))).astype(BF16)


def _trunk(x3, mem, p):
    bsz, seq_len, d = x3.shape
    x = x3.reshape(bsz * seq_len, d)
    kn_all, v_all = mem_kv(mem, p['norm_mem'], p['x_w_kv'], p['x_k_norm'])
    for layer in range(DEPTH):
        if layer % 2 == 0:
            e = layer // 2
            proj = norm_matmul(x, p['norm_mix'][layer], p['ev_w_in'][e], F32, EV_TN)
            ys = s5_scan(proj[:, :S5_WIDTH], bsz, seq_len, p['s5_ops'][e])
            o_f, o_b = gla_scan(proj, bsz, seq_len, p['gla_w_gate'][e], p['gla_b_gate'][e])
            x = even_out(x, ys, o_f, o_b, proj, p['s5_w_glu'][e], p['gla_norm'][e], p['ev_w_out'][e])
        else:
            o = layer // 2
            lambda_init = 0.8 - 0.6 * math.exp(-0.3 * layer)
            qkv = qkv_project(x, p['norm_mix'][layer], p['od_w_in'][o],
                              p['diff_q_norm'][o], p['diff_k_norm'][o])
            att = diff_attention(qkv, bsz, seq_len, p['diff_lambda_q1'][o], p['diff_lambda_k1'][o],
                                 p['diff_lambda_q2'][o], p['diff_lambda_k2'][o], p['diff_norm'][o],
                                 lambda_init)
            x = matmul_residual(att, p['od_w_out'][o], x)
        x = cross_block(x, seq_len, p['norm_cross'][layer], p['x_w_q'][layer], p['x_q_norm'][layer],
                        kn_all[layer], v_all[layer], p['x_w_o'][layer])
        x = mlp_block(x, p['norm_mlp'][layer], p['mlp_w1'][layer], p['mlp_w2'][layer])
    return x.reshape(bsz, seq_len, d)


def kernel(x_prompt, x_sample, mem_prompt, mem_sample, norm_mix, norm_cross, norm_mem, norm_mlp,
           ev_w_in, ev_w_out, s5_lambda_re, s5_lambda_im, s5_log_step, s5_b_re, s5_b_im,
           s5_c_re, s5_c_im, s5_d, s5_w_glu, gla_w_gate, gla_b_gate, gla_norm,
           od_w_in, od_w_out, diff_q_norm, diff_k_norm, diff_lambda_q1, diff_lambda_k1,
           diff_lambda_q2, diff_lambda_k2, diff_norm, x_w_q, x_w_kv, x_w_o, x_q_norm, x_k_norm,
           mlp_w1, mlp_w2):
    n_even = ev_w_in.shape[0]
    bf = lambda w: w.astype(BF16)
    p = dict(
        norm_mix=norm_mix, norm_cross=norm_cross, norm_mem=norm_mem, norm_mlp=norm_mlp,
        ev_w_in=[_pad_ev_w_in(ev_w_in[e]) for e in range(n_even)], ev_w_out=bf(ev_w_out),
        s5_ops=[s5_operators(s5_lambda_re[e], s5_lambda_im[e], s5_log_step[e], s5_b_re[e], s5_b_im[e],
                             s5_c_re[e], s5_c_im[e], s5_d[e]) for e in range(n_even)],
        s5_w_glu=bf(s5_w_glu), gla_w_gate=gla_w_gate, gla_b_gate=gla_b_gate, gla_norm=gla_norm,
        od_w_in=bf(od_w_in), od_w_out=bf(od_w_out), diff_q_norm=diff_q_norm, diff_k_norm=diff_k_norm,
        diff_lambda_q1=diff_lambda_q1, diff_lambda_k1=diff_lambda_k1,
        diff_lambda_q2=diff_lambda_q2, diff_lambda_k2=diff_lambda_k2, diff_norm=diff_norm,
        x_w_q=bf(x_w_q), x_w_kv=bf(x_w_kv), x_w_o=bf(x_w_o), x_q_norm=x_q_norm, x_k_norm=x_k_norm,
        mlp_w1=bf(mlp_w1), mlp_w2=bf(mlp_w2))
    return (_trunk(x_prompt, mem_prompt, p), _trunk(x_sample, mem_sample, p))
```

```python
import functools
import math

import numpy as np
import jax
import jax.numpy as jnp
from jax import lax
from jax.experimental import pallas as pl
from jax.experimental.pallas import tpu as pltpu

F32 = jnp.float32
BF16 = jnp.bfloat16
HIGHEST = lax.Precision.HIGHEST

D_MODEL = 1024
DEPTH = 4
EPS = 1e-6
S5_WIDTH = 512
S5_GROUP = 16
S5_GROUPS = 32
S5_STATE = 64
S5_CHUNK = 64
GLA_HEADS = 4
GLA_DV = 128
GLA_DK = 64
GLA_RANK = 16
GLA_TAU = 16.0
GLA_CHUNK = 64
GLA_QK = GLA_HEADS * GLA_DK
GLA_V = GLA_HEADS * GLA_DV
EV_PAD_COLS = 2304
EV_TN = 768
DIFF_HEADS = 8
DIFF_DK = 64
DIFF_DV = 128
X_HEADS = 4
X_DH = 256
D_FF = 4096

ROW_TILE = 512
VMEM_LIMIT = 48 * 1024 * 1024

NT_DIMS = (((1,), (1,)), ((), ()))
TN_DIMS = (((0,), (0,)), ((), ()))


def _params(*sem):
    return pltpu.CompilerParams(dimension_semantics=sem, vmem_limit_bytes=VMEM_LIMIT)


def _rms(x, gain):
    ms = jnp.mean(x * x, axis=-1, keepdims=True)
    return x * lax.rsqrt(ms + EPS) * gain


def _sigmoid(x):
    return 1.0 / (1.0 + jnp.exp(-x))


def _norm_matmul_body(x_ref, g_ref, w_ref, o_ref, h_ref):
    @pl.when(pl.program_id(1) == 0)
    def _():
        h_ref[...] = _rms(x_ref[...], g_ref[...]).astype(BF16)

    o_ref[...] = jnp.dot(h_ref[...], w_ref[...], preferred_element_type=F32).astype(o_ref.dtype)


def norm_matmul(x, gain, w, out_dtype, tn, tm=ROW_TILE):
    t, d = x.shape
    n = w.shape[1]
    return pl.pallas_call(
        _norm_matmul_body,
        out_shape=jax.ShapeDtypeStruct((t, n), out_dtype),
        grid=(t // tm, n // tn),
        in_specs=[pl.BlockSpec((tm, d), lambda i, j: (i, 0)),
                  pl.BlockSpec((1, d), lambda i, j: (0, 0)),
                  pl.BlockSpec((d, tn), lambda i, j: (0, j))],
        out_specs=pl.BlockSpec((tm, tn), lambda i, j: (i, j)),
        scratch_shapes=[pltpu.VMEM((tm, d), BF16)],
        compiler_params=_params("parallel", "arbitrary"),
        name="norm_matmul",
    )(x, gain.reshape(1, d), w)


def _matmul_res_body(a_ref, w_ref, r_ref, o_ref):
    o_ref[...] = r_ref[...] + jnp.dot(a_ref[...].astype(BF16), w_ref[...], preferred_element_type=F32)


def matmul_residual(a, w, res, tn=512, tm=ROW_TILE):
    t, k = a.shape
    n = w.shape[1]
    return pl.pallas_call(
        _matmul_res_body,
        out_shape=jax.ShapeDtypeStruct((t, n), F32),
        grid=(t // tm, n // tn),
        in_specs=[pl.BlockSpec((tm, k), lambda i, j: (i, 0)),
                  pl.BlockSpec((k, tn), lambda i, j: (0, j)),
                  pl.BlockSpec((tm, tn), lambda i, j: (i, j))],
        out_specs=pl.BlockSpec((tm, tn), lambda i, j: (i, j)),
        compiler_params=_params("parallel", "arbitrary"),
        name="matmul_residual",
    )(a, w, res)


def _mlp_body(x_ref, g_ref, w1_ref, w2_ref, o_ref, h_ref, acc_ref):
    f = pl.program_id(1)

    @pl.when(f == 0)
    def _():
        h_ref[...] = _rms(x_ref[...], g_ref[...]).astype(BF16)
        acc_ref[...] = jnp.zeros_like(acc_ref)

    hid = jnp.dot(h_ref[...], w1_ref[...], preferred_element_type=F32)
    hid = jnp.square(jnp.maximum(hid, 0.0)).astype(BF16)
    acc_ref[...] += jnp.dot(hid, w2_ref[...], preferred_element_type=F32)

    @pl.when(f == pl.num_programs(1) - 1)
    def _():
        o_ref[...] = x_ref[...] + acc_ref[...]


def mlp_block(x, gain, w1, w2, tf=512, tm=ROW_TILE):
    t, d = x.shape
    ff = w1.shape[1]
    return pl.pallas_call(
        _mlp_body,
        out_shape=jax.ShapeDtypeStruct((t, d), F32),
        grid=(t // tm, ff // tf),
        in_specs=[pl.BlockSpec((tm, d), lambda i, f: (i, 0)),
                  pl.BlockSpec((1, d), lambda i, f: (0, 0)),
                  pl.BlockSpec((d, tf), lambda i, f: (0, f)),
                  pl.BlockSpec((tf, d), lambda i, f: (f, 0))],
        out_specs=pl.BlockSpec((tm, d), lambda i, f: (i, 0)),
        scratch_shapes=[pltpu.VMEM((tm, d), BF16), pltpu.VMEM((tm, d), F32)],
        compiler_params=_params("parallel", "arbitrary"),
        name="mlp_block",
    )(x, gain.reshape(1, d), w1, w2)


def _mem_kv_body(m_ref, g_ref, w_ref, kg_ref, k_ref, v_ref):
    h = _rms(m_ref[0], g_ref[0]).astype(BF16)
    kv = jnp.dot(h, w_ref[0], preferred_element_type=F32)
    for hd in range(X_HEADS):
        sl = slice(hd * X_DH, (hd + 1) * X_DH)
        k_ref[0, 0, :, sl] = _rms(kv[:, sl], kg_ref[0]).astype(BF16)
    v_ref[0, 0] = kv[:, D_MODEL:].astype(BF16)


def mem_kv(mem, norm_mem, w_kv, k_norm):
    bm, nm, d = mem.shape
    out = jax.ShapeDtypeStruct((DEPTH, bm, nm, d), BF16)
    return pl.pallas_call(
        _mem_kv_body,
        out_shape=(out, out),
        grid=(DEPTH, bm),
        in_specs=[pl.BlockSpec((1, nm, d), lambda l, b: (b, 0, 0)),
                  pl.BlockSpec((1, 1, d), lambda l, b: (l, 0, 0)),
                  pl.BlockSpec((1, d, 2 * d), lambda l, b: (l, 0, 0)),
                  pl.BlockSpec((1, 1, X_DH), lambda l, b: (l, 0, 0))],
        out_specs=(pl.BlockSpec((1, 1, nm, d), lambda l, b: (l, b, 0, 0)),
                   pl.BlockSpec((1, 1, nm, d), lambda l, b: (l, b, 0, 0))),
        compiler_params=_params("arbitrary", "arbitrary"),
        name="mem_kv",
    )(mem, norm_mem.reshape(DEPTH, 1, d), w_kv, k_norm.reshape(DEPTH, 1, X_DH))


def _cross_body(x_ref, g_ref, wq_ref, qg_ref, k_ref, v_ref, wo_ref, o_ref):
    x = x_ref[...]
    h = _rms(x, g_ref[...]).astype(BF16)
    q = jnp.dot(h, wq_ref[...], preferred_element_type=F32)
    heads = []
    for hd in range(X_HEADS):
        sl = slice(hd * X_DH, (hd + 1) * X_DH)
        qn = _rms(q[:, sl], qg_ref[...]).astype(BF16)
        s = lax.dot_general(qn, k_ref[0, :, sl], NT_DIMS, preferred_element_type=F32)
        p = jnp.exp(s - jnp.max(s, axis=-1, keepdims=True))
        l = jnp.sum(p, axis=-1, keepdims=True)
        oh = jnp.dot(p.astype(BF16), v_ref[0, :, sl], preferred_element_type=F32) / l
        heads.append(oh.astype(BF16))
    o = jnp.concatenate(heads, axis=-1)
    o_ref[...] = x + jnp.dot(o, wo_ref[...], preferred_element_type=F32)


def cross_block(x, seq_len, gain, w_q, q_gain, kn, v, w_o, tm=ROW_TILE):
    t, d = x.shape
    nm = kn.shape[1]
    per_seq = seq_len // tm
    return pl.pallas_call(
        _cross_body,
        out_shape=jax.ShapeDtypeStruct((t, d), F32),
        grid=(t // tm,),
        in_specs=[pl.BlockSpec((tm, d), lambda i: (i, 0)),
                  pl.BlockSpec((1, d), lambda i: (0, 0)),
                  pl.BlockSpec((d, d), lambda i: (0, 0)),
                  pl.BlockSpec((1, X_DH), lambda i: (0, 0)),
                  pl.BlockSpec((1, nm, d), lambda i: (i // per_seq, 0, 0)),
                  pl.BlockSpec((1, nm, d), lambda i: (i // per_seq, 0, 0)),
                  pl.BlockSpec((d, d), lambda i: (0, 0))],
        out_specs=pl.BlockSpec((tm, d), lambda i: (i, 0)),
        compiler_params=_params("parallel"),
        name="cross_block",
    )(x, gain.reshape(1, d), w_q, (q_gain * X_DH ** -0.5).reshape(1, X_DH), kn, v, w_o)


def s5_operators(lam_re, lam_im, log_step, b_re, b_im, c_re, c_im, d):
    lc = S5_CHUNK
    lam = lax.complex(lam_re.astype(F32), lam_im.astype(F32))
    step = jnp.exp(log_step.astype(F32))[..., None]
    lam_bar = jnp.exp(lam * step)
    b_bar = ((lam_bar - 1.0) / lam)[..., None] * lax.complex(b_re.astype(F32), b_im.astype(F32))
    c = lax.complex(c_re.astype(F32), c_im.astype(F32))
    pw = jnp.cumprod(jnp.broadcast_to(lam_bar[..., None], lam_bar.shape + (lc,)), axis=-1)
    pw = jnp.concatenate([jnp.ones_like(pw[..., :1]), pw], axis=-1)
    kern = jnp.einsum('zgcp,zgpt,zgpd->zgtcd', c, pw[..., :lc], b_bar, precision=HIGHEST).real
    kf, kb = kern[0], kern[1]
    k0 = kf[:, :1] + kb[:, :1] + (d.astype(F32)[:, :, None] * jnp.eye(S5_GROUP, dtype=F32))[:, None]
    kern_full = jnp.concatenate([kb[:, :0:-1], k0, kf[:, 1:]], axis=1)
    lag = np.arange(lc)[None, :] - np.arange(lc)[:, None] + lc - 1
    toep = kern_full[:, lag]
    toep = toep.transpose(0, 1, 4, 2, 3).reshape(S5_GROUPS, lc * S5_GROUP, lc * S5_GROUP)

    pf = jnp.einsum('gps,gpd->gsdp', pw[0][..., lc - 1::-1], b_bar[0])
    pb = jnp.einsum('gps,gpd->gsdp', pw[1][..., :lc], b_bar[1])
    p_op = jnp.concatenate([pf.real, pf.imag, pb.real, pb.imag], axis=-1)
    p_op = p_op.reshape(S5_GROUPS, lc * S5_GROUP, 4 * S5_STATE)

    qf = jnp.einsum('gcp,gpt->gptc', c[0], pw[0][..., 1:])
    qb = jnp.einsum('gcp,gpt->gptc', c[1], pw[1][..., :0:-1])
    q_op = jnp.concatenate([qf.real, -qf.imag, qb.real, -qb.imag], axis=1)
    q_op = q_op.reshape(S5_GROUPS, 4 * S5_STATE, lc * S5_GROUP)

    a = pw[..., lc]
    coef = jnp.stack([jnp.concatenate([a[0].real, a[0].real], -1),
                      jnp.concatenate([-a[0].imag, a[0].imag], -1),
                      jnp.concatenate([a[1].real, a[1].real], -1),
                      jnp.concatenate([-a[1].imag, a[1].imag], -1)])
    return toep.astype(BF16), p_op, q_op, coef


def _s5_state_in_body(u_ref, p_ref, v_ref):
    v_ref[0] = jnp.dot(u_ref[0], p_ref[0], preferred_element_type=F32, precision=HIGHEST)


def _s5_scan_body(vf_ref, vb_ref, cf_ref, xf_ref, xb_ref):
    n = vf_ref.shape[0]
    c1f, c2f, c1b, c2b = cf_ref[0], cf_ref[1], cf_ref[2], cf_ref[3]

    def step(i, carry):
        xf, xb = carry
        xf_ref[i] = xf
        xf = xf * c1f + pltpu.roll(xf, S5_STATE, 1) * c2f + vf_ref[i]
        j = n - 1 - i
        xb_ref[j] = xb
        xb = xb * c1b + pltpu.roll(xb, S5_STATE, 1) * c2b + vb_ref[j]
        return xf, xb

    zero = jnp.zeros(vf_ref.shape[1:], F32)
    lax.fori_loop(0, n, step, (zero, zero))


def _s5_out_body(u_ref, t_ref, x_ref, q_ref, y_ref):
    y = jnp.dot(u_ref[0].astype(BF16), t_ref[0], preferred_element_type=F32)
    y_ref[0] = y + jnp.dot(x_ref[0], q_ref[0], preferred_element_type=F32, precision=HIGHEST)


def s5_scan(u, bsz, seq_len, ops):
    toep, p_op, q_op, coef = ops
    lc, g, w = S5_CHUNK, S5_GROUPS, S5_CHUNK * S5_GROUP
    n = seq_len // lc
    c = bsz * n
    ns = 4 * S5_STATE
    ug = u.reshape(bsz, n, lc, g, S5_GROUP).transpose(3, 0, 1, 2, 4).reshape(g, c, w)
    v = pl.pallas_call(
        _s5_state_in_body,
        out_shape=jax.ShapeDtypeStruct((g, c, ns), F32),
        grid=(g,),
        in_specs=[pl.BlockSpec((1, c, w), lambda i: (i, 0, 0)),
                  pl.BlockSpec((1, w, ns), lambda i: (i, 0, 0))],
        out_specs=pl.BlockSpec((1, c, ns), lambda i: (i, 0, 0)),
        compiler_params=_params("parallel"),
        name="s5_state_in",
    )(ug, p_op)
    v = v.reshape(g, bsz, n, ns).transpose(2, 1, 0, 3).reshape(n, bsz * g, ns)
    half = ns // 2
    cf = jnp.tile(coef, (1, bsz, 1))
    xf, xb = pl.pallas_call(
        _s5_scan_body,
        out_shape=(jax.ShapeDtypeStruct((n, bsz * g, half), F32),) * 2,
        compiler_params=pltpu.CompilerParams(vmem_limit_bytes=VMEM_LIMIT),
        name="s5_chunk_scan",
    )(v[..., :half], v[..., half:], cf)
    x = jnp.concatenate([xf, xb], axis=-1).reshape(n, bsz, g, ns).transpose(2, 1, 0, 3).reshape(g, c, ns)
    y = pl.pallas_call(
        _s5_out_body,
        out_shape=jax.ShapeDtypeStruct((g, c, w), F32),
        grid=(g,),
        in_specs=[pl.BlockSpec((1, c, w), lambda i: (i, 0, 0)),
                  pl.BlockSpec((1, w, w), lambda i: (i, 0, 0)),
                  pl.BlockSpec((1, c, ns), lambda i: (i, 0, 0)),
                  pl.BlockSpec((1, ns, w), lambda i: (i, 0, 0))],
        out_specs=pl.BlockSpec((1, c, w), lambda i: (i, 0, 0)),
        compiler_params=_params("parallel"),
        name="s5_out",
    )(ug, toep, x, q_op)
    return y.reshape(g, bsz, n, lc, S5_GROUP).transpose(1, 2, 3, 0, 4).reshape(bsz * seq_len, S5_WIDTH)


def _gla_chunk(q, k, v, glr, wg, bg, s_ref, forward):
    cs = GLA_CHUNK
    logit = jnp.dot(glr, wg, preferred_element_type=F32, precision=HIGHEST) + bg
    g = (jnp.minimum(logit, 0.0) - jnp.log1p(jnp.exp(-jnp.abs(logit)))) / GLA_TAU
    row = lax.broadcasted_iota(jnp.int32, (cs, cs), 0)
    col = lax.broadcasted_iota(jnp.int32, (cs, cs), 1)
    keep = (col <= row) if forward else (col >= row)
    bcum = jnp.dot(keep.astype(F32), g, preferred_element_type=F32, precision=HIGHEST)
    i_ref = cs // 2 - 1 if forward else cs // 2
    i_last = cs - 1 if forward else 0
    bref = bcum[i_ref:i_ref + 1]
    blast = bcum[i_last:i_last + 1]
    q_rel = q * jnp.exp(bcum - bref)
    k_rel = (k * jnp.exp(bref - bcum)).astype(BF16)
    k_out = (k * jnp.exp(blast - bcum)).astype(BF16)
    q_dec = q * jnp.exp(bcum)
    decay = jnp.exp(blast)
    lane = lax.broadcasted_iota(jnp.int32, (1, GLA_QK), 1)
    outs = []
    for h in range(GLA_HEADS):
        hm = ((lane >= h * GLA_DK) & (lane < (h + 1) * GLA_DK)).astype(F32)
        vh = v[:, h * GLA_DV:(h + 1) * GLA_DV].astype(BF16)
        s = lax.dot_general((q_rel * hm).astype(BF16), k_rel, NT_DIMS, preferred_element_type=F32)
        s = jnp.where(keep, s, 0.0)
        o = jnp.dot(s.astype(BF16), vh, preferred_element_type=F32)
        st = s_ref[h]
        o = o + lax.dot_general((q_dec * hm).astype(BF16), st.astype(BF16), NT_DIMS,
                                preferred_element_type=F32)
        kv = lax.dot_general(vh, k_out, TN_DIMS, preferred_element_type=F32)
        s_ref[h] = decay * st + kv * hm
        outs.append(o)
    return jnp.concatenate(outs, axis=-1)


def _gla_body(qf_ref, kf_ref, vf_ref, gf_ref, qb_ref, kb_ref, vb_ref, gb_ref, wg_ref, bg_ref,
              of_ref, ob_ref, sf_ref, sb_ref, *, chunks):
    @pl.when(pl.program_id(1) == 0)
    def _():
        sf_ref[...] = jnp.zeros_like(sf_ref)
        sb_ref[...] = jnp.zeros_like(sb_ref)

    cs = GLA_CHUNK
    scale = GLA_DK ** -0.5
    for ci in range(chunks):
        rows = pl.ds(ci * cs, cs)
        of_ref[rows, :] = _gla_chunk(qf_ref[rows, :] * scale, kf_ref[rows, :], vf_ref[rows, :],
                                     gf_ref[rows, :], wg_ref[0], bg_ref[0], sf_ref, True)
    for ci in reversed(range(chunks)):
        rows = pl.ds(ci * cs, cs)
        ob_ref[rows, :] = _gla_chunk(qb_ref[rows, :] * scale, kb_ref[rows, :], vb_ref[rows, :],
                                     gb_ref[rows, :], wg_ref[1], bg_ref[1], sb_ref, False)


def gla_scan(proj, bsz, seq_len, w_gate, b_gate, chunks=4):
    t = proj.shape[0]
    tb = chunks * GLA_CHUNK
    nb = seq_len // tb
    fwd = lambda cb: (lambda b, i: (b * nb + i, cb))
    bwd = lambda cb: (lambda b, i: (b * nb + nb - 1 - i, cb))
    qc, kc, vc, gc = 512 // GLA_QK, 768 // GLA_QK, 1024 // GLA_V, 2048 // 128
    wg = jnp.zeros((2, 128, GLA_QK), F32)
    wg = wg.at[0, :GLA_RANK].set(w_gate[0].astype(F32)).at[1, GLA_RANK:2 * GLA_RANK].set(w_gate[1].astype(F32))
    out = jax.ShapeDtypeStruct((t, GLA_V), F32)
    state = pltpu.VMEM((GLA_HEADS, GLA_DV, GLA_QK), F32)
    return pl.pallas_call(
        functools.partial(_gla_body, chunks=chunks),
        out_shape=(out, out),
        grid=(bsz, nb),
        in_specs=[pl.BlockSpec((tb, GLA_QK), fwd(qc)), pl.BlockSpec((tb, GLA_QK), fwd(kc)),
                  pl.BlockSpec((tb, GLA_V), fwd(vc)), pl.BlockSpec((tb, 128), fwd(gc)),
                  pl.BlockSpec((tb, GLA_QK), bwd(qc)), pl.BlockSpec((tb, GLA_QK), bwd(kc)),
                  pl.BlockSpec((tb, GLA_V), bwd(vc)), pl.BlockSpec((tb, 128), bwd(gc)),
                  pl.BlockSpec((2, 128, GLA_QK), lambda b, i: (0, 0, 0)),
                  pl.BlockSpec((2, 1, GLA_QK), lambda b, i: (0, 0, 0))],
        out_specs=(pl.BlockSpec((tb, GLA_V), fwd(0)), pl.BlockSpec((tb, GLA_V), bwd(0))),
        scratch_shapes=[state, state],
        compiler_params=_params("parallel", "arbitrary"),
        name="gla_scan",
    )(proj, proj, proj, proj, proj, proj, proj, proj, wg, b_gate.astype(F32).reshape(2, 1, GLA_QK))


def _even_out_body(x_ref, ys_ref, of_ref, ob_ref, og_ref, wglu_ref, gn_ref, wtop_ref, wbot_ref, o_ref):
    y = ys_ref[...]
    y = 0.5 * y * (1.0 + jnp.tanh(math.sqrt(2.0 / math.pi) * (y + 0.044715 * (y * y * y))))
    gate = jnp.dot(y.astype(BF16), wglu_ref[...], preferred_element_type=F32)
    y = y * _sigmoid(gate)
    o = of_ref[...] + ob_ref[...]
    og = og_ref[...]
    heads = []
    for h in range(GLA_HEADS):
        sl = slice(h * GLA_DV, (h + 1) * GLA_DV)
        heads.append(_rms(o[:, sl], gn_ref[...]))
    o = jnp.concatenate(heads, axis=-1) * (og * _sigmoid(og))
    o_ref[...] = (x_ref[...]
                  + jnp.dot(y.astype(BF16), wtop_ref[...], preferred_element_type=F32)
                  + jnp.dot(o.astype(BF16), wbot_ref[...], preferred_element_type=F32))


def even_out(x, ys, o_f, o_b, proj, w_glu, gla_norm, w_out, tm=ROW_TILE):
    t, d = x.shape
    row = lambda i: (i, 0)
    const = lambda i: (0, 0)
    return pl.pallas_call(
        _even_out_body,
        out_shape=jax.ShapeDtypeStruct((t, d), F32),
        grid=(t // tm,),
        in_specs=[pl.BlockSpec((tm, d), row), pl.BlockSpec((tm, S5_WIDTH), row),
                  pl.BlockSpec((tm, GLA_V), row), pl.BlockSpec((tm, GLA_V), row),
                  pl.BlockSpec((tm, GLA_V), lambda i: (i, 1536 // GLA_V)),
                  pl.BlockSpec((S5_WIDTH, S5_WIDTH), const), pl.BlockSpec((1, GLA_DV), const),
                  pl.BlockSpec((S5_WIDTH, d), const), pl.BlockSpec((GLA_V, d), const)],
        out_specs=pl.BlockSpec((tm, d), row),
        compiler_params=_params("parallel"),
        name="even_out",
    )(x, ys, o_f, o_b, proj, w_glu, gla_norm.astype(F32).reshape(1, GLA_DV),
      w_out[:S5_WIDTH], w_out[S5_WIDTH:])


def _seg_rms(a, gain):
    a2 = a * a
    lane = lax.broadcasted_iota(jnp.int32, a.shape, 1)
    lo = lane < DIFF_DK
    s_lo = jnp.sum(jnp.where(lo, a2, 0.0), axis=-1, keepdims=True)
    s_hi = jnp.sum(jnp.where(lo, 0.0, a2), axis=-1, keepdims=True)
    ms = jnp.where(lo, s_lo, s_hi) * (1.0 / DIFF_DK)
    return a * lax.rsqrt(ms + EPS) * gain


def _qkv_body(x_ref, g_ref, w_ref, qk_gain_ref, o_ref, h_ref):
    j = pl.program_id(1)

    @pl.when(j == 0)
    def _():
        h_ref[...] = _rms(x_ref[...], g_ref[...]).astype(BF16)

    acc = jnp.dot(h_ref[...], w_ref[...], preferred_element_type=F32)

    @pl.when(j < 2)
    def _():
        gain = qk_gain_ref[0]
        for c in range(DIFF_HEADS):
            sl = slice(c * 2 * DIFF_DK, (c + 1) * 2 * DIFF_DK)
            o_ref[:, sl] = _seg_rms(acc[:, sl], gain).astype(o_ref.dtype)

    @pl.when(j == 2)
    def _():
        o_ref[...] = acc.astype(o_ref.dtype)


def qkv_project(x, gain, w, q_norm, k_norm, tm=ROW_TILE):
    t, d = x.shape
    qg = jnp.tile(q_norm.astype(F32) * DIFF_DK ** -0.5, 2)
    kg = jnp.tile(k_norm.astype(F32), 2)
    qk_gain = jnp.stack([qg, kg]).reshape(2, 1, 2 * DIFF_DK)
    return pl.pallas_call(
        _qkv_body,
        out_shape=jax.ShapeDtypeStruct((t, 3 * d), BF16),
        grid=(t // tm, 3),
        in_specs=[pl.BlockSpec((tm, d), lambda i, j: (i, 0)),
                  pl.BlockSpec((1, d), lambda i, j: (0, 0)),
                  pl.BlockSpec((d, d), lambda i, j: (0, j)),
                  pl.BlockSpec((1, 1, 2 * DIFF_DK), lambda i, j: (jnp.minimum(j, 1), 0, 0))],
        out_specs=pl.BlockSpec((tm, d), lambda i, j: (i, j)),
        scratch_shapes=[pltpu.VMEM((tm, d), BF16)],
        compiler_params=_params("parallel", "arbitrary"),
        name="qkv_project",
    )(x, gain.reshape(1, d), w, qk_gain)


POS_SPLIT = 16
POS_SHIFT = 4
AUG_LANE = DIFF_DK


def _pos_terms(shape):
    lane = lax.broadcasted_iota(jnp.int32, shape, 1)
    pos = lax.broadcasted_iota(jnp.int32, shape, 0)
    hi = lax.shift_right_logical(pos, POS_SHIFT).astype(F32)
    lo = (pos & (POS_SPLIT - 1)).astype(F32)
    return lane, hi, lo


def _diff_attn_body(slope_ref, q_ref, k_ref, v_ref, lq1_ref, lk1_ref, lq2_ref, lk2_ref, sub_ref,
                    o_ref, ka_ref, va_ref, qs_ref, m_ref, acc_ref, *, blk, seq_len, lambda_init):
    h = pl.program_id(1)
    qi = pl.program_id(2)
    slope = slope_ref[h]
    nk = seq_len // blk

    @pl.when(qi == 0)
    def _():
        va_ref[:, :DIFF_DV] = v_ref[...]
        va_ref[:, DIFF_DV:] = jnp.ones((seq_len, DIFF_DV), BF16)

        def build(t, carry):
            rows = pl.ds(pl.multiple_of(t * blk, blk), blk)
            kf = k_ref[rows, :].astype(F32)
            lane, hi, lo = _pos_terms(kf.shape)
            aug = jnp.where(lane == AUG_LANE, -POS_SPLIT * slope,
                  jnp.where(lane == AUG_LANE + 1, -slope,
                  jnp.where(lane == AUG_LANE + 2, (POS_SPLIT * slope) * hi,
                  jnp.where(lane == AUG_LANE + 3, slope * lo, 0.0))))
            for z, kz in enumerate((kf, pltpu.roll(kf, DIFF_DK, 1))):
                ka_ref[z, rows, :] = jnp.where(lane < DIFF_DK, kz, aug).astype(BF16)
            return carry

        lax.fori_loop(0, nk, build, 0)

    qf = q_ref[...].astype(F32)
    lane, hi, lo = _pos_terms(qf.shape)
    qa = jnp.where(lane == AUG_LANE, hi,
         jnp.where(lane == AUG_LANE + 1, lo,
         jnp.where((lane == AUG_LANE + 2) | (lane == AUG_LANE + 3), 1.0, 0.0)))
    for z, qz in enumerate((qf, pltpu.roll(qf, DIFF_DK, 1))):
        qs_ref[z] = jnp.where(lane < DIFF_DK, qz, 0.0).astype(BF16)
        qs_ref[2 + z] = jnp.where(lane < DIFF_DK, qz, qa).astype(BF16)
        qs_ref[4 + z] = jnp.where(lane < DIFF_DK, qz, -qa).astype(BF16)

    def scores(j, z, variant):
        rows = pl.ds(pl.multiple_of(j * blk, blk), blk)
        return lax.dot_general(qs_ref[variant + z], ka_ref[z, rows, :], NT_DIMS, preferred_element_type=F32)

    def values(j):
        return va_ref[pl.ds(pl.multiple_of(j * blk, blk), blk), :]

    rel = (lax.broadcasted_iota(jnp.int32, (blk, blk), 0) - lax.broadcasted_iota(jnp.int32, (blk, blk), 1))
    bias = -slope * jnp.abs(rel).astype(F32)
    for z in range(2):
        s = scores(qi, z, 0) + bias
        m = jnp.max(s, axis=-1, keepdims=True)
        p = jnp.exp(s - m).astype(BF16)
        acc_ref[z] = jnp.dot(p, values(qi), preferred_element_type=F32)
        m_ref[z] = m

    def off_diagonal(variant):
        def step(j, carry):
            c = -slope * (jnp.abs(qi - j) * blk).astype(F32)
            v = values(j)
            for z in range(2):
                s = scores(j, z, variant)
                m_prev = m_ref[z]
                m_new = jnp.maximum(m_prev, jnp.max(s, axis=-1, keepdims=True) + c)
                alpha = jnp.exp(m_prev - m_new)
                p = jnp.exp(s - (m_new - c)).astype(BF16)
                acc_ref[z] = alpha * acc_ref[z] + jnp.dot(p, v, preferred_element_type=F32)
                m_ref[z] = m_new
            return carry
        return step

    lax.fori_loop(0, qi, off_diagonal(2), 0)
    lax.fori_loop(qi + 1, nk, off_diagonal(4), 0)

    lam = (jnp.exp(jnp.sum(lq1_ref[...] * lk1_ref[...], axis=-1, keepdims=True))
           - jnp.exp(jnp.sum(lq2_ref[...] * lk2_ref[...], axis=-1, keepdims=True)) + lambda_init)
    a0 = acc_ref[0]
    a1 = acc_ref[1]
    o = (a0[:, :DIFF_DV] / a0[:, DIFF_DV:DIFF_DV + 1]
         - lam * (a1[:, :DIFF_DV] / a1[:, DIFF_DV:DIFF_DV + 1]))
    o_ref[...] = (_rms(o, sub_ref[...]) * (1.0 - lambda_init)).astype(o_ref.dtype)


def diff_attention(qkv, bsz, seq_len, lq1, lk1, lq2, lk2, sub_norm, lambda_init, blk=512):
    t = qkv.shape[0]
    nq = seq_len // blk
    assert blk // POS_SPLIT <= 256, "hi part of a block position must stay exact in bf16"
    slopes = jnp.asarray(2.0 ** (-8.0 * np.arange(1, DIFF_HEADS + 1, dtype=np.float32) / DIFF_HEADS), F32)
    vec = lambda a: a.astype(F32).reshape(1, DIFF_DK)
    const = lambda b, h, i: (0, 0)
    return pl.pallas_call(
        functools.partial(_diff_attn_body, blk=blk, seq_len=seq_len, lambda_init=lambda_init),
        out_shape=jax.ShapeDtypeStruct((t, DIFF_HEADS * DIFF_DV), BF16),
        grid=(bsz, DIFF_HEADS, nq),
        in_specs=[pl.BlockSpec(memory_space=pltpu.SMEM),
                  pl.BlockSpec((blk, 2 * DIFF_DK), lambda b, h, i: (b * nq + i, h)),
                  pl.BlockSpec((seq_len, 2 * DIFF_DK), lambda b, h, i: (b, DIFF_HEADS + h)),
                  pl.BlockSpec((seq_len, DIFF_DV), lambda b, h, i: (b, 2 * DIFF_HEADS + h)),
                  pl.BlockSpec((1, DIFF_DK), const), pl.BlockSpec((1, DIFF_DK), const),
                  pl.BlockSpec((1, DIFF_DK), const), pl.BlockSpec((1, DIFF_DK), const),
                  pl.BlockSpec((1, DIFF_DV), const)],
        out_specs=pl.BlockSpec((blk, DIFF_DV), lambda b, h, i: (b * nq + i, h)),
        scratch_shapes=[pltpu.VMEM((2, seq_len, 2 * DIFF_DK), BF16),
                        pltpu.VMEM((seq_len, 2 * DIFF_DV), BF16),
                        pltpu.VMEM((6, blk, 2 * DIFF_DK), BF16),
                        pltpu.VMEM((2, blk, 1), F32),
                        pltpu.VMEM((2, blk, 2 * DIFF_DV), F32)],
        compiler_params=_params("parallel", "parallel", "arbitrary"),
        name="diff_attention",
    )(slopes, qkv, qkv, qkv, vec(lq1), vec(lk1), vec(lq2), vec(lk2),
      sub_norm.astype(F32).reshape(1, DIFF_DV))


def _pad_ev_w_in(w):
    return jnp.pad(w, ((0, 0), (0, EV_PAD_COLS - w.shape[1]))).astype(BF16)


def _trunk(x3, mem, p):
    bsz, seq_len, d = x3.shape
    x = x3.reshape(bsz * seq_len, d)
    kn_all, v_all = mem_kv(mem, p['norm_mem'], p['x_w_kv'], p['x_k_norm'])
    for layer in range(DEPTH):
        if layer % 2 == 0:
            e = layer // 2
            proj = norm_matmul(x, p['norm_mix'][layer], p['ev_w_in'][e], F32, EV_TN)
            ys = s5_scan(proj[:, :S5_WIDTH], bsz, seq_len, p['s5_ops'][e])
            o_f, o_b = gla_scan(proj, bsz, seq_len, p['gla_w_gate'][e], p['gla_b_gate'][e])
            x = even_out(x, ys, o_f, o_b, proj, p['s5_w_glu'][e], p['gla_norm'][e], p['ev_w_out'][e])
        else:
            o = layer // 2
            lambda_init = 0.8 - 0.6 * math.exp(-0.3 * layer)
            qkv = qkv_project(x, p['norm_mix'][layer], p['od_w_in'][o],
                              p['diff_q_norm'][o], p['diff_k_norm'][o])
            att = diff_attention(qkv, bsz, seq_len, p['diff_lambda_q1'][o], p['diff_lambda_k1'][o],
                                 p['diff_lambda_q2'][o], p['diff_lambda_k2'][o], p['diff_norm'][o],
                                 lambda_init)
            x = matmul_residual(att, p['od_w_out'][o], x)
        x = cross_block(x, seq_len, p['norm_cross'][layer], p['x_w_q'][layer], p['x_q_norm'][layer],
                        kn_all[layer], v_all[layer], p['x_w_o'][layer])
        x = mlp_block(x, p['norm_mlp'][layer], p['mlp_w1'][layer], p['mlp_w2'][layer])
    return x.reshape(bsz, seq_len, d)


def kernel(x_prompt, x_sample, mem_prompt, mem_sample, norm_mix, norm_cross, norm_mem, norm_mlp,
           ev_w_in, ev_w_out, s5_lambda_re, s5_lambda_im, s5_log_step, s5_b_re, s5_b_im,
           s5_c_re, s5_c_im, s5_d, s5_w_glu, gla_w_gate, gla_b_gate, gla_norm,
           od_w_in, od_w_out, diff_q_norm, diff_k_norm, diff_lambda_q1, diff_lambda_k1,
           diff_lambda_q2, diff_lambda_k2, diff_norm, x_w_q, x_w_kv, x_w_o, x_q_norm, x_k_norm,
           mlp_w1, mlp_w2):
    n_even = ev_w_in.shape[0]
    bf = lambda w: w.astype(BF16)
    p = dict(
        norm_mix=norm_mix, norm_cross=norm_cross, norm_mem=norm_mem, norm_mlp=norm_mlp,
        ev_w_in=[_pad_ev_w_in(ev_w_in[e]) for e in range(n_even)], ev_w_out=bf(ev_w_out),
        s5_ops=[s5_operators(s5_lambda_re[e], s5_lambda_im[e], s5_log_step[e], s5_b_re[e], s5_b_im[e],
                             s5_c_re[e], s5_c_im[e], s5_d[e]) for e in range(n_even)],
        s5_w_glu=bf(s5_w_glu), gla_w_gate=gla_w_gate, gla_b_gate=gla_b_gate, gla_norm=gla_norm,
        od_w_in=bf(od_w_in), od_w_out=bf(od_w_out), diff_q_norm=diff_q_norm, diff_k_norm=diff_k_norm,
        diff_lambda_q1=diff_lambda_q1, diff_lambda_k1=diff_lambda_k1,
        diff_lambda_q2=diff_lambda_q2, diff_lambda_k2=diff_lambda_k2, diff_norm=diff_norm,
        x_w_q=bf(x_w_q), x_w_kv=bf(x_w_kv), x_w_o=bf(x_w_o), x_q_norm=x_q_norm, x_k_norm=x_k_norm,
        mlp_w1=bf(mlp_w1), mlp_w2=bf(mlp_w2))
    return (_trunk(x_prompt, mem_prompt, p), _trunk(x_sample, mem_sample, p))
```

```python
import functools
import math

import numpy as np
import jax
import jax.numpy as jnp
from jax import lax
from jax.experimental import pallas as pl
from jax.experimental.pallas import tpu as pltpu

F32 = jnp.float32
BF16 = jnp.bfloat16
HIGHEST = lax.Precision.HIGHEST

D_MODEL = 1024
DEPTH = 4
EPS = 1e-6
S5_WIDTH = 512
S5_GROUP = 16
S5_GROUPS = 32
S5_STATE = 64
S5_CHUNK = 64
GLA_HEADS = 4
GLA_DV = 128
GLA_DK = 64
GLA_RANK = 16
GLA_TAU = 16.0
GLA_CHUNK = 64
GLA_QK = GLA_HEADS * GLA_DK
GLA_V = GLA_HEADS * GLA_DV
EV_PAD_COLS = 2304
EV_TN = 768
DIFF_HEADS = 8
DIFF_DK = 64
DIFF_DV = 128
X_HEADS = 4
X_DH = 256
D_FF = 4096

ROW_TILE = 512
VMEM_LIMIT = 48 * 1024 * 1024

NT_DIMS = (((1,), (1,)), ((), ()))
TN_DIMS = (((0,), (0,)), ((), ()))


def _params(*sem):
    return pltpu.CompilerParams(dimension_semantics=sem, vmem_limit_bytes=VMEM_LIMIT)


def _rms(x, gain):
    ms = jnp.mean(x * x, axis=-1, keepdims=True)
    return x * lax.rsqrt(ms + EPS) * gain


def _sigmoid(x):
    return 1.0 / (1.0 + jnp.exp(-x))


def _norm_matmul_body(x_ref, g_ref, w_ref, o_ref, h_ref):
    @pl.when(pl.program_id(1) == 0)
    def _():
        h_ref[...] = _rms(x_ref[...], g_ref[...]).astype(BF16)

    o_ref[...] = jnp.dot(h_ref[...], w_ref[...], preferred_element_type=F32).astype(o_ref.dtype)


def norm_matmul(x, gain, w, out_dtype, tn, tm=ROW_TILE):
    t, d = x.shape
    n = w.shape[1]
    return pl.pallas_call(
        _norm_matmul_body,
        out_shape=jax.ShapeDtypeStruct((t, n), out_dtype),
        grid=(t // tm, n // tn),
        in_specs=[pl.BlockSpec((tm, d), lambda i, j: (i, 0)),
                  pl.BlockSpec((1, d), lambda i, j: (0, 0)),
                  pl.BlockSpec((d, tn), lambda i, j: (0, j))],
        out_specs=pl.BlockSpec((tm, tn), lambda i, j: (i, j)),
        scratch_shapes=[pltpu.VMEM((tm, d), BF16)],
        compiler_params=_params("parallel", "arbitrary"),
        name="norm_matmul",
    )(x, gain.reshape(1, d), w)


def _matmul_res_body(a_ref, w_ref, r_ref, o_ref):
    o_ref[...] = r_ref[...] + jnp.dot(a_ref[...].astype(BF16), w_ref[...], preferred_element_type=F32)


def matmul_residual(a, w, res, tn=512, tm=ROW_TILE):
    t, k = a.shape
    n = w.shape[1]
    return pl.pallas_call(
        _matmul_res_body,
        out_shape=jax.ShapeDtypeStruct((t, n), F32),
        grid=(t // tm, n // tn),
        in_specs=[pl.BlockSpec((tm, k), lambda i, j: (i, 0)),
                  pl.BlockSpec((k, tn), lambda i, j: (0, j)),
                  pl.BlockSpec((tm, tn), lambda i, j: (i, j))],
        out_specs=pl.BlockSpec((tm, tn), lambda i, j: (i, j)),
        compiler_params=_params("parallel", "arbitrary"),
        name="matmul_residual",
    )(a, w, res)


def _mlp_body(x_ref, g_ref, w1_ref, w2_ref, o_ref, h_ref, acc_ref):
    f = pl.program_id(1)

    @pl.when(f == 0)
    def _():
        h_ref[...] = _rms(x_ref[...], g_ref[...]).astype(BF16)
        acc_ref[...] = jnp.zeros_like(acc_ref)

    hid = jnp.dot(h_ref[...], w1_ref[...], preferred_element_type=F32)
    hid = jnp.square(jnp.maximum(hid, 0.0)).astype(BF16)
    acc_ref[...] += jnp.dot(hid, w2_ref[...], preferred_element_type=F32)

    @pl.when(f == pl.num_programs(1) - 1)
    def _():
        o_ref[...] = x_ref[...] + acc_ref[...]


def mlp_block(x, gain, w1, w2, tf=512, tm=2 * ROW_TILE):
    t, d = x.shape
    ff = w1.shape[1]
    return pl.pallas_call(
        _mlp_body,
        out_shape=jax.ShapeDtypeStruct((t, d), F32),
        grid=(t // tm, ff // tf),
        in_specs=[pl.BlockSpec((tm, d), lambda i, f: (i, 0)),
                  pl.BlockSpec((1, d), lambda i, f: (0, 0)),
                  pl.BlockSpec((d, tf), lambda i, f: (0, f)),
                  pl.BlockSpec((tf, d), lambda i, f: (f, 0))],
        out_specs=pl.BlockSpec((tm, d), lambda i, f: (i, 0)),
        scratch_shapes=[pltpu.VMEM((tm, d), BF16), pltpu.VMEM((tm, d), F32)],
        compiler_params=_params("parallel", "arbitrary"),
        name="mlp_block",
    )(x, gain.reshape(1, d), w1, w2)


def _mem_kv_body(m_ref, g_ref, w_ref, kg_ref, k_ref, v_ref):
    h = _rms(m_ref[0], g_ref[0]).astype(BF16)
    kv = jnp.dot(h, w_ref[0], preferred_element_type=F32)
    for hd in range(X_HEADS):
        sl = slice(hd * X_DH, (hd + 1) * X_DH)
        k_ref[0, 0, :, sl] = _rms(kv[:, sl], kg_ref[0]).astype(BF16)
    v_ref[0, 0] = kv[:, D_MODEL:].astype(BF16)


def mem_kv(mem, norm_mem, w_kv, k_norm):
    bm, nm, d = mem.shape
    out = jax.ShapeDtypeStruct((DEPTH, bm, nm, d), BF16)
    return pl.pallas_call(
        _mem_kv_body,
        out_shape=(out, out),
        grid=(DEPTH, bm),
        in_specs=[pl.BlockSpec((1, nm, d), lambda l, b: (b, 0, 0)),
                  pl.BlockSpec((1, 1, d), lambda l, b: (l, 0, 0)),
                  pl.BlockSpec((1, d, 2 * d), lambda l, b: (l, 0, 0)),
                  pl.BlockSpec((1, 1, X_DH), lambda l, b: (l, 0, 0))],
        out_specs=(pl.BlockSpec((1, 1, nm, d), lambda l, b: (l, b, 0, 0)),
                   pl.BlockSpec((1, 1, nm, d), lambda l, b: (l, b, 0, 0))),
        compiler_params=_params("arbitrary", "arbitrary"),
        name="mem_kv",
    )(mem, norm_mem.reshape(DEPTH, 1, d), w_kv, k_norm.reshape(DEPTH, 1, X_DH))


def _cross_body(x_ref, g_ref, wq_ref, qg_ref, k_ref, v_ref, wo_ref, o_ref):
    x = x_ref[...]
    h = _rms(x, g_ref[...]).astype(BF16)
    q = jnp.dot(h, wq_ref[...], preferred_element_type=F32)
    heads = []
    for hd in range(X_HEADS):
        sl = slice(hd * X_DH, (hd + 1) * X_DH)
        qn = _rms(q[:, sl], qg_ref[...]).astype(BF16)
        s = lax.dot_general(qn, k_ref[0, :, sl], NT_DIMS, preferred_element_type=F32)
        p = jnp.exp(s - jnp.max(s, axis=-1, keepdims=True))
        l = jnp.sum(p, axis=-1, keepdims=True)
        oh = jnp.dot(p.astype(BF16), v_ref[0, :, sl], preferred_element_type=F32) / l
        heads.append(oh.astype(BF16))
    o = jnp.concatenate(heads, axis=-1)
    o_ref[...] = x + jnp.dot(o, wo_ref[...], preferred_element_type=F32)


def cross_block(x, seq_len, gain, w_q, q_gain, kn, v, w_o, tm=ROW_TILE):
    t, d = x.shape
    nm = kn.shape[1]
    per_seq = seq_len // tm
    return pl.pallas_call(
        _cross_body,
        out_shape=jax.ShapeDtypeStruct((t, d), F32),
        grid=(t // tm,),
        in_specs=[pl.BlockSpec((tm, d), lambda i: (i, 0)),
                  pl.BlockSpec((1, d), lambda i: (0, 0)),
                  pl.BlockSpec((d, d), lambda i: (0, 0)),
                  pl.BlockSpec((1, X_DH), lambda i: (0, 0)),
                  pl.BlockSpec((1, nm, d), lambda i: (i // per_seq, 0, 0)),
                  pl.BlockSpec((1, nm, d), lambda i: (i // per_seq, 0, 0)),
                  pl.BlockSpec((d, d), lambda i: (0, 0))],
        out_specs=pl.BlockSpec((tm, d), lambda i: (i, 0)),
        compiler_params=_params("parallel"),
        name="cross_block",
    )(x, gain.reshape(1, d), w_q, (q_gain * X_DH ** -0.5).reshape(1, X_DH), kn, v, w_o)


def s5_operators(lam_re, lam_im, log_step, b_re, b_im, c_re, c_im, d):
    lc = S5_CHUNK
    lam = lax.complex(lam_re.astype(F32), lam_im.astype(F32))
    step = jnp.exp(log_step.astype(F32))[..., None]
    lam_bar = jnp.exp(lam * step)
    b_bar = ((lam_bar - 1.0) / lam)[..., None] * lax.complex(b_re.astype(F32), b_im.astype(F32))
    c = lax.complex(c_re.astype(F32), c_im.astype(F32))
    pw = jnp.cumprod(jnp.broadcast_to(lam_bar[..., None], lam_bar.shape + (lc,)), axis=-1)
    pw = jnp.concatenate([jnp.ones_like(pw[..., :1]), pw], axis=-1)
    kern = jnp.einsum('zgcp,zgpt,zgpd->zgtcd', c, pw[..., :lc], b_bar, precision=HIGHEST).real
    kf, kb = kern[0], kern[1]
    k0 = kf[:, :1] + kb[:, :1] + (d.astype(F32)[:, :, None] * jnp.eye(S5_GROUP, dtype=F32))[:, None]
    kern_full = jnp.concatenate([kb[:, :0:-1], k0, kf[:, 1:]], axis=1)
    lag = np.arange(lc)[None, :] - np.arange(lc)[:, None] + lc - 1
    toep = kern_full[:, lag]
    toep = toep.transpose(0, 1, 4, 2, 3).reshape(S5_GROUPS, lc * S5_GROUP, lc * S5_GROUP)

    pf = jnp.einsum('gps,gpd->gsdp', pw[0][..., lc - 1::-1], b_bar[0])
    pb = jnp.einsum('gps,gpd->gsdp', pw[1][..., :lc], b_bar[1])
    p_op = jnp.concatenate([pf.real, pf.imag, pb.real, pb.imag], axis=-1)
    p_op = p_op.reshape(S5_GROUPS, lc * S5_GROUP, 4 * S5_STATE)

    qf = jnp.einsum('gcp,gpt->gptc', c[0], pw[0][..., 1:])
    qb = jnp.einsum('gcp,gpt->gptc', c[1], pw[1][..., :0:-1])
    q_op = jnp.concatenate([qf.real, -qf.imag, qb.real, -qb.imag], axis=1)
    q_op = q_op.reshape(S5_GROUPS, 4 * S5_STATE, lc * S5_GROUP)

    a = pw[..., lc]
    coef = jnp.stack([jnp.concatenate([a[0].real, a[0].real], -1),
                      jnp.concatenate([-a[0].imag, a[0].imag], -1),
                      jnp.concatenate([a[1].real, a[1].real], -1),
                      jnp.concatenate([-a[1].imag, a[1].imag], -1)])
    return toep.astype(BF16), p_op, q_op, coef


def _s5_state_in_body(u_ref, p_ref, v_ref):
    v_ref[0] = jnp.dot(u_ref[0], p_ref[0], preferred_element_type=F32, precision=HIGHEST)


def _s5_scan_body(vf_ref, vb_ref, cf_ref, xf_ref, xb_ref):
    n = vf_ref.shape[0]
    c1f, c2f, c1b, c2b = cf_ref[0], cf_ref[1], cf_ref[2], cf_ref[3]

    def step(i, carry):
        xf, xb = carry
        xf_ref[i] = xf
        xf = xf * c1f + pltpu.roll(xf, S5_STATE, 1) * c2f + vf_ref[i]
        j = n - 1 - i
        xb_ref[j] = xb
        xb = xb * c1b + pltpu.roll(xb, S5_STATE, 1) * c2b + vb_ref[j]
        return xf, xb

    zero = jnp.zeros(vf_ref.shape[1:], F32)
    lax.fori_loop(0, n, step, (zero, zero))


def _s5_out_body(u_ref, t_ref, x_ref, q_ref, y_ref):
    y = jnp.dot(u_ref[0].astype(BF16), t_ref[0], preferred_element_type=F32)
    y_ref[0] = y + jnp.dot(x_ref[0], q_ref[0], preferred_element_type=F32, precision=HIGHEST)


def s5_scan(u, bsz, seq_len, ops):
    toep, p_op, q_op, coef = ops
    lc, g, w = S5_CHUNK, S5_GROUPS, S5_CHUNK * S5_GROUP
    n = seq_len // lc
    c = bsz * n
    ns = 4 * S5_STATE
    ug = u.reshape(bsz, n, lc, g, S5_GROUP).transpose(3, 0, 1, 2, 4).reshape(g, c, w)
    v = pl.pallas_call(
        _s5_state_in_body,
        out_shape=jax.ShapeDtypeStruct((g, c, ns), F32),
        grid=(g,),
        in_specs=[pl.BlockSpec((1, c, w), lambda i: (i, 0, 0)),
                  pl.BlockSpec((1, w, ns), lambda i: (i, 0, 0))],
        out_specs=pl.BlockSpec((1, c, ns), lambda i: (i, 0, 0)),
        compiler_params=_params("parallel"),
        name="s5_state_in",
    )(ug, p_op)
    v = v.reshape(g, bsz, n, ns).transpose(2, 1, 0, 3).reshape(n, bsz * g, ns)
    half = ns // 2
    cf = jnp.tile(coef, (1, bsz, 1))
    xf, xb = pl.pallas_call(
        _s5_scan_body,
        out_shape=(jax.ShapeDtypeStruct((n, bsz * g, half), F32),) * 2,
        compiler_params=pltpu.CompilerParams(vmem_limit_bytes=VMEM_LIMIT),
        name="s5_chunk_scan",
    )(v[..., :half], v[..., half:], cf)
    x = jnp.concatenate([xf, xb], axis=-1).reshape(n, bsz, g, ns).transpose(2, 1, 0, 3).reshape(g, c, ns)
    y = pl.pallas_call(
        _s5_out_body,
        out_shape=jax.ShapeDtypeStruct((g, c, w), F32),
        grid=(g,),
        in_specs=[pl.BlockSpec((1, c, w), lambda i: (i, 0, 0)),
                  pl.BlockSpec((1, w, w), lambda i: (i, 0, 0)),
                  pl.BlockSpec((1, c, ns), lambda i: (i, 0, 0)),
                  pl.BlockSpec((1, ns, w), lambda i: (i, 0, 0))],
        out_specs=pl.BlockSpec((1, c, w), lambda i: (i, 0, 0)),
        compiler_params=_params("parallel"),
        name="s5_out",
    )(ug, toep, x, q_op)
    return y.reshape(g, bsz, n, lc, S5_GROUP).transpose(1, 2, 3, 0, 4).reshape(bsz * seq_len, S5_WIDTH)


def _gla_chunk(q, k, v, glr, wg, bg, s_ref, forward):
    cs = GLA_CHUNK
    logit = jnp.dot(glr, wg, preferred_element_type=F32, precision=HIGHEST) + bg
    g = (jnp.minimum(logit, 0.0) - jnp.log1p(jnp.exp(-jnp.abs(logit)))) / GLA_TAU
    row = lax.broadcasted_iota(jnp.int32, (cs, cs), 0)
    col = lax.broadcasted_iota(jnp.int32, (cs, cs), 1)
    keep = (col <= row) if forward else (col >= row)
    bcum = jnp.dot(keep.astype(F32), g, preferred_element_type=F32, precision=HIGHEST)
    i_ref = cs // 2 - 1 if forward else cs // 2
    i_last = cs - 1 if forward else 0
    bref = bcum[i_ref:i_ref + 1]
    blast = bcum[i_last:i_last + 1]
    q_rel = q * jnp.exp(bcum - bref)
    k_rel = (k * jnp.exp(bref - bcum)).astype(BF16)
    k_out = (k * jnp.exp(blast - bcum)).astype(BF16)
    q_dec = q * jnp.exp(bcum)
    decay = jnp.exp(blast)
    lane = lax.broadcasted_iota(jnp.int32, (1, GLA_QK), 1)
    outs = []
    for h in range(GLA_HEADS):
        hm = ((lane >= h * GLA_DK) & (lane < (h + 1) * GLA_DK)).astype(F32)
        vh = v[:, h * GLA_DV:(h + 1) * GLA_DV].astype(BF16)
        s = lax.dot_general((q_rel * hm).astype(BF16), k_rel, NT_DIMS, preferred_element_type=F32)
        s = jnp.where(keep, s, 0.0)
        o = jnp.dot(s.astype(BF16), vh, preferred_element_type=F32)
        st = s_ref[h]
        o = o + lax.dot_general((q_dec * hm).astype(BF16), st.astype(BF16), NT_DIMS,
                                preferred_element_type=F32)
        kv = lax.dot_general(vh, k_out, TN_DIMS, preferred_element_type=F32)
        s_ref[h] = decay * st + kv * hm
        outs.append(o)
    return jnp.concatenate(outs, axis=-1)


def _gla_body(qf_ref, kf_ref, vf_ref, gf_ref, qb_ref, kb_ref, vb_ref, gb_ref, wg_ref, bg_ref,
              of_ref, ob_ref, sf_ref, sb_ref, *, chunks):
    @pl.when(pl.program_id(1) == 0)
    def _():
        sf_ref[...] = jnp.zeros_like(sf_ref)
        sb_ref[...] = jnp.zeros_like(sb_ref)

    cs = GLA_CHUNK
    scale = GLA_DK ** -0.5
    for ci in range(chunks):
        rows = pl.ds(ci * cs, cs)
        of_ref[rows, :] = _gla_chunk(qf_ref[rows, :] * scale, kf_ref[rows, :], vf_ref[rows, :],
                                     gf_ref[rows, :], wg_ref[0], bg_ref[0], sf_ref, True)
    for ci in reversed(range(chunks)):
        rows = pl.ds(ci * cs, cs)
        ob_ref[rows, :] = _gla_chunk(qb_ref[rows, :] * scale, kb_ref[rows, :], vb_ref[rows, :],
                                     gb_ref[rows, :], wg_ref[1], bg_ref[1], sb_ref, False)


def gla_scan(proj, bsz, seq_len, w_gate, b_gate, chunks=4):
    t = proj.shape[0]
    tb = chunks * GLA_CHUNK
    nb = seq_len // tb
    fwd = lambda cb: (lambda b, i: (b * nb + i, cb))
    bwd = lambda cb: (lambda b, i: (b * nb + nb - 1 - i, cb))
    qc, kc, vc, gc = 512 // GLA_QK, 768 // GLA_QK, 1024 // GLA_V, 2048 // 128
    wg = jnp.zeros((2, 128, GLA_QK), F32)
    wg = wg.at[0, :GLA_RANK].set(w_gate[0].astype(F32)).at[1, GLA_RANK:2 * GLA_RANK].set(w_gate[1].astype(F32))
    out = jax.ShapeDtypeStruct((t, GLA_V), F32)
    state = pltpu.VMEM((GLA_HEADS, GLA_DV, GLA_QK), F32)
    return pl.pallas_call(
        functools.partial(_gla_body, chunks=chunks),
        out_shape=(out, out),
        grid=(bsz, nb),
        in_specs=[pl.BlockSpec((tb, GLA_QK), fwd(qc)), pl.BlockSpec((tb, GLA_QK), fwd(kc)),
                  pl.BlockSpec((tb, GLA_V), fwd(vc)), pl.BlockSpec((tb, 128), fwd(gc)),
                  pl.BlockSpec((tb, GLA_QK), bwd(qc)), pl.BlockSpec((tb, GLA_QK), bwd(kc)),
                  pl.BlockSpec((tb, GLA_V), bwd(vc)), pl.BlockSpec((tb, 128), bwd(gc)),
                  pl.BlockSpec((2, 128, GLA_QK), lambda b, i: (0, 0, 0)),
                  pl.BlockSpec((2, 1, GLA_QK), lambda b, i: (0, 0, 0))],
        out_specs=(pl.BlockSpec((tb, GLA_V), fwd(0)), pl.BlockSpec((tb, GLA_V), bwd(0))),
        scratch_shapes=[state, state],
        compiler_params=_params("parallel", "arbitrary"),
        name="gla_scan",
    )(proj, proj, proj, proj, proj, proj, proj, proj, wg, b_gate.astype(F32).reshape(2, 1, GLA_QK))


def _even_out_body(x_ref, ys_ref, of_ref, ob_ref, og_ref, wglu_ref, gn_ref, wtop_ref, wbot_ref, o_ref):
    y = ys_ref[...]
    y = 0.5 * y * (1.0 + jnp.tanh(math.sqrt(2.0 / math.pi) * (y + 0.044715 * (y * y * y))))
    gate = jnp.dot(y.astype(BF16), wglu_ref[...], preferred_element_type=F32)
    y = y * _sigmoid(gate)
    o = of_ref[...] + ob_ref[...]
    og = og_ref[...]
    heads = []
    for h in range(GLA_HEADS):
        sl = slice(h * GLA_DV, (h + 1) * GLA_DV)
        heads.append(_rms(o[:, sl], gn_ref[...]))
    o = jnp.concatenate(heads, axis=-1) * (og * _sigmoid(og))
    o_ref[...] = (x_ref[...]
                  + jnp.dot(y.astype(BF16), wtop_ref[...], preferred_element_type=F32)
                  + jnp.dot(o.astype(BF16), wbot_ref[...], preferred_element_type=F32))


def even_out(x, ys, o_f, o_b, proj, w_glu, gla_norm, w_out, tm=ROW_TILE):
    t, d = x.shape
    row = lambda i: (i, 0)
    const = lambda i: (0, 0)
    return pl.pallas_call(
        _even_out_body,
        out_shape=jax.ShapeDtypeStruct((t, d), F32),
        grid=(t // tm,),
        in_specs=[pl.BlockSpec((tm, d), row), pl.BlockSpec((tm, S5_WIDTH), row),
                  pl.BlockSpec((tm, GLA_V), row), pl.BlockSpec((tm, GLA_V), row),
                  pl.BlockSpec((tm, GLA_V), lambda i: (i, 1536 // GLA_V)),
                  pl.BlockSpec((S5_WIDTH, S5_WIDTH), const), pl.BlockSpec((1, GLA_DV), const),
                  pl.BlockSpec((S5_WIDTH, d), const), pl.BlockSpec((GLA_V, d), const)],
        out_specs=pl.BlockSpec((tm, d), row),
        compiler_params=_params("parallel"),
        name="even_out",
    )(x, ys, o_f, o_b, proj, w_glu, gla_norm.astype(F32).reshape(1, GLA_DV),
      w_out[:S5_WIDTH], w_out[S5_WIDTH:])


def _seg_rms(a, gain):
    a2 = a * a
    lane = lax.broadcasted_iota(jnp.int32, a.shape, 1)
    lo = lane < DIFF_DK
    s_lo = jnp.sum(jnp.where(lo, a2, 0.0), axis=-1, keepdims=True)
    s_hi = jnp.sum(jnp.where(lo, 0.0, a2), axis=-1, keepdims=True)
    ms = jnp.where(lo, s_lo, s_hi) * (1.0 / DIFF_DK)
    return a * lax.rsqrt(ms + EPS) * gain


def _qkv_body(x_ref, g_ref, w_ref, qk_gain_ref, o_ref, h_ref):
    j = pl.program_id(1)

    @pl.when(j == 0)
    def _():
        h_ref[...] = _rms(x_ref[...], g_ref[...]).astype(BF16)

    acc = jnp.dot(h_ref[...], w_ref[...], preferred_element_type=F32)

    @pl.when(j < 2)
    def _():
        gain = qk_gain_ref[0]
        for c in range(DIFF_HEADS):
            sl = slice(c * 2 * DIFF_DK, (c + 1) * 2 * DIFF_DK)
            o_ref[:, sl] = _seg_rms(acc[:, sl], gain).astype(o_ref.dtype)

    @pl.when(j == 2)
    def _():
        o_ref[...] = acc.astype(o_ref.dtype)


def qkv_project(x, gain, w, q_norm, k_norm, tm=ROW_TILE):
    t, d = x.shape
    qg = jnp.tile(q_norm.astype(F32) * DIFF_DK ** -0.5, 2)
    kg = jnp.tile(k_norm.astype(F32), 2)
    qk_gain = jnp.stack([qg, kg]).reshape(2, 1, 2 * DIFF_DK)
    return pl.pallas_call(
        _qkv_body,
        out_shape=jax.ShapeDtypeStruct((t, 3 * d), BF16),
        grid=(t // tm, 3),
        in_specs=[pl.BlockSpec((tm, d), lambda i, j: (i, 0)),
                  pl.BlockSpec((1, d), lambda i, j: (0, 0)),
                  pl.BlockSpec((d, d), lambda i, j: (0, j)),
                  pl.BlockSpec((1, 1, 2 * DIFF_DK), lambda i, j: (jnp.minimum(j, 1), 0, 0))],
        out_specs=pl.BlockSpec((tm, d), lambda i, j: (i, j)),
        scratch_shapes=[pltpu.VMEM((tm, d), BF16)],
        compiler_params=_params("parallel", "arbitrary"),
        name="qkv_project",
    )(x, gain.reshape(1, d), w, qk_gain)


POS_SPLIT = 16
POS_SHIFT = 4
AUG_LANE = DIFF_DK
SOFTMAX_GROUP_ELEMS = 32 * 1024


def _pos_terms(shape):
    lane = lax.broadcasted_iota(jnp.int32, shape, 1)
    pos = lax.broadcasted_iota(jnp.int32, shape, 0)
    hi = lax.shift_right_logical(pos, POS_SHIFT).astype(F32)
    lo = (pos & (POS_SPLIT - 1)).astype(F32)
    return lane, hi, lo


def _lane_select(lane, first, values):
    out = 0.0
    for n, val in reversed(list(enumerate(values))):
        out = jnp.where(lane == first + n, val, out)
    return out


def _diff_attn_body(slope_ref, q_ref, k_ref, v_ref, lq1_ref, lk1_ref, lq2_ref, lk2_ref, sub_ref,
                    o_ref, ka_ref, va_ref, qs_ref, s0_ref, s1_ref, p0_ref, p1_ref, acc0_ref, acc1_ref,
                    *, blk, seq_len, lambda_init):
    h = pl.program_id(1)
    qi = pl.program_id(2)
    slope = slope_ref[h]
    nk = seq_len // blk
    s_refs, p_refs, acc_refs = (s0_ref, s1_ref), (p0_ref, p1_ref), (acc0_ref, acc1_ref)

    @pl.when(qi == 0)
    def _():
        va_ref[:, :DIFF_DV] = v_ref[...]
        va_ref[:, DIFF_DV:] = jnp.ones((seq_len, DIFF_DV), BF16)

        def build(t, carry):
            rows = pl.ds(pl.multiple_of(t * blk, blk), blk)
            kf = k_ref[rows, :].astype(F32)
            lane, hi, lo = _pos_terms(kf.shape)
            aug = _lane_select(lane, AUG_LANE,
                               [-blk * slope, -POS_SPLIT * slope, -slope,
                                (blk * slope) * t.astype(F32), (POS_SPLIT * slope) * hi, slope * lo])
            for z, kz in enumerate((kf, pltpu.roll(kf, DIFF_DK, 1))):
                ka_ref[z, rows, :] = jnp.where(lane < DIFF_DK, kz, aug).astype(BF16)
            return carry

        lax.fori_loop(0, nk, build, 0)

    qf = q_ref[...].astype(F32)
    lane, hi, lo = _pos_terms(qf.shape)
    qa = _lane_select(lane, AUG_LANE, [qi.astype(F32), hi, lo, 1.0, 1.0, 1.0])
    for z, qz in enumerate((qf, pltpu.roll(qf, DIFF_DK, 1))):
        qs_ref[z] = jnp.where(lane < DIFF_DK, qz, 0.0).astype(BF16)
        qs_ref[2 + z] = jnp.where(lane < DIFF_DK, qz, qa).astype(BF16)
        qs_ref[4 + z] = jnp.where(lane < DIFF_DK, qz, -qa).astype(BF16)

    def key_block(t):
        if t == 0:
            return qi, 0
        j = (t - 1) + (qi <= t - 1).astype(jnp.int32)
        return j, jnp.where(j < qi, 2, 4)

    def key_rows(t):
        j, _ = key_block(t)
        return pl.ds(pl.multiple_of(j * blk, blk), blk)

    for z, s_ref in enumerate(s_refs):
        for t in range(nk):
            _, variant = key_block(t)
            s_ref[:, t * blk:(t + 1) * blk] = lax.dot_general(
                qs_ref[variant + z], ka_ref[z, key_rows(t), :], NT_DIMS, preferred_element_type=F32)

    rg = SOFTMAX_GROUP_ELEMS // seq_len
    rel = (lax.broadcasted_iota(jnp.int32, (rg, blk), 0) - lax.broadcasted_iota(jnp.int32, (rg, blk), 1))
    for s_ref, p_ref, acc_ref in zip(s_refs, p_refs, acc_refs):
        for g in range(blk // rg):
            rows = slice(g * rg, (g + 1) * rg)
            s_diag = s_ref[rows, :blk] - slope * jnp.abs(rel + g * rg).astype(F32)
            s_rest = s_ref[rows, blk:]
            m = jnp.maximum(jnp.max(s_diag, axis=-1, keepdims=True), jnp.max(s_rest, axis=-1, keepdims=True))
            p_ref[rows, :blk] = jnp.exp(s_diag - m).astype(BF16)
            p_ref[rows, blk:] = jnp.exp(s_rest - m).astype(BF16)
        for t in range(nk):
            part = jnp.dot(p_ref[:, t * blk:(t + 1) * blk], va_ref[key_rows(t), :],
                           preferred_element_type=F32)
            if t == 0:
                acc_ref[...] = part
            else:
                acc_ref[...] += part

    lam = (jnp.exp(jnp.sum(lq1_ref[...] * lk1_ref[...], axis=-1, keepdims=True))
           - jnp.exp(jnp.sum(lq2_ref[...] * lk2_ref[...], axis=-1, keepdims=True)) + lambda_init)
    a0 = acc_refs[0][...]
    a1 = acc_refs[1][...]
    o = (a0[:, :DIFF_DV] / a0[:, DIFF_DV:DIFF_DV + 1]
         - lam * (a1[:, :DIFF_DV] / a1[:, DIFF_DV:DIFF_DV + 1]))
    o_ref[...] = (_rms(o, sub_ref[...]) * (1.0 - lambda_init)).astype(o_ref.dtype)


def diff_attention(qkv, bsz, seq_len, lq1, lk1, lq2, lk2, sub_norm, lambda_init, blk=512):
    t = qkv.shape[0]
    nq = seq_len // blk
    assert blk // POS_SPLIT <= 256, "hi part of a block position must stay exact in bf16"
    slopes = jnp.asarray(2.0 ** (-8.0 * np.arange(1, DIFF_HEADS + 1, dtype=np.float32) / DIFF_HEADS), F32)
    vec = lambda a: a.astype(F32).reshape(1, DIFF_DK)
    const = lambda b, h, i: (0, 0)
    return pl.pallas_call(
        functools.partial(_diff_attn_body, blk=blk, seq_len=seq_len, lambda_init=lambda_init),
        out_shape=jax.ShapeDtypeStruct((t, DIFF_HEADS * DIFF_DV), BF16),
        grid=(bsz, DIFF_HEADS, nq),
        in_specs=[pl.BlockSpec(memory_space=pltpu.SMEM),
                  pl.BlockSpec((blk, 2 * DIFF_DK), lambda b, h, i: (b * nq + i, h)),
                  pl.BlockSpec((seq_len, 2 * DIFF_DK), lambda b, h, i: (b, DIFF_HEADS + h)),
                  pl.BlockSpec((seq_len, DIFF_DV), lambda b, h, i: (b, 2 * DIFF_HEADS + h)),
                  pl.BlockSpec((1, DIFF_DK), const), pl.BlockSpec((1, DIFF_DK), const),
                  pl.BlockSpec((1, DIFF_DK), const), pl.BlockSpec((1, DIFF_DK), const),
                  pl.BlockSpec((1, DIFF_DV), const)],
        out_specs=pl.BlockSpec((blk, DIFF_DV), lambda b, h, i: (b * nq + i, h)),
        scratch_shapes=[pltpu.VMEM((2, seq_len, 2 * DIFF_DK), BF16),
                        pltpu.VMEM((seq_len, 2 * DIFF_DV), BF16),
                        pltpu.VMEM((6, blk, 2 * DIFF_DK), BF16),
                        pltpu.VMEM((blk, seq_len), F32),
                        pltpu.VMEM((blk, seq_len), F32),
                        pltpu.VMEM((blk, seq_len), BF16),
                        pltpu.VMEM((blk, seq_len), BF16),
                        pltpu.VMEM((blk, 2 * DIFF_DV), F32),
                        pltpu.VMEM((blk, 2 * DIFF_DV), F32)],
        compiler_params=_params("parallel", "parallel", "arbitrary"),
        name="diff_attention",
    )(slopes, qkv, qkv, qkv, vec(lq1), vec(lk1), vec(lq2), vec(lk2),
      sub_norm.astype(F32).reshape(1, DIFF_DV))


def _pad_ev_w_in(w):
    return jnp.pad(w, ((0, 0), (0, EV_PAD_COLS - w.shape[1]))).astype(BF16)


def _trunk(x3, mem, p):
    bsz, seq_len, d = x3.shape
    x = x3.reshape(bsz * seq_len, d)
    kn_all, v_all = mem_kv(mem, p['norm_mem'], p['x_w_kv'], p['x_k_norm'])
    for layer in range(DEPTH):
        if layer % 2 == 0:
            e = layer // 2
            proj = norm_matmul(x, p['norm_mix'][layer], p['ev_w_in'][e], F32, EV_TN)
            ys = s5_scan(proj[:, :S5_WIDTH], bsz, seq_len, p['s5_ops'][e])
            o_f, o_b = gla_scan(proj, bsz, seq_len, p['gla_w_gate'][e], p['gla_b_gate'][e])
            x = even_out(x, ys, o_f, o_b, proj, p['s5_w_glu'][e], p['gla_norm'][e], p['ev_w_out'][e])
        else:
            o = layer // 2
            lambda_init = 0.8 - 0.6 * math.exp(-0.3 * layer)
            qkv = qkv_project(x, p['norm_mix'][layer], p['od_w_in'][o],
                              p['diff_q_norm'][o], p['diff_k_norm'][o])
            att = diff_attention(qkv, bsz, seq_len, p['diff_lambda_q1'][o], p['diff_lambda_k1'][o],
                                 p['diff_lambda_q2'][o], p['diff_lambda_k2'][o], p['diff_norm'][o],
                                 lambda_init)
            x = matmul_residual(att, p['od_w_out'][o], x)
        x = cross_block(x, seq_len, p['norm_cross'][layer], p['x_w_q'][layer], p['x_q_norm'][layer],
                        kn_all[layer], v_all[layer], p['x_w_o'][layer])
        x = mlp_block(x, p['norm_mlp'][layer], p['mlp_w1'][layer], p['mlp_w2'][layer])
    return x.reshape(bsz, seq_len, d)


def kernel(x_prompt, x_sample, mem_prompt, mem_sample, norm_mix, norm_cross, norm_mem, norm_mlp,
           ev_w_in, ev_w_out, s5_lambda_re, s5_lambda_im, s5_log_step, s5_b_re, s5_b_im,
           s5_c_re, s5_c_im, s5_d, s5_w_glu, gla_w_gate, gla_b_gate, gla_norm,
           od_w_in, od_w_out, diff_q_norm, diff_k_norm, diff_lambda_q1, diff_lambda_k1,
           diff_lambda_q2, diff_lambda_k2, diff_norm, x_w_q, x_w_kv, x_w_o, x_q_norm, x_k_norm,
           mlp_w1, mlp_w2):
    n_even = ev_w_in.shape[0]
    bf = lambda w: w.astype(BF16)
    p = dict(
        norm_mix=norm_mix, norm_cross=norm_cross, norm_mem=norm_mem, norm_mlp=norm_mlp,
        ev_w_in=[_pad_ev_w_in(ev_w_in[e]) for e in range(n_even)], ev_w_out=bf(ev_w_out),
        s5_ops=[s5_operators(s5_lambda_re[e], s5_lambda_im[e], s5_log_step[e], s5_b_re[e], s5_b_im[e],
                             s5_c_re[e], s5_c_im[e], s5_d[e]) for e in range(n_even)],
        s5_w_glu=bf(s5_w_glu), gla_w_gate=gla_w_gate, gla_b_gate=gla_b_gate, gla_norm=gla_norm,
        od_w_in=bf(od_w_in), od_w_out=bf(od_w_out), diff_q_norm=diff_q_norm, diff_k_norm=diff_k_norm,
        diff_lambda_q1=diff_lambda_q1, diff_lambda_k1=diff_lambda_k1,
        diff_lambda_q2=diff_lambda_q2, diff_lambda_k2=diff_lambda_k2, diff_norm=diff_norm,
        x_w_q=bf(x_w_q), x_w_kv=bf(x_w_kv), x_w_o=bf(x_w_o), x_q_norm=x_q_norm, x_k_norm=x_k_norm,
        mlp_w1=bf(mlp_w1), mlp_w2=bf(mlp_w2))
    return (_trunk(x_prompt, mem_prompt, p), _trunk(x_sample, mem_sample, p))
```

```python
import functools
import math

import numpy as np
import jax
import jax.numpy as jnp
from jax import lax
from jax.experimental import pallas as pl
from jax.experimental.pallas import tpu as pltpu

F32 = jnp.float32
BF16 = jnp.bfloat16
HIGHEST = lax.Precision.HIGHEST

D_MODEL = 1024
DEPTH = 4
EPS = 1e-6
S5_WIDTH = 512
S5_GROUP = 16
S5_GROUPS = 32
S5_STATE = 64
S5_CHUNK = 64
GLA_HEADS = 4
GLA_DV = 128
GLA_DK = 64
GLA_RANK = 16
GLA_TAU = 16.0
GLA_CHUNK = 64
GLA_QK = GLA_HEADS * GLA_DK
GLA_V = GLA_HEADS * GLA_DV
EV_PAD_COLS = 2304
EV_TN = 768
DIFF_HEADS = 8
DIFF_DK = 64
DIFF_DV = 128
X_HEADS = 4
X_DH = 256
D_FF = 4096

ROW_TILE = 512
VMEM_LIMIT = 48 * 1024 * 1024

NT_DIMS = (((1,), (1,)), ((), ()))
TN_DIMS = (((0,), (0,)), ((), ()))


def _params(*sem):
    return pltpu.CompilerParams(dimension_semantics=sem, vmem_limit_bytes=VMEM_LIMIT)


def _rms(x, gain):
    ms = jnp.mean(x * x, axis=-1, keepdims=True)
    return x * lax.rsqrt(ms + EPS) * gain


def _sigmoid(x):
    return 1.0 / (1.0 + jnp.exp(-x))


def _norm_matmul_body(x_ref, g_ref, w_ref, o_ref, h_ref):
    @pl.when(pl.program_id(1) == 0)
    def _():
        h_ref[...] = _rms(x_ref[...], g_ref[...]).astype(BF16)

    o_ref[...] = jnp.dot(h_ref[...], w_ref[...], preferred_element_type=F32).astype(o_ref.dtype)


def norm_matmul(x, gain, w, out_dtype, tn, tm=ROW_TILE):
    t, d = x.shape
    n = w.shape[1]
    return pl.pallas_call(
        _norm_matmul_body,
        out_shape=jax.ShapeDtypeStruct((t, n), out_dtype),
        grid=(t // tm, n // tn),
        in_specs=[pl.BlockSpec((tm, d), lambda i, j: (i, 0)),
                  pl.BlockSpec((1, d), lambda i, j: (0, 0)),
                  pl.BlockSpec((d, tn), lambda i, j: (0, j))],
        out_specs=pl.BlockSpec((tm, tn), lambda i, j: (i, j)),
        scratch_shapes=[pltpu.VMEM((tm, d), BF16)],
        compiler_params=_params("parallel", "arbitrary"),
        name="norm_matmul",
    )(x, gain.reshape(1, d), w)


def _matmul_res_body(a_ref, w_ref, r_ref, o_ref):
    o_ref[...] = r_ref[...] + jnp.dot(a_ref[...].astype(BF16), w_ref[...], preferred_element_type=F32)


def matmul_residual(a, w, res, tn=512, tm=ROW_TILE):
    t, k = a.shape
    n = w.shape[1]
    return pl.pallas_call(
        _matmul_res_body,
        out_shape=jax.ShapeDtypeStruct((t, n), F32),
        grid=(t // tm, n // tn),
        in_specs=[pl.BlockSpec((tm, k), lambda i, j: (i, 0)),
                  pl.BlockSpec((k, tn), lambda i, j: (0, j)),
                  pl.BlockSpec((tm, tn), lambda i, j: (i, j))],
        out_specs=pl.BlockSpec((tm, tn), lambda i, j: (i, j)),
        compiler_params=_params("parallel", "arbitrary"),
        name="matmul_residual",
    )(a, w, res)


def _mlp_body(x_ref, g_ref, w1_ref, w2_ref, o_ref, h_ref, acc_ref):
    f = pl.program_id(1)

    @pl.when(f == 0)
    def _():
        h_ref[...] = _rms(x_ref[...], g_ref[...]).astype(BF16)
        acc_ref[...] = jnp.zeros_like(acc_ref)

    hid = jnp.dot(h_ref[...], w1_ref[...], preferred_element_type=F32)
    hid = jnp.square(jnp.maximum(hid, 0.0)).astype(BF16)
    acc_ref[...] += jnp.dot(hid, w2_ref[...], preferred_element_type=F32)

    @pl.when(f == pl.num_programs(1) - 1)
    def _():
        o_ref[...] = x_ref[...] + acc_ref[...]


def mlp_block(x, gain, w1, w2, tf=512, tm=2 * ROW_TILE):
    t, d = x.shape
    ff = w1.shape[1]
    return pl.pallas_call(
        _mlp_body,
        out_shape=jax.ShapeDtypeStruct((t, d), F32),
        grid=(t // tm, ff // tf),
        in_specs=[pl.BlockSpec((tm, d), lambda i, f: (i, 0)),
                  pl.BlockSpec((1, d), lambda i, f: (0, 0)),
                  pl.BlockSpec((d, tf), lambda i, f: (0, f)),
                  pl.BlockSpec((tf, d), lambda i, f: (f, 0))],
        out_specs=pl.BlockSpec((tm, d), lambda i, f: (i, 0)),
        scratch_shapes=[pltpu.VMEM((tm, d), BF16), pltpu.VMEM((tm, d), F32)],
        compiler_params=_params("parallel", "arbitrary"),
        name="mlp_block",
    )(x, gain.reshape(1, d), w1, w2)


def _mem_kv_body(m_ref, g_ref, w_ref, kg_ref, k_ref, v_ref):
    h = _rms(m_ref[0], g_ref[0]).astype(BF16)
    kv = jnp.dot(h, w_ref[0], preferred_element_type=F32)
    for hd in range(X_HEADS):
        sl = slice(hd * X_DH, (hd + 1) * X_DH)
        k_ref[0, 0, :, sl] = _rms(kv[:, sl], kg_ref[0]).astype(BF16)
    v_ref[0, 0] = kv[:, D_MODEL:].astype(BF16)


def mem_kv(mem, norm_mem, w_kv, k_norm):
    bm, nm, d = mem.shape
    out = jax.ShapeDtypeStruct((DEPTH, bm, nm, d), BF16)
    return pl.pallas_call(
        _mem_kv_body,
        out_shape=(out, out),
        grid=(DEPTH, bm),
        in_specs=[pl.BlockSpec((1, nm, d), lambda l, b: (b, 0, 0)),
                  pl.BlockSpec((1, 1, d), lambda l, b: (l, 0, 0)),
                  pl.BlockSpec((1, d, 2 * d), lambda l, b: (l, 0, 0)),
                  pl.BlockSpec((1, 1, X_DH), lambda l, b: (l, 0, 0))],
        out_specs=(pl.BlockSpec((1, 1, nm, d), lambda l, b: (l, b, 0, 0)),
                   pl.BlockSpec((1, 1, nm, d), lambda l, b: (l, b, 0, 0))),
        compiler_params=_params("arbitrary", "arbitrary"),
        name="mem_kv",
    )(mem, norm_mem.reshape(DEPTH, 1, d), w_kv, k_norm.reshape(DEPTH, 1, X_DH))


def _cross_body(x_ref, g_ref, wq_ref, qg_ref, k_ref, v_ref, wo_ref, o_ref):
    x = x_ref[...]
    h = _rms(x, g_ref[...]).astype(BF16)
    q = jnp.dot(h, wq_ref[...], preferred_element_type=F32)
    heads = []
    for hd in range(X_HEADS):
        sl = slice(hd * X_DH, (hd + 1) * X_DH)
        qn = _rms(q[:, sl], qg_ref[...]).astype(BF16)
        s = lax.dot_general(qn, k_ref[0, :, sl], NT_DIMS, preferred_element_type=F32)
        p = jnp.exp(s - jnp.max(s, axis=-1, keepdims=True))
        l = jnp.sum(p, axis=-1, keepdims=True)
        oh = jnp.dot(p.astype(BF16), v_ref[0, :, sl], preferred_element_type=F32) / l
        heads.append(oh.astype(BF16))
    o = jnp.concatenate(heads, axis=-1)
    o_ref[...] = x + jnp.dot(o, wo_ref[...], preferred_element_type=F32)


def cross_block(x, seq_len, gain, w_q, q_gain, kn, v, w_o, tm=ROW_TILE):
    t, d = x.shape
    nm = kn.shape[1]
    per_seq = seq_len // tm
    return pl.pallas_call(
        _cross_body,
        out_shape=jax.ShapeDtypeStruct((t, d), F32),
        grid=(t // tm,),
        in_specs=[pl.BlockSpec((tm, d), lambda i: (i, 0)),
                  pl.BlockSpec((1, d), lambda i: (0, 0)),
                  pl.BlockSpec((d, d), lambda i: (0, 0)),
                  pl.BlockSpec((1, X_DH), lambda i: (0, 0)),
                  pl.BlockSpec((1, nm, d), lambda i: (i // per_seq, 0, 0)),
                  pl.BlockSpec((1, nm, d), lambda i: (i // per_seq, 0, 0)),
                  pl.BlockSpec((d, d), lambda i: (0, 0))],
        out_specs=pl.BlockSpec((tm, d), lambda i: (i, 0)),
        compiler_params=_params("parallel"),
        name="cross_block",
    )(x, gain.reshape(1, d), w_q, (q_gain * X_DH ** -0.5).reshape(1, X_DH), kn, v, w_o)


def s5_operators(lam_re, lam_im, log_step, b_re, b_im, c_re, c_im, d):
    lc = S5_CHUNK
    lam = lax.complex(lam_re.astype(F32), lam_im.astype(F32))
    step = jnp.exp(log_step.astype(F32))[..., None]
    lam_bar = jnp.exp(lam * step)
    b_bar = ((lam_bar - 1.0) / lam)[..., None] * lax.complex(b_re.astype(F32), b_im.astype(F32))
    c = lax.complex(c_re.astype(F32), c_im.astype(F32))
    pw = jnp.cumprod(jnp.broadcast_to(lam_bar[..., None], lam_bar.shape + (lc,)), axis=-1)
    pw = jnp.concatenate([jnp.ones_like(pw[..., :1]), pw], axis=-1)
    kern = jnp.einsum('zgcp,zgpt,zgpd->zgtcd', c, pw[..., :lc], b_bar, precision=HIGHEST).real
    kf, kb = kern[0], kern[1]
    k0 = kf[:, :1] + kb[:, :1] + (d.astype(F32)[:, :, None] * jnp.eye(S5_GROUP, dtype=F32))[:, None]
    kern_full = jnp.concatenate([kb[:, :0:-1], k0, kf[:, 1:]], axis=1)
    width = lc * S5_GROUP
    kern_flat = kern_full.transpose(0, 3, 1, 2).reshape(S5_GROUPS, S5_GROUP, (2 * lc - 1) * S5_GROUP)
    kern_flat = kern_flat.astype(BF16)
    toep = jnp.stack([kern_flat[:, :, (lc - 1 - s) * S5_GROUP:(lc - 1 - s) * S5_GROUP + width]
                      for s in range(lc)], axis=1).reshape(S5_GROUPS, width, width)

    pf = jnp.einsum('gps,gpd->gsdp', pw[0][..., lc - 1::-1], b_bar[0])
    pb = jnp.einsum('gps,gpd->gsdp', pw[1][..., :lc], b_bar[1])
    p_op = jnp.concatenate([pf.real, pf.imag, pb.real, pb.imag], axis=-1)
    p_op = p_op.reshape(S5_GROUPS, lc * S5_GROUP, 4 * S5_STATE)

    qf = jnp.einsum('gcp,gpt->gptc', c[0], pw[0][..., 1:])
    qb = jnp.einsum('gcp,gpt->gptc', c[1], pw[1][..., :0:-1])
    q_op = jnp.concatenate([qf.real, -qf.imag, qb.real, -qb.imag], axis=1)
    q_op = q_op.reshape(S5_GROUPS, 4 * S5_STATE, lc * S5_GROUP)

    a = pw[..., lc]
    coef = jnp.stack([jnp.concatenate([a[0].real, a[0].real], -1),
                      jnp.concatenate([-a[0].imag, a[0].imag], -1),
                      jnp.concatenate([a[1].real, a[1].real], -1),
                      jnp.concatenate([-a[1].imag, a[1].imag], -1)])
    return toep, p_op, q_op, coef


def _s5_state_in_body(u_ref, p_ref, v_ref):
    v_ref[0] = jnp.dot(u_ref[0].astype(F32), p_ref[0], preferred_element_type=F32, precision=HIGHEST)


def _s5_scan_body(vf_ref, vb_ref, cf_ref, xf_ref, xb_ref):
    n = vf_ref.shape[0]
    c1f, c2f, c1b, c2b = cf_ref[0], cf_ref[1], cf_ref[2], cf_ref[3]

    def step(i, carry):
        xf, xb = carry
        xf_ref[i] = xf
        xf = xf * c1f + pltpu.roll(xf, S5_STATE, 1) * c2f + vf_ref[i]
        j = n - 1 - i
        xb_ref[j] = xb
        xb = xb * c1b + pltpu.roll(xb, S5_STATE, 1) * c2b + vb_ref[j]
        return xf, xb

    zero = jnp.zeros(vf_ref.shape[1:], F32)
    lax.fori_loop(0, n, step, (zero, zero))


def _s5_out_body(u_ref, t_ref, x_ref, q_ref, y_ref):
    y = jnp.dot(u_ref[0], t_ref[0], preferred_element_type=F32)
    y = y + jnp.dot(x_ref[0], q_ref[0], preferred_element_type=F32, precision=HIGHEST)
    y_ref[0] = y.astype(y_ref.dtype)


def s5_scan(u, bsz, seq_len, ops):
    toep, p_op, q_op, coef = ops
    lc, g, w = S5_CHUNK, S5_GROUPS, S5_CHUNK * S5_GROUP
    n = seq_len // lc
    c = bsz * n
    ns = 4 * S5_STATE
    ug = u.reshape(bsz, n, lc, g, S5_GROUP).transpose(3, 0, 1, 2, 4).reshape(g, c, w)
    v = pl.pallas_call(
        _s5_state_in_body,
        out_shape=jax.ShapeDtypeStruct((g, c, ns), F32),
        grid=(g,),
        in_specs=[pl.BlockSpec((1, c, w), lambda i: (i, 0, 0)),
                  pl.BlockSpec((1, w, ns), lambda i: (i, 0, 0))],
        out_specs=pl.BlockSpec((1, c, ns), lambda i: (i, 0, 0)),
        compiler_params=_params("parallel"),
        name="s5_state_in",
    )(ug, p_op)
    v = v.reshape(g, bsz, n, ns).transpose(2, 1, 0, 3).reshape(n, bsz * g, ns)
    half = ns // 2
    cf = jnp.tile(coef, (1, bsz, 1))
    xf, xb = pl.pallas_call(
        _s5_scan_body,
        out_shape=(jax.ShapeDtypeStruct((n, bsz * g, half), F32),) * 2,
        compiler_params=pltpu.CompilerParams(vmem_limit_bytes=VMEM_LIMIT),
        name="s5_chunk_scan",
    )(v[..., :half], v[..., half:], cf)
    x = jnp.concatenate([xf, xb], axis=-1).reshape(n, bsz, g, ns).transpose(2, 1, 0, 3).reshape(g, c, ns)
    y = pl.pallas_call(
        _s5_out_body,
        out_shape=jax.ShapeDtypeStruct((g, c, w), BF16),
        grid=(g,),
        in_specs=[pl.BlockSpec((1, c, w), lambda i: (i, 0, 0)),
                  pl.BlockSpec((1, w, w), lambda i: (i, 0, 0)),
                  pl.BlockSpec((1, c, ns), lambda i: (i, 0, 0)),
                  pl.BlockSpec((1, ns, w), lambda i: (i, 0, 0))],
        out_specs=pl.BlockSpec((1, c, w), lambda i: (i, 0, 0)),
        compiler_params=_params("parallel"),
        name="s5_out",
    )(ug, toep, x, q_op)
    return y.reshape(g, bsz, n, lc, S5_GROUP).transpose(1, 2, 3, 0, 4).reshape(bsz * seq_len, S5_WIDTH)


def _gla_direction(q_ref, k_ref, v_ref, g_ref, wg, bg, s_ref, o_ref, forward, chunks):
    cs = GLA_CHUNK
    tb = chunks * cs
    logit = jnp.dot(g_ref[...].astype(F32), wg, preferred_element_type=F32, precision=HIGHEST) + bg
    g = (jnp.minimum(logit, 0.0) - jnp.log1p(jnp.exp(-jnp.abs(logit)))) / GLA_TAU
    row = lax.broadcasted_iota(jnp.int32, (tb, tb), 0)
    col = lax.broadcasted_iota(jnp.int32, (tb, tb), 1)
    same_chunk = (row // cs) == (col // cs)
    within = same_chunk & ((col <= row) if forward else (col >= row))
    bcum = jnp.dot(within.astype(F32), g, preferred_element_type=F32, precision=HIGHEST)

    srow = lax.broadcasted_iota(jnp.int32, (GLA_HEADS * cs, cs), 0) & (cs - 1)
    scol = lax.broadcasted_iota(jnp.int32, (GLA_HEADS * cs, cs), 1)
    keep = (scol <= srow) if forward else (scol >= srow)
    lane = lax.broadcasted_iota(jnp.int32, (1, GLA_QK), 1)
    head_lanes = [((lane >= h * GLA_DK) & (lane < (h + 1) * GLA_DK)).astype(F32) for h in range(GLA_HEADS)]
    state_mask = (lax.broadcasted_iota(jnp.int32, (GLA_V, GLA_QK), 0) // GLA_DV
                  == lax.broadcasted_iota(jnp.int32, (GLA_V, GLA_QK), 1) // GLA_DK)

    i_ref = cs // 2 - 1 if forward else cs // 2
    i_last = cs - 1 if forward else 0
    scale = GLA_DK ** -0.5
    for ci in (range(chunks) if forward else reversed(range(chunks))):
        rows = slice(ci * cs, (ci + 1) * cs)
        b = bcum[rows]
        bref = b[i_ref:i_ref + 1]
        blast = b[i_last:i_last + 1]
        q = q_ref[rows, :].astype(F32) * scale
        k = k_ref[rows, :].astype(F32)
        v = v_ref[rows, :]
        q_rel = q * jnp.exp(b - bref)
        k_rel = (k * jnp.exp(bref - b)).astype(BF16)
        k_out = (k * jnp.exp(blast - b)).astype(BF16)
        q_dec = (q * jnp.exp(b)).astype(BF16)
        decay = jnp.exp(blast)
        q_heads = jnp.concatenate([q_rel * hm for hm in head_lanes], axis=0).astype(BF16)
        s = lax.dot_general(q_heads, k_rel, NT_DIMS, preferred_element_type=F32)
        s = jnp.where(keep, s, 0.0).astype(BF16)
        st = s_ref[...]
        o_inter = lax.dot_general(q_dec, st.astype(BF16), NT_DIMS, preferred_element_type=F32)
        o_intra = [jnp.dot(s[h * cs:(h + 1) * cs], v[:, h * GLA_DV:(h + 1) * GLA_DV],
                           preferred_element_type=F32) for h in range(GLA_HEADS)]
        o_ref[rows, :] = o_inter + jnp.concatenate(o_intra, axis=-1)
        kv = lax.dot_general(v, k_out, TN_DIMS, preferred_element_type=F32)
        s_ref[...] = decay * st + jnp.where(state_mask, kv, 0.0)


def _gla_body(qf_ref, kf_ref, vf_ref, gf_ref, qb_ref, kb_ref, vb_ref, gb_ref, wg_ref, bg_ref,
              of_ref, ob_ref, sf_ref, sb_ref, *, chunks):
    @pl.when(pl.program_id(1) == 0)
    def _():
        sf_ref[...] = jnp.zeros_like(sf_ref)
        sb_ref[...] = jnp.zeros_like(sb_ref)

    _gla_direction(qf_ref, kf_ref, vf_ref, gf_ref, wg_ref[0], bg_ref[0], sf_ref, of_ref, True, chunks)
    _gla_direction(qb_ref, kb_ref, vb_ref, gb_ref, wg_ref[1], bg_ref[1], sb_ref, ob_ref, False, chunks)


def gla_scan(proj, bsz, seq_len, w_gate, b_gate, chunks=4):
    t = proj.shape[0]
    tb = chunks * GLA_CHUNK
    nb = seq_len // tb
    fwd = lambda cb: (lambda b, i: (b * nb + i, cb))
    bwd = lambda cb: (lambda b, i: (b * nb + nb - 1 - i, cb))
    qc, kc, vc, gc = 512 // GLA_QK, 768 // GLA_QK, 1024 // GLA_V, 2048 // 128
    wg = jnp.zeros((2, 128, GLA_QK), F32)
    wg = wg.at[0, :GLA_RANK].set(w_gate[0].astype(F32)).at[1, GLA_RANK:2 * GLA_RANK].set(w_gate[1].astype(F32))
    out = jax.ShapeDtypeStruct((t, GLA_V), F32)
    state = pltpu.VMEM((GLA_V, GLA_QK), F32)
    return pl.pallas_call(
        functools.partial(_gla_body, chunks=chunks),
        out_shape=(out, out),
        grid=(bsz, nb),
        in_specs=[pl.BlockSpec((tb, GLA_QK), fwd(qc)), pl.BlockSpec((tb, GLA_QK), fwd(kc)),
                  pl.BlockSpec((tb, GLA_V), fwd(vc)), pl.BlockSpec((tb, 128), fwd(gc)),
                  pl.BlockSpec((tb, GLA_QK), bwd(qc)), pl.BlockSpec((tb, GLA_QK), bwd(kc)),
                  pl.BlockSpec((tb, GLA_V), bwd(vc)), pl.BlockSpec((tb, 128), bwd(gc)),
                  pl.BlockSpec((2, 128, GLA_QK), lambda b, i: (0, 0, 0)),
                  pl.BlockSpec((2, 1, GLA_QK), lambda b, i: (0, 0, 0))],
        out_specs=(pl.BlockSpec((tb, GLA_V), fwd(0)), pl.BlockSpec((tb, GLA_V), bwd(0))),
        scratch_shapes=[state, state],
        compiler_params=_params("parallel", "arbitrary"),
        name="gla_scan",
    )(proj, proj, proj, proj, proj, proj, proj, proj, wg, b_gate.astype(F32).reshape(2, 1, GLA_QK))


def _even_out_body(x_ref, ys_ref, of_ref, ob_ref, og_ref, wglu_ref, gn_ref, wtop_ref, wbot_ref, o_ref):
    y = ys_ref[...].astype(F32)
    y = 0.5 * y * (1.0 + jnp.tanh(math.sqrt(2.0 / math.pi) * (y + 0.044715 * (y * y * y))))
    gate = jnp.dot(y.astype(BF16), wglu_ref[...], preferred_element_type=F32)
    y = y * _sigmoid(gate)
    o = of_ref[...] + ob_ref[...]
    og = og_ref[...].astype(F32)
    heads = []
    for h in range(GLA_HEADS):
        sl = slice(h * GLA_DV, (h + 1) * GLA_DV)
        heads.append(_rms(o[:, sl], gn_ref[...]))
    o = jnp.concatenate(heads, axis=-1) * (og * _sigmoid(og))
    o_ref[...] = (x_ref[...]
                  + jnp.dot(y.astype(BF16), wtop_ref[...], preferred_element_type=F32)
                  + jnp.dot(o.astype(BF16), wbot_ref[...], preferred_element_type=F32))


def even_out(x, ys, o_f, o_b, proj, w_glu, gla_norm, w_out, tm=ROW_TILE):
    t, d = x.shape
    row = lambda i: (i, 0)
    const = lambda i: (0, 0)
    return pl.pallas_call(
        _even_out_body,
        out_shape=jax.ShapeDtypeStruct((t, d), F32),
        grid=(t // tm,),
        in_specs=[pl.BlockSpec((tm, d), row), pl.BlockSpec((tm, S5_WIDTH), row),
                  pl.BlockSpec((tm, GLA_V), row), pl.BlockSpec((tm, GLA_V), row),
                  pl.BlockSpec((tm, GLA_V), lambda i: (i, 1536 // GLA_V)),
                  pl.BlockSpec((S5_WIDTH, S5_WIDTH), const), pl.BlockSpec((1, GLA_DV), const),
                  pl.BlockSpec((S5_WIDTH, d), const), pl.BlockSpec((GLA_V, d), const)],
        out_specs=pl.BlockSpec((tm, d), row),
        compiler_params=_params("parallel"),
        name="even_out",
    )(x, ys, o_f, o_b, proj, w_glu, gla_norm.astype(F32).reshape(1, GLA_DV),
      w_out[:S5_WIDTH], w_out[S5_WIDTH:])


def _seg_rms(a, gain):
    a2 = a * a
    lane = lax.broadcasted_iota(jnp.int32, a.shape, 1)
    lo = lane < DIFF_DK
    s_lo = jnp.sum(jnp.where(lo, a2, 0.0), axis=-1, keepdims=True)
    s_hi = jnp.sum(jnp.where(lo, 0.0, a2), axis=-1, keepdims=True)
    ms = jnp.where(lo, s_lo, s_hi) * (1.0 / DIFF_DK)
    return a * lax.rsqrt(ms + EPS) * gain


def _qkv_body(x_ref, g_ref, w_ref, qk_gain_ref, o_ref, h_ref):
    j = pl.program_id(1)

    @pl.when(j == 0)
    def _():
        h_ref[...] = _rms(x_ref[...], g_ref[...]).astype(BF16)

    acc = jnp.dot(h_ref[...], w_ref[...], preferred_element_type=F32)

    @pl.when(j < 2)
    def _():
        gain = qk_gain_ref[0]
        for c in range(DIFF_HEADS):
            sl = slice(c * 2 * DIFF_DK, (c + 1) * 2 * DIFF_DK)
            o_ref[:, sl] = _seg_rms(acc[:, sl], gain).astype(o_ref.dtype)

    @pl.when(j == 2)
    def _():
        o_ref[...] = acc.astype(o_ref.dtype)


def qkv_project(x, gain, w, q_norm, k_norm, tm=ROW_TILE):
    t, d = x.shape
    qg = jnp.tile(q_norm.astype(F32) * DIFF_DK ** -0.5, 2)
    kg = jnp.tile(k_norm.astype(F32), 2)
    qk_gain = jnp.stack([qg, kg]).reshape(2, 1, 2 * DIFF_DK)
    return pl.pallas_call(
        _qkv_body,
        out_shape=jax.ShapeDtypeStruct((t, 3 * d), BF16),
        grid=(t // tm, 3),
        in_specs=[pl.BlockSpec((tm, d), lambda i, j: (i, 0)),
                  pl.BlockSpec((1, d), lambda i, j: (0, 0)),
                  pl.BlockSpec((d, d), lambda i, j: (0, j)),
                  pl.BlockSpec((1, 1, 2 * DIFF_DK), lambda i, j: (jnp.minimum(j, 1), 0, 0))],
        out_specs=pl.BlockSpec((tm, d), lambda i, j: (i, j)),
        scratch_shapes=[pltpu.VMEM((tm, d), BF16)],
        compiler_params=_params("parallel", "arbitrary"),
        name="qkv_project",
    )(x, gain.reshape(1, d), w, qk_gain)


POS_SPLIT = 16
POS_SHIFT = 4
AUG_LANE = DIFF_DK
SOFTMAX_GROUP_ELEMS = 32 * 1024


def _pos_terms(shape):
    lane = lax.broadcasted_iota(jnp.int32, shape, 1)
    pos = lax.broadcasted_iota(jnp.int32, shape, 0)
    hi = lax.shift_right_logical(pos, POS_SHIFT).astype(F32)
    lo = (pos & (POS_SPLIT - 1)).astype(F32)
    return lane, hi, lo


def _lane_select(lane, first, values):
    out = 0.0
    for n, val in reversed(list(enumerate(values))):
        out = jnp.where(lane == first + n, val, out)
    return out


def _diff_attn_body(slope_ref, q_ref, k_ref, v_ref, lq1_ref, lk1_ref, lq2_ref, lk2_ref, sub_ref,
                    o_ref, ka_ref, va_ref, qs_ref, s0_ref, s1_ref, p0_ref, p1_ref, acc0_ref, acc1_ref,
                    *, blk, seq_len, lambda_init):
    h = pl.program_id(1)
    qi = pl.program_id(2)
    slope = slope_ref[h]
    nk = seq_len // blk
    s_refs, p_refs, acc_refs = (s0_ref, s1_ref), (p0_ref, p1_ref), (acc0_ref, acc1_ref)

    @pl.when(qi == 0)
    def _():
        va_ref[:, :DIFF_DV] = v_ref[...]
        va_ref[:, DIFF_DV:] = jnp.ones((seq_len, DIFF_DV), BF16)

        def build(t, carry):
            rows = pl.ds(pl.multiple_of(t * blk, blk), blk)
            kf = k_ref[rows, :].astype(F32)
            lane, hi, lo = _pos_terms(kf.shape)
            aug = _lane_select(lane, AUG_LANE,
                               [-blk * slope, -POS_SPLIT * slope, -slope,
                                (blk * slope) * jnp.asarray(t, F32), (POS_SPLIT * slope) * hi, slope * lo])
            for z, kz in enumerate((kf, pltpu.roll(kf, DIFF_DK, 1))):
                ka_ref[z, rows, :] = jnp.where(lane < DIFF_DK, kz, aug).astype(BF16)
            return carry

        lax.fori_loop(0, nk, build, 0)

    qf = q_ref[...].astype(F32)
    lane, hi, lo = _pos_terms(qf.shape)
    qa = _lane_select(lane, AUG_LANE, [jnp.asarray(qi, F32), hi, lo, 1.0, 1.0, 1.0])
    for z, qz in enumerate((qf, pltpu.roll(qf, DIFF_DK, 1))):
        qs_ref[z] = jnp.where(lane < DIFF_DK, qz, 0.0).astype(BF16)
        qs_ref[2 + z] = jnp.where(lane < DIFF_DK, qz, qa).astype(BF16)
        qs_ref[4 + z] = jnp.where(lane < DIFF_DK, qz, -qa).astype(BF16)

    def key_block(t):
        if t == 0:
            return qi, 0
        j = (t - 1) + jnp.asarray(qi <= t - 1, jnp.int32)
        return j, jnp.where(j < qi, 2, 4)

    def key_rows(t):
        j, _ = key_block(t)
        return pl.ds(pl.multiple_of(j * blk, blk), blk)

    for z, s_ref in enumerate(s_refs):
        for t in range(nk):
            _, variant = key_block(t)
            s_ref[:, t * blk:(t + 1) * blk] = lax.dot_general(
                qs_ref[variant + z], ka_ref[z, key_rows(t), :], NT_DIMS, preferred_element_type=F32)

    rg = SOFTMAX_GROUP_ELEMS // seq_len
    rel = (lax.broadcasted_iota(jnp.int32, (rg, blk), 0) - lax.broadcasted_iota(jnp.int32, (rg, blk), 1))
    for s_ref, p_ref, acc_ref in zip(s_refs, p_refs, acc_refs):
        for g in range(blk // rg):
            rows = slice(g * rg, (g + 1) * rg)
            s_diag = s_ref[rows, :blk] - slope * jnp.abs(rel + g * rg).astype(F32)
            s_rest = s_ref[rows, blk:]
            m = jnp.maximum(jnp.max(s_diag, axis=-1, keepdims=True), jnp.max(s_rest, axis=-1, keepdims=True))
            p_ref[rows, :blk] = jnp.exp(s_diag - m).astype(BF16)
            p_ref[rows, blk:] = jnp.exp(s_rest - m).astype(BF16)
        for t in range(nk):
            part = jnp.dot(p_ref[:, t * blk:(t + 1) * blk], va_ref[key_rows(t), :],
                           preferred_element_type=F32)
            if t == 0:
                acc_ref[...] = part
            else:
                acc_ref[...] += part

    lam = (jnp.exp(jnp.sum(lq1_ref[...] * lk1_ref[...], axis=-1, keepdims=True))
           - jnp.exp(jnp.sum(lq2_ref[...] * lk2_ref[...], axis=-1, keepdims=True)) + lambda_init)
    a0 = acc_refs[0][...]
    a1 = acc_refs[1][...]
    o = (a0[:, :DIFF_DV] / a0[:, DIFF_DV:DIFF_DV + 1]
         - lam * (a1[:, :DIFF_DV] / a1[:, DIFF_DV:DIFF_DV + 1]))
    o_ref[...] = (_rms(o, sub_ref[...]) * (1.0 - lambda_init)).astype(o_ref.dtype)


def diff_attention(qkv, bsz, seq_len, lq1, lk1, lq2, lk2, sub_norm, lambda_init, blk=512):
    t = qkv.shape[0]
    nq = seq_len // blk
    assert blk // POS_SPLIT <= 256, "hi part of a block position must stay exact in bf16"
    slopes = jnp.asarray(2.0 ** (-8.0 * np.arange(1, DIFF_HEADS + 1, dtype=np.float32) / DIFF_HEADS), F32)
    vec = lambda a: a.astype(F32).reshape(1, DIFF_DK)
    const = lambda b, h, i: (0, 0)
    return pl.pallas_call(
        functools.partial(_diff_attn_body, blk=blk, seq_len=seq_len, lambda_init=lambda_init),
        out_shape=jax.ShapeDtypeStruct((t, DIFF_HEADS * DIFF_DV), BF16),
        grid=(bsz, DIFF_HEADS, nq),
        in_specs=[pl.BlockSpec(memory_space=pltpu.SMEM),
                  pl.BlockSpec((blk, 2 * DIFF_DK), lambda b, h, i: (b * nq + i, h)),
                  pl.BlockSpec((seq_len, 2 * DIFF_DK), lambda b, h, i: (b, DIFF_HEADS + h)),
                  pl.BlockSpec((seq_len, DIFF_DV), lambda b, h, i: (b, 2 * DIFF_HEADS + h)),
                  pl.BlockSpec((1, DIFF_DK), const), pl.BlockSpec((1, DIFF_DK), const),
                  pl.BlockSpec((1, DIFF_DK), const), pl.BlockSpec((1, DIFF_DK), const),
                  pl.BlockSpec((1, DIFF_DV), const)],
        out_specs=pl.BlockSpec((blk, DIFF_DV), lambda b, h, i: (b * nq + i, h)),
        scratch_shapes=[pltpu.VMEM((2, seq_len, 2 * DIFF_DK), BF16),
                        pltpu.VMEM((seq_len, 2 * DIFF_DV), BF16),
                        pltpu.VMEM((6, blk, 2 * DIFF_DK), BF16),
                        pltpu.VMEM((blk, seq_len), F32),
                        pltpu.VMEM((blk, seq_len), F32),
                        pltpu.VMEM((blk, seq_len), BF16),
                        pltpu.VMEM((blk, seq_len), BF16),
                        pltpu.VMEM((blk, 2 * DIFF_DV), F32),
                        pltpu.VMEM((blk, 2 * DIFF_DV), F32)],
        compiler_params=_params("parallel", "parallel", "arbitrary"),
        name="diff_attention",
    )(slopes, qkv, qkv, qkv, vec(lq1), vec(lk1), vec(lq2), vec(lk2),
      sub_norm.astype(F32).reshape(1, DIFF_DV))


def _pad_ev_w_in(w):
    return jnp.pad(w, ((0, 0), (0, EV_PAD_COLS - w.shape[1]))).astype(BF16)


def _trunk(x3, mem, p):
    bsz, seq_len, d = x3.shape
    x = x3.reshape(bsz * seq_len, d)
    kn_all, v_all = mem_kv(mem, p['norm_mem'], p['x_w_kv'], p['x_k_norm'])
    for layer in range(DEPTH):
        if layer % 2 == 0:
            e = layer // 2
            proj = norm_matmul(x, p['norm_mix'][layer], p['ev_w_in'][e], BF16, EV_TN)
            ys = s5_scan(proj[:, :S5_WIDTH], bsz, seq_len, p['s5_ops'][e])
            o_f, o_b = gla_scan(proj, bsz, seq_len, p['gla_w_gate'][e], p['gla_b_gate'][e])
            x = even_out(x, ys, o_f, o_b, proj, p['s5_w_glu'][e], p['gla_norm'][e], p['ev_w_out'][e])
        else:
            o = layer // 2
            lambda_init = 0.8 - 0.6 * math.exp(-0.3 * layer)
            qkv = qkv_project(x, p['norm_mix'][layer], p['od_w_in'][o],
                              p['diff_q_norm'][o], p['diff_k_norm'][o])
            att = diff_attention(qkv, bsz, seq_len, p['diff_lambda_q1'][o], p['diff_lambda_k1'][o],
                                 p['diff_lambda_q2'][o], p['diff_lambda_k2'][o], p['diff_norm'][o],
                                 lambda_init)
            x = matmul_residual(att, p['od_w_out'][o], x)
        x = cross_block(x, seq_len, p['norm_cross'][layer], p['x_w_q'][layer], p['x_q_norm'][layer],
                        kn_all[layer], v_all[layer], p['x_w_o'][layer])
        x = mlp_block(x, p['norm_mlp'][layer], p['mlp_w1'][layer], p['mlp_w2'][layer])
    return x.reshape(bsz, seq_len, d)


def kernel(x_prompt, x_sample, mem_prompt, mem_sample, norm_mix, norm_cross, norm_mem, norm_mlp,
           ev_w_in, ev_w_out, s5_lambda_re, s5_lambda_im, s5_log_step, s5_b_re, s5_b_im,
           s5_c_re, s5_c_im, s5_d, s5_w_glu, gla_w_gate, gla_b_gate, gla_norm,
           od_w_in, od_w_out, diff_q_norm, diff_k_norm, diff_lambda_q1, diff_lambda_k1,
           diff_lambda_q2, diff_lambda_k2, diff_norm, x_w_q, x_w_kv, x_w_o, x_q_norm, x_k_norm,
           mlp_w1, mlp_w2):
    n_even = ev_w_in.shape[0]
    bf = lambda w: w.astype(BF16)
    p = dict(
        norm_mix=norm_mix, norm_cross=norm_cross, norm_mem=norm_mem, norm_mlp=norm_mlp,
        ev_w_in=[_pad_ev_w_in(ev_w_in[e]) for e in range(n_even)], ev_w_out=bf(ev_w_out),
        s5_ops=[s5_operators(s5_lambda_re[e], s5_lambda_im[e], s5_log_step[e], s5_b_re[e], s5_b_im[e],
                             s5_c_re[e], s5_c_im[e], s5_d[e]) for e in range(n_even)],
        s5_w_glu=bf(s5_w_glu), gla_w_gate=gla_w_gate, gla_b_gate=gla_b_gate, gla_norm=gla_norm,
        od_w_in=bf(od_w_in), od_w_out=bf(od_w_out), diff_q_norm=diff_q_norm, diff_k_norm=diff_k_norm,
        diff_lambda_q1=diff_lambda_q1, diff_lambda_k1=diff_lambda_k1,
        diff_lambda_q2=diff_lambda_q2, diff_lambda_k2=diff_lambda_k2, diff_norm=diff_norm,
        x_w_q=bf(x_w_q), x_w_kv=bf(x_w_kv), x_w_o=bf(x_w_o), x_q_norm=x_q_norm, x_k_norm=x_k_norm,
        mlp_w1=bf(mlp_w1), mlp_w2=bf(mlp_w2))
    return (_trunk(x_prompt, mem_prompt, p), _trunk(x_sample, mem_sample, p))
```

```python
import functools
import math

import numpy as np
import jax
import jax.numpy as jnp
from jax import lax
from jax.experimental import pallas as pl
from jax.experimental.pallas import tpu as pltpu

F32 = jnp.float32
BF16 = jnp.bfloat16
HIGHEST = lax.Precision.HIGHEST

D_MODEL = 1024
DEPTH = 4
EPS = 1e-6
S5_WIDTH = 512
S5_GROUP = 16
S5_GROUPS = 32
S5_STATE = 64
S5_CHUNK = 64
GLA_HEADS = 4
GLA_DV = 128
GLA_DK = 64
GLA_RANK = 16
GLA_TAU = 16.0
GLA_CHUNK = 64
GLA_QK = GLA_HEADS * GLA_DK
GLA_V = GLA_HEADS * GLA_DV
EV_PAD_COLS = 2304
EV_TN = 768
DIFF_HEADS = 8
DIFF_DK = 64
DIFF_DV = 128
X_HEADS = 4
X_DH = 256
D_FF = 4096

ROW_TILE = 512
VMEM_LIMIT = 48 * 1024 * 1024

NT_DIMS = (((1,), (1,)), ((), ()))
TN_DIMS = (((0,), (0,)), ((), ()))


def _params(*sem):
    return pltpu.CompilerParams(dimension_semantics=sem, vmem_limit_bytes=VMEM_LIMIT)


def _rms(x, gain):
    ms = jnp.mean(x * x, axis=-1, keepdims=True)
    return x * lax.rsqrt(ms + EPS) * gain


def _sigmoid(x):
    return 1.0 / (1.0 + jnp.exp(-x))


def _norm_matmul_body(x_ref, g_ref, w_ref, o_ref, h_ref):
    @pl.when(pl.program_id(1) == 0)
    def _():
        h_ref[...] = _rms(x_ref[...], g_ref[...]).astype(BF16)

    o_ref[...] = jnp.dot(h_ref[...], w_ref[...], preferred_element_type=F32).astype(o_ref.dtype)


def norm_matmul(x, gain, w, out_dtype, tn, tm=ROW_TILE):
    t, d = x.shape
    n = w.shape[1]
    return pl.pallas_call(
        _norm_matmul_body,
        out_shape=jax.ShapeDtypeStruct((t, n), out_dtype),
        grid=(t // tm, n // tn),
        in_specs=[pl.BlockSpec((tm, d), lambda i, j: (i, 0)),
                  pl.BlockSpec((1, d), lambda i, j: (0, 0)),
                  pl.BlockSpec((d, tn), lambda i, j: (0, j))],
        out_specs=pl.BlockSpec((tm, tn), lambda i, j: (i, j)),
        scratch_shapes=[pltpu.VMEM((tm, d), BF16)],
        compiler_params=_params("parallel", "arbitrary"),
        name="norm_matmul",
    )(x, gain.reshape(1, d), w)


def _matmul_res_body(a_ref, w_ref, r_ref, o_ref):
    o_ref[...] = r_ref[...] + jnp.dot(a_ref[...].astype(BF16), w_ref[...], preferred_element_type=F32)


def matmul_residual(a, w, res, tn=512, tm=ROW_TILE):
    t, k = a.shape
    n = w.shape[1]
    return pl.pallas_call(
        _matmul_res_body,
        out_shape=jax.ShapeDtypeStruct((t, n), F32),
        grid=(t // tm, n // tn),
        in_specs=[pl.BlockSpec((tm, k), lambda i, j: (i, 0)),
                  pl.BlockSpec((k, tn), lambda i, j: (0, j)),
                  pl.BlockSpec((tm, tn), lambda i, j: (i, j))],
        out_specs=pl.BlockSpec((tm, tn), lambda i, j: (i, j)),
        compiler_params=_params("parallel", "arbitrary"),
        name="matmul_residual",
    )(a, w, res)


def _mlp_body(x_ref, g_ref, w1_ref, w2_ref, o_ref, h_ref, acc_ref):
    f = pl.program_id(1)

    @pl.when(f == 0)
    def _():
        h_ref[...] = _rms(x_ref[...], g_ref[...]).astype(BF16)
        acc_ref[...] = jnp.zeros_like(acc_ref)

    hid = jnp.dot(h_ref[...], w1_ref[...], preferred_element_type=F32)
    hid = jnp.square(jnp.maximum(hid, 0.0)).astype(BF16)
    acc_ref[...] += jnp.dot(hid, w2_ref[...], preferred_element_type=F32)

    @pl.when(f == pl.num_programs(1) - 1)
    def _():
        o_ref[...] = x_ref[...] + acc_ref[...]


def mlp_block(x, gain, w1, w2, tf=512, tm=2 * ROW_TILE):
    t, d = x.shape
    ff = w1.shape[1]
    return pl.pallas_call(
        _mlp_body,
        out_shape=jax.ShapeDtypeStruct((t, d), F32),
        grid=(t // tm, ff // tf),
        in_specs=[pl.BlockSpec((tm, d), lambda i, f: (i, 0)),
                  pl.BlockSpec((1, d), lambda i, f: (0, 0)),
                  pl.BlockSpec((d, tf), lambda i, f: (0, f)),
                  pl.BlockSpec((tf, d), lambda i, f: (f, 0))],
        out_specs=pl.BlockSpec((tm, d), lambda i, f: (i, 0)),
        scratch_shapes=[pltpu.VMEM((tm, d), BF16), pltpu.VMEM((tm, d), F32)],
        compiler_params=_params("parallel", "arbitrary"),
        name="mlp_block",
    )(x, gain.reshape(1, d), w1, w2)


def _mem_kv_body(m_ref, g_ref, w_ref, kg_ref, k_ref, v_ref):
    h = _rms(m_ref[0], g_ref[0]).astype(BF16)
    kv = jnp.dot(h, w_ref[0], preferred_element_type=F32)
    for hd in range(X_HEADS):
        sl = slice(hd * X_DH, (hd + 1) * X_DH)
        k_ref[0, 0, :, sl] = _rms(kv[:, sl], kg_ref[0]).astype(BF16)
    v_ref[0, 0] = kv[:, D_MODEL:].astype(BF16)


def mem_kv(mem, norm_mem, w_kv, k_norm):
    bm, nm, d = mem.shape
    out = jax.ShapeDtypeStruct((DEPTH, bm, nm, d), BF16)
    return pl.pallas_call(
        _mem_kv_body,
        out_shape=(out, out),
        grid=(DEPTH, bm),
        in_specs=[pl.BlockSpec((1, nm, d), lambda l, b: (b, 0, 0)),
                  pl.BlockSpec((1, 1, d), lambda l, b: (l, 0, 0)),
                  pl.BlockSpec((1, d, 2 * d), lambda l, b: (l, 0, 0)),
                  pl.BlockSpec((1, 1, X_DH), lambda l, b: (l, 0, 0))],
        out_specs=(pl.BlockSpec((1, 1, nm, d), lambda l, b: (l, b, 0, 0)),
                   pl.BlockSpec((1, 1, nm, d), lambda l, b: (l, b, 0, 0))),
        compiler_params=_params("arbitrary", "arbitrary"),
        name="mem_kv",
    )(mem, norm_mem.reshape(DEPTH, 1, d), w_kv, k_norm.reshape(DEPTH, 1, X_DH))


def _cross_body(x_ref, g_ref, wq_ref, qg_ref, k_ref, v_ref, wo_ref, o_ref):
    x = x_ref[...]
    h = _rms(x, g_ref[...]).astype(BF16)
    q = jnp.dot(h, wq_ref[...], preferred_element_type=F32)
    heads = []
    for hd in range(X_HEADS):
        sl = slice(hd * X_DH, (hd + 1) * X_DH)
        qn = _rms(q[:, sl], qg_ref[...]).astype(BF16)
        s = lax.dot_general(qn, k_ref[0, :, sl], NT_DIMS, preferred_element_type=F32)
        p = jnp.exp(s - jnp.max(s, axis=-1, keepdims=True))
        l = jnp.sum(p, axis=-1, keepdims=True)
        oh = jnp.dot(p.astype(BF16), v_ref[0, :, sl], preferred_element_type=F32) / l
        heads.append(oh.astype(BF16))
    o = jnp.concatenate(heads, axis=-1)
    o_ref[...] = x + jnp.dot(o, wo_ref[...], preferred_element_type=F32)


def cross_block(x, seq_len, gain, w_q, q_gain, kn, v, w_o, tm=ROW_TILE):
    t, d = x.shape
    nm = kn.shape[1]
    per_seq = seq_len // tm
    return pl.pallas_call(
        _cross_body,
        out_shape=jax.ShapeDtypeStruct((t, d), F32),
        grid=(t // tm,),
        in_specs=[pl.BlockSpec((tm, d), lambda i: (i, 0)),
                  pl.BlockSpec((1, d), lambda i: (0, 0)),
                  pl.BlockSpec((d, d), lambda i: (0, 0)),
                  pl.BlockSpec((1, X_DH), lambda i: (0, 0)),
                  pl.BlockSpec((1, nm, d), lambda i: (i // per_seq, 0, 0)),
                  pl.BlockSpec((1, nm, d), lambda i: (i // per_seq, 0, 0)),
                  pl.BlockSpec((d, d), lambda i: (0, 0))],
        out_specs=pl.BlockSpec((tm, d), lambda i: (i, 0)),
        compiler_params=_params("parallel"),
        name="cross_block",
    )(x, gain.reshape(1, d), w_q, (q_gain * X_DH ** -0.5).reshape(1, X_DH), kn, v, w_o)


LANES = 128


def _toeplitz_body(k_ref, o_ref):
    lc, gs = S5_CHUNK, S5_GROUP
    kern = k_ref[0]
    n = kern.shape[1]
    per_tile = LANES // gs
    for r in range(per_tile):
        shifted = pltpu.roll(kern, n - gs * r, 1) if r else kern
        for a in range(lc // per_tile):
            s = lc - 1 - (per_tile * a + r)
            o_ref[0, s * gs:(s + 1) * gs, :] = shifted[:, LANES * a:LANES * a + lc * gs].astype(o_ref.dtype)


def toeplitz_expand(kern_flat):
    g, gs, n = kern_flat.shape
    width = S5_CHUNK * S5_GROUP
    return pl.pallas_call(
        _toeplitz_body,
        out_shape=jax.ShapeDtypeStruct((g, width, width), BF16),
        grid=(g,),
        in_specs=[pl.BlockSpec((1, gs, n), lambda i: (i, 0, 0))],
        out_specs=pl.BlockSpec((1, width, width), lambda i: (i, 0, 0)),
        compiler_params=_params("parallel"),
        name="toeplitz_expand",
    )(kern_flat)


def s5_operators(lam_re, lam_im, log_step, b_re, b_im, c_re, c_im, d):
    lc = S5_CHUNK
    lam = lax.complex(lam_re.astype(F32), lam_im.astype(F32))
    step = jnp.exp(log_step.astype(F32))[..., None]
    lam_bar = jnp.exp(lam * step)
    b_bar = ((lam_bar - 1.0) / lam)[..., None] * lax.complex(b_re.astype(F32), b_im.astype(F32))
    c = lax.complex(c_re.astype(F32), c_im.astype(F32))
    pw = jnp.cumprod(jnp.broadcast_to(lam_bar[..., None], lam_bar.shape + (lc,)), axis=-1)
    pw = jnp.concatenate([jnp.ones_like(pw[..., :1]), pw], axis=-1)
    kern = jnp.einsum('zgcp,zgpt,zgpd->zgtcd', c, pw[..., :lc], b_bar, precision=HIGHEST).real
    kf, kb = kern[0], kern[1]
    k0 = kf[:, :1] + kb[:, :1] + (d.astype(F32)[:, :, None] * jnp.eye(S5_GROUP, dtype=F32))[:, None]
    kern_full = jnp.concatenate([kb[:, :0:-1], k0, kf[:, 1:]], axis=1)
    kern_flat = kern_full.transpose(0, 3, 1, 2).reshape(S5_GROUPS, S5_GROUP, (2 * lc - 1) * S5_GROUP)
    toep = toeplitz_expand(jnp.pad(kern_flat, ((0, 0), (0, 0), (0, S5_GROUP))))

    pf = jnp.einsum('gps,gpd->gsdp', pw[0][..., lc - 1::-1], b_bar[0])
    pb = jnp.einsum('gps,gpd->gsdp', pw[1][..., :lc], b_bar[1])
    p_op = jnp.concatenate([pf.real, pf.imag, pb.real, pb.imag], axis=-1)
    p_op = p_op.reshape(S5_GROUPS, lc * S5_GROUP, 4 * S5_STATE)

    qf = jnp.einsum('gcp,gpt->gptc', c[0], pw[0][..., 1:])
    qb = jnp.einsum('gcp,gpt->gptc', c[1], pw[1][..., :0:-1])
    q_op = jnp.concatenate([qf.real, -qf.imag, qb.real, -qb.imag], axis=1)
    q_op = q_op.reshape(S5_GROUPS, 4 * S5_STATE, lc * S5_GROUP)

    a = pw[..., lc]
    coef = jnp.stack([jnp.concatenate([a[0].real, a[0].real], -1),
                      jnp.concatenate([-a[0].imag, a[0].imag], -1),
                      jnp.concatenate([a[1].real, a[1].real], -1),
                      jnp.concatenate([-a[1].imag, a[1].imag], -1)])
    return toep, p_op, q_op, coef


def _s5_state_in_body(u_ref, p_ref, v_ref):
    v_ref[0] = jnp.dot(u_ref[0].astype(F32), p_ref[0], preferred_element_type=F32, precision=HIGHEST)


def _s5_scan_body(vf_ref, vb_ref, cf_ref, xf_ref, xb_ref):
    n = vf_ref.shape[0]
    c1f, c2f, c1b, c2b = cf_ref[0], cf_ref[1], cf_ref[2], cf_ref[3]

    def step(i, carry):
        xf, xb = carry
        xf_ref[i] = xf
        xf = xf * c1f + pltpu.roll(xf, S5_STATE, 1) * c2f + vf_ref[i]
        j = n - 1 - i
        xb_ref[j] = xb
        xb = xb * c1b + pltpu.roll(xb, S5_STATE, 1) * c2b + vb_ref[j]
        return xf, xb

    zero = jnp.zeros(vf_ref.shape[1:], F32)
    lax.fori_loop(0, n, step, (zero, zero))


def _s5_out_body(u_ref, t_ref, x_ref, q_ref, y_ref):
    y = jnp.dot(u_ref[0], t_ref[0], preferred_element_type=F32)
    y = y + jnp.dot(x_ref[0], q_ref[0], preferred_element_type=F32, precision=HIGHEST)
    y_ref[0] = y.astype(y_ref.dtype)


def s5_scan(u, bsz, seq_len, ops):
    toep, p_op, q_op, coef = ops
    lc, g, w = S5_CHUNK, S5_GROUPS, S5_CHUNK * S5_GROUP
    n = seq_len // lc
    c = bsz * n
    ns = 4 * S5_STATE
    ug = u.reshape(bsz, n, lc, g, S5_GROUP).transpose(3, 0, 1, 2, 4).reshape(g, c, w)
    v = pl.pallas_call(
        _s5_state_in_body,
        out_shape=jax.ShapeDtypeStruct((g, c, ns), F32),
        grid=(g,),
        in_specs=[pl.BlockSpec((1, c, w), lambda i: (i, 0, 0)),
                  pl.BlockSpec((1, w, ns), lambda i: (i, 0, 0))],
        out_specs=pl.BlockSpec((1, c, ns), lambda i: (i, 0, 0)),
        compiler_params=_params("parallel"),
        name="s5_state_in",
    )(ug, p_op)
    v = v.reshape(g, bsz, n, ns).transpose(2, 1, 0, 3).reshape(n, bsz * g, ns)
    half = ns // 2
    cf = jnp.tile(coef, (1, bsz, 1))
    xf, xb = pl.pallas_call(
        _s5_scan_body,
        out_shape=(jax.ShapeDtypeStruct((n, bsz * g, half), F32),) * 2,
        compiler_params=pltpu.CompilerParams(vmem_limit_bytes=VMEM_LIMIT),
        name="s5_chunk_scan",
    )(v[..., :half], v[..., half:], cf)
    x = jnp.concatenate([xf, xb], axis=-1).reshape(n, bsz, g, ns).transpose(2, 1, 0, 3).reshape(g, c, ns)
    y = pl.pallas_call(
        _s5_out_body,
        out_shape=jax.ShapeDtypeStruct((g, c, w), BF16),
        grid=(g,),
        in_specs=[pl.BlockSpec((1, c, w), lambda i: (i, 0, 0)),
                  pl.BlockSpec((1, w, w), lambda i: (i, 0, 0)),
                  pl.BlockSpec((1, c, ns), lambda i: (i, 0, 0)),
                  pl.BlockSpec((1, ns, w), lambda i: (i, 0, 0))],
        out_specs=pl.BlockSpec((1, c, w), lambda i: (i, 0, 0)),
        compiler_params=_params("parallel"),
        name="s5_out",
    )(ug, toep, x, q_op)
    return y.reshape(g, bsz, n, lc, S5_GROUP).transpose(1, 2, 3, 0, 4).reshape(bsz * seq_len, S5_WIDTH)


def _gla_direction(q_ref, k_ref, v_ref, g_ref, wg, bg, s_ref, o_ref, forward, chunks):
    cs = GLA_CHUNK
    tb = chunks * cs
    logit = jnp.dot(g_ref[...].astype(F32), wg, preferred_element_type=F32, precision=HIGHEST) + bg
    g = (jnp.minimum(logit, 0.0) - jnp.log1p(jnp.exp(-jnp.abs(logit)))) / GLA_TAU
    row = lax.broadcasted_iota(jnp.int32, (tb, tb), 0)
    col = lax.broadcasted_iota(jnp.int32, (tb, tb), 1)
    same_chunk = (row // cs) == (col // cs)
    within = same_chunk & ((col <= row) if forward else (col >= row))
    bcum = jnp.dot(within.astype(F32), g, preferred_element_type=F32, precision=HIGHEST)

    srow = lax.broadcasted_iota(jnp.int32, (GLA_HEADS * cs, cs), 0) & (cs - 1)
    scol = lax.broadcasted_iota(jnp.int32, (GLA_HEADS * cs, cs), 1)
    keep = (scol <= srow) if forward else (scol >= srow)
    lane = lax.broadcasted_iota(jnp.int32, (1, GLA_QK), 1)
    head_lanes = [((lane >= h * GLA_DK) & (lane < (h + 1) * GLA_DK)).astype(F32) for h in range(GLA_HEADS)]
    state_mask = (lax.broadcasted_iota(jnp.int32, (GLA_V, GLA_QK), 0) // GLA_DV
                  == lax.broadcasted_iota(jnp.int32, (GLA_V, GLA_QK), 1) // GLA_DK)

    i_ref = cs // 2 - 1 if forward else cs // 2
    i_last = cs - 1 if forward else 0
    scale = GLA_DK ** -0.5
    for ci in (range(chunks) if forward else reversed(range(chunks))):
        rows = slice(ci * cs, (ci + 1) * cs)
        b = bcum[rows]
        bref = b[i_ref:i_ref + 1]
        blast = b[i_last:i_last + 1]
        q = q_ref[rows, :].astype(F32) * scale
        k = k_ref[rows, :].astype(F32)
        v = v_ref[rows, :]
        q_rel = q * jnp.exp(b - bref)
        k_rel = (k * jnp.exp(bref - b)).astype(BF16)
        k_out = (k * jnp.exp(blast - b)).astype(BF16)
        q_dec = (q * jnp.exp(b)).astype(BF16)
        decay = jnp.exp(blast)
        q_heads = jnp.concatenate([q_rel * hm for hm in head_lanes], axis=0).astype(BF16)
        s = lax.dot_general(q_heads, k_rel, NT_DIMS, preferred_element_type=F32)
        s = jnp.where(keep, s, 0.0).astype(BF16)
        st = s_ref[...]
        o_inter = lax.dot_general(q_dec, st.astype(BF16), NT_DIMS, preferred_element_type=F32)
        o_intra = [jnp.dot(s[h * cs:(h + 1) * cs], v[:, h * GLA_DV:(h + 1) * GLA_DV],
                           preferred_element_type=F32) for h in range(GLA_HEADS)]
        o_ref[rows, :] = o_inter + jnp.concatenate(o_intra, axis=-1)
        kv = lax.dot_general(v, k_out, TN_DIMS, preferred_element_type=F32)
        s_ref[...] = decay * st + jnp.where(state_mask, kv, 0.0)


def _gla_body(qf_ref, kf_ref, vf_ref, gf_ref, qb_ref, kb_ref, vb_ref, gb_ref, wg_ref, bg_ref,
              of_ref, ob_ref, sf_ref, sb_ref, *, chunks):
    @pl.when(pl.program_id(1) == 0)
    def _():
        sf_ref[...] = jnp.zeros_like(sf_ref)
        sb_ref[...] = jnp.zeros_like(sb_ref)

    _gla_direction(qf_ref, kf_ref, vf_ref, gf_ref, wg_ref[0], bg_ref[0], sf_ref, of_ref, True, chunks)
    _gla_direction(qb_ref, kb_ref, vb_ref, gb_ref, wg_ref[1], bg_ref[1], sb_ref, ob_ref, False, chunks)


def gla_scan(proj, bsz, seq_len, w_gate, b_gate, chunks=4):
    t = proj.shape[0]
    tb = chunks * GLA_CHUNK
    nb = seq_len // tb
    fwd = lambda cb: (lambda b, i: (b * nb + i, cb))
    bwd = lambda cb: (lambda b, i: (b * nb + nb - 1 - i, cb))
    qc, kc, vc, gc = 512 // GLA_QK, 768 // GLA_QK, 1024 // GLA_V, 2048 // 128
    wg = jnp.zeros((2, 128, GLA_QK), F32)
    wg = wg.at[0, :GLA_RANK].set(w_gate[0].astype(F32)).at[1, GLA_RANK:2 * GLA_RANK].set(w_gate[1].astype(F32))
    out = jax.ShapeDtypeStruct((t, GLA_V), F32)
    state = pltpu.VMEM((GLA_V, GLA_QK), F32)
    return pl.pallas_call(
        functools.partial(_gla_body, chunks=chunks),
        out_shape=(out, out),
        grid=(bsz, nb),
        in_specs=[pl.BlockSpec((tb, GLA_QK), fwd(qc)), pl.BlockSpec((tb, GLA_QK), fwd(kc)),
                  pl.BlockSpec((tb, GLA_V), fwd(vc)), pl.BlockSpec((tb, 128), fwd(gc)),
                  pl.BlockSpec((tb, GLA_QK), bwd(qc)), pl.BlockSpec((tb, GLA_QK), bwd(kc)),
                  pl.BlockSpec((tb, GLA_V), bwd(vc)), pl.BlockSpec((tb, 128), bwd(gc)),
                  pl.BlockSpec((2, 128, GLA_QK), lambda b, i: (0, 0, 0)),
                  pl.BlockSpec((2, 1, GLA_QK), lambda b, i: (0, 0, 0))],
        out_specs=(pl.BlockSpec((tb, GLA_V), fwd(0)), pl.BlockSpec((tb, GLA_V), bwd(0))),
        scratch_shapes=[state, state],
        compiler_params=_params("parallel", "arbitrary"),
        name="gla_scan",
    )(proj, proj, proj, proj, proj, proj, proj, proj, wg, b_gate.astype(F32).reshape(2, 1, GLA_QK))


def _even_out_body(x_ref, ys_ref, of_ref, ob_ref, og_ref, wglu_ref, gn_ref, wtop_ref, wbot_ref, o_ref):
    y = ys_ref[...].astype(F32)
    y = 0.5 * y * (1.0 + jnp.tanh(math.sqrt(2.0 / math.pi) * (y + 0.044715 * (y * y * y))))
    gate = jnp.dot(y.astype(BF16), wglu_ref[...], preferred_element_type=F32)
    y = y * _sigmoid(gate)
    o = of_ref[...] + ob_ref[...]
    og = og_ref[...].astype(F32)
    heads = []
    for h in range(GLA_HEADS):
        sl = slice(h * GLA_DV, (h + 1) * GLA_DV)
        heads.append(_rms(o[:, sl], gn_ref[...]))
    o = jnp.concatenate(heads, axis=-1) * (og * _sigmoid(og))
    o_ref[...] = (x_ref[...]
                  + jnp.dot(y.astype(BF16), wtop_ref[...], preferred_element_type=F32)
                  + jnp.dot(o.astype(BF16), wbot_ref[...], preferred_element_type=F32))


def even_out(x, ys, o_f, o_b, proj, w_glu, gla_norm, w_out, tm=ROW_TILE):
    t, d = x.shape
    row = lambda i: (i, 0)
    const = lambda i: (0, 0)
    return pl.pallas_call(
        _even_out_body,
        out_shape=jax.ShapeDtypeStruct((t, d), F32),
        grid=(t // tm,),
        in_specs=[pl.BlockSpec((tm, d), row), pl.BlockSpec((tm, S5_WIDTH), row),
                  pl.BlockSpec((tm, GLA_V), row), pl.BlockSpec((tm, GLA_V), row),
                  pl.BlockSpec((tm, GLA_V), lambda i: (i, 1536 // GLA_V)),
                  pl.BlockSpec((S5_WIDTH, S5_WIDTH), const), pl.BlockSpec((1, GLA_DV), const),
                  pl.BlockSpec((S5_WIDTH, d), const), pl.BlockSpec((GLA_V, d), const)],
        out_specs=pl.BlockSpec((tm, d), row),
        compiler_params=_params("parallel"),
        name="even_out",
    )(x, ys, o_f, o_b, proj, w_glu, gla_norm.astype(F32).reshape(1, GLA_DV),
      w_out[:S5_WIDTH], w_out[S5_WIDTH:])


QK_NORM_WIDTH = 256


def _seg_rms(a, gain, same_seg):
    ssq = jnp.dot((a * a).astype(BF16), same_seg, preferred_element_type=F32)
    return a * lax.rsqrt(ssq * (1.0 / DIFF_DK) + EPS) * gain


def _qkv_body(x_ref, g_ref, w_ref, qk_gain_ref, o_ref, h_ref):
    j = pl.program_id(1)

    @pl.when(j == 0)
    def _():
        h_ref[...] = _rms(x_ref[...], g_ref[...]).astype(BF16)

    acc = jnp.dot(h_ref[...], w_ref[...], preferred_element_type=F32)

    @pl.when(j < 2)
    def _():
        gain = qk_gain_ref[0]
        w = QK_NORM_WIDTH
        same_seg = jnp.where(lax.broadcasted_iota(jnp.int32, (w, w), 0) // DIFF_DK
                             == lax.broadcasted_iota(jnp.int32, (w, w), 1) // DIFF_DK, 1.0, 0.0).astype(BF16)
        for c in range(acc.shape[1] // w):
            sl = slice(c * w, (c + 1) * w)
            o_ref[:, sl] = _seg_rms(acc[:, sl], gain, same_seg).astype(o_ref.dtype)

    @pl.when(j == 2)
    def _():
        o_ref[...] = acc.astype(o_ref.dtype)


def qkv_project(x, gain, w, q_norm, k_norm, tm=ROW_TILE):
    t, d = x.shape
    reps = QK_NORM_WIDTH // DIFF_DK
    qg = jnp.tile(q_norm.astype(F32) * DIFF_DK ** -0.5, reps)
    kg = jnp.tile(k_norm.astype(F32), reps)
    qk_gain = jnp.stack([qg, kg]).reshape(2, 1, QK_NORM_WIDTH)
    return pl.pallas_call(
        _qkv_body,
        out_shape=jax.ShapeDtypeStruct((t, 3 * d), BF16),
        grid=(t // tm, 3),
        in_specs=[pl.BlockSpec((tm, d), lambda i, j: (i, 0)),
                  pl.BlockSpec((1, d), lambda i, j: (0, 0)),
                  pl.BlockSpec((d, d), lambda i, j: (0, j)),
                  pl.BlockSpec((1, 1, QK_NORM_WIDTH), lambda i, j: (jnp.minimum(j, 1), 0, 0))],
        out_specs=pl.BlockSpec((tm, d), lambda i, j: (i, j)),
        scratch_shapes=[pltpu.VMEM((tm, d), BF16)],
        compiler_params=_params("parallel", "arbitrary"),
        name="qkv_project",
    )(x, gain.reshape(1, d), w, qk_gain)


POS_SPLIT = 16
POS_SHIFT = 4
AUG_LANE = DIFF_DK
SOFTMAX_GROUP_ELEMS = 32 * 1024


def _pos_terms(shape):
    lane = lax.broadcasted_iota(jnp.int32, shape, 1)
    pos = lax.broadcasted_iota(jnp.int32, shape, 0)
    hi = lax.shift_right_logical(pos, POS_SHIFT).astype(F32)
    lo = (pos & (POS_SPLIT - 1)).astype(F32)
    return lane, hi, lo


def _lane_select(lane, first, values):
    out = 0.0
    for n, val in reversed(list(enumerate(values))):
        out = jnp.where(lane == first + n, val, out)
    return out


def _diff_attn_body(slope_ref, q_ref, k_ref, v_ref, lq1_ref, lk1_ref, lq2_ref, lk2_ref, sub_ref,
                    o_ref, ka_ref, va_ref, qs_ref, s0_ref, s1_ref, p0_ref, p1_ref, acc0_ref, acc1_ref,
                    *, blk, seq_len, lambda_init):
    h = pl.program_id(1)
    qi = pl.program_id(2)
    slope = slope_ref[h]
    nk = seq_len // blk
    s_refs, p_refs, acc_refs = (s0_ref, s1_ref), (p0_ref, p1_ref), (acc0_ref, acc1_ref)

    @pl.when(qi == 0)
    def _():
        va_ref[:, :DIFF_DV] = v_ref[...]
        va_ref[:, DIFF_DV:] = jnp.ones((seq_len, DIFF_DV), BF16)

        def build(t, carry):
            rows = pl.ds(pl.multiple_of(t * blk, blk), blk)
            kf = k_ref[rows, :].astype(F32)
            lane, hi, lo = _pos_terms(kf.shape)
            aug = _lane_select(lane, AUG_LANE,
                               [-blk * slope, -POS_SPLIT * slope, -slope,
                                (blk * slope) * jnp.asarray(t, F32), (POS_SPLIT * slope) * hi, slope * lo])
            for z, kz in enumerate((kf, pltpu.roll(kf, DIFF_DK, 1))):
                ka_ref[z, rows, :] = jnp.where(lane < DIFF_DK, kz, aug).astype(BF16)
            return carry

        lax.fori_loop(0, nk, build, 0)

    qf = q_ref[...].astype(F32)
    lane, hi, lo = _pos_terms(qf.shape)
    qa = _lane_select(lane, AUG_LANE, [jnp.asarray(qi, F32), hi, lo, 1.0, 1.0, 1.0])
    for z, qz in enumerate((qf, pltpu.roll(qf, DIFF_DK, 1))):
        qs_ref[z] = jnp.where(lane < DIFF_DK, qz, 0.0).astype(BF16)
        qs_ref[2 + z] = jnp.where(lane < DIFF_DK, qz, qa).astype(BF16)
        qs_ref[4 + z] = jnp.where(lane < DIFF_DK, qz, -qa).astype(BF16)

    def key_block(t):
        if t == 0:
            return qi, 0
        j = (t - 1) + jnp.asarray(qi <= t - 1, jnp.int32)
        return j, jnp.where(j < qi, 2, 4)

    def key_rows(t):
        j, _ = key_block(t)
        return pl.ds(pl.multiple_of(j * blk, blk), blk)

    for z, s_ref in enumerate(s_refs):
        for t in range(nk):
            _, variant = key_block(t)
            s_ref[:, t * blk:(t + 1) * blk] = lax.dot_general(
                qs_ref[variant + z], ka_ref[z, key_rows(t), :], NT_DIMS, preferred_element_type=F32)

    rg = SOFTMAX_GROUP_ELEMS // seq_len
    rel = (lax.broadcasted_iota(jnp.int32, (rg, blk), 0) - lax.broadcasted_iota(jnp.int32, (rg, blk), 1))
    for s_ref, p_ref, acc_ref in zip(s_refs, p_refs, acc_refs):
        for g in range(blk // rg):
            rows = slice(g * rg, (g + 1) * rg)
            s_diag = s_ref[rows, :blk] - slope * jnp.abs(rel + g * rg).astype(F32)
            s_rest = s_ref[rows, blk:]
            m = jnp.maximum(jnp.max(s_diag, axis=-1, keepdims=True), jnp.max(s_rest, axis=-1, keepdims=True))
            p_ref[rows, :blk] = jnp.exp(s_diag - m).astype(BF16)
            p_ref[rows, blk:] = jnp.exp(s_rest - m).astype(BF16)
        for t in range(nk):
            part = jnp.dot(p_ref[:, t * blk:(t + 1) * blk], va_ref[key_rows(t), :],
                           preferred_element_type=F32)
            if t == 0:
                acc_ref[...] = part
            else:
                acc_ref[...] += part

    lam = (jnp.exp(jnp.sum(lq1_ref[...] * lk1_ref[...], axis=-1, keepdims=True))
           - jnp.exp(jnp.sum(lq2_ref[...] * lk2_ref[...], axis=-1, keepdims=True)) + lambda_init)
    a0 = acc_refs[0][...]
    a1 = acc_refs[1][...]
    o = (a0[:, :DIFF_DV] / a0[:, DIFF_DV:DIFF_DV + 1]
         - lam * (a1[:, :DIFF_DV] / a1[:, DIFF_DV:DIFF_DV + 1]))
    o_ref[...] = (_rms(o, sub_ref[...]) * (1.0 - lambda_init)).astype(o_ref.dtype)


def diff_attention(qkv, bsz, seq_len, lq1, lk1, lq2, lk2, sub_norm, lambda_init, blk=512):
    t = qkv.shape[0]
    nq = seq_len // blk
    assert blk // POS_SPLIT <= 256, "hi part of a block position must stay exact in bf16"
    slopes = jnp.asarray(2.0 ** (-8.0 * np.arange(1, DIFF_HEADS + 1, dtype=np.float32) / DIFF_HEADS), F32)
    vec = lambda a: a.astype(F32).reshape(1, DIFF_DK)
    const = lambda b, h, i: (0, 0)
    return pl.pallas_call(
        functools.partial(_diff_attn_body, blk=blk, seq_len=seq_len, lambda_init=lambda_init),
        out_shape=jax.ShapeDtypeStruct((t, DIFF_HEADS * DIFF_DV), BF16),
        grid=(bsz, DIFF_HEADS, nq),
        in_specs=[pl.BlockSpec(memory_space=pltpu.SMEM),
                  pl.BlockSpec((blk, 2 * DIFF_DK), lambda b, h, i: (b * nq + i, h)),
                  pl.BlockSpec((seq_len, 2 * DIFF_DK), lambda b, h, i: (b, DIFF_HEADS + h)),
                  pl.BlockSpec((seq_len, DIFF_DV), lambda b, h, i: (b, 2 * DIFF_HEADS + h)),
                  pl.BlockSpec((1, DIFF_DK), const), pl.BlockSpec((1, DIFF_DK), const),
                  pl.BlockSpec((1, DIFF_DK), const), pl.BlockSpec((1, DIFF_DK), const),
                  pl.BlockSpec((1, DIFF_DV), const)],
        out_specs=pl.BlockSpec((blk, DIFF_DV), lambda b, h, i: (b * nq + i, h)),
        scratch_shapes=[pltpu.VMEM((2, seq_len, 2 * DIFF_DK), BF16),
                        pltpu.VMEM((seq_len, 2 * DIFF_DV), BF16),
                        pltpu.VMEM((6, blk, 2 * DIFF_DK), BF16),
                        pltpu.VMEM((blk, seq_len), F32),
                        pltpu.VMEM((blk, seq_len), F32),
                        pltpu.VMEM((blk, seq_len), BF16),
                        pltpu.VMEM((blk, seq_len), BF16),
                        pltpu.VMEM((blk, 2 * DIFF_DV), F32),
                        pltpu.VMEM((blk, 2 * DIFF_DV), F32)],
        compiler_params=_params("parallel", "parallel", "arbitrary"),
        name="diff_attention",
    )(slopes, qkv, qkv, qkv, vec(lq1), vec(lk1), vec(lq2), vec(lk2),
      sub_norm.astype(F32).reshape(1, DIFF_DV))


def _pad_ev_w_in(w):
    return jnp.pad(w, ((0, 0), (0, EV_PAD_COLS - w.shape[1]))).astype(BF16)


def _trunk(x3, mem, p):
    bsz, seq_len, d = x3.shape
    x = x3.reshape(bsz * seq_len, d)
    kn_all, v_all = mem_kv(mem, p['norm_mem'], p['x_w_kv'], p['x_k_norm'])
    for layer in range(DEPTH):
        if layer % 2 == 0:
            e = layer // 2
            proj = norm_matmul(x, p['norm_mix'][layer], p['ev_w_in'][e], BF16, EV_TN)
            ys = s5_scan(proj[:, :S5_WIDTH], bsz, seq_len, p['s5_ops'][e])
            o_f, o_b = gla_scan(proj, bsz, seq_len, p['gla_w_gate'][e], p['gla_b_gate'][e])
            x = even_out(x, ys, o_f, o_b, proj, p['s5_w_glu'][e], p['gla_norm'][e], p['ev_w_out'][e])
        else:
            o = layer // 2
            lambda_init = 0.8 - 0.6 * math.exp(-0.3 * layer)
            qkv = qkv_project(x, p['norm_mix'][layer], p['od_w_in'][o],
                              p['diff_q_norm'][o], p['diff_k_norm'][o])
            att = diff_attention(qkv, bsz, seq_len, p['diff_lambda_q1'][o], p['diff_lambda_k1'][o],
                                 p['diff_lambda_q2'][o], p['diff_lambda_k2'][o], p['diff_norm'][o],
                                 lambda_init)
            x = matmul_residual(att, p['od_w_out'][o], x)
        x = cross_block(x, seq_len, p['norm_cross'][layer], p['x_w_q'][layer], p['x_q_norm'][layer],
                        kn_all[layer], v_all[layer], p['x_w_o'][layer])
        x = mlp_block(x, p['norm_mlp'][layer], p['mlp_w1'][layer], p['mlp_w2'][layer])
    return x.reshape(bsz, seq_len, d)


def kernel(x_prompt, x_sample, mem_prompt, mem_sample, norm_mix, norm_cross, norm_mem, norm_mlp,
           ev_w_in, ev_w_out, s5_lambda_re, s5_lambda_im, s5_log_step, s5_b_re, s5_b_im,
           s5_c_re, s5_c_im, s5_d, s5_w_glu, gla_w_gate, gla_b_gate, gla_norm,
           od_w_in, od_w_out, diff_q_norm, diff_k_norm, diff_lambda_q1, diff_lambda_k1,
           diff_lambda_q2, diff_lambda_k2, diff_norm, x_w_q, x_w_kv, x_w_o, x_q_norm, x_k_norm,
           mlp_w1, mlp_w2):
    n_even = ev_w_in.shape[0]
    bf = lambda w: w.astype(BF16)
    p = dict(
        norm_mix=norm_mix, norm_cross=norm_cross, norm_mem=norm_mem, norm_mlp=norm_mlp,
        ev_w_in=[_pad_ev_w_in(ev_w_in[e]) for e in range(n_even)], ev_w_out=bf(ev_w_out),
        s5_ops=[s5_operators(s5_lambda_re[e], s5_lambda_im[e], s5_log_step[e], s5_b_re[e], s5_b_im[e],
                             s5_c_re[e], s5_c_im[e], s5_d[e]) for e in range(n_even)],
        s5_w_glu=bf(s5_w_glu), gla_w_gate=gla_w_gate, gla_b_gate=gla_b_gate, gla_norm=gla_norm,
        od_w_in=bf(od_w_in), od_w_out=bf(od_w_out), diff_q_norm=diff_q_norm, diff_k_norm=diff_k_norm,
        diff_lambda_q1=diff_lambda_q1, diff_lambda_k1=diff_lambda_k1,
        diff_lambda_q2=diff_lambda_q2, diff_lambda_k2=diff_lambda_k2, diff_norm=diff_norm,
        x_w_q=bf(x_w_q), x_w_kv=bf(x_w_kv), x_w_o=bf(x_w_o), x_q_norm=x_q_norm, x_k_norm=x_k_norm,
        mlp_w1=bf(mlp_w1), mlp_w2=bf(mlp_w2))
    return (_trunk(x_prompt, mem_prompt, p), _trunk(x_sample, mem_sample, p))
```

```python
import functools
import math

import numpy as np
import jax
import jax.numpy as jnp
from jax import lax
from jax.experimental import pallas as pl
from jax.experimental.pallas import tpu as pltpu

F32 = jnp.float32
BF16 = jnp.bfloat16
HIGHEST = lax.Precision.HIGHEST

D_MODEL = 1024
DEPTH = 4
EPS = 1e-6
S5_WIDTH = 512
S5_GROUP = 16
S5_GROUPS = 32
S5_STATE = 64
S5_CHUNK = 64
GLA_HEADS = 4
GLA_DV = 128
GLA_DK = 64
GLA_RANK = 16
GLA_TAU = 16.0
GLA_CHUNK = 64
GLA_QK = GLA_HEADS * GLA_DK
GLA_V = GLA_HEADS * GLA_DV
EV_PAD_COLS = 2304
EV_TN = 768
DIFF_HEADS = 8
DIFF_DK = 64
DIFF_DV = 128
X_HEADS = 4
X_DH = 256
D_FF = 4096

ROW_TILE = 512
VMEM_LIMIT = 48 * 1024 * 1024

NT_DIMS = (((1,), (1,)), ((), ()))
TN_DIMS = (((0,), (0,)), ((), ()))


def _params(*sem, flags=None):
    return pltpu.CompilerParams(dimension_semantics=sem, vmem_limit_bytes=VMEM_LIMIT, flags=flags)


def _rms(x, gain):
    ms = jnp.mean(x * x, axis=-1, keepdims=True)
    return x * lax.rsqrt(ms + EPS) * gain


def _sigmoid(x):
    return 1.0 / (1.0 + jnp.exp(-x))


def _norm_matmul_body(x_ref, g_ref, w_ref, o_ref, h_ref):
    @pl.when(pl.program_id(1) == 0)
    def _():
        h_ref[...] = _rms(x_ref[...], g_ref[...]).astype(BF16)

    o_ref[...] = jnp.dot(h_ref[...], w_ref[...], preferred_element_type=F32).astype(o_ref.dtype)


def norm_matmul(x, gain, w, out_dtype, tn, tm=ROW_TILE):
    t, d = x.shape
    n = w.shape[1]
    return pl.pallas_call(
        _norm_matmul_body,
        out_shape=jax.ShapeDtypeStruct((t, n), out_dtype),
        grid=(t // tm, n // tn),
        in_specs=[pl.BlockSpec((tm, d), lambda i, j: (i, 0)),
                  pl.BlockSpec((1, d), lambda i, j: (0, 0)),
                  pl.BlockSpec((d, tn), lambda i, j: (0, j))],
        out_specs=pl.BlockSpec((tm, tn), lambda i, j: (i, j)),
        scratch_shapes=[pltpu.VMEM((tm, d), BF16)],
        compiler_params=_params("parallel", "arbitrary"),
        name="norm_matmul",
    )(x, gain.reshape(1, d), w)


def _matmul_res_body(a_ref, w_ref, r_ref, o_ref):
    o_ref[...] = r_ref[...] + jnp.dot(a_ref[...].astype(BF16), w_ref[...], preferred_element_type=F32)


def matmul_residual(a, w, res, tn=512, tm=ROW_TILE):
    t, k = a.shape
    n = w.shape[1]
    return pl.pallas_call(
        _matmul_res_body,
        out_shape=jax.ShapeDtypeStruct((t, n), F32),
        grid=(t // tm, n // tn),
        in_specs=[pl.BlockSpec((tm, k), lambda i, j: (i, 0)),
                  pl.BlockSpec((k, tn), lambda i, j: (0, j)),
                  pl.BlockSpec((tm, tn), lambda i, j: (i, j))],
        out_specs=pl.BlockSpec((tm, tn), lambda i, j: (i, j)),
        compiler_params=_params("parallel", "arbitrary"),
        name="matmul_residual",
    )(a, w, res)


def _mlp_body(x_ref, g_ref, w1_ref, w2_ref, o_ref, h_ref, acc_ref):
    f = pl.program_id(1)

    @pl.when(f == 0)
    def _():
        h_ref[...] = _rms(x_ref[...], g_ref[...]).astype(BF16)
        acc_ref[...] = jnp.zeros_like(acc_ref)

    hid = jnp.dot(h_ref[...], w1_ref[...], preferred_element_type=F32)
    hid = jnp.square(jnp.maximum(hid, 0.0)).astype(BF16)
    acc_ref[...] += jnp.dot(hid, w2_ref[...], preferred_element_type=F32)

    @pl.when(f == pl.num_programs(1) - 1)
    def _():
        o_ref[...] = x_ref[...] + acc_ref[...]


def mlp_block(x, gain, w1, w2, tf=512, tm=2 * ROW_TILE):
    t, d = x.shape
    ff = w1.shape[1]
    return pl.pallas_call(
        _mlp_body,
        out_shape=jax.ShapeDtypeStruct((t, d), F32),
        grid=(t // tm, ff // tf),
        in_specs=[pl.BlockSpec((tm, d), lambda i, f: (i, 0)),
                  pl.BlockSpec((1, d), lambda i, f: (0, 0)),
                  pl.BlockSpec((d, tf), lambda i, f: (0, f)),
                  pl.BlockSpec((tf, d), lambda i, f: (f, 0))],
        out_specs=pl.BlockSpec((tm, d), lambda i, f: (i, 0)),
        scratch_shapes=[pltpu.VMEM((tm, d), BF16), pltpu.VMEM((tm, d), F32)],
        compiler_params=_params("parallel", "arbitrary"),
        name="mlp_block",
    )(x, gain.reshape(1, d), w1, w2)


def _mem_kv_body(m_ref, g_ref, w_ref, kg_ref, k_ref, v_ref):
    h = _rms(m_ref[0], g_ref[0]).astype(BF16)
    kv = jnp.dot(h, w_ref[0], preferred_element_type=F32)
    for hd in range(X_HEADS):
        sl = slice(hd * X_DH, (hd + 1) * X_DH)
        k_ref[0, 0, :, sl] = _rms(kv[:, sl], kg_ref[0]).astype(BF16)
    v_ref[0, 0] = kv[:, D_MODEL:].astype(BF16)


def mem_kv(mem, norm_mem, w_kv, k_norm):
    bm, nm, d = mem.shape
    out = jax.ShapeDtypeStruct((DEPTH, bm, nm, d), BF16)
    return pl.pallas_call(
        _mem_kv_body,
        out_shape=(out, out),
        grid=(DEPTH, bm),
        in_specs=[pl.BlockSpec((1, nm, d), lambda l, b: (b, 0, 0)),
                  pl.BlockSpec((1, 1, d), lambda l, b: (l, 0, 0)),
                  pl.BlockSpec((1, d, 2 * d), lambda l, b: (l, 0, 0)),
                  pl.BlockSpec((1, 1, X_DH), lambda l, b: (l, 0, 0))],
        out_specs=(pl.BlockSpec((1, 1, nm, d), lambda l, b: (l, b, 0, 0)),
                   pl.BlockSpec((1, 1, nm, d), lambda l, b: (l, b, 0, 0))),
        compiler_params=_params("arbitrary", "arbitrary"),
        name="mem_kv",
    )(mem, norm_mem.reshape(DEPTH, 1, d), w_kv, k_norm.reshape(DEPTH, 1, X_DH))


def _cross_body(x_ref, g_ref, wq_ref, qg_ref, k_ref, v_ref, wo_ref, o_ref):
    x = x_ref[...]
    h = _rms(x, g_ref[...]).astype(BF16)
    q = jnp.dot(h, wq_ref[...], preferred_element_type=F32)
    heads = []
    for hd in range(X_HEADS):
        sl = slice(hd * X_DH, (hd + 1) * X_DH)
        qn = _rms(q[:, sl], qg_ref[...]).astype(BF16)
        s = lax.dot_general(qn, k_ref[0, :, sl], NT_DIMS, preferred_element_type=F32)
        p = jnp.exp(s - jnp.max(s, axis=-1, keepdims=True))
        l = jnp.sum(p, axis=-1, keepdims=True)
        oh = jnp.dot(p.astype(BF16), v_ref[0, :, sl], preferred_element_type=F32) / l
        heads.append(oh.astype(BF16))
    o = jnp.concatenate(heads, axis=-1)
    o_ref[...] = x + jnp.dot(o, wo_ref[...], preferred_element_type=F32)


def cross_block(x, seq_len, gain, w_q, q_gain, kn, v, w_o, tm=ROW_TILE):
    t, d = x.shape
    nm = kn.shape[1]
    per_seq = seq_len // tm
    return pl.pallas_call(
        _cross_body,
        out_shape=jax.ShapeDtypeStruct((t, d), F32),
        grid=(t // tm,),
        in_specs=[pl.BlockSpec((tm, d), lambda i: (i, 0)),
                  pl.BlockSpec((1, d), lambda i: (0, 0)),
                  pl.BlockSpec((d, d), lambda i: (0, 0)),
                  pl.BlockSpec((1, X_DH), lambda i: (0, 0)),
                  pl.BlockSpec((1, nm, d), lambda i: (i // per_seq, 0, 0)),
                  pl.BlockSpec((1, nm, d), lambda i: (i // per_seq, 0, 0)),
                  pl.BlockSpec((d, d), lambda i: (0, 0))],
        out_specs=pl.BlockSpec((tm, d), lambda i: (i, 0)),
        compiler_params=_params("parallel"),
        name="cross_block",
    )(x, gain.reshape(1, d), w_q, (q_gain * X_DH ** -0.5).reshape(1, X_DH), kn, v, w_o)


LANES = 128


def _toeplitz_body(k_ref, o_ref):
    lc, gs = S5_CHUNK, S5_GROUP
    kern = k_ref[0]
    n = kern.shape[1]
    per_tile = LANES // gs
    for r in range(per_tile):
        shifted = pltpu.roll(kern, n - gs * r, 1) if r else kern
        for a in range(lc // per_tile):
            s = lc - 1 - (per_tile * a + r)
            o_ref[0, s * gs:(s + 1) * gs, :] = shifted[:, LANES * a:LANES * a + lc * gs].astype(o_ref.dtype)


def toeplitz_expand(kern_flat):
    g, gs, n = kern_flat.shape
    width = S5_CHUNK * S5_GROUP
    return pl.pallas_call(
        _toeplitz_body,
        out_shape=jax.ShapeDtypeStruct((g, width, width), BF16),
        grid=(g,),
        in_specs=[pl.BlockSpec((1, gs, n), lambda i: (i, 0, 0))],
        out_specs=pl.BlockSpec((1, width, width), lambda i: (i, 0, 0)),
        compiler_params=_params("parallel"),
        name="toeplitz_expand",
    )(kern_flat)


def s5_operators(lam_re, lam_im, log_step, b_re, b_im, c_re, c_im, d):
    lc = S5_CHUNK
    lam = lax.complex(lam_re.astype(F32), lam_im.astype(F32))
    step = jnp.exp(log_step.astype(F32))[..., None]
    lam_bar = jnp.exp(lam * step)
    b_bar = ((lam_bar - 1.0) / lam)[..., None] * lax.complex(b_re.astype(F32), b_im.astype(F32))
    c = lax.complex(c_re.astype(F32), c_im.astype(F32))
    pw = jnp.cumprod(jnp.broadcast_to(lam_bar[..., None], lam_bar.shape + (lc,)), axis=-1)
    pw = jnp.concatenate([jnp.ones_like(pw[..., :1]), pw], axis=-1)
    kern = jnp.einsum('zgcp,zgpt,zgpd->zgtcd', c, pw[..., :lc], b_bar, precision=HIGHEST).real
    kf, kb = kern[0], kern[1]
    k0 = kf[:, :1] + kb[:, :1] + (d.astype(F32)[:, :, None] * jnp.eye(S5_GROUP, dtype=F32))[:, None]
    kern_full = jnp.concatenate([kb[:, :0:-1], k0, kf[:, 1:]], axis=1)
    kern_flat = kern_full.transpose(0, 3, 1, 2).reshape(S5_GROUPS, S5_GROUP, (2 * lc - 1) * S5_GROUP)
    toep = toeplitz_expand(jnp.pad(kern_flat, ((0, 0), (0, 0), (0, S5_GROUP))))

    pf = jnp.einsum('gps,gpd->gsdp', pw[0][..., lc - 1::-1], b_bar[0])
    pb = jnp.einsum('gps,gpd->gsdp', pw[1][..., :lc], b_bar[1])
    p_op = jnp.concatenate([pf.real, pf.imag, pb.real, pb.imag], axis=-1)
    p_op = p_op.reshape(S5_GROUPS, lc * S5_GROUP, 4 * S5_STATE)

    qf = jnp.einsum('gcp,gpt->gptc', c[0], pw[0][..., 1:])
    qb = jnp.einsum('gcp,gpt->gptc', c[1], pw[1][..., :0:-1])
    q_op = jnp.concatenate([qf.real, -qf.imag, qb.real, -qb.imag], axis=1)
    q_op = q_op.reshape(S5_GROUPS, 4 * S5_STATE, lc * S5_GROUP)

    a = pw[..., lc]
    coef = jnp.stack([jnp.concatenate([a[0].real, a[0].real], -1),
                      jnp.concatenate([-a[0].imag, a[0].imag], -1),
                      jnp.concatenate([a[1].real, a[1].real], -1),
                      jnp.concatenate([-a[1].imag, a[1].imag], -1)])
    return toep, p_op, q_op, coef


def _s5_state_in_body(u_ref, p_ref, v_ref):
    v_ref[0] = jnp.dot(u_ref[0].astype(F32), p_ref[0], preferred_element_type=F32, precision=HIGHEST)


def _s5_scan_body(vf_ref, vb_ref, cf_ref, xf_ref, xb_ref):
    n = vf_ref.shape[0]
    c1f, c2f, c1b, c2b = cf_ref[0], cf_ref[1], cf_ref[2], cf_ref[3]

    def step(i, carry):
        xf, xb = carry
        xf_ref[i] = xf
        xf = xf * c1f + pltpu.roll(xf, S5_STATE, 1) * c2f + vf_ref[i]
        j = n - 1 - i
        xb_ref[j] = xb
        xb = xb * c1b + pltpu.roll(xb, S5_STATE, 1) * c2b + vb_ref[j]
        return xf, xb

    zero = jnp.zeros(vf_ref.shape[1:], F32)
    lax.fori_loop(0, n, step, (zero, zero))


def _s5_out_body(u_ref, t_ref, x_ref, q_ref, y_ref):
    y = jnp.dot(u_ref[0], t_ref[0], preferred_element_type=F32)
    y = y + jnp.dot(x_ref[0], q_ref[0], preferred_element_type=F32, precision=HIGHEST)
    y_ref[0] = y.astype(y_ref.dtype)


def s5_scan(u, bsz, seq_len, ops):
    toep, p_op, q_op, coef = ops
    lc, g, w = S5_CHUNK, S5_GROUPS, S5_CHUNK * S5_GROUP
    n = seq_len // lc
    c = bsz * n
    ns = 4 * S5_STATE
    ug = u.reshape(bsz, n, lc, g, S5_GROUP).transpose(3, 0, 1, 2, 4).reshape(g, c, w)
    v = pl.pallas_call(
        _s5_state_in_body,
        out_shape=jax.ShapeDtypeStruct((g, c, ns), F32),
        grid=(g,),
        in_specs=[pl.BlockSpec((1, c, w), lambda i: (i, 0, 0)),
                  pl.BlockSpec((1, w, ns), lambda i: (i, 0, 0))],
        out_specs=pl.BlockSpec((1, c, ns), lambda i: (i, 0, 0)),
        compiler_params=_params("parallel"),
        name="s5_state_in",
    )(ug, p_op)
    v = v.reshape(g, bsz, n, ns).transpose(2, 1, 0, 3).reshape(n, bsz * g, ns)
    half = ns // 2
    cf = jnp.tile(coef, (1, bsz, 1))
    xf, xb = pl.pallas_call(
        _s5_scan_body,
        out_shape=(jax.ShapeDtypeStruct((n, bsz * g, half), F32),) * 2,
        compiler_params=pltpu.CompilerParams(vmem_limit_bytes=VMEM_LIMIT),
        name="s5_chunk_scan",
    )(v[..., :half], v[..., half:], cf)
    x = jnp.concatenate([xf, xb], axis=-1).reshape(n, bsz, g, ns).transpose(2, 1, 0, 3).reshape(g, c, ns)
    y = pl.pallas_call(
        _s5_out_body,
        out_shape=jax.ShapeDtypeStruct((g, c, w), BF16),
        grid=(g,),
        in_specs=[pl.BlockSpec((1, c, w), lambda i: (i, 0, 0)),
                  pl.BlockSpec((1, w, w), lambda i: (i, 0, 0)),
                  pl.BlockSpec((1, c, ns), lambda i: (i, 0, 0)),
                  pl.BlockSpec((1, ns, w), lambda i: (i, 0, 0))],
        out_specs=pl.BlockSpec((1, c, w), lambda i: (i, 0, 0)),
        compiler_params=_params("parallel"),
        name="s5_out",
    )(ug, toep, x, q_op)
    return y.reshape(g, bsz, n, lc, S5_GROUP).transpose(1, 2, 3, 0, 4).reshape(bsz * seq_len, S5_WIDTH)


def _gla_direction(q_ref, k_ref, v_ref, g_ref, wg, bg, s_ref, o_ref, forward, chunks):
    cs = GLA_CHUNK
    tb = chunks * cs
    logit = jnp.dot(g_ref[...].astype(F32), wg, preferred_element_type=F32, precision=HIGHEST) + bg
    g = (jnp.minimum(logit, 0.0) - jnp.log1p(jnp.exp(-jnp.abs(logit)))) / GLA_TAU
    row = lax.broadcasted_iota(jnp.int32, (tb, tb), 0)
    col = lax.broadcasted_iota(jnp.int32, (tb, tb), 1)
    same_chunk = (row // cs) == (col // cs)
    within = same_chunk & ((col <= row) if forward else (col >= row))
    bcum = jnp.dot(within.astype(F32), g, preferred_element_type=F32, precision=HIGHEST)

    srow = lax.broadcasted_iota(jnp.int32, (GLA_HEADS * cs, cs), 0) & (cs - 1)
    scol = lax.broadcasted_iota(jnp.int32, (GLA_HEADS * cs, cs), 1)
    keep = (scol <= srow) if forward else (scol >= srow)
    lane = lax.broadcasted_iota(jnp.int32, (1, GLA_QK), 1)
    head_lanes = [((lane >= h * GLA_DK) & (lane < (h + 1) * GLA_DK)).astype(F32) for h in range(GLA_HEADS)]
    state_mask = (lax.broadcasted_iota(jnp.int32, (GLA_V, GLA_QK), 0) // GLA_DV
                  == lax.broadcasted_iota(jnp.int32, (GLA_V, GLA_QK), 1) // GLA_DK)

    i_ref = cs // 2 - 1 if forward else cs // 2
    i_last = cs - 1 if forward else 0
    scale = GLA_DK ** -0.5
    for ci in (range(chunks) if forward else reversed(range(chunks))):
        rows = slice(ci * cs, (ci + 1) * cs)
        b = bcum[rows]
        bref = b[i_ref:i_ref + 1]
        blast = b[i_last:i_last + 1]
        q = q_ref[rows, :].astype(F32) * scale
        k = k_ref[rows, :].astype(F32)
        v = v_ref[rows, :]
        q_rel = q * jnp.exp(b - bref)
        k_rel = (k * jnp.exp(bref - b)).astype(BF16)
        k_out = (k * jnp.exp(blast - b)).astype(BF16)
        q_dec = (q * jnp.exp(b)).astype(BF16)
        decay = jnp.exp(blast)
        q_heads = jnp.concatenate([q_rel * hm for hm in head_lanes], axis=0).astype(BF16)
        s = lax.dot_general(q_heads, k_rel, NT_DIMS, preferred_element_type=F32)
        s = jnp.where(keep, s, 0.0).astype(BF16)
        st = s_ref[...]
        o_inter = lax.dot_general(q_dec, st.astype(BF16), NT_DIMS, preferred_element_type=F32)
        o_intra = [jnp.dot(s[h * cs:(h + 1) * cs], v[:, h * GLA_DV:(h + 1) * GLA_DV],
                           preferred_element_type=F32) for h in range(GLA_HEADS)]
        o_ref[rows, :] = o_inter + jnp.concatenate(o_intra, axis=-1)
        kv = lax.dot_general(v, k_out, TN_DIMS, preferred_element_type=F32)
        s_ref[...] = decay * st + jnp.where(state_mask, kv, 0.0)


def _gla_body(qf_ref, kf_ref, vf_ref, gf_ref, qb_ref, kb_ref, vb_ref, gb_ref, wg_ref, bg_ref,
              of_ref, ob_ref, sf_ref, sb_ref, *, chunks):
    @pl.when(pl.program_id(1) == 0)
    def _():
        sf_ref[...] = jnp.zeros_like(sf_ref)
        sb_ref[...] = jnp.zeros_like(sb_ref)

    _gla_direction(qf_ref, kf_ref, vf_ref, gf_ref, wg_ref[0], bg_ref[0], sf_ref, of_ref, True, chunks)
    _gla_direction(qb_ref, kb_ref, vb_ref, gb_ref, wg_ref[1], bg_ref[1], sb_ref, ob_ref, False, chunks)


def gla_scan(proj, bsz, seq_len, w_gate, b_gate, chunks=4):
    t = proj.shape[0]
    tb = chunks * GLA_CHUNK
    nb = seq_len // tb
    fwd = lambda cb: (lambda b, i: (b * nb + i, cb))
    bwd = lambda cb: (lambda b, i: (b * nb + nb - 1 - i, cb))
    qc, kc, vc, gc = 512 // GLA_QK, 768 // GLA_QK, 1024 // GLA_V, 2048 // 128
    wg = jnp.zeros((2, 128, GLA_QK), F32)
    wg = wg.at[0, :GLA_RANK].set(w_gate[0].astype(F32)).at[1, GLA_RANK:2 * GLA_RANK].set(w_gate[1].astype(F32))
    out = jax.ShapeDtypeStruct((t, GLA_V), F32)
    state = pltpu.VMEM((GLA_V, GLA_QK), F32)
    return pl.pallas_call(
        functools.partial(_gla_body, chunks=chunks),
        out_shape=(out, out),
        grid=(bsz, nb),
        in_specs=[pl.BlockSpec((tb, GLA_QK), fwd(qc)), pl.BlockSpec((tb, GLA_QK), fwd(kc)),
                  pl.BlockSpec((tb, GLA_V), fwd(vc)), pl.BlockSpec((tb, 128), fwd(gc)),
                  pl.BlockSpec((tb, GLA_QK), bwd(qc)), pl.BlockSpec((tb, GLA_QK), bwd(kc)),
                  pl.BlockSpec((tb, GLA_V), bwd(vc)), pl.BlockSpec((tb, 128), bwd(gc)),
                  pl.BlockSpec((2, 128, GLA_QK), lambda b, i: (0, 0, 0)),
                  pl.BlockSpec((2, 1, GLA_QK), lambda b, i: (0, 0, 0))],
        out_specs=(pl.BlockSpec((tb, GLA_V), fwd(0)), pl.BlockSpec((tb, GLA_V), bwd(0))),
        scratch_shapes=[state, state],
        compiler_params=_params("parallel", "arbitrary"),
        name="gla_scan",
    )(proj, proj, proj, proj, proj, proj, proj, proj, wg, b_gate.astype(F32).reshape(2, 1, GLA_QK))


def _even_out_body(x_ref, ys_ref, of_ref, ob_ref, og_ref, wglu_ref, gn_ref, wtop_ref, wbot_ref, o_ref):
    y = ys_ref[...].astype(F32)
    y = 0.5 * y * (1.0 + jnp.tanh(math.sqrt(2.0 / math.pi) * (y + 0.044715 * (y * y * y))))
    gate = jnp.dot(y.astype(BF16), wglu_ref[...], preferred_element_type=F32)
    y = y * _sigmoid(gate)
    o = of_ref[...] + ob_ref[...]
    og = og_ref[...].astype(F32)
    heads = []
    for h in range(GLA_HEADS):
        sl = slice(h * GLA_DV, (h + 1) * GLA_DV)
        heads.append(_rms(o[:, sl], gn_ref[...]))
    o = jnp.concatenate(heads, axis=-1) * (og * _sigmoid(og))
    o_ref[...] = (x_ref[...]
                  + jnp.dot(y.astype(BF16), wtop_ref[...], preferred_element_type=F32)
                  + jnp.dot(o.astype(BF16), wbot_ref[...], preferred_element_type=F32))


def even_out(x, ys, o_f, o_b, proj, w_glu, gla_norm, w_out, tm=ROW_TILE):
    t, d = x.shape
    row = lambda i: (i, 0)
    const = lambda i: (0, 0)
    return pl.pallas_call(
        _even_out_body,
        out_shape=jax.ShapeDtypeStruct((t, d), F32),
        grid=(t // tm,),
        in_specs=[pl.BlockSpec((tm, d), row), pl.BlockSpec((tm, S5_WIDTH), row),
                  pl.BlockSpec((tm, GLA_V), row), pl.BlockSpec((tm, GLA_V), row),
                  pl.BlockSpec((tm, GLA_V), lambda i: (i, 1536 // GLA_V)),
                  pl.BlockSpec((S5_WIDTH, S5_WIDTH), const), pl.BlockSpec((1, GLA_DV), const),
                  pl.BlockSpec((S5_WIDTH, d), const), pl.BlockSpec((GLA_V, d), const)],
        out_specs=pl.BlockSpec((tm, d), row),
        compiler_params=_params("parallel"),
        name="even_out",
    )(x, ys, o_f, o_b, proj, w_glu, gla_norm.astype(F32).reshape(1, GLA_DV),
      w_out[:S5_WIDTH], w_out[S5_WIDTH:])


QK_NORM_WIDTH = 256


def _seg_rms(a, gain, same_seg):
    ssq = jnp.dot((a * a).astype(BF16), same_seg, preferred_element_type=F32)
    return a * lax.rsqrt(ssq * (1.0 / DIFF_DK) + EPS) * gain


def _qkv_body(x_ref, g_ref, w_ref, qk_gain_ref, o_ref, h_ref):
    j = pl.program_id(1)

    @pl.when(j == 0)
    def _():
        h_ref[...] = _rms(x_ref[...], g_ref[...]).astype(BF16)

    acc = jnp.dot(h_ref[...], w_ref[...], preferred_element_type=F32)

    @pl.when(j < 2)
    def _():
        gain = qk_gain_ref[0]
        w = QK_NORM_WIDTH
        same_seg = jnp.where(lax.broadcasted_iota(jnp.int32, (w, w), 0) // DIFF_DK
                             == lax.broadcasted_iota(jnp.int32, (w, w), 1) // DIFF_DK, 1.0, 0.0).astype(BF16)
        for c in range(acc.shape[1] // w):
            sl = slice(c * w, (c + 1) * w)
            o_ref[:, sl] = _seg_rms(acc[:, sl], gain, same_seg).astype(o_ref.dtype)

    @pl.when(j == 2)
    def _():
        o_ref[...] = acc.astype(o_ref.dtype)


def qkv_project(x, gain, w, q_norm, k_norm, tm=ROW_TILE):
    t, d = x.shape
    reps = QK_NORM_WIDTH // DIFF_DK
    qg = jnp.tile(q_norm.astype(F32) * DIFF_DK ** -0.5, reps)
    kg = jnp.tile(k_norm.astype(F32), reps)
    qk_gain = jnp.stack([qg, kg]).reshape(2, 1, QK_NORM_WIDTH)
    return pl.pallas_call(
        _qkv_body,
        out_shape=jax.ShapeDtypeStruct((t, 3 * d), BF16),
        grid=(t // tm, 3),
        in_specs=[pl.BlockSpec((tm, d), lambda i, j: (i, 0)),
                  pl.BlockSpec((1, d), lambda i, j: (0, 0)),
                  pl.BlockSpec((d, d), lambda i, j: (0, j)),
                  pl.BlockSpec((1, 1, QK_NORM_WIDTH), lambda i, j: (jnp.minimum(j, 1), 0, 0))],
        out_specs=pl.BlockSpec((tm, d), lambda i, j: (i, j)),
        scratch_shapes=[pltpu.VMEM((tm, d), BF16)],
        compiler_params=_params("parallel", "arbitrary"),
        name="qkv_project",
    )(x, gain.reshape(1, d), w, qk_gain)


POS_SPLIT = 16
POS_SHIFT = 4
AUG_LANE = DIFF_DK
SOFTMAX_GROUP_ELEMS = 32 * 1024
ROW_PARTS = 2


def _pos_terms(shape):
    lane = lax.broadcasted_iota(jnp.int32, shape, 1)
    pos = lax.broadcasted_iota(jnp.int32, shape, 0)
    hi = lax.shift_right_logical(pos, POS_SHIFT).astype(F32)
    lo = (pos & (POS_SPLIT - 1)).astype(F32)
    return lane, hi, lo


def _lane_select(lane, first, values):
    out = 0.0
    for n, val in reversed(list(enumerate(values))):
        out = jnp.where(lane == first + n, val, out)
    return out


def _diff_attn_body(slope_ref, q_ref, k_ref, v_ref, lq1_ref, lk1_ref, lq2_ref, lk2_ref, sub_ref,
                    o_ref, ka_ref, va_ref, qs_ref, s0_ref, s1_ref, p0_ref, p1_ref, vs_ref,
                    *, blk, seq_len, lambda_init):
    h = pl.program_id(1)
    qi = pl.program_id(2)
    slope = slope_ref[h]
    nk = seq_len // blk
    s_refs, p_refs = (s0_ref, s1_ref), (p0_ref, p1_ref)

    @pl.when(qi == 0)
    def _():
        va_ref[:, :DIFF_DV] = v_ref[...]
        va_ref[:, DIFF_DV:] = jnp.ones((seq_len, DIFF_DV), BF16)

        def build(t, carry):
            rows = pl.ds(pl.multiple_of(t * blk, blk), blk)
            kf = k_ref[rows, :].astype(F32)
            lane, hi, lo = _pos_terms(kf.shape)
            aug = _lane_select(lane, AUG_LANE,
                               [-blk * slope, -POS_SPLIT * slope, -slope,
                                (blk * slope) * jnp.asarray(t, F32), (POS_SPLIT * slope) * hi, slope * lo])
            for z, kz in enumerate((kf, pltpu.roll(kf, DIFF_DK, 1))):
                ka_ref[z, rows, :] = jnp.where(lane < DIFF_DK, kz, aug).astype(BF16)
            return carry

        lax.fori_loop(0, nk, build, 0)

    qf = q_ref[...].astype(F32)
    lane, hi, lo = _pos_terms(qf.shape)
    qa = _lane_select(lane, AUG_LANE, [jnp.asarray(qi, F32), hi, lo, 1.0, 1.0, 1.0])
    for z, qz in enumerate((qf, pltpu.roll(qf, DIFF_DK, 1))):
        qs_ref[z] = jnp.where(lane < DIFF_DK, qz, 0.0).astype(BF16)
        qs_ref[2 + z] = jnp.where(lane < DIFF_DK, qz, qa).astype(BF16)
        qs_ref[4 + z] = jnp.where(lane < DIFF_DK, qz, -qa).astype(BF16)

    def key_block(t):
        if t == 0:
            return qi, 0
        j = (t - 1) + jnp.asarray(qi <= t - 1, jnp.int32)
        return j, jnp.where(j < qi, 2, 4)

    def key_rows(t):
        j, _ = key_block(t)
        return pl.ds(pl.multiple_of(j * blk, blk), blk)

    part = blk // ROW_PARTS
    parts = [slice(r * part, (r + 1) * part) for r in range(ROW_PARTS)]
    for z, s_ref in enumerate(s_refs):
        for rows in parts:
            for t in range(nk):
                _, variant = key_block(t)
                s_ref[rows, t * blk:(t + 1) * blk] = lax.dot_general(
                    qs_ref[variant + z, rows, :], ka_ref[z, key_rows(t), :], NT_DIMS,
                    preferred_element_type=F32)

    rg = SOFTMAX_GROUP_ELEMS // seq_len
    rel = (lax.broadcasted_iota(jnp.int32, (rg, blk), 0) - lax.broadcasted_iota(jnp.int32, (rg, blk), 1))
    for t in range(nk):
        vs_ref[t * blk:(t + 1) * blk, :] = va_ref[key_rows(t), :]

    acc = []
    for s_ref, p_ref in zip(s_refs, p_refs):
        for g in range(blk // rg):
            rows = slice(g * rg, (g + 1) * rg)
            s_diag = s_ref[rows, :blk] - slope * jnp.abs(rel + g * rg).astype(F32)
            s_rest = s_ref[rows, blk:]
            m = jnp.maximum(jnp.max(s_diag, axis=-1, keepdims=True), jnp.max(s_rest, axis=-1, keepdims=True))
            p_ref[rows, :blk] = jnp.exp(s_diag - m).astype(BF16)
            p_ref[rows, blk:] = jnp.exp(s_rest - m).astype(BF16)
        acc.append([jnp.dot(p_ref[rows, :], vs_ref[...], preferred_element_type=F32) for rows in parts])

    lam = (jnp.exp(jnp.sum(lq1_ref[...] * lk1_ref[...], axis=-1, keepdims=True))
           - jnp.exp(jnp.sum(lq2_ref[...] * lk2_ref[...], axis=-1, keepdims=True)) + lambda_init)
    for rows, a0, a1 in zip(parts, *acc):
        o = (a0[:, :DIFF_DV] / a0[:, DIFF_DV:DIFF_DV + 1]
             - lam * (a1[:, :DIFF_DV] / a1[:, DIFF_DV:DIFF_DV + 1]))
        o_ref[rows, :] = (_rms(o, sub_ref[...]) * (1.0 - lambda_init)).astype(o_ref.dtype)


def diff_attention(qkv, bsz, seq_len, lq1, lk1, lq2, lk2, sub_norm, lambda_init, blk=512):
    t = qkv.shape[0]
    nq = seq_len // blk
    assert blk // POS_SPLIT <= 256, "hi part of a block position must stay exact in bf16"
    slopes = jnp.asarray(2.0 ** (-8.0 * np.arange(1, DIFF_HEADS + 1, dtype=np.float32) / DIFF_HEADS), F32)
    vec = lambda a: a.astype(F32).reshape(1, DIFF_DK)
    const = lambda b, h, i: (0, 0)
    return pl.pallas_call(
        functools.partial(_diff_attn_body, blk=blk, seq_len=seq_len, lambda_init=lambda_init),
        out_shape=jax.ShapeDtypeStruct((t, DIFF_HEADS * DIFF_DV), BF16),
        grid=(bsz, DIFF_HEADS, nq),
        in_specs=[pl.BlockSpec(memory_space=pltpu.SMEM),
                  pl.BlockSpec((blk, 2 * DIFF_DK), lambda b, h, i: (b * nq + i, h)),
                  pl.BlockSpec((seq_len, 2 * DIFF_DK), lambda b, h, i: (b, DIFF_HEADS + h)),
                  pl.BlockSpec((seq_len, DIFF_DV), lambda b, h, i: (b, 2 * DIFF_HEADS + h)),
                  pl.BlockSpec((1, DIFF_DK), const), pl.BlockSpec((1, DIFF_DK), const),
                  pl.BlockSpec((1, DIFF_DK), const), pl.BlockSpec((1, DIFF_DK), const),
                  pl.BlockSpec((1, DIFF_DV), const)],
        out_specs=pl.BlockSpec((blk, DIFF_DV), lambda b, h, i: (b * nq + i, h)),
        scratch_shapes=[pltpu.VMEM((2, seq_len, 2 * DIFF_DK), BF16),
                        pltpu.VMEM((seq_len, 2 * DIFF_DV), BF16),
                        pltpu.VMEM((6, blk, 2 * DIFF_DK), BF16),
                        pltpu.VMEM((blk, seq_len), F32),
                        pltpu.VMEM((blk, seq_len), F32),
                        pltpu.VMEM((blk, seq_len), BF16),
                        pltpu.VMEM((blk, seq_len), BF16),
                        pltpu.VMEM((seq_len, 2 * DIFF_DV), BF16)],
        compiler_params=_params("parallel", "parallel", "arbitrary"),
        name="diff_attention",
    )(slopes, qkv, qkv, qkv, vec(lq1), vec(lk1), vec(lq2), vec(lk2),
      sub_norm.astype(F32).reshape(1, DIFF_DV))


def _pad_ev_w_in(w):
    return jnp.pad(w, ((0, 0), (0, EV_PAD_COLS - w.shape[1]))).astype(BF16)


def _trunk(x3, mem, p):
    bsz, seq_len, d = x3.shape
    x = x3.reshape(bsz * seq_len, d)
    kn_all, v_all = mem_kv(mem, p['norm_mem'], p['x_w_kv'], p['x_k_norm'])
    for layer in range(DEPTH):
        if layer % 2 == 0:
            e = layer // 2
            proj = norm_matmul(x, p['norm_mix'][layer], p['ev_w_in'][e], BF16, EV_TN)
            ys = s5_scan(proj[:, :S5_WIDTH], bsz, seq_len, p['s5_ops'][e])
            o_f, o_b = gla_scan(proj, bsz, seq_len, p['gla_w_gate'][e], p['gla_b_gate'][e])
            x = even_out(x, ys, o_f, o_b, proj, p['s5_w_glu'][e], p['gla_norm'][e], p['ev_w_out'][e])
        else:
            o = layer // 2
            lambda_init = 0.8 - 0.6 * math.exp(-0.3 * layer)
            qkv = qkv_project(x, p['norm_mix'][layer], p['od_w_in'][o],
                              p['diff_q_norm'][o], p['diff_k_norm'][o])
            att = diff_attention(qkv, bsz, seq_len, p['diff_lambda_q1'][o], p['diff_lambda_k1'][o],
                                 p['diff_lambda_q2'][o], p['diff_lambda_k2'][o], p['diff_norm'][o],
                                 lambda_init)
            x = matmul_residual(att, p['od_w_out'][o], x)
        x = cross_block(x, seq_len, p['norm_cross'][layer], p['x_w_q'][layer], p['x_q_norm'][layer],
                        kn_all[layer], v_all[layer], p['x_w_o'][layer])
        x = mlp_block(x, p['norm_mlp'][layer], p['mlp_w1'][layer], p['mlp_w2'][layer])
    return x.reshape(bsz, seq_len, d)


def kernel(x_prompt, x_sample, mem_prompt, mem_sample, norm_mix, norm_cross, norm_mem, norm_mlp,
           ev_w_in, ev_w_out, s5_lambda_re, s5_lambda_im, s5_log_step, s5_b_re, s5_b_im,
           s5_c_re, s5_c_im, s5_d, s5_w_glu, gla_w_gate, gla_b_gate, gla_norm,
           od_w_in, od_w_out, diff_q_norm, diff_k_norm, diff_lambda_q1, diff_lambda_k1,
           diff_lambda_q2, diff_lambda_k2, diff_norm, x_w_q, x_w_kv, x_w_o, x_q_norm, x_k_norm,
           mlp_w1, mlp_w2):
    n_even = ev_w_in.shape[0]
    bf = lambda w: w.astype(BF16)
    p = dict(
        norm_mix=norm_mix, norm_cross=norm_cross, norm_mem=norm_mem, norm_mlp=norm_mlp,
        ev_w_in=[_pad_ev_w_in(ev_w_in[e]) for e in range(n_even)], ev_w_out=bf(ev_w_out),
        s5_ops=[s5_operators(s5_lambda_re[e], s5_lambda_im[e], s5_log_step[e], s5_b_re[e], s5_b_im[e],
                             s5_c_re[e], s5_c_im[e], s5_d[e]) for e in range(n_even)],
        s5_w_glu=bf(s5_w_glu), gla_w_gate=gla_w_gate, gla_b_gate=gla_b_gate, gla_norm=gla_norm,
        od_w_in=bf(od_w_in), od_w_out=bf(od_w_out), diff_q_norm=diff_q_norm, diff_k_norm=diff_k_norm,
        diff_lambda_q1=diff_lambda_q1, diff_lambda_k1=diff_lambda_k1,
        diff_lambda_q2=diff_lambda_q2, diff_lambda_k2=diff_lambda_k2, diff_norm=diff_norm,
        x_w_q=bf(x_w_q), x_w_kv=bf(x_w_kv), x_w_o=bf(x_w_o), x_q_norm=x_q_norm, x_k_norm=x_k_norm,
        mlp_w1=bf(mlp_w1), mlp_w2=bf(mlp_w2))
    return (_trunk(x_prompt, mem_prompt, p), _trunk(x_sample, mem_sample, p))
```

```python
import functools
import math

import numpy as np
import jax
import jax.numpy as jnp
from jax import lax
from jax.experimental import pallas as pl
from jax.experimental.pallas import tpu as pltpu

F32 = jnp.float32
BF16 = jnp.bfloat16
HIGHEST = lax.Precision.HIGHEST

D_MODEL = 1024
DEPTH = 4
EPS = 1e-6
S5_WIDTH = 512
S5_GROUP = 16
S5_GROUPS = 32
S5_STATE = 64
S5_CHUNK = 64
GLA_HEADS = 4
GLA_DV = 128
GLA_DK = 64
GLA_RANK = 16
GLA_TAU = 16.0
GLA_CHUNK = 64
GLA_QK = GLA_HEADS * GLA_DK
GLA_V = GLA_HEADS * GLA_DV
EV_PAD_COLS = 2304
EV_TN = 768
DIFF_HEADS = 8
DIFF_DK = 64
DIFF_DV = 128
X_HEADS = 4
X_DH = 256
D_FF = 4096

ROW_TILE = 512
VMEM_LIMIT = 48 * 1024 * 1024

NT_DIMS = (((1,), (1,)), ((), ()))
TN_DIMS = (((0,), (0,)), ((), ()))


def _params(*sem, flags=None):
    return pltpu.CompilerParams(dimension_semantics=sem, vmem_limit_bytes=VMEM_LIMIT, flags=flags)


def _rms(x, gain):
    ms = jnp.mean(x * x, axis=-1, keepdims=True)
    return x * lax.rsqrt(ms + EPS) * gain


def _sigmoid(x):
    return 1.0 / (1.0 + jnp.exp(-x))


def _norm_matmul_body(x_ref, g_ref, w_ref, o_ref, h_ref):
    @pl.when(pl.program_id(1) == 0)
    def _():
        h_ref[...] = _rms(x_ref[...], g_ref[...]).astype(BF16)

    o_ref[...] = jnp.dot(h_ref[...], w_ref[...], preferred_element_type=F32).astype(o_ref.dtype)


def norm_matmul(x, gain, w, out_dtype, tn, tm=ROW_TILE):
    t, d = x.shape
    n = w.shape[1]
    return pl.pallas_call(
        _norm_matmul_body,
        out_shape=jax.ShapeDtypeStruct((t, n), out_dtype),
        grid=(t // tm, n // tn),
        in_specs=[pl.BlockSpec((tm, d), lambda i, j: (i, 0)),
                  pl.BlockSpec((1, d), lambda i, j: (0, 0)),
                  pl.BlockSpec((d, tn), lambda i, j: (0, j))],
        out_specs=pl.BlockSpec((tm, tn), lambda i, j: (i, j)),
        scratch_shapes=[pltpu.VMEM((tm, d), BF16)],
        compiler_params=_params("parallel", "arbitrary"),
        name="norm_matmul",
    )(x, gain.reshape(1, d), w)


def _matmul_res_body(a_ref, w_ref, r_ref, o_ref):
    o_ref[...] = r_ref[...] + jnp.dot(a_ref[...].astype(BF16), w_ref[...], preferred_element_type=F32)


def matmul_residual(a, w, res, tn=512, tm=ROW_TILE):
    t, k = a.shape
    n = w.shape[1]
    return pl.pallas_call(
        _matmul_res_body,
        out_shape=jax.ShapeDtypeStruct((t, n), F32),
        grid=(t // tm, n // tn),
        in_specs=[pl.BlockSpec((tm, k), lambda i, j: (i, 0)),
                  pl.BlockSpec((k, tn), lambda i, j: (0, j)),
                  pl.BlockSpec((tm, tn), lambda i, j: (i, j))],
        out_specs=pl.BlockSpec((tm, tn), lambda i, j: (i, j)),
        compiler_params=_params("parallel", "arbitrary"),
        name="matmul_residual",
    )(a, w, res)


def _mlp_body(x_ref, g_ref, w1_ref, w2_ref, o_ref, h_ref, acc_ref):
    f = pl.program_id(1)

    @pl.when(f == 0)
    def _():
        h_ref[...] = _rms(x_ref[...], g_ref[...]).astype(BF16)
        acc_ref[...] = jnp.zeros_like(acc_ref)

    hid = jnp.dot(h_ref[...], w1_ref[...], preferred_element_type=F32)
    hid = jnp.square(jnp.maximum(hid, 0.0)).astype(BF16)
    acc_ref[...] += jnp.dot(hid, w2_ref[...], preferred_element_type=F32)

    @pl.when(f == pl.num_programs(1) - 1)
    def _():
        o_ref[...] = x_ref[...] + acc_ref[...]


def mlp_block(x, gain, w1, w2, tf=512, tm=2 * ROW_TILE):
    t, d = x.shape
    ff = w1.shape[1]
    return pl.pallas_call(
        _mlp_body,
        out_shape=jax.ShapeDtypeStruct((t, d), F32),
        grid=(t // tm, ff // tf),
        in_specs=[pl.BlockSpec((tm, d), lambda i, f: (i, 0)),
                  pl.BlockSpec((1, d), lambda i, f: (0, 0)),
                  pl.BlockSpec((d, tf), lambda i, f: (0, f)),
                  pl.BlockSpec((tf, d), lambda i, f: (f, 0))],
        out_specs=pl.BlockSpec((tm, d), lambda i, f: (i, 0)),
        scratch_shapes=[pltpu.VMEM((tm, d), BF16), pltpu.VMEM((tm, d), F32)],
        compiler_params=_params("parallel", "arbitrary"),
        name="mlp_block",
    )(x, gain.reshape(1, d), w1, w2)


def _mem_kv_body(m_ref, g_ref, w_ref, kg_ref, k_ref, v_ref):
    h = _rms(m_ref[0], g_ref[0]).astype(BF16)
    kv = jnp.dot(h, w_ref[0], preferred_element_type=F32)
    for hd in range(X_HEADS):
        sl = slice(hd * X_DH, (hd + 1) * X_DH)
        k_ref[0, 0, :, sl] = _rms(kv[:, sl], kg_ref[0]).astype(BF16)
    v_ref[0, 0] = kv[:, D_MODEL:].astype(BF16)


def mem_kv(mem, norm_mem, w_kv, k_norm):
    bm, nm, d = mem.shape
    out = jax.ShapeDtypeStruct((DEPTH, bm, nm, d), BF16)
    return pl.pallas_call(
        _mem_kv_body,
        out_shape=(out, out),
        grid=(DEPTH, bm),
        in_specs=[pl.BlockSpec((1, nm, d), lambda l, b: (b, 0, 0)),
                  pl.BlockSpec((1, 1, d), lambda l, b: (l, 0, 0)),
                  pl.BlockSpec((1, d, 2 * d), lambda l, b: (l, 0, 0)),
                  pl.BlockSpec((1, 1, X_DH), lambda l, b: (l, 0, 0))],
        out_specs=(pl.BlockSpec((1, 1, nm, d), lambda l, b: (l, b, 0, 0)),
                   pl.BlockSpec((1, 1, nm, d), lambda l, b: (l, b, 0, 0))),
        compiler_params=_params("arbitrary", "arbitrary"),
        name="mem_kv",
    )(mem, norm_mem.reshape(DEPTH, 1, d), w_kv, k_norm.reshape(DEPTH, 1, X_DH))


def _cross_body(x_ref, g_ref, wq_ref, qg_ref, k_ref, v_ref, wo_ref, o_ref):
    x = x_ref[...]
    h = _rms(x, g_ref[...]).astype(BF16)
    q = jnp.dot(h, wq_ref[...], preferred_element_type=F32)
    heads = []
    for hd in range(X_HEADS):
        sl = slice(hd * X_DH, (hd + 1) * X_DH)
        qn = _rms(q[:, sl], qg_ref[...]).astype(BF16)
        s = lax.dot_general(qn, k_ref[0, :, sl], NT_DIMS, preferred_element_type=F32)
        p = jnp.exp(s - jnp.max(s, axis=-1, keepdims=True))
        l = jnp.sum(p, axis=-1, keepdims=True)
        oh = jnp.dot(p.astype(BF16), v_ref[0, :, sl], preferred_element_type=F32) / l
        heads.append(oh.astype(BF16))
    o = jnp.concatenate(heads, axis=-1)
    o_ref[...] = x + jnp.dot(o, wo_ref[...], preferred_element_type=F32)


def cross_block(x, seq_len, gain, w_q, q_gain, kn, v, w_o, tm=ROW_TILE):
    t, d = x.shape
    nm = kn.shape[1]
    per_seq = seq_len // tm
    return pl.pallas_call(
        _cross_body,
        out_shape=jax.ShapeDtypeStruct((t, d), F32),
        grid=(t // tm,),
        in_specs=[pl.BlockSpec((tm, d), lambda i: (i, 0)),
                  pl.BlockSpec((1, d), lambda i: (0, 0)),
                  pl.BlockSpec((d, d), lambda i: (0, 0)),
                  pl.BlockSpec((1, X_DH), lambda i: (0, 0)),
                  pl.BlockSpec((1, nm, d), lambda i: (i // per_seq, 0, 0)),
                  pl.BlockSpec((1, nm, d), lambda i: (i // per_seq, 0, 0)),
                  pl.BlockSpec((d, d), lambda i: (0, 0))],
        out_specs=pl.BlockSpec((tm, d), lambda i: (i, 0)),
        compiler_params=_params("parallel"),
        name="cross_block",
    )(x, gain.reshape(1, d), w_q, (q_gain * X_DH ** -0.5).reshape(1, X_DH), kn, v, w_o)


LANES = 128


def _toeplitz_body(k_ref, o_ref):
    lc, gs = S5_CHUNK, S5_GROUP
    kern = k_ref[0]
    n = kern.shape[1]
    per_tile = LANES // gs
    for r in range(per_tile):
        shifted = pltpu.roll(kern, n - gs * r, 1) if r else kern
        for a in range(lc // per_tile):
            s = lc - 1 - (per_tile * a + r)
            o_ref[0, s * gs:(s + 1) * gs, :] = shifted[:, LANES * a:LANES * a + lc * gs].astype(o_ref.dtype)


def toeplitz_expand(kern_flat):
    g, gs, n = kern_flat.shape
    width = S5_CHUNK * S5_GROUP
    return pl.pallas_call(
        _toeplitz_body,
        out_shape=jax.ShapeDtypeStruct((g, width, width), BF16),
        grid=(g,),
        in_specs=[pl.BlockSpec((1, gs, n), lambda i: (i, 0, 0))],
        out_specs=pl.BlockSpec((1, width, width), lambda i: (i, 0, 0)),
        compiler_params=_params("parallel"),
        name="toeplitz_expand",
    )(kern_flat)


def s5_operators(lam_re, lam_im, log_step, b_re, b_im, c_re, c_im, d):
    lc = S5_CHUNK
    lam = lax.complex(lam_re.astype(F32), lam_im.astype(F32))
    step = jnp.exp(log_step.astype(F32))[..., None]
    lam_bar = jnp.exp(lam * step)
    b_bar = ((lam_bar - 1.0) / lam)[..., None] * lax.complex(b_re.astype(F32), b_im.astype(F32))
    c = lax.complex(c_re.astype(F32), c_im.astype(F32))
    pw = jnp.cumprod(jnp.broadcast_to(lam_bar[..., None], lam_bar.shape + (lc,)), axis=-1)
    pw = jnp.concatenate([jnp.ones_like(pw[..., :1]), pw], axis=-1)
    kern = jnp.einsum('zgcp,zgpt,zgpd->zgtcd', c, pw[..., :lc], b_bar, precision=HIGHEST).real
    kf, kb = kern[0], kern[1]
    k0 = kf[:, :1] + kb[:, :1] + (d.astype(F32)[:, :, None] * jnp.eye(S5_GROUP, dtype=F32))[:, None]
    kern_full = jnp.concatenate([kb[:, :0:-1], k0, kf[:, 1:]], axis=1)
    kern_flat = kern_full.transpose(0, 3, 1, 2).reshape(S5_GROUPS, S5_GROUP, (2 * lc - 1) * S5_GROUP)
    toep = toeplitz_expand(jnp.pad(kern_flat, ((0, 0), (0, 0), (0, S5_GROUP))))

    pf = jnp.einsum('gps,gpd->gsdp', pw[0][..., lc - 1::-1], b_bar[0])
    pb = jnp.einsum('gps,gpd->gsdp', pw[1][..., :lc], b_bar[1])
    p_op = jnp.concatenate([pf.real, pf.imag, pb.real, pb.imag], axis=-1)
    p_op = p_op.reshape(S5_GROUPS, lc * S5_GROUP, 4 * S5_STATE)

    qf = jnp.einsum('gcp,gpt->gptc', c[0], pw[0][..., 1:])
    qb = jnp.einsum('gcp,gpt->gptc', c[1], pw[1][..., :0:-1])
    q_op = jnp.concatenate([qf.real, -qf.imag, qb.real, -qb.imag], axis=1)
    q_op = q_op.reshape(S5_GROUPS, 4 * S5_STATE, lc * S5_GROUP)

    a = pw[..., lc]
    coef = jnp.stack([jnp.concatenate([a[0].real, a[0].real], -1),
                      jnp.concatenate([-a[0].imag, a[0].imag], -1),
                      jnp.concatenate([a[1].real, a[1].real], -1),
                      jnp.concatenate([-a[1].imag, a[1].imag], -1)], axis=1)
    return toep, _split_bf16(p_op), _split_bf16(q_op), coef


def _split_bf16(x):
    hi = x.astype(BF16)
    lo = (x - hi.astype(F32)).astype(BF16)
    return jnp.stack([hi, lo], axis=1)


SUBLANES = 8


def _s5_group_body(u_ref, t_ref, p_ref, q_ref, c_ref, y_ref, v_ref, x_ref, *, n_chunks, bp):
    u = u_ref[0]
    v_ref[...] = (jnp.dot(u, p_ref[0, 0], preferred_element_type=F32)
                  + jnp.dot(u, p_ref[0, 1], preferred_element_type=F32))
    coef = c_ref[0]
    c1f, c2f, c1b, c2b = coef[0:1], coef[1:2], coef[2:3], coef[3:4]
    half = 2 * S5_STATE

    def step(i, carry):
        xf, xb = carry
        rows_f = pl.ds(pl.multiple_of(i * bp, bp), bp)
        x_ref[rows_f, :half] = xf
        xf = xf * c1f + pltpu.roll(xf, S5_STATE, 1) * c2f + v_ref[rows_f, :half]
        rows_b = pl.ds(pl.multiple_of((n_chunks - 1 - i) * bp, bp), bp)
        x_ref[rows_b, half:] = xb
        xb = xb * c1b + pltpu.roll(xb, S5_STATE, 1) * c2b + v_ref[rows_b, half:]
        return xf, xb

    zero = jnp.zeros((bp, half), F32)
    lax.fori_loop(0, n_chunks, step, (zero, zero))
    x = x_ref[...]
    x_hi = x.astype(BF16)
    x_lo = (x - x_hi.astype(F32)).astype(BF16)
    y = (jnp.dot(u, t_ref[0], preferred_element_type=F32)
         + jnp.dot(x_hi, q_ref[0, 0], preferred_element_type=F32)
         + jnp.dot(x_lo, q_ref[0, 0], preferred_element_type=F32)
         + jnp.dot(x_hi, q_ref[0, 1], preferred_element_type=F32))
    y_ref[0] = y.astype(y_ref.dtype)


def s5_scan(u, bsz, seq_len, ops):
    toep, p_op, q_op, coef = ops
    lc, g, w = S5_CHUNK, S5_GROUPS, S5_CHUNK * S5_GROUP
    n = seq_len // lc
    bp = -(-bsz // SUBLANES) * SUBLANES
    c = n * bp
    ns = 4 * S5_STATE
    ug = u.reshape(bsz, n, lc, g, S5_GROUP).transpose(3, 1, 0, 2, 4)
    ug = jnp.pad(ug, ((0, 0), (0, 0), (0, bp - bsz), (0, 0), (0, 0))).reshape(g, c, w)
    group = lambda *shape: pl.BlockSpec((1,) + shape, lambda i: (i,) + (0,) * len(shape))
    y = pl.pallas_call(
        functools.partial(_s5_group_body, n_chunks=n, bp=bp),
        out_shape=jax.ShapeDtypeStruct((g, c, w), BF16),
        grid=(g,),
        in_specs=[group(c, w), group(w, w), group(2, w, ns), group(2, ns, w), group(4, ns // 2)],
        out_specs=group(c, w),
        scratch_shapes=[pltpu.VMEM((c, ns), F32), pltpu.VMEM((c, ns), F32)],
        compiler_params=_params("parallel"),
        name="s5_group",
    )(ug, toep, p_op, q_op, coef)
    y = y.reshape(g, n, bp, lc, S5_GROUP)[:, :, :bsz]
    return y.transpose(2, 1, 3, 0, 4).reshape(bsz * seq_len, S5_WIDTH)


def _gla_direction(q_ref, k_ref, v_ref, g_ref, wg, bg, s_ref, o_ref, forward, chunks):
    cs = GLA_CHUNK
    tb = chunks * cs
    glr = g_ref[...]
    logit = (jnp.dot(glr, wg[0], preferred_element_type=F32)
             + jnp.dot(glr, wg[1], preferred_element_type=F32) + bg)
    g = (jnp.minimum(logit, 0.0) - jnp.log1p(jnp.exp(-jnp.abs(logit)))) / GLA_TAU
    row = lax.broadcasted_iota(jnp.int32, (tb, tb), 0)
    col = lax.broadcasted_iota(jnp.int32, (tb, tb), 1)
    same_chunk = (row // cs) == (col // cs)
    within = same_chunk & ((col <= row) if forward else (col >= row))
    ones = jnp.where(within, 1.0, 0.0).astype(BF16)
    bcum = jnp.zeros_like(g)
    rest = g
    for _ in range(3):
        term = rest.astype(BF16)
        bcum = bcum + jnp.dot(ones, term, preferred_element_type=F32)
        rest = rest - term.astype(F32)

    srow = lax.broadcasted_iota(jnp.int32, (GLA_HEADS * cs, cs), 0) & (cs - 1)
    scol = lax.broadcasted_iota(jnp.int32, (GLA_HEADS * cs, cs), 1)
    keep = (scol <= srow) if forward else (scol >= srow)
    lane = lax.broadcasted_iota(jnp.int32, (1, GLA_QK), 1)
    head_lanes = [((lane >= h * GLA_DK) & (lane < (h + 1) * GLA_DK)).astype(F32) for h in range(GLA_HEADS)]
    state_mask = (lax.broadcasted_iota(jnp.int32, (GLA_V, GLA_QK), 0) // GLA_DV
                  == lax.broadcasted_iota(jnp.int32, (GLA_V, GLA_QK), 1) // GLA_DK)

    i_ref = cs // 2 - 1 if forward else cs // 2
    i_last = cs - 1 if forward else 0
    scale = GLA_DK ** -0.5
    for ci in (range(chunks) if forward else reversed(range(chunks))):
        rows = slice(ci * cs, (ci + 1) * cs)
        b = bcum[rows]
        bref = b[i_ref:i_ref + 1]
        blast = b[i_last:i_last + 1]
        q = q_ref[rows, :].astype(F32) * scale
        k = k_ref[rows, :].astype(F32)
        v = v_ref[rows, :]
        q_rel = q * jnp.exp(b - bref)
        k_rel = (k * jnp.exp(bref - b)).astype(BF16)
        k_out = (k * jnp.exp(blast - b)).astype(BF16)
        q_dec = (q * jnp.exp(b)).astype(BF16)
        decay = jnp.exp(blast)
        q_heads = jnp.concatenate([q_rel * hm for hm in head_lanes], axis=0).astype(BF16)
        s = lax.dot_general(q_heads, k_rel, NT_DIMS, preferred_element_type=F32)
        s = jnp.where(keep, s, 0.0).astype(BF16)
        st = s_ref[...]
        o_inter = lax.dot_general(q_dec, st.astype(BF16), NT_DIMS, preferred_element_type=F32)
        o_intra = [jnp.dot(s[h * cs:(h + 1) * cs], v[:, h * GLA_DV:(h + 1) * GLA_DV],
                           preferred_element_type=F32) for h in range(GLA_HEADS)]
        o_ref[rows, :] = o_inter + jnp.concatenate(o_intra, axis=-1)
        kv = lax.dot_general(v, k_out, TN_DIMS, preferred_element_type=F32)
        s_ref[...] = decay * st + jnp.where(state_mask, kv, 0.0)


def _gla_body(qf_ref, kf_ref, vf_ref, gf_ref, qb_ref, kb_ref, vb_ref, gb_ref, wg_ref, bg_ref,
              of_ref, ob_ref, sf_ref, sb_ref, *, chunks):
    @pl.when(pl.program_id(1) == 0)
    def _():
        sf_ref[...] = jnp.zeros_like(sf_ref)
        sb_ref[...] = jnp.zeros_like(sb_ref)

    _gla_direction(qf_ref, kf_ref, vf_ref, gf_ref, wg_ref[0], bg_ref[0], sf_ref, of_ref, True, chunks)
    _gla_direction(qb_ref, kb_ref, vb_ref, gb_ref, wg_ref[1], bg_ref[1], sb_ref, ob_ref, False, chunks)


def gla_scan(proj, bsz, seq_len, w_gate, b_gate, chunks=4):
    t = proj.shape[0]
    tb = chunks * GLA_CHUNK
    nb = seq_len // tb
    fwd = lambda cb: (lambda b, i: (b * nb + i, cb))
    bwd = lambda cb: (lambda b, i: (b * nb + nb - 1 - i, cb))
    qc, kc, vc, gc = 512 // GLA_QK, 768 // GLA_QK, 1024 // GLA_V, 2048 // 128
    wg = jnp.zeros((2, 128, GLA_QK), F32)
    wg = wg.at[0, :GLA_RANK].set(w_gate[0].astype(F32)).at[1, GLA_RANK:2 * GLA_RANK].set(w_gate[1].astype(F32))
    wg = _split_bf16(wg)
    out = jax.ShapeDtypeStruct((t, GLA_V), F32)
    state = pltpu.VMEM((GLA_V, GLA_QK), F32)
    return pl.pallas_call(
        functools.partial(_gla_body, chunks=chunks),
        out_shape=(out, out),
        grid=(bsz, nb),
        in_specs=[pl.BlockSpec((tb, GLA_QK), fwd(qc)), pl.BlockSpec((tb, GLA_QK), fwd(kc)),
                  pl.BlockSpec((tb, GLA_V), fwd(vc)), pl.BlockSpec((tb, 128), fwd(gc)),
                  pl.BlockSpec((tb, GLA_QK), bwd(qc)), pl.BlockSpec((tb, GLA_QK), bwd(kc)),
                  pl.BlockSpec((tb, GLA_V), bwd(vc)), pl.BlockSpec((tb, 128), bwd(gc)),
                  pl.BlockSpec((2, 2, 128, GLA_QK), lambda b, i: (0, 0, 0, 0)),
                  pl.BlockSpec((2, 1, GLA_QK), lambda b, i: (0, 0, 0))],
        out_specs=(pl.BlockSpec((tb, GLA_V), fwd(0)), pl.BlockSpec((tb, GLA_V), bwd(0))),
        scratch_shapes=[state, state],
        compiler_params=_params("parallel", "arbitrary"),
        name="gla_scan",
    )(proj, proj, proj, proj, proj, proj, proj, proj, wg, b_gate.astype(F32).reshape(2, 1, GLA_QK))


def _even_out_body(x_ref, ys_ref, of_ref, ob_ref, og_ref, wglu_ref, gn_ref, wtop_ref, wbot_ref, o_ref):
    y = ys_ref[...].astype(F32)
    y = 0.5 * y * (1.0 + jnp.tanh(math.sqrt(2.0 / math.pi) * (y + 0.044715 * (y * y * y))))
    gate = jnp.dot(y.astype(BF16), wglu_ref[...], preferred_element_type=F32)
    y = y * _sigmoid(gate)
    o = of_ref[...] + ob_ref[...]
    og = og_ref[...].astype(F32)
    heads = []
    for h in range(GLA_HEADS):
        sl = slice(h * GLA_DV, (h + 1) * GLA_DV)
        heads.append(_rms(o[:, sl], gn_ref[...]))
    o = jnp.concatenate(heads, axis=-1) * (og * _sigmoid(og))
    o_ref[...] = (x_ref[...]
                  + jnp.dot(y.astype(BF16), wtop_ref[...], preferred_element_type=F32)
                  + jnp.dot(o.astype(BF16), wbot_ref[...], preferred_element_type=F32))


def even_out(x, ys, o_f, o_b, proj, w_glu, gla_norm, w_out, tm=ROW_TILE):
    t, d = x.shape
    row = lambda i: (i, 0)
    const = lambda i: (0, 0)
    return pl.pallas_call(
        _even_out_body,
        out_shape=jax.ShapeDtypeStruct((t, d), F32),
        grid=(t // tm,),
        in_specs=[pl.BlockSpec((tm, d), row), pl.BlockSpec((tm, S5_WIDTH), row),
                  pl.BlockSpec((tm, GLA_V), row), pl.BlockSpec((tm, GLA_V), row),
                  pl.BlockSpec((tm, GLA_V), lambda i: (i, 1536 // GLA_V)),
                  pl.BlockSpec((S5_WIDTH, S5_WIDTH), const), pl.BlockSpec((1, GLA_DV), const),
                  pl.BlockSpec((S5_WIDTH, d), const), pl.BlockSpec((GLA_V, d), const)],
        out_specs=pl.BlockSpec((tm, d), row),
        compiler_params=_params("parallel"),
        name="even_out",
    )(x, ys, o_f, o_b, proj, w_glu, gla_norm.astype(F32).reshape(1, GLA_DV),
      w_out[:S5_WIDTH], w_out[S5_WIDTH:])


QK_NORM_WIDTH = 256


def _seg_rms(a, gain, same_seg):
    ssq = jnp.dot((a * a).astype(BF16), same_seg, preferred_element_type=F32)
    return a * lax.rsqrt(ssq * (1.0 / DIFF_DK) + EPS) * gain


def _qkv_body(x_ref, g_ref, w_ref, qk_gain_ref, o_ref, h_ref):
    j = pl.program_id(1)

    @pl.when(j == 0)
    def _():
        h_ref[...] = _rms(x_ref[...], g_ref[...]).astype(BF16)

    acc = jnp.dot(h_ref[...], w_ref[...], preferred_element_type=F32)

    @pl.when(j < 2)
    def _():
        gain = qk_gain_ref[0]
        w = QK_NORM_WIDTH
        same_seg = jnp.where(lax.broadcasted_iota(jnp.int32, (w, w), 0) // DIFF_DK
                             == lax.broadcasted_iota(jnp.int32, (w, w), 1) // DIFF_DK, 1.0, 0.0).astype(BF16)
        for c in range(acc.shape[1] // w):
            sl = slice(c * w, (c + 1) * w)
            o_ref[:, sl] = _seg_rms(acc[:, sl], gain, same_seg).astype(o_ref.dtype)

    @pl.when(j == 2)
    def _():
        o_ref[...] = acc.astype(o_ref.dtype)


def qkv_project(x, gain, w, q_norm, k_norm, tm=ROW_TILE):
    t, d = x.shape
    reps = QK_NORM_WIDTH // DIFF_DK
    qg = jnp.tile(q_norm.astype(F32) * DIFF_DK ** -0.5, reps)
    kg = jnp.tile(k_norm.astype(F32), reps)
    qk_gain = jnp.stack([qg, kg]).reshape(2, 1, QK_NORM_WIDTH)
    return pl.pallas_call(
        _qkv_body,
        out_shape=jax.ShapeDtypeStruct((t, 3 * d), BF16),
        grid=(t // tm, 3),
        in_specs=[pl.BlockSpec((tm, d), lambda i, j: (i, 0)),
                  pl.BlockSpec((1, d), lambda i, j: (0, 0)),
                  pl.BlockSpec((d, d), lambda i, j: (0, j)),
                  pl.BlockSpec((1, 1, QK_NORM_WIDTH), lambda i, j: (jnp.minimum(j, 1), 0, 0))],
        out_specs=pl.BlockSpec((tm, d), lambda i, j: (i, j)),
        scratch_shapes=[pltpu.VMEM((tm, d), BF16)],
        compiler_params=_params("parallel", "arbitrary"),
        name="qkv_project",
    )(x, gain.reshape(1, d), w, qk_gain)


POS_SPLIT = 16
POS_SHIFT = 4
AUG_LANE = DIFF_DK
SOFTMAX_GROUP_ELEMS = 32 * 1024
ROW_PARTS = 2


def _pos_terms(shape):
    lane = lax.broadcasted_iota(jnp.int32, shape, 1)
    pos = lax.broadcasted_iota(jnp.int32, shape, 0)
    hi = lax.shift_right_logical(pos, POS_SHIFT).astype(F32)
    lo = (pos & (POS_SPLIT - 1)).astype(F32)
    return lane, hi, lo


def _lane_select(lane, first, values):
    out = 0.0
    for n, val in reversed(list(enumerate(values))):
        out = jnp.where(lane == first + n, val, out)
    return out


def _diff_attn_body(slope_ref, q_ref, k_ref, v_ref, lq1_ref, lk1_ref, lq2_ref, lk2_ref, sub_ref,
                    o_ref, ka_ref, va_ref, qs_ref, s0_ref, s1_ref, p0_ref, p1_ref, vs_ref,
                    *, blk, seq_len, lambda_init):
    h = pl.program_id(1)
    qi = pl.program_id(2)
    slope = slope_ref[h]
    nk = seq_len // blk
    s_refs, p_refs = (s0_ref, s1_ref), (p0_ref, p1_ref)

    @pl.when(qi == 0)
    def _():
        va_ref[:, :DIFF_DV] = v_ref[...]
        va_ref[:, DIFF_DV:] = jnp.ones((seq_len, DIFF_DV), BF16)

        def build(t, carry):
            rows = pl.ds(pl.multiple_of(t * blk, blk), blk)
            kf = k_ref[rows, :].astype(F32)
            lane, hi, lo = _pos_terms(kf.shape)
            aug = _lane_select(lane, AUG_LANE,
                               [-blk * slope, -POS_SPLIT * slope, -slope,
                                (blk * slope) * jnp.asarray(t, F32), (POS_SPLIT * slope) * hi, slope * lo])
            for z, kz in enumerate((kf, pltpu.roll(kf, DIFF_DK, 1))):
                ka_ref[z, rows, :] = jnp.where(lane < DIFF_DK, kz, aug).astype(BF16)
            return carry

        lax.fori_loop(0, nk, build, 0)

    qf = q_ref[...].astype(F32)
    lane, hi, lo = _pos_terms(qf.shape)
    qa = _lane_select(lane, AUG_LANE, [jnp.asarray(qi, F32), hi, lo, 1.0, 1.0, 1.0])
    for z, qz in enumerate((qf, pltpu.roll(qf, DIFF_DK, 1))):
        qs_ref[z] = jnp.where(lane < DIFF_DK, qz, 0.0).astype(BF16)
        qs_ref[2 + z] = jnp.where(lane < DIFF_DK, qz, qa).astype(BF16)
        qs_ref[4 + z] = jnp.where(lane < DIFF_DK, qz, -qa).astype(BF16)

    def key_block(t):
        if t == 0:
            return qi, 0
        j = (t - 1) + jnp.asarray(qi <= t - 1, jnp.int32)
        return j, jnp.where(j < qi, 2, 4)

    def key_rows(t):
        j, _ = key_block(t)
        return pl.ds(pl.multiple_of(j * blk, blk), blk)

    part = blk // ROW_PARTS
    parts = [slice(r * part, (r + 1) * part) for r in range(ROW_PARTS)]
    for z, s_ref in enumerate(s_refs):
        for rows in parts:
            for t in range(nk):
                _, variant = key_block(t)
                s_ref[rows, t * blk:(t + 1) * blk] = lax.dot_general(
                    qs_ref[variant + z, rows, :], ka_ref[z, key_rows(t), :], NT_DIMS,
                    preferred_element_type=F32)

    rg = SOFTMAX_GROUP_ELEMS // seq_len
    rel = (lax.broadcasted_iota(jnp.int32, (rg, blk), 0) - lax.broadcasted_iota(jnp.int32, (rg, blk), 1))
    for t in range(nk):
        vs_ref[t * blk:(t + 1) * blk, :] = va_ref[key_rows(t), :]

    acc = []
    for s_ref, p_ref in zip(s_refs, p_refs):
        for g in range(blk // rg):
            rows = slice(g * rg, (g + 1) * rg)
            s_diag = s_ref[rows, :blk] - slope * jnp.abs(rel + g * rg).astype(F32)
            s_rest = s_ref[rows, blk:]
            m = jnp.maximum(jnp.max(s_diag, axis=-1, keepdims=True), jnp.max(s_rest, axis=-1, keepdims=True))
            p_ref[rows, :blk] = jnp.exp(s_diag - m).astype(BF16)
            p_ref[rows, blk:] = jnp.exp(s_rest - m).astype(BF16)
        acc.append([jnp.dot(p_ref[rows, :], vs_ref[...], preferred_element_type=F32) for rows in parts])

    lam = (jnp.exp(jnp.sum(lq1_ref[...] * lk1_ref[...], axis=-1, keepdims=True))
           - jnp.exp(jnp.sum(lq2_ref[...] * lk2_ref[...], axis=-1, keepdims=True)) + lambda_init)
    for rows, a0, a1 in zip(parts, *acc):
        o = (a0[:, :DIFF_DV] / a0[:, DIFF_DV:DIFF_DV + 1]
             - lam * (a1[:, :DIFF_DV] / a1[:, DIFF_DV:DIFF_DV + 1]))
        o_ref[rows, :] = (_rms(o, sub_ref[...]) * (1.0 - lambda_init)).astype(o_ref.dtype)


def diff_attention(qkv, bsz, seq_len, lq1, lk1, lq2, lk2, sub_norm, lambda_init, blk=512):
    t = qkv.shape[0]
    nq = seq_len // blk
    assert blk // POS_SPLIT <= 256, "hi part of a block position must stay exact in bf16"
    slopes = jnp.asarray(2.0 ** (-8.0 * np.arange(1, DIFF_HEADS + 1, dtype=np.float32) / DIFF_HEADS), F32)
    vec = lambda a: a.astype(F32).reshape(1, DIFF_DK)
    const = lambda b, h, i: (0, 0)
    return pl.pallas_call(
        functools.partial(_diff_attn_body, blk=blk, seq_len=seq_len, lambda_init=lambda_init),
        out_shape=jax.ShapeDtypeStruct((t, DIFF_HEADS * DIFF_DV), BF16),
        grid=(bsz, DIFF_HEADS, nq),
        in_specs=[pl.BlockSpec(memory_space=pltpu.SMEM),
                  pl.BlockSpec((blk, 2 * DIFF_DK), lambda b, h, i: (b * nq + i, h)),
                  pl.BlockSpec((seq_len, 2 * DIFF_DK), lambda b, h, i: (b, DIFF_HEADS + h)),
                  pl.BlockSpec((seq_len, DIFF_DV), lambda b, h, i: (b, 2 * DIFF_HEADS + h)),
                  pl.BlockSpec((1, DIFF_DK), const), pl.BlockSpec((1, DIFF_DK), const),
                  pl.BlockSpec((1, DIFF_DK), const), pl.BlockSpec((1, DIFF_DK), const),
                  pl.BlockSpec((1, DIFF_DV), const)],
        out_specs=pl.BlockSpec((blk, DIFF_DV), lambda b, h, i: (b * nq + i, h)),
        scratch_shapes=[pltpu.VMEM((2, seq_len, 2 * DIFF_DK), BF16),
                        pltpu.VMEM((seq_len, 2 * DIFF_DV), BF16),
                        pltpu.VMEM((6, blk, 2 * DIFF_DK), BF16),
                        pltpu.VMEM((blk, seq_len), F32),
                        pltpu.VMEM((blk, seq_len), F32),
                        pltpu.VMEM((blk, seq_len), BF16),
                        pltpu.VMEM((blk, seq_len), BF16),
                        pltpu.VMEM((seq_len, 2 * DIFF_DV), BF16)],
        compiler_params=_params("parallel", "parallel", "arbitrary"),
        name="diff_attention",
    )(slopes, qkv, qkv, qkv, vec(lq1), vec(lk1), vec(lq2), vec(lk2),
      sub_norm.astype(F32).reshape(1, DIFF_DV))


def _pad_ev_w_in(w):
    return jnp.pad(w, ((0, 0), (0, EV_PAD_COLS - w.shape[1]))).astype(BF16)


def _trunk(x3, mem, p):
    bsz, seq_len, d = x3.shape
    x = x3.reshape(bsz * seq_len, d)
    kn_all, v_all = mem_kv(mem, p['norm_mem'], p['x_w_kv'], p['x_k_norm'])
    for layer in range(DEPTH):
        if layer % 2 == 0:
            e = layer // 2
            proj = norm_matmul(x, p['norm_mix'][layer], p['ev_w_in'][e], BF16, EV_TN)
            ys = s5_scan(proj[:, :S5_WIDTH], bsz, seq_len, p['s5_ops'][e])
            o_f, o_b = gla_scan(proj, bsz, seq_len, p['gla_w_gate'][e], p['gla_b_gate'][e])
            x = even_out(x, ys, o_f, o_b, proj, p['s5_w_glu'][e], p['gla_norm'][e], p['ev_w_out'][e])
        else:
            o = layer // 2
            lambda_init = 0.8 - 0.6 * math.exp(-0.3 * layer)
            qkv = qkv_project(x, p['norm_mix'][layer], p['od_w_in'][o],
                              p['diff_q_norm'][o], p['diff_k_norm'][o])
            att = diff_attention(qkv, bsz, seq_len, p['diff_lambda_q1'][o], p['diff_lambda_k1'][o],
                                 p['diff_lambda_q2'][o], p['diff_lambda_k2'][o], p['diff_norm'][o],
                                 lambda_init)
            x = matmul_residual(att, p['od_w_out'][o], x, tn=d)
        x = cross_block(x, seq_len, p['norm_cross'][layer], p['x_w_q'][layer], p['x_q_norm'][layer],
                        kn_all[layer], v_all[layer], p['x_w_o'][layer])
        x = mlp_block(x, p['norm_mlp'][layer], p['mlp_w1'][layer], p['mlp_w2'][layer])
    return x.reshape(bsz, seq_len, d)


def kernel(x_prompt, x_sample, mem_prompt, mem_sample, norm_mix, norm_cross, norm_mem, norm_mlp,
           ev_w_in, ev_w_out, s5_lambda_re, s5_lambda_im, s5_log_step, s5_b_re, s5_b_im,
           s5_c_re, s5_c_im, s5_d, s5_w_glu, gla_w_gate, gla_b_gate, gla_norm,
           od_w_in, od_w_out, diff_q_norm, diff_k_norm, diff_lambda_q1, diff_lambda_k1,
           diff_lambda_q2, diff_lambda_k2, diff_norm, x_w_q, x_w_kv, x_w_o, x_q_norm, x_k_norm,
           mlp_w1, mlp_w2):
    n_even = ev_w_in.shape[0]
    bf = lambda w: w.astype(BF16)
    p = dict(
        norm_mix=norm_mix, norm_cross=norm_cross, norm_mem=norm_mem, norm_mlp=norm_mlp,
        ev_w_in=[_pad_ev_w_in(ev_w_in[e]) for e in range(n_even)], ev_w_out=bf(ev_w_out),
        s5_ops=[s5_operators(s5_lambda_re[e], s5_lambda_im[e], s5_log_step[e], s5_b_re[e], s5_b_im[e],
                             s5_c_re[e], s5_c_im[e], s5_d[e]) for e in range(n_even)],
        s5_w_glu=bf(s5_w_glu), gla_w_gate=gla_w_gate, gla_b_gate=gla_b_gate, gla_norm=gla_norm,
        od_w_in=bf(od_w_in), od_w_out=bf(od_w_out), diff_q_norm=diff_q_norm, diff_k_norm=diff_k_norm,
        diff_lambda_q1=diff_lambda_q1, diff_lambda_k1=diff_lambda_k1,
        diff_lambda_q2=diff_lambda_q2, diff_lambda_k2=diff_lambda_k2, diff_norm=diff_norm,
        x_w_q=bf(x_w_q), x_w_kv=bf(x_w_kv), x_w_o=bf(x_w_o), x_q_norm=x_q_norm, x_k_norm=x_k_norm,
        mlp_w1=bf(mlp_w1), mlp_w2=bf(mlp_w2))
    return (_trunk(x_prompt, mem_prompt, p), _trunk(x_sample, mem_sample, p))
```

```python
import functools
import math

import numpy as np
import jax
import jax.numpy as jnp
from jax import lax
from jax.experimental import pallas as pl
from jax.experimental.pallas import tpu as pltpu

F32 = jnp.float32
BF16 = jnp.bfloat16
HIGHEST = lax.Precision.HIGHEST

D_MODEL = 1024
DEPTH = 4
EPS = 1e-6
S5_WIDTH = 512
S5_GROUP = 16
S5_GROUPS = 32
S5_STATE = 64
S5_CHUNK = 64
GLA_HEADS = 4
GLA_DV = 128
GLA_DK = 64
GLA_RANK = 16
GLA_TAU = 16.0
GLA_CHUNK = 64
GLA_QK = GLA_HEADS * GLA_DK
GLA_V = GLA_HEADS * GLA_DV
EV_PAD_COLS = 2304
EV_TN = 768
DIFF_HEADS = 8
DIFF_DK = 64
DIFF_DV = 128
X_HEADS = 4
X_DH = 256
D_FF = 4096

ROW_TILE = 512
VMEM_LIMIT = 48 * 1024 * 1024

NT_DIMS = (((1,), (1,)), ((), ()))
TN_DIMS = (((0,), (0,)), ((), ()))


def _params(*sem, flags=None):
    return pltpu.CompilerParams(dimension_semantics=sem, vmem_limit_bytes=VMEM_LIMIT, flags=flags)


def _rms(x, gain):
    ms = jnp.mean(x * x, axis=-1, keepdims=True)
    return x * lax.rsqrt(ms + EPS) * gain


def _sigmoid(x):
    return 1.0 / (1.0 + jnp.exp(-x))


def _norm_matmul_body(x_ref, g_ref, w_ref, o_ref, h_ref):
    @pl.when(pl.program_id(1) == 0)
    def _():
        h_ref[...] = _rms(x_ref[...], g_ref[...]).astype(BF16)

    o_ref[...] = jnp.dot(h_ref[...], w_ref[...], preferred_element_type=F32).astype(o_ref.dtype)


def norm_matmul(x, gain, w, out_dtype, tn, tm=ROW_TILE):
    t, d = x.shape
    n = w.shape[1]
    return pl.pallas_call(
        _norm_matmul_body,
        out_shape=jax.ShapeDtypeStruct((t, n), out_dtype),
        grid=(t // tm, n // tn),
        in_specs=[pl.BlockSpec((tm, d), lambda i, j: (i, 0)),
                  pl.BlockSpec((1, d), lambda i, j: (0, 0)),
                  pl.BlockSpec((d, tn), lambda i, j: (0, j))],
        out_specs=pl.BlockSpec((tm, tn), lambda i, j: (i, j)),
        scratch_shapes=[pltpu.VMEM((tm, d), BF16)],
        compiler_params=_params("parallel", "arbitrary"),
        name="norm_matmul",
    )(x, gain.reshape(1, d), w)


def _matmul_res_body(a_ref, w_ref, r_ref, o_ref):
    o_ref[...] = r_ref[...] + jnp.dot(a_ref[...].astype(BF16), w_ref[...], preferred_element_type=F32)


def matmul_residual(a, w, res, tn=512, tm=ROW_TILE):
    t, k = a.shape
    n = w.shape[1]
    return pl.pallas_call(
        _matmul_res_body,
        out_shape=jax.ShapeDtypeStruct((t, n), F32),
        grid=(t // tm, n // tn),
        in_specs=[pl.BlockSpec((tm, k), lambda i, j: (i, 0)),
                  pl.BlockSpec((k, tn), lambda i, j: (0, j)),
                  pl.BlockSpec((tm, tn), lambda i, j: (i, j))],
        out_specs=pl.BlockSpec((tm, tn), lambda i, j: (i, j)),
        compiler_params=_params("parallel", "arbitrary"),
        name="matmul_residual",
    )(a, w, res)


def _mlp_body(x_ref, g_ref, w1_ref, w2_ref, o_ref, h_ref, acc_ref):
    f = pl.program_id(1)

    @pl.when(f == 0)
    def _():
        h_ref[...] = _rms(x_ref[...], g_ref[...]).astype(BF16)
        acc_ref[...] = jnp.zeros_like(acc_ref)

    hid = jnp.dot(h_ref[...], w1_ref[...], preferred_element_type=F32)
    hid = jnp.square(jnp.maximum(hid, 0.0)).astype(BF16)
    acc_ref[...] += jnp.dot(hid, w2_ref[...], preferred_element_type=F32)

    @pl.when(f == pl.num_programs(1) - 1)
    def _():
        o_ref[...] = x_ref[...] + acc_ref[...]


def mlp_block(x, gain, w1, w2, tf=512, tm=2 * ROW_TILE):
    t, d = x.shape
    ff = w1.shape[1]
    return pl.pallas_call(
        _mlp_body,
        out_shape=jax.ShapeDtypeStruct((t, d), F32),
        grid=(t // tm, ff // tf),
        in_specs=[pl.BlockSpec((tm, d), lambda i, f: (i, 0)),
                  pl.BlockSpec((1, d), lambda i, f: (0, 0)),
                  pl.BlockSpec((d, tf), lambda i, f: (0, f)),
                  pl.BlockSpec((tf, d), lambda i, f: (f, 0))],
        out_specs=pl.BlockSpec((tm, d), lambda i, f: (i, 0)),
        scratch_shapes=[pltpu.VMEM((tm, d), BF16), pltpu.VMEM((tm, d), F32)],
        compiler_params=_params("parallel", "arbitrary"),
        name="mlp_block",
    )(x, gain.reshape(1, d), w1, w2)


def _mem_kv_body(m_ref, g_ref, w_ref, kg_ref, k_ref, v_ref):
    h = _rms(m_ref[0], g_ref[0]).astype(BF16)
    kv = jnp.dot(h, w_ref[0], preferred_element_type=F32)
    for hd in range(X_HEADS):
        sl = slice(hd * X_DH, (hd + 1) * X_DH)
        k_ref[0, 0, :, sl] = _rms(kv[:, sl], kg_ref[0]).astype(BF16)
    v_ref[0, 0] = kv[:, D_MODEL:].astype(BF16)


def mem_kv(mem, norm_mem, w_kv, k_norm):
    bm, nm, d = mem.shape
    out = jax.ShapeDtypeStruct((DEPTH, bm, nm, d), BF16)
    return pl.pallas_call(
        _mem_kv_body,
        out_shape=(out, out),
        grid=(DEPTH, bm),
        in_specs=[pl.BlockSpec((1, nm, d), lambda l, b: (b, 0, 0)),
                  pl.BlockSpec((1, 1, d), lambda l, b: (l, 0, 0)),
                  pl.BlockSpec((1, d, 2 * d), lambda l, b: (l, 0, 0)),
                  pl.BlockSpec((1, 1, X_DH), lambda l, b: (l, 0, 0))],
        out_specs=(pl.BlockSpec((1, 1, nm, d), lambda l, b: (l, b, 0, 0)),
                   pl.BlockSpec((1, 1, nm, d), lambda l, b: (l, b, 0, 0))),
        compiler_params=_params("arbitrary", "arbitrary"),
        name="mem_kv",
    )(mem, norm_mem.reshape(DEPTH, 1, d), w_kv, k_norm.reshape(DEPTH, 1, X_DH))


def _cross_body(x_ref, g_ref, wq_ref, qg_ref, k_ref, v_ref, wo_ref, o_ref):
    x = x_ref[...]
    h = _rms(x, g_ref[...]).astype(BF16)
    q = jnp.dot(h, wq_ref[...], preferred_element_type=F32)
    heads = []
    for hd in range(X_HEADS):
        sl = slice(hd * X_DH, (hd + 1) * X_DH)
        qn = _rms(q[:, sl], qg_ref[...]).astype(BF16)
        s = lax.dot_general(qn, k_ref[0, :, sl], NT_DIMS, preferred_element_type=F32)
        p = jnp.exp(s - jnp.max(s, axis=-1, keepdims=True))
        l = jnp.sum(p, axis=-1, keepdims=True)
        oh = jnp.dot(p.astype(BF16), v_ref[0, :, sl], preferred_element_type=F32) / l
        heads.append(oh.astype(BF16))
    o = jnp.concatenate(heads, axis=-1)
    o_ref[...] = x + jnp.dot(o, wo_ref[...], preferred_element_type=F32)


def cross_block(x, seq_len, gain, w_q, q_gain, kn, v, w_o, tm=ROW_TILE):
    t, d = x.shape
    nm = kn.shape[1]
    per_seq = seq_len // tm
    return pl.pallas_call(
        _cross_body,
        out_shape=jax.ShapeDtypeStruct((t, d), F32),
        grid=(t // tm,),
        in_specs=[pl.BlockSpec((tm, d), lambda i: (i, 0)),
                  pl.BlockSpec((1, d), lambda i: (0, 0)),
                  pl.BlockSpec((d, d), lambda i: (0, 0)),
                  pl.BlockSpec((1, X_DH), lambda i: (0, 0)),
                  pl.BlockSpec((1, nm, d), lambda i: (i // per_seq, 0, 0)),
                  pl.BlockSpec((1, nm, d), lambda i: (i // per_seq, 0, 0)),
                  pl.BlockSpec((d, d), lambda i: (0, 0))],
        out_specs=pl.BlockSpec((tm, d), lambda i: (i, 0)),
        compiler_params=_params("parallel"),
        name="cross_block",
    )(x, gain.reshape(1, d), w_q, (q_gain * X_DH ** -0.5).reshape(1, X_DH), kn, v, w_o)


LANES = 128


def _toeplitz_body(k_ref, o_ref):
    lc, gs = S5_CHUNK, S5_GROUP
    kern = k_ref[0]
    n = kern.shape[1]
    per_tile = LANES // gs
    for r in range(per_tile):
        shifted = pltpu.roll(kern, n - gs * r, 1) if r else kern
        for a in range(lc // per_tile):
            s = lc - 1 - (per_tile * a + r)
            o_ref[0, s * gs:(s + 1) * gs, :] = shifted[:, LANES * a:LANES * a + lc * gs].astype(o_ref.dtype)


def toeplitz_expand(kern_flat):
    g, gs, n = kern_flat.shape
    width = S5_CHUNK * S5_GROUP
    return pl.pallas_call(
        _toeplitz_body,
        out_shape=jax.ShapeDtypeStruct((g, width, width), BF16),
        grid=(g,),
        in_specs=[pl.BlockSpec((1, gs, n), lambda i: (i, 0, 0))],
        out_specs=pl.BlockSpec((1, width, width), lambda i: (i, 0, 0)),
        compiler_params=_params("parallel"),
        name="toeplitz_expand",
    )(kern_flat)


def s5_operators(lam_re, lam_im, log_step, b_re, b_im, c_re, c_im, d):
    lc = S5_CHUNK
    lam = lax.complex(lam_re.astype(F32), lam_im.astype(F32))
    step = jnp.exp(log_step.astype(F32))[..., None]
    lam_bar = jnp.exp(lam * step)
    b_bar = ((lam_bar - 1.0) / lam)[..., None] * lax.complex(b_re.astype(F32), b_im.astype(F32))
    c = lax.complex(c_re.astype(F32), c_im.astype(F32))
    pw = jnp.cumprod(jnp.broadcast_to(lam_bar[..., None], lam_bar.shape + (lc,)), axis=-1)
    pw = jnp.concatenate([jnp.ones_like(pw[..., :1]), pw], axis=-1)
    kern = jnp.einsum('zgcp,zgpt,zgpd->zgtcd', c, pw[..., :lc], b_bar, precision=HIGHEST).real
    kf, kb = kern[0], kern[1]
    k0 = kf[:, :1] + kb[:, :1] + (d.astype(F32)[:, :, None] * jnp.eye(S5_GROUP, dtype=F32))[:, None]
    kern_full = jnp.concatenate([kb[:, :0:-1], k0, kf[:, 1:]], axis=1)
    kern_flat = kern_full.transpose(0, 3, 1, 2).reshape(S5_GROUPS, S5_GROUP, (2 * lc - 1) * S5_GROUP)
    toep = toeplitz_expand(jnp.pad(kern_flat, ((0, 0), (0, 0), (0, S5_GROUP))))

    pf = jnp.einsum('gps,gpd->gsdp', pw[0][..., lc - 1::-1], b_bar[0])
    pb = jnp.einsum('gps,gpd->gsdp', pw[1][..., :lc], b_bar[1])
    p_op = jnp.concatenate([pf.real, pb.real, pf.imag, pb.imag], axis=-1)
    p_op = p_op.reshape(S5_GROUPS, lc * S5_GROUP, 4 * S5_STATE)

    qf = jnp.einsum('gcp,gpt->gptc', c[0], pw[0][..., 1:])
    qb = jnp.einsum('gcp,gpt->gptc', c[1], pw[1][..., :0:-1])
    q_op = jnp.concatenate([qf.real, qb.real, -qf.imag, -qb.imag], axis=1)
    q_op = q_op.reshape(S5_GROUPS, 4 * S5_STATE, lc * S5_GROUP)

    a = pw[..., lc]
    coef = jnp.stack([jnp.concatenate([a[0].real, a[1].real], -1),
                      jnp.concatenate([a[0].imag, a[1].imag], -1)], axis=1)
    return toep, _split_bf16(p_op), _split_bf16(q_op), coef


def _split_bf16(x):
    hi = x.astype(BF16)
    lo = (x - hi.astype(F32)).astype(BF16)
    return jnp.stack([hi, lo], axis=1)


SUBLANES = 8


def _s5_group_body(u_ref, t_ref, p_ref, q_ref, c_ref, y_ref,
                   v_re_ref, v_im_ref, xf_re_ref, xf_im_ref, xb_re_ref, xb_im_ref, *, n_chunks, bsz):
    u = u_ref[0]
    half = 2 * S5_STATE
    v = (jnp.dot(u, p_ref[0, 0], preferred_element_type=F32)
         + jnp.dot(u, p_ref[0, 1], preferred_element_type=F32))
    v_re_ref[...] = v[:, :half]
    v_im_ref[...] = v[:, half:]
    coef = c_ref[0]
    a_re, a_im = coef[0:1], coef[1:2]
    fwd = lax.broadcasted_iota(jnp.int32, (bsz, half), 1) < S5_STATE

    def step(i, carry):
        re, im = carry
        rows_f = pl.ds(i, bsz, stride=n_chunks)
        rows_b = pl.ds(n_chunks - 1 - i, bsz, stride=n_chunks)
        xf_re_ref[rows_f, :] = re
        xf_im_ref[rows_f, :] = im
        xb_re_ref[rows_b, :] = re
        xb_im_ref[rows_b, :] = im
        re_next = a_re * re - a_im * im + jnp.where(fwd, v_re_ref[rows_f, :], v_re_ref[rows_b, :])
        im_next = a_re * im + a_im * re + jnp.where(fwd, v_im_ref[rows_f, :], v_im_ref[rows_b, :])
        return re_next, im_next

    zero = jnp.zeros((bsz, half), F32)
    lax.fori_loop(0, n_chunks, step, (zero, zero))
    is_fwd = lax.broadcasted_iota(jnp.int32, xf_re_ref.shape, 1) < S5_STATE
    x = jnp.concatenate([jnp.where(is_fwd, xf_re_ref[...], xb_re_ref[...]),
                         jnp.where(is_fwd, xf_im_ref[...], xb_im_ref[...])], axis=1)
    x_hi = x.astype(BF16)
    x_lo = (x - x_hi.astype(F32)).astype(BF16)
    y = (jnp.dot(u, t_ref[0], preferred_element_type=F32)
         + jnp.dot(x_hi, q_ref[0, 0], preferred_element_type=F32)
         + jnp.dot(x_lo, q_ref[0, 0], preferred_element_type=F32)
         + jnp.dot(x_hi, q_ref[0, 1], preferred_element_type=F32))
    y_ref[0] = y.astype(y_ref.dtype)


def s5_scan(u, bsz, seq_len, ops):
    toep, p_op, q_op, coef = ops
    lc, g, w = S5_CHUNK, S5_GROUPS, S5_CHUNK * S5_GROUP
    n = seq_len // lc
    c = bsz * n
    ns = 4 * S5_STATE
    ug = u.reshape(bsz, n, lc, g, S5_GROUP).transpose(3, 0, 1, 2, 4).reshape(g, c, w)
    group = lambda *shape: pl.BlockSpec((1,) + shape, lambda i: (i,) + (0,) * len(shape))
    y = pl.pallas_call(
        functools.partial(_s5_group_body, n_chunks=n, bsz=bsz),
        out_shape=jax.ShapeDtypeStruct((g, c, w), BF16),
        grid=(g,),
        in_specs=[group(c, w), group(w, w), group(2, w, ns), group(2, ns, w), group(2, ns // 2)],
        out_specs=group(c, w),
        scratch_shapes=[pltpu.VMEM((c, ns // 2), F32)] * 6,
        compiler_params=_params("parallel"),
        name="s5_group",
    )(ug, toep, p_op, q_op, coef)
    return y.reshape(g, bsz, n, lc, S5_GROUP).transpose(1, 2, 3, 0, 4).reshape(bsz * seq_len, S5_WIDTH)


def _gla_direction(q_ref, k_ref, v_ref, g_ref, wg, bg, s_ref, o_ref, forward, chunks):
    cs = GLA_CHUNK
    tb = chunks * cs
    glr = g_ref[...]
    logit = (jnp.dot(glr, wg[0], preferred_element_type=F32)
             + jnp.dot(glr, wg[1], preferred_element_type=F32) + bg)
    g = (jnp.minimum(logit, 0.0) - jnp.log1p(jnp.exp(-jnp.abs(logit)))) / GLA_TAU
    row = lax.broadcasted_iota(jnp.int32, (tb, tb), 0)
    col = lax.broadcasted_iota(jnp.int32, (tb, tb), 1)
    same_chunk = (row // cs) == (col // cs)
    within = same_chunk & ((col <= row) if forward else (col >= row))
    ones = jnp.where(within, 1.0, 0.0).astype(BF16)
    bcum = jnp.zeros_like(g)
    rest = g
    for _ in range(3):
        term = rest.astype(BF16)
        bcum = bcum + jnp.dot(ones, term, preferred_element_type=F32)
        rest = rest - term.astype(F32)

    srow = lax.broadcasted_iota(jnp.int32, (GLA_HEADS * cs, cs), 0) & (cs - 1)
    scol = lax.broadcasted_iota(jnp.int32, (GLA_HEADS * cs, cs), 1)
    keep = (scol <= srow) if forward else (scol >= srow)
    lane = lax.broadcasted_iota(jnp.int32, (1, GLA_QK), 1)
    head_lanes = [((lane >= h * GLA_DK) & (lane < (h + 1) * GLA_DK)).astype(F32) for h in range(GLA_HEADS)]
    state_mask = (lax.broadcasted_iota(jnp.int32, (GLA_V, GLA_QK), 0) // GLA_DV
                  == lax.broadcasted_iota(jnp.int32, (GLA_V, GLA_QK), 1) // GLA_DK)

    i_ref = cs // 2 - 1 if forward else cs // 2
    i_last = cs - 1 if forward else 0
    scale = GLA_DK ** -0.5
    for ci in (range(chunks) if forward else reversed(range(chunks))):
        rows = slice(ci * cs, (ci + 1) * cs)
        b = bcum[rows]
        bref = b[i_ref:i_ref + 1]
        blast = b[i_last:i_last + 1]
        q = q_ref[rows, :].astype(F32) * scale
        k = k_ref[rows, :].astype(F32)
        v = v_ref[rows, :]
        q_rel = q * jnp.exp(b - bref)
        k_rel = (k * jnp.exp(bref - b)).astype(BF16)
        k_out = (k * jnp.exp(blast - b)).astype(BF16)
        q_dec = (q * jnp.exp(b)).astype(BF16)
        decay = jnp.exp(blast)
        q_heads = jnp.concatenate([q_rel * hm for hm in head_lanes], axis=0).astype(BF16)
        s = lax.dot_general(q_heads, k_rel, NT_DIMS, preferred_element_type=F32)
        s = jnp.where(keep, s, 0.0).astype(BF16)
        st = s_ref[...]
        o_inter = lax.dot_general(q_dec, st.astype(BF16), NT_DIMS, preferred_element_type=F32)
        o_intra = [jnp.dot(s[h * cs:(h + 1) * cs], v[:, h * GLA_DV:(h + 1) * GLA_DV],
                           preferred_element_type=F32) for h in range(GLA_HEADS)]
        o_ref[rows, :] = o_inter + jnp.concatenate(o_intra, axis=-1)
        kv = lax.dot_general(v, k_out, TN_DIMS, preferred_element_type=F32)
        s_ref[...] = decay * st + jnp.where(state_mask, kv, 0.0)


def _gla_body(qf_ref, kf_ref, vf_ref, gf_ref, qb_ref, kb_ref, vb_ref, gb_ref, wg_ref, bg_ref,
              of_ref, ob_ref, sf_ref, sb_ref, *, chunks):
    @pl.when(pl.program_id(1) == 0)
    def _():
        sf_ref[...] = jnp.zeros_like(sf_ref)
        sb_ref[...] = jnp.zeros_like(sb_ref)

    _gla_direction(qf_ref, kf_ref, vf_ref, gf_ref, wg_ref[0], bg_ref[0], sf_ref, of_ref, True, chunks)
    _gla_direction(qb_ref, kb_ref, vb_ref, gb_ref, wg_ref[1], bg_ref[1], sb_ref, ob_ref, False, chunks)


def gla_scan(proj, bsz, seq_len, w_gate, b_gate, chunks=4):
    t = proj.shape[0]
    tb = chunks * GLA_CHUNK
    nb = seq_len // tb
    fwd = lambda cb: (lambda b, i: (b * nb + i, cb))
    bwd = lambda cb: (lambda b, i: (b * nb + nb - 1 - i, cb))
    qc, kc, vc, gc = 512 // GLA_QK, 768 // GLA_QK, 1024 // GLA_V, 2048 // 128
    wg = jnp.zeros((2, 128, GLA_QK), F32)
    wg = wg.at[0, :GLA_RANK].set(w_gate[0].astype(F32)).at[1, GLA_RANK:2 * GLA_RANK].set(w_gate[1].astype(F32))
    wg = _split_bf16(wg)
    out = jax.ShapeDtypeStruct((t, GLA_V), F32)
    state = pltpu.VMEM((GLA_V, GLA_QK), F32)
    return pl.pallas_call(
        functools.partial(_gla_body, chunks=chunks),
        out_shape=(out, out),
        grid=(bsz, nb),
        in_specs=[pl.BlockSpec((tb, GLA_QK), fwd(qc)), pl.BlockSpec((tb, GLA_QK), fwd(kc)),
                  pl.BlockSpec((tb, GLA_V), fwd(vc)), pl.BlockSpec((tb, 128), fwd(gc)),
                  pl.BlockSpec((tb, GLA_QK), bwd(qc)), pl.BlockSpec((tb, GLA_QK), bwd(kc)),
                  pl.BlockSpec((tb, GLA_V), bwd(vc)), pl.BlockSpec((tb, 128), bwd(gc)),
                  pl.BlockSpec((2, 2, 128, GLA_QK), lambda b, i: (0, 0, 0, 0)),
                  pl.BlockSpec((2, 1, GLA_QK), lambda b, i: (0, 0, 0))],
        out_specs=(pl.BlockSpec((tb, GLA_V), fwd(0)), pl.BlockSpec((tb, GLA_V), bwd(0))),
        scratch_shapes=[state, state],
        compiler_params=_params("parallel", "arbitrary"),
        name="gla_scan",
    )(proj, proj, proj, proj, proj, proj, proj, proj, wg, b_gate.astype(F32).reshape(2, 1, GLA_QK))


def _even_out_body(x_ref, ys_ref, of_ref, ob_ref, og_ref, wglu_ref, gn_ref, wtop_ref, wbot_ref, o_ref):
    y = ys_ref[...].astype(F32)
    y = 0.5 * y * (1.0 + jnp.tanh(math.sqrt(2.0 / math.pi) * (y + 0.044715 * (y * y * y))))
    gate = jnp.dot(y.astype(BF16), wglu_ref[...], preferred_element_type=F32)
    y = y * _sigmoid(gate)
    o = of_ref[...] + ob_ref[...]
    og = og_ref[...].astype(F32)
    heads = []
    for h in range(GLA_HEADS):
        sl = slice(h * GLA_DV, (h + 1) * GLA_DV)
        heads.append(_rms(o[:, sl], gn_ref[...]))
    o = jnp.concatenate(heads, axis=-1) * (og * _sigmoid(og))
    o_ref[...] = (x_ref[...]
                  + jnp.dot(y.astype(BF16), wtop_ref[...], preferred_element_type=F32)
                  + jnp.dot(o.astype(BF16), wbot_ref[...], preferred_element_type=F32))


def even_out(x, ys, o_f, o_b, proj, w_glu, gla_norm, w_out, tm=ROW_TILE):
    t, d = x.shape
    row = lambda i: (i, 0)
    const = lambda i: (0, 0)
    return pl.pallas_call(
        _even_out_body,
        out_shape=jax.ShapeDtypeStruct((t, d), F32),
        grid=(t // tm,),
        in_specs=[pl.BlockSpec((tm, d), row), pl.BlockSpec((tm, S5_WIDTH), row),
                  pl.BlockSpec((tm, GLA_V), row), pl.BlockSpec((tm, GLA_V), row),
                  pl.BlockSpec((tm, GLA_V), lambda i: (i, 1536 // GLA_V)),
                  pl.BlockSpec((S5_WIDTH, S5_WIDTH), const), pl.BlockSpec((1, GLA_DV), const),
                  pl.BlockSpec((S5_WIDTH, d), const), pl.BlockSpec((GLA_V, d), const)],
        out_specs=pl.BlockSpec((tm, d), row),
        compiler_params=_params("parallel"),
        name="even_out",
    )(x, ys, o_f, o_b, proj, w_glu, gla_norm.astype(F32).reshape(1, GLA_DV),
      w_out[:S5_WIDTH], w_out[S5_WIDTH:])


QK_NORM_WIDTH = 256


def _seg_rms(a, gain, same_seg):
    ssq = jnp.dot((a * a).astype(BF16), same_seg, preferred_element_type=F32)
    return a * lax.rsqrt(ssq * (1.0 / DIFF_DK) + EPS) * gain


def _qkv_body(x_ref, g_ref, w_ref, qk_gain_ref, o_ref, h_ref):
    j = pl.program_id(1)

    @pl.when(j == 0)
    def _():
        h_ref[...] = _rms(x_ref[...], g_ref[...]).astype(BF16)

    acc = jnp.dot(h_ref[...], w_ref[...], preferred_element_type=F32)

    @pl.when(j < 2)
    def _():
        gain = qk_gain_ref[0]
        w = QK_NORM_WIDTH
        same_seg = jnp.where(lax.broadcasted_iota(jnp.int32, (w, w), 0) // DIFF_DK
                             == lax.broadcasted_iota(jnp.int32, (w, w), 1) // DIFF_DK, 1.0, 0.0).astype(BF16)
        for c in range(acc.shape[1] // w):
            sl = slice(c * w, (c + 1) * w)
            o_ref[:, sl] = _seg_rms(acc[:, sl], gain, same_seg).astype(o_ref.dtype)

    @pl.when(j == 2)
    def _():
        o_ref[...] = acc.astype(o_ref.dtype)


def qkv_project(x, gain, w, q_norm, k_norm, tm=ROW_TILE):
    t, d = x.shape
    reps = QK_NORM_WIDTH // DIFF_DK
    qg = jnp.tile(q_norm.astype(F32) * DIFF_DK ** -0.5, reps)
    kg = jnp.tile(k_norm.astype(F32), reps)
    qk_gain = jnp.stack([qg, kg]).reshape(2, 1, QK_NORM_WIDTH)
    return pl.pallas_call(
        _qkv_body,
        out_shape=jax.ShapeDtypeStruct((t, 3 * d), BF16),
        grid=(t // tm, 3),
        in_specs=[pl.BlockSpec((tm, d), lambda i, j: (i, 0)),
                  pl.BlockSpec((1, d), lambda i, j: (0, 0)),
                  pl.BlockSpec((d, d), lambda i, j: (0, j)),
                  pl.BlockSpec((1, 1, QK_NORM_WIDTH), lambda i, j: (jnp.minimum(j, 1), 0, 0))],
        out_specs=pl.BlockSpec((tm, d), lambda i, j: (i, j)),
        scratch_shapes=[pltpu.VMEM((tm, d), BF16)],
        compiler_params=_params("parallel", "arbitrary"),
        name="qkv_project",
    )(x, gain.reshape(1, d), w, qk_gain)


POS_SPLIT = 16
POS_SHIFT = 4
AUG_LANE = DIFF_DK
SOFTMAX_GROUP_ELEMS = 32 * 1024
ROW_PARTS = 2


def _pos_terms(shape):
    lane = lax.broadcasted_iota(jnp.int32, shape, 1)
    pos = lax.broadcasted_iota(jnp.int32, shape, 0)
    hi = lax.shift_right_logical(pos, POS_SHIFT).astype(F32)
    lo = (pos & (POS_SPLIT - 1)).astype(F32)
    return lane, hi, lo


def _lane_select(lane, first, values):
    out = 0.0
    for n, val in reversed(list(enumerate(values))):
        out = jnp.where(lane == first + n, val, out)
    return out


def _diff_attn_body(slope_ref, q_ref, k_ref, v_ref, lq1_ref, lk1_ref, lq2_ref, lk2_ref, sub_ref,
                    o_ref, ka_ref, va_ref, qs_ref, s0_ref, s1_ref, p0_ref, p1_ref, vs_ref,
                    *, blk, seq_len, lambda_init):
    h = pl.program_id(1)
    qi = pl.program_id(2)
    slope = slope_ref[h]
    nk = seq_len // blk
    s_refs, p_refs = (s0_ref, s1_ref), (p0_ref, p1_ref)

    @pl.when(qi == 0)
    def _():
        va_ref[:, :DIFF_DV] = v_ref[...]
        va_ref[:, DIFF_DV:] = jnp.ones((seq_len, DIFF_DV), BF16)

        def build(t, carry):
            rows = pl.ds(pl.multiple_of(t * blk, blk), blk)
            kf = k_ref[rows, :].astype(F32)
            lane, hi, lo = _pos_terms(kf.shape)
            aug = _lane_select(lane, AUG_LANE,
                               [-blk * slope, -POS_SPLIT * slope, -slope,
                                (blk * slope) * jnp.asarray(t, F32), (POS_SPLIT * slope) * hi, slope * lo])
            for z, kz in enumerate((kf, pltpu.roll(kf, DIFF_DK, 1))):
                ka_ref[z, rows, :] = jnp.where(lane < DIFF_DK, kz, aug).astype(BF16)
            return carry

        lax.fori_loop(0, nk, build, 0)

    qf = q_ref[...].astype(F32)
    lane, hi, lo = _pos_terms(qf.shape)
    qa = _lane_select(lane, AUG_LANE, [jnp.asarray(qi, F32), hi, lo, 1.0, 1.0, 1.0])
    for z, qz in enumerate((qf, pltpu.roll(qf, DIFF_DK, 1))):
        qs_ref[z] = jnp.where(lane < DIFF_DK, qz, 0.0).astype(BF16)
        qs_ref[2 + z] = jnp.where(lane < DIFF_DK, qz, qa).astype(BF16)
        qs_ref[4 + z] = jnp.where(lane < DIFF_DK, qz, -qa).astype(BF16)

    def key_block(t):
        if t == 0:
            return qi, 0
        j = (t - 1) + jnp.asarray(qi <= t - 1, jnp.int32)
        return j, jnp.where(j < qi, 2, 4)

    def key_rows(t):
        j, _ = key_block(t)
        return pl.ds(pl.multiple_of(j * blk, blk), blk)

    part = blk // ROW_PARTS
    parts = [slice(r * part, (r + 1) * part) for r in range(ROW_PARTS)]
    for z, s_ref in enumerate(s_refs):
        for rows in parts:
            for t in range(nk):
                _, variant = key_block(t)
                s_ref[rows, t * blk:(t + 1) * blk] = lax.dot_general(
                    qs_ref[variant + z, rows, :], ka_ref[z, key_rows(t), :], NT_DIMS,
                    preferred_element_type=F32)

    rg = SOFTMAX_GROUP_ELEMS // seq_len
    rel = (lax.broadcasted_iota(jnp.int32, (rg, blk), 0) - lax.broadcasted_iota(jnp.int32, (rg, blk), 1))
    for t in range(nk):
        vs_ref[t * blk:(t + 1) * blk, :] = va_ref[key_rows(t), :]

    acc = []
    for s_ref, p_ref in zip(s_refs, p_refs):
        for g in range(blk // rg):
            rows = slice(g * rg, (g + 1) * rg)
            s_diag = s_ref[rows, :blk] - slope * jnp.abs(rel + g * rg).astype(F32)
            s_rest = s_ref[rows, blk:]
            m = jnp.maximum(jnp.max(s_diag, axis=-1, keepdims=True), jnp.max(s_rest, axis=-1, keepdims=True))
            p_ref[rows, :blk] = jnp.exp(s_diag - m).astype(BF16)
            p_ref[rows, blk:] = jnp.exp(s_rest - m).astype(BF16)
        acc.append([jnp.dot(p_ref[rows, :], vs_ref[...], preferred_element_type=F32) for rows in parts])

    lam = (jnp.exp(jnp.sum(lq1_ref[...] * lk1_ref[...], axis=-1, keepdims=True))
           - jnp.exp(jnp.sum(lq2_ref[...] * lk2_ref[...], axis=-1, keepdims=True)) + lambda_init)
    for rows, a0, a1 in zip(parts, *acc):
        o = (a0[:, :DIFF_DV] / a0[:, DIFF_DV:DIFF_DV + 1]
             - lam * (a1[:, :DIFF_DV] / a1[:, DIFF_DV:DIFF_DV + 1]))
        o_ref[rows, :] = (_rms(o, sub_ref[...]) * (1.0 - lambda_init)).astype(o_ref.dtype)


def diff_attention(qkv, bsz, seq_len, lq1, lk1, lq2, lk2, sub_norm, lambda_init, blk=512):
    t = qkv.shape[0]
    nq = seq_len // blk
    assert blk // POS_SPLIT <= 256, "hi part of a block position must stay exact in bf16"
    slopes = jnp.asarray(2.0 ** (-8.0 * np.arange(1, DIFF_HEADS + 1, dtype=np.float32) / DIFF_HEADS), F32)
    vec = lambda a: a.astype(F32).reshape(1, DIFF_DK)
    const = lambda b, h, i: (0, 0)
    return pl.pallas_call(
        functools.partial(_diff_attn_body, blk=blk, seq_len=seq_len, lambda_init=lambda_init),
        out_shape=jax.ShapeDtypeStruct((t, DIFF_HEADS * DIFF_DV), BF16),
        grid=(bsz, DIFF_HEADS, nq),
        in_specs=[pl.BlockSpec(memory_space=pltpu.SMEM),
                  pl.BlockSpec((blk, 2 * DIFF_DK), lambda b, h, i: (b * nq + i, h)),
                  pl.BlockSpec((seq_len, 2 * DIFF_DK), lambda b, h, i: (b, DIFF_HEADS + h)),
                  pl.BlockSpec((seq_len, DIFF_DV), lambda b, h, i: (b, 2 * DIFF_HEADS + h)),
                  pl.BlockSpec((1, DIFF_DK), const), pl.BlockSpec((1, DIFF_DK), const),
                  pl.BlockSpec((1, DIFF_DK), const), pl.BlockSpec((1, DIFF_DK), const),
                  pl.BlockSpec((1, DIFF_DV), const)],
        out_specs=pl.BlockSpec((blk, DIFF_DV), lambda b, h, i: (b * nq + i, h)),
        scratch_shapes=[pltpu.VMEM((2, seq_len, 2 * DIFF_DK), BF16),
                        pltpu.VMEM((seq_len, 2 * DIFF_DV), BF16),
                        pltpu.VMEM((6, blk, 2 * DIFF_DK), BF16),
                        pltpu.VMEM((blk, seq_len), F32),
                        pltpu.VMEM((blk, seq_len), F32),
                        pltpu.VMEM((blk, seq_len), BF16),
                        pltpu.VMEM((blk, seq_len), BF16),
                        pltpu.VMEM((seq_len, 2 * DIFF_DV), BF16)],
        compiler_params=_params("parallel", "parallel", "arbitrary"),
        name="diff_attention",
    )(slopes, qkv, qkv, qkv, vec(lq1), vec(lk1), vec(lq2), vec(lk2),
      sub_norm.astype(F32).reshape(1, DIFF_DV))


def _pad_ev_w_in(w):
    return jnp.pad(w, ((0, 0), (0, EV_PAD_COLS - w.shape[1]))).astype(BF16)


def _trunk(x3, mem, p):
    bsz, seq_len, d = x3.shape
    x = x3.reshape(bsz * seq_len, d)
    kn_all, v_all = mem_kv(mem, p['norm_mem'], p['x_w_kv'], p['x_k_norm'])
    for layer in range(DEPTH):
        if layer % 2 == 0:
            e = layer // 2
            proj = norm_matmul(x, p['norm_mix'][layer], p['ev_w_in'][e], BF16, EV_TN)
            ys = s5_scan(proj[:, :S5_WIDTH], bsz, seq_len, p['s5_ops'][e])
            o_f, o_b = gla_scan(proj, bsz, seq_len, p['gla_w_gate'][e], p['gla_b_gate'][e])
            x = even_out(x, ys, o_f, o_b, proj, p['s5_w_glu'][e], p['gla_norm'][e], p['ev_w_out'][e])
        else:
            o = layer // 2
            lambda_init = 0.8 - 0.6 * math.exp(-0.3 * layer)
            qkv = qkv_project(x, p['norm_mix'][layer], p['od_w_in'][o],
                              p['diff_q_norm'][o], p['diff_k_norm'][o])
            att = diff_attention(qkv, bsz, seq_len, p['diff_lambda_q1'][o], p['diff_lambda_k1'][o],
                                 p['diff_lambda_q2'][o], p['diff_lambda_k2'][o], p['diff_norm'][o],
                                 lambda_init)
            x = matmul_residual(att, p['od_w_out'][o], x, tn=d)
        x = cross_block(x, seq_len, p['norm_cross'][layer], p['x_w_q'][layer], p['x_q_norm'][layer],
                        kn_all[layer], v_all[layer], p['x_w_o'][layer])
        x = mlp_block(x, p['norm_mlp'][layer], p['mlp_w1'][layer], p['mlp_w2'][layer])
    return x.reshape(bsz, seq_len, d)


def kernel(x_prompt, x_sample, mem_prompt, mem_sample, norm_mix, norm_cross, norm_mem, norm_mlp,
           ev_w_in, ev_w_out, s5_lambda_re, s5_lambda_im, s5_log_step, s5_b_re, s5_b_im,
           s5_c_re, s5_c_im, s5_d, s5_w_glu, gla_w_gate, gla_b_gate, gla_norm,
           od_w_in, od_w_out, diff_q_norm, diff_k_norm, diff_lambda_q1, diff_lambda_k1,
           diff_lambda_q2, diff_lambda_k2, diff_norm, x_w_q, x_w_kv, x_w_o, x_q_norm, x_k_norm,
           mlp_w1, mlp_w2):
    n_even = ev_w_in.shape[0]
    bf = lambda w: w.astype(BF16)
    p = dict(
        norm_mix=norm_mix, norm_cross=norm_cross, norm_mem=norm_mem, norm_mlp=norm_mlp,
        ev_w_in=[_pad_ev_w_in(ev_w_in[e]) for e in range(n_even)], ev_w_out=bf(ev_w_out),
        s5_ops=[s5_operators(s5_lambda_re[e], s5_lambda_im[e], s5_log_step[e], s5_b_re[e], s5_b_im[e],
                             s5_c_re[e], s5_c_im[e], s5_d[e]) for e in range(n_even)],
        s5_w_glu=bf(s5_w_glu), gla_w_gate=gla_w_gate, gla_b_gate=gla_b_gate, gla_norm=gla_norm,
        od_w_in=bf(od_w_in), od_w_out=bf(od_w_out), diff_q_norm=diff_q_norm, diff_k_norm=diff_k_norm,
        diff_lambda_q1=diff_lambda_q1, diff_lambda_k1=diff_lambda_k1,
        diff_lambda_q2=diff_lambda_q2, diff_lambda_k2=diff_lambda_k2, diff_norm=diff_norm,
        x_w_q=bf(x_w_q), x_w_kv=bf(x_w_kv), x_w_o=bf(x_w_o), x_q_norm=x_q_norm, x_k_norm=x_k_norm,
        mlp_w1=bf(mlp_w1), mlp_w2=bf(mlp_w2))
    return (_trunk(x_prompt, mem_prompt, p), _trunk(x_sample, mem_sample, p))
```

```python
import functools
import math

import numpy as np
import jax
import jax.numpy as jnp
from jax import lax
from jax.experimental import pallas as pl
from jax.experimental.pallas import tpu as pltpu

F32 = jnp.float32
BF16 = jnp.bfloat16
HIGHEST = lax.Precision.HIGHEST

D_MODEL = 1024
DEPTH = 4
EPS = 1e-6
S5_WIDTH = 512
S5_GROUP = 16
S5_GROUPS = 32
S5_STATE = 64
S5_CHUNK = 64
GLA_HEADS = 4
GLA_DV = 128
GLA_DK = 64
GLA_RANK = 16
GLA_TAU = 16.0
GLA_CHUNK = 64
GLA_QK = GLA_HEADS * GLA_DK
GLA_V = GLA_HEADS * GLA_DV
EV_PAD_COLS = 2304
EV_TN = 768
DIFF_HEADS = 8
DIFF_DK = 64
DIFF_DV = 128
X_HEADS = 4
X_DH = 256
D_FF = 4096

ROW_TILE = 512
VMEM_LIMIT = 48 * 1024 * 1024

NT_DIMS = (((1,), (1,)), ((), ()))
TN_DIMS = (((0,), (0,)), ((), ()))


def _params(*sem, flags=None):
    return pltpu.CompilerParams(dimension_semantics=sem, vmem_limit_bytes=VMEM_LIMIT, flags=flags)


def _rms(x, gain):
    ms = jnp.mean(x * x, axis=-1, keepdims=True)
    return x * lax.rsqrt(ms + EPS) * gain


def _sigmoid(x):
    return 1.0 / (1.0 + jnp.exp(-x))


def _norm_matmul_body(x_ref, g_ref, w_ref, o_ref, *, tn):
    h = _rms(x_ref[...], g_ref[...]).astype(BF16)
    for j in range(w_ref.shape[1] // tn):
        cols = slice(j * tn, (j + 1) * tn)
        o_ref[:, cols] = jnp.dot(h, w_ref[:, cols], preferred_element_type=F32).astype(o_ref.dtype)


def norm_matmul(x, gain, w, out_dtype, tn, tm=ROW_TILE):
    t, d = x.shape
    n = w.shape[1]
    return pl.pallas_call(
        functools.partial(_norm_matmul_body, tn=tn),
        out_shape=jax.ShapeDtypeStruct((t, n), out_dtype),
        grid=(t // tm,),
        in_specs=[pl.BlockSpec((tm, d), lambda i: (i, 0)),
                  pl.BlockSpec((1, d), lambda i: (0, 0)),
                  pl.BlockSpec((d, n), lambda i: (0, 0))],
        out_specs=pl.BlockSpec((tm, n), lambda i: (i, 0)),
        compiler_params=_params("parallel"),
        name="norm_matmul",
    )(x, gain.reshape(1, d), w)


def _matmul_res_body(a_ref, w_ref, r_ref, o_ref):
    o_ref[...] = r_ref[...] + jnp.dot(a_ref[...].astype(BF16), w_ref[...], preferred_element_type=F32)


def matmul_residual(a, w, res, tn=512, tm=ROW_TILE):
    t, k = a.shape
    n = w.shape[1]
    return pl.pallas_call(
        _matmul_res_body,
        out_shape=jax.ShapeDtypeStruct((t, n), F32),
        grid=(t // tm, n // tn),
        in_specs=[pl.BlockSpec((tm, k), lambda i, j: (i, 0)),
                  pl.BlockSpec((k, tn), lambda i, j: (0, j)),
                  pl.BlockSpec((tm, tn), lambda i, j: (i, j))],
        out_specs=pl.BlockSpec((tm, tn), lambda i, j: (i, j)),
        compiler_params=_params("parallel", "arbitrary"),
        name="matmul_residual",
    )(a, w, res)


def _mlp_body(x_ref, g_ref, w1_ref, w2_ref, o_ref, h_ref, acc_ref):
    f = pl.program_id(1)

    @pl.when(f == 0)
    def _():
        h_ref[...] = _rms(x_ref[...], g_ref[...]).astype(BF16)
        acc_ref[...] = jnp.zeros_like(acc_ref)

    hid = jnp.dot(h_ref[...], w1_ref[...], preferred_element_type=F32)
    hid = jnp.square(jnp.maximum(hid, 0.0)).astype(BF16)
    acc_ref[...] += jnp.dot(hid, w2_ref[...], preferred_element_type=F32)

    @pl.when(f == pl.num_programs(1) - 1)
    def _():
        o_ref[...] = x_ref[...] + acc_ref[...]


def mlp_block(x, gain, w1, w2, tf=512, tm=2 * ROW_TILE):
    t, d = x.shape
    ff = w1.shape[1]
    return pl.pallas_call(
        _mlp_body,
        out_shape=jax.ShapeDtypeStruct((t, d), F32),
        grid=(t // tm, ff // tf),
        in_specs=[pl.BlockSpec((tm, d), lambda i, f: (i, 0)),
                  pl.BlockSpec((1, d), lambda i, f: (0, 0)),
                  pl.BlockSpec((d, tf), lambda i, f: (0, f)),
                  pl.BlockSpec((tf, d), lambda i, f: (f, 0))],
        out_specs=pl.BlockSpec((tm, d), lambda i, f: (i, 0)),
        scratch_shapes=[pltpu.VMEM((tm, d), BF16), pltpu.VMEM((tm, d), F32)],
        compiler_params=_params("parallel", "arbitrary"),
        name="mlp_block",
    )(x, gain.reshape(1, d), w1, w2)


def _mem_kv_body(m_ref, g_ref, w_ref, kg_ref, k_ref, v_ref):
    h = _rms(m_ref[0], g_ref[0]).astype(BF16)
    kv = jnp.dot(h, w_ref[0], preferred_element_type=F32)
    for hd in range(X_HEADS):
        sl = slice(hd * X_DH, (hd + 1) * X_DH)
        k_ref[0, 0, :, sl] = _rms(kv[:, sl], kg_ref[0]).astype(BF16)
    v_ref[0, 0] = kv[:, D_MODEL:].astype(BF16)


def mem_kv(mem, norm_mem, w_kv, k_norm):
    bm, nm, d = mem.shape
    out = jax.ShapeDtypeStruct((DEPTH, bm, nm, d), BF16)
    return pl.pallas_call(
        _mem_kv_body,
        out_shape=(out, out),
        grid=(DEPTH, bm),
        in_specs=[pl.BlockSpec((1, nm, d), lambda l, b: (b, 0, 0)),
                  pl.BlockSpec((1, 1, d), lambda l, b: (l, 0, 0)),
                  pl.BlockSpec((1, d, 2 * d), lambda l, b: (l, 0, 0)),
                  pl.BlockSpec((1, 1, X_DH), lambda l, b: (l, 0, 0))],
        out_specs=(pl.BlockSpec((1, 1, nm, d), lambda l, b: (l, b, 0, 0)),
                   pl.BlockSpec((1, 1, nm, d), lambda l, b: (l, b, 0, 0))),
        compiler_params=_params("arbitrary", "arbitrary"),
        name="mem_kv",
    )(mem, norm_mem.reshape(DEPTH, 1, d), w_kv, k_norm.reshape(DEPTH, 1, X_DH))


def _cross_body(x_ref, g_ref, wq_ref, qg_ref, k_ref, v_ref, wo_ref, o_ref):
    x = x_ref[...]
    h = _rms(x, g_ref[...]).astype(BF16)
    q = jnp.dot(h, wq_ref[...], preferred_element_type=F32)
    heads = []
    for hd in range(X_HEADS):
        sl = slice(hd * X_DH, (hd + 1) * X_DH)
        qn = _rms(q[:, sl], qg_ref[...]).astype(BF16)
        s = lax.dot_general(qn, k_ref[0, :, sl], NT_DIMS, preferred_element_type=F32)
        p = jnp.exp(s - jnp.max(s, axis=-1, keepdims=True))
        l = jnp.sum(p, axis=-1, keepdims=True)
        oh = jnp.dot(p.astype(BF16), v_ref[0, :, sl], preferred_element_type=F32) / l
        heads.append(oh.astype(BF16))
    o = jnp.concatenate(heads, axis=-1)
    o_ref[...] = x + jnp.dot(o, wo_ref[...], preferred_element_type=F32)


def cross_block(x, seq_len, gain, w_q, q_gain, kn, v, w_o, tm=ROW_TILE):
    t, d = x.shape
    nm = kn.shape[1]
    per_seq = seq_len // tm
    return pl.pallas_call(
        _cross_body,
        out_shape=jax.ShapeDtypeStruct((t, d), F32),
        grid=(t // tm,),
        in_specs=[pl.BlockSpec((tm, d), lambda i: (i, 0)),
                  pl.BlockSpec((1, d), lambda i: (0, 0)),
                  pl.BlockSpec((d, d), lambda i: (0, 0)),
                  pl.BlockSpec((1, X_DH), lambda i: (0, 0)),
                  pl.BlockSpec((1, nm, d), lambda i: (i // per_seq, 0, 0)),
                  pl.BlockSpec((1, nm, d), lambda i: (i // per_seq, 0, 0)),
                  pl.BlockSpec((d, d), lambda i: (0, 0))],
        out_specs=pl.BlockSpec((tm, d), lambda i: (i, 0)),
        compiler_params=_params("parallel"),
        name="cross_block",
    )(x, gain.reshape(1, d), w_q, (q_gain * X_DH ** -0.5).reshape(1, X_DH), kn, v, w_o)


LANES = 128


def _toeplitz_body(k_ref, o_ref):
    lc, gs = S5_CHUNK, S5_GROUP
    kern = k_ref[0]
    n = kern.shape[1]
    per_tile = LANES // gs
    for r in range(per_tile):
        shifted = pltpu.roll(kern, n - gs * r, 1) if r else kern
        for a in range(lc // per_tile):
            s = lc - 1 - (per_tile * a + r)
            o_ref[0, s * gs:(s + 1) * gs, :] = shifted[:, LANES * a:LANES * a + lc * gs].astype(o_ref.dtype)


def toeplitz_expand(kern_flat):
    g, gs, n = kern_flat.shape
    width = S5_CHUNK * S5_GROUP
    return pl.pallas_call(
        _toeplitz_body,
        out_shape=jax.ShapeDtypeStruct((g, width, width), BF16),
        grid=(g,),
        in_specs=[pl.BlockSpec((1, gs, n), lambda i: (i, 0, 0))],
        out_specs=pl.BlockSpec((1, width, width), lambda i: (i, 0, 0)),
        compiler_params=_params("parallel"),
        name="toeplitz_expand",
    )(kern_flat)


def s5_operators(lam_re, lam_im, log_step, b_re, b_im, c_re, c_im, d):
    lc = S5_CHUNK
    lam = lax.complex(lam_re.astype(F32), lam_im.astype(F32))
    step = jnp.exp(log_step.astype(F32))[..., None]
    lam_bar = jnp.exp(lam * step)
    b_bar = ((lam_bar - 1.0) / lam)[..., None] * lax.complex(b_re.astype(F32), b_im.astype(F32))
    c = lax.complex(c_re.astype(F32), c_im.astype(F32))
    pw = jnp.cumprod(jnp.broadcast_to(lam_bar[..., None], lam_bar.shape + (lc,)), axis=-1)
    pw = jnp.concatenate([jnp.ones_like(pw[..., :1]), pw], axis=-1)
    kern = jnp.einsum('zgcp,zgpt,zgpd->zgtcd', c, pw[..., :lc], b_bar, precision=HIGHEST).real
    kf, kb = kern[0], kern[1]
    k0 = kf[:, :1] + kb[:, :1] + (d.astype(F32)[:, :, None] * jnp.eye(S5_GROUP, dtype=F32))[:, None]
    kern_full = jnp.concatenate([kb[:, :0:-1], k0, kf[:, 1:]], axis=1)
    kern_flat = kern_full.transpose(0, 3, 1, 2).reshape(S5_GROUPS, S5_GROUP, (2 * lc - 1) * S5_GROUP)
    toep = toeplitz_expand(jnp.pad(kern_flat, ((0, 0), (0, 0), (0, S5_GROUP))))

    pf = jnp.einsum('gps,gpd->gsdp', pw[0][..., lc - 1::-1], b_bar[0])
    pb = jnp.einsum('gps,gpd->gsdp', pw[1][..., :lc], b_bar[1])
    p_op = jnp.concatenate([pf.real, pb.real, pf.imag, pb.imag], axis=-1)
    p_op = p_op.reshape(S5_GROUPS, lc * S5_GROUP, 4 * S5_STATE)

    qf = jnp.einsum('gcp,gpt->gptc', c[0], pw[0][..., 1:])
    qb = jnp.einsum('gcp,gpt->gptc', c[1], pw[1][..., :0:-1])
    q_op = jnp.concatenate([qf.real, qb.real, -qf.imag, -qb.imag], axis=1)
    q_op = q_op.reshape(S5_GROUPS, 4 * S5_STATE, lc * S5_GROUP)

    a = pw[..., lc]
    coef = jnp.stack([jnp.concatenate([a[0].real, a[1].real], -1),
                      jnp.concatenate([a[0].imag, a[1].imag], -1)], axis=1)
    return toep, _split_bf16(p_op), _split_bf16(q_op), coef


def _split_bf16(x):
    hi = x.astype(BF16)
    lo = (x - hi.astype(F32)).astype(BF16)
    return jnp.stack([hi, lo], axis=1)


SUBLANES = 8


def _s5_group_body(u_ref, t_ref, p_ref, q_ref, c_ref, y_ref,
                   v_re_ref, v_im_ref, xf_re_ref, xf_im_ref, xb_re_ref, xb_im_ref, *, n_chunks, bsz):
    u = u_ref[0]
    half = 2 * S5_STATE
    v = (jnp.dot(u, p_ref[0, 0], preferred_element_type=F32)
         + jnp.dot(u, p_ref[0, 1], preferred_element_type=F32))
    v_re_ref[...] = v[:, :half]
    v_im_ref[...] = v[:, half:]
    coef = c_ref[0]
    a_re, a_im = coef[0:1], coef[1:2]
    fwd = lax.broadcasted_iota(jnp.int32, (bsz, half), 1) < S5_STATE

    def step(i, carry):
        re, im = carry
        rows_f = pl.ds(i, bsz, stride=n_chunks)
        rows_b = pl.ds(n_chunks - 1 - i, bsz, stride=n_chunks)
        xf_re_ref[rows_f, :] = re
        xf_im_ref[rows_f, :] = im
        xb_re_ref[rows_b, :] = re
        xb_im_ref[rows_b, :] = im
        re_next = a_re * re - a_im * im + jnp.where(fwd, v_re_ref[rows_f, :], v_re_ref[rows_b, :])
        im_next = a_re * im + a_im * re + jnp.where(fwd, v_im_ref[rows_f, :], v_im_ref[rows_b, :])
        return re_next, im_next

    zero = jnp.zeros((bsz, half), F32)
    lax.fori_loop(0, n_chunks, step, (zero, zero))
    is_fwd = lax.broadcasted_iota(jnp.int32, xf_re_ref.shape, 1) < S5_STATE
    x = jnp.concatenate([jnp.where(is_fwd, xf_re_ref[...], xb_re_ref[...]),
                         jnp.where(is_fwd, xf_im_ref[...], xb_im_ref[...])], axis=1)
    x_hi = x.astype(BF16)
    x_lo = (x - x_hi.astype(F32)).astype(BF16)
    y = (jnp.dot(u, t_ref[0], preferred_element_type=F32)
         + jnp.dot(x_hi, q_ref[0, 0], preferred_element_type=F32)
         + jnp.dot(x_lo, q_ref[0, 0], preferred_element_type=F32)
         + jnp.dot(x_hi, q_ref[0, 1], preferred_element_type=F32))
    y_ref[0] = y.astype(y_ref.dtype)


def s5_scan(u, bsz, seq_len, ops):
    toep, p_op, q_op, coef = ops
    lc, g, w = S5_CHUNK, S5_GROUPS, S5_CHUNK * S5_GROUP
    n = seq_len // lc
    c = bsz * n
    ns = 4 * S5_STATE
    ug = u.reshape(bsz, n, lc, g, S5_GROUP).transpose(3, 0, 1, 2, 4).reshape(g, c, w)
    group = lambda *shape: pl.BlockSpec((1,) + shape, lambda i: (i,) + (0,) * len(shape))
    y = pl.pallas_call(
        functools.partial(_s5_group_body, n_chunks=n, bsz=bsz),
        out_shape=jax.ShapeDtypeStruct((g, c, w), BF16),
        grid=(g,),
        in_specs=[group(c, w), group(w, w), group(2, w, ns), group(2, ns, w), group(2, ns // 2)],
        out_specs=group(c, w),
        scratch_shapes=[pltpu.VMEM((c, ns // 2), F32)] * 6,
        compiler_params=_params("parallel"),
        name="s5_group",
    )(ug, toep, p_op, q_op, coef)
    return y.reshape(g, bsz, n, lc, S5_GROUP).transpose(1, 2, 3, 0, 4).reshape(bsz * seq_len, S5_WIDTH)


def _gla_direction(q_ref, k_ref, v_ref, g_ref, wg, bg, s_ref, o_ref, forward, chunks):
    cs = GLA_CHUNK
    tb = chunks * cs
    glr = g_ref[...]
    logit = (jnp.dot(glr, wg[0], preferred_element_type=F32)
             + jnp.dot(glr, wg[1], preferred_element_type=F32) + bg)
    g = (jnp.minimum(logit, 0.0) - jnp.log1p(jnp.exp(-jnp.abs(logit)))) / GLA_TAU
    row = lax.broadcasted_iota(jnp.int32, (tb, tb), 0)
    col = lax.broadcasted_iota(jnp.int32, (tb, tb), 1)
    same_chunk = (row // cs) == (col // cs)
    within = same_chunk & ((col <= row) if forward else (col >= row))
    ones = jnp.where(within, 1.0, 0.0).astype(BF16)
    bcum = jnp.zeros_like(g)
    rest = g
    for _ in range(3):
        term = rest.astype(BF16)
        bcum = bcum + jnp.dot(ones, term, preferred_element_type=F32)
        rest = rest - term.astype(F32)

    srow = lax.broadcasted_iota(jnp.int32, (GLA_HEADS * cs, cs), 0) & (cs - 1)
    scol = lax.broadcasted_iota(jnp.int32, (GLA_HEADS * cs, cs), 1)
    keep = (scol <= srow) if forward else (scol >= srow)
    lane = lax.broadcasted_iota(jnp.int32, (1, GLA_QK), 1)
    head_lanes = [((lane >= h * GLA_DK) & (lane < (h + 1) * GLA_DK)).astype(F32) for h in range(GLA_HEADS)]
    state_mask = (lax.broadcasted_iota(jnp.int32, (GLA_V, GLA_QK), 0) // GLA_DV
                  == lax.broadcasted_iota(jnp.int32, (GLA_V, GLA_QK), 1) // GLA_DK)

    i_ref = cs // 2 - 1 if forward else cs // 2
    i_last = cs - 1 if forward else 0
    scale = GLA_DK ** -0.5
    for ci in (range(chunks) if forward else reversed(range(chunks))):
        rows = slice(ci * cs, (ci + 1) * cs)
        b = bcum[rows]
        bref = b[i_ref:i_ref + 1]
        blast = b[i_last:i_last + 1]
        q = q_ref[rows, :].astype(F32) * scale
        k = k_ref[rows, :].astype(F32)
        v = v_ref[rows, :]
        q_rel = q * jnp.exp(b - bref)
        k_rel = (k * jnp.exp(bref - b)).astype(BF16)
        k_out = (k * jnp.exp(blast - b)).astype(BF16)
        q_dec = (q * jnp.exp(b)).astype(BF16)
        decay = jnp.exp(blast)
        q_heads = jnp.concatenate([q_rel * hm for hm in head_lanes], axis=0).astype(BF16)
        s = lax.dot_general(q_heads, k_rel, NT_DIMS, preferred_element_type=F32)
        s = jnp.where(keep, s, 0.0).astype(BF16)
        st = s_ref[...]
        o_inter = lax.dot_general(q_dec, st.astype(BF16), NT_DIMS, preferred_element_type=F32)
        o_intra = [jnp.dot(s[h * cs:(h + 1) * cs], v[:, h * GLA_DV:(h + 1) * GLA_DV],
                           preferred_element_type=F32) for h in range(GLA_HEADS)]
        o_ref[rows, :] = o_inter + jnp.concatenate(o_intra, axis=-1)
        kv = lax.dot_general(v, k_out, TN_DIMS, preferred_element_type=F32)
        s_ref[...] = decay * st + jnp.where(state_mask, kv, 0.0)


def _gla_body(qf_ref, kf_ref, vf_ref, gf_ref, qb_ref, kb_ref, vb_ref, gb_ref, wg_ref, bg_ref,
              of_ref, ob_ref, sf_ref, sb_ref, *, chunks):
    @pl.when(pl.program_id(1) == 0)
    def _():
        sf_ref[...] = jnp.zeros_like(sf_ref)
        sb_ref[...] = jnp.zeros_like(sb_ref)

    _gla_direction(qf_ref, kf_ref, vf_ref, gf_ref, wg_ref[0], bg_ref[0], sf_ref, of_ref, True, chunks)
    _gla_direction(qb_ref, kb_ref, vb_ref, gb_ref, wg_ref[1], bg_ref[1], sb_ref, ob_ref, False, chunks)


def gla_scan(proj, bsz, seq_len, w_gate, b_gate, chunks=4):
    t = proj.shape[0]
    tb = chunks * GLA_CHUNK
    nb = seq_len // tb
    fwd = lambda cb: (lambda b, i: (b * nb + i, cb))
    bwd = lambda cb: (lambda b, i: (b * nb + nb - 1 - i, cb))
    qc, kc, vc, gc = 512 // GLA_QK, 768 // GLA_QK, 1024 // GLA_V, 2048 // 128
    wg = jnp.zeros((2, 128, GLA_QK), F32)
    wg = wg.at[0, :GLA_RANK].set(w_gate[0].astype(F32)).at[1, GLA_RANK:2 * GLA_RANK].set(w_gate[1].astype(F32))
    wg = _split_bf16(wg)
    out = jax.ShapeDtypeStruct((t, GLA_V), F32)
    state = pltpu.VMEM((GLA_V, GLA_QK), F32)
    return pl.pallas_call(
        functools.partial(_gla_body, chunks=chunks),
        out_shape=(out, out),
        grid=(bsz, nb),
        in_specs=[pl.BlockSpec((tb, GLA_QK), fwd(qc)), pl.BlockSpec((tb, GLA_QK), fwd(kc)),
                  pl.BlockSpec((tb, GLA_V), fwd(vc)), pl.BlockSpec((tb, 128), fwd(gc)),
                  pl.BlockSpec((tb, GLA_QK), bwd(qc)), pl.BlockSpec((tb, GLA_QK), bwd(kc)),
                  pl.BlockSpec((tb, GLA_V), bwd(vc)), pl.BlockSpec((tb, 128), bwd(gc)),
                  pl.BlockSpec((2, 2, 128, GLA_QK), lambda b, i: (0, 0, 0, 0)),
                  pl.BlockSpec((2, 1, GLA_QK), lambda b, i: (0, 0, 0))],
        out_specs=(pl.BlockSpec((tb, GLA_V), fwd(0)), pl.BlockSpec((tb, GLA_V), bwd(0))),
        scratch_shapes=[state, state],
        compiler_params=_params("parallel", "arbitrary"),
        name="gla_scan",
    )(proj, proj, proj, proj, proj, proj, proj, proj, wg, b_gate.astype(F32).reshape(2, 1, GLA_QK))


def _even_out_body(x_ref, ys_ref, of_ref, ob_ref, og_ref, wglu_ref, gn_ref, wtop_ref, wbot_ref, o_ref):
    y = ys_ref[...].astype(F32)
    y = 0.5 * y * (1.0 + jnp.tanh(math.sqrt(2.0 / math.pi) * (y + 0.044715 * (y * y * y))))
    gate = jnp.dot(y.astype(BF16), wglu_ref[...], preferred_element_type=F32)
    y = y * _sigmoid(gate)
    o = of_ref[...] + ob_ref[...]
    og = og_ref[...].astype(F32)
    heads = []
    for h in range(GLA_HEADS):
        sl = slice(h * GLA_DV, (h + 1) * GLA_DV)
        heads.append(_rms(o[:, sl], gn_ref[...]))
    o = jnp.concatenate(heads, axis=-1) * (og * _sigmoid(og))
    o_ref[...] = (x_ref[...]
                  + jnp.dot(y.astype(BF16), wtop_ref[...], preferred_element_type=F32)
                  + jnp.dot(o.astype(BF16), wbot_ref[...], preferred_element_type=F32))


def even_out(x, ys, o_f, o_b, proj, w_glu, gla_norm, w_out, tm=ROW_TILE):
    t, d = x.shape
    row = lambda i: (i, 0)
    const = lambda i: (0, 0)
    return pl.pallas_call(
        _even_out_body,
        out_shape=jax.ShapeDtypeStruct((t, d), F32),
        grid=(t // tm,),
        in_specs=[pl.BlockSpec((tm, d), row), pl.BlockSpec((tm, S5_WIDTH), row),
                  pl.BlockSpec((tm, GLA_V), row), pl.BlockSpec((tm, GLA_V), row),
                  pl.BlockSpec((tm, GLA_V), lambda i: (i, 1536 // GLA_V)),
                  pl.BlockSpec((S5_WIDTH, S5_WIDTH), const), pl.BlockSpec((1, GLA_DV), const),
                  pl.BlockSpec((S5_WIDTH, d), const), pl.BlockSpec((GLA_V, d), const)],
        out_specs=pl.BlockSpec((tm, d), row),
        compiler_params=_params("parallel"),
        name="even_out",
    )(x, ys, o_f, o_b, proj, w_glu, gla_norm.astype(F32).reshape(1, GLA_DV),
      w_out[:S5_WIDTH], w_out[S5_WIDTH:])


QK_NORM_WIDTH = 256


def _seg_rms(a, gain, same_seg):
    ssq = jnp.dot((a * a).astype(BF16), same_seg, preferred_element_type=F32)
    return a * lax.rsqrt(ssq * (1.0 / DIFF_DK) + EPS) * gain


def _qkv_body(x_ref, g_ref, w_ref, qk_gain_ref, o_ref):
    d = x_ref.shape[1]
    h = _rms(x_ref[...], g_ref[...]).astype(BF16)
    w = QK_NORM_WIDTH
    same_seg = jnp.where(lax.broadcasted_iota(jnp.int32, (w, w), 0) // DIFF_DK
                         == lax.broadcasted_iota(jnp.int32, (w, w), 1) // DIFF_DK, 1.0, 0.0).astype(BF16)
    for part in range(2):
        acc = jnp.dot(h, w_ref[:, part * d:(part + 1) * d], preferred_element_type=F32)
        gain = qk_gain_ref[part]
        for c in range(d // w):
            sl = slice(c * w, (c + 1) * w)
            o_ref[:, part * d + c * w:part * d + (c + 1) * w] = (
                _seg_rms(acc[:, sl], gain, same_seg).astype(o_ref.dtype))
    o_ref[:, 2 * d:] = jnp.dot(h, w_ref[:, 2 * d:], preferred_element_type=F32).astype(o_ref.dtype)


def qkv_project(x, gain, w, q_norm, k_norm, tm=ROW_TILE):
    t, d = x.shape
    reps = QK_NORM_WIDTH // DIFF_DK
    qg = jnp.tile(q_norm.astype(F32) * DIFF_DK ** -0.5, reps)
    kg = jnp.tile(k_norm.astype(F32), reps)
    qk_gain = jnp.stack([qg, kg]).reshape(2, 1, QK_NORM_WIDTH)
    return pl.pallas_call(
        _qkv_body,
        out_shape=jax.ShapeDtypeStruct((t, 3 * d), BF16),
        grid=(t // tm,),
        in_specs=[pl.BlockSpec((tm, d), lambda i: (i, 0)),
                  pl.BlockSpec((1, d), lambda i: (0, 0)),
                  pl.BlockSpec((d, 3 * d), lambda i: (0, 0)),
                  pl.BlockSpec((2, 1, QK_NORM_WIDTH), lambda i: (0, 0, 0))],
        out_specs=pl.BlockSpec((tm, 3 * d), lambda i: (i, 0)),
        compiler_params=_params("parallel"),
        name="qkv_project",
    )(x, gain.reshape(1, d), w, qk_gain)


POS_SPLIT = 16
POS_SHIFT = 4
AUG_LANE = DIFF_DK
SOFTMAX_GROUP_ELEMS = 32 * 1024
ROW_PARTS = 2


def _pos_terms(shape):
    lane = lax.broadcasted_iota(jnp.int32, shape, 1)
    pos = lax.broadcasted_iota(jnp.int32, shape, 0)
    hi = lax.shift_right_logical(pos, POS_SHIFT).astype(F32)
    lo = (pos & (POS_SPLIT - 1)).astype(F32)
    return lane, hi, lo


def _lane_select(lane, first, values):
    out = 0.0
    for n, val in reversed(list(enumerate(values))):
        out = jnp.where(lane == first + n, val, out)
    return out


def _diff_attn_body(slope_ref, q_ref, k_ref, v_ref, lq1_ref, lk1_ref, lq2_ref, lk2_ref, sub_ref,
                    o_ref, ka_ref, va_ref, qs_ref, s0_ref, s1_ref, p0_ref, p1_ref, vs_ref,
                    *, blk, seq_len, lambda_init):
    h = pl.program_id(1)
    qi = pl.program_id(2)
    slope = slope_ref[h]
    nk = seq_len // blk
    s_refs, p_refs = (s0_ref, s1_ref), (p0_ref, p1_ref)

    @pl.when(qi == 0)
    def _():
        va_ref[:, :DIFF_DV] = v_ref[...]
        va_ref[:, DIFF_DV:] = jnp.ones((seq_len, DIFF_DV), BF16)

        def build(t, carry):
            rows = pl.ds(pl.multiple_of(t * blk, blk), blk)
            kf = k_ref[rows, :].astype(F32)
            lane, hi, lo = _pos_terms(kf.shape)
            aug = _lane_select(lane, AUG_LANE,
                               [-blk * slope, -POS_SPLIT * slope, -slope,
                                (blk * slope) * jnp.asarray(t, F32), (POS_SPLIT * slope) * hi, slope * lo])
            for z, kz in enumerate((kf, pltpu.roll(kf, DIFF_DK, 1))):
                ka_ref[z, rows, :] = jnp.where(lane < DIFF_DK, kz, aug).astype(BF16)
            return carry

        lax.fori_loop(0, nk, build, 0)

    qf = q_ref[...].astype(F32)
    lane, hi, lo = _pos_terms(qf.shape)
    qa = _lane_select(lane, AUG_LANE, [jnp.asarray(qi, F32), hi, lo, 1.0, 1.0, 1.0])
    for z, qz in enumerate((qf, pltpu.roll(qf, DIFF_DK, 1))):
        qs_ref[z] = jnp.where(lane < DIFF_DK, qz, 0.0).astype(BF16)
        qs_ref[2 + z] = jnp.where(lane < DIFF_DK, qz, qa).astype(BF16)
        qs_ref[4 + z] = jnp.where(lane < DIFF_DK, qz, -qa).astype(BF16)

    def key_block(t):
        if t == 0:
            return qi, 0
        j = (t - 1) + jnp.asarray(qi <= t - 1, jnp.int32)
        return j, jnp.where(j < qi, 2, 4)

    def key_rows(t):
        j, _ = key_block(t)
        return pl.ds(pl.multiple_of(j * blk, blk), blk)

    part = blk // ROW_PARTS
    parts = [slice(r * part, (r + 1) * part) for r in range(ROW_PARTS)]
    for z, s_ref in enumerate(s_refs):
        for rows in parts:
            for t in range(nk):
                _, variant = key_block(t)
                s_ref[rows, t * blk:(t + 1) * blk] = lax.dot_general(
                    qs_ref[variant + z, rows, :], ka_ref[z, key_rows(t), :], NT_DIMS,
                    preferred_element_type=F32)

    rg = SOFTMAX_GROUP_ELEMS // seq_len
    rel = (lax.broadcasted_iota(jnp.int32, (rg, blk), 0) - lax.broadcasted_iota(jnp.int32, (rg, blk), 1))
    for t in range(nk):
        vs_ref[t * blk:(t + 1) * blk, :] = va_ref[key_rows(t), :]

    acc = []
    for s_ref, p_ref in zip(s_refs, p_refs):
        for g in range(blk // rg):
            rows = slice(g * rg, (g + 1) * rg)
            s_diag = s_ref[rows, :blk] - slope * jnp.abs(rel + g * rg).astype(F32)
            s_rest = s_ref[rows, blk:]
            m = jnp.maximum(jnp.max(s_diag, axis=-1, keepdims=True), jnp.max(s_rest, axis=-1, keepdims=True))
            p_ref[rows, :blk] = jnp.exp(s_diag - m).astype(BF16)
            p_ref[rows, blk:] = jnp.exp(s_rest - m).astype(BF16)
        acc.append([jnp.dot(p_ref[rows, :], vs_ref[...], preferred_element_type=F32) for rows in parts])

    lam = (jnp.exp(jnp.sum(lq1_ref[...] * lk1_ref[...], axis=-1, keepdims=True))
           - jnp.exp(jnp.sum(lq2_ref[...] * lk2_ref[...], axis=-1, keepdims=True)) + lambda_init)
    for rows, a0, a1 in zip(parts, *acc):
        o = (a0[:, :DIFF_DV] / a0[:, DIFF_DV:DIFF_DV + 1]
             - lam * (a1[:, :DIFF_DV] / a1[:, DIFF_DV:DIFF_DV + 1]))
        o_ref[rows, :] = (_rms(o, sub_ref[...]) * (1.0 - lambda_init)).astype(o_ref.dtype)


def diff_attention(qkv, bsz, seq_len, lq1, lk1, lq2, lk2, sub_norm, lambda_init, blk=512):
    t = qkv.shape[0]
    nq = seq_len // blk
    assert blk // POS_SPLIT <= 256, "hi part of a block position must stay exact in bf16"
    slopes = jnp.asarray(2.0 ** (-8.0 * np.arange(1, DIFF_HEADS + 1, dtype=np.float32) / DIFF_HEADS), F32)
    vec = lambda a: a.astype(F32).reshape(1, DIFF_DK)
    const = lambda b, h, i: (0, 0)
    return pl.pallas_call(
        functools.partial(_diff_attn_body, blk=blk, seq_len=seq_len, lambda_init=lambda_init),
        out_shape=jax.ShapeDtypeStruct((t, DIFF_HEADS * DIFF_DV), BF16),
        grid=(bsz, DIFF_HEADS, nq),
        in_specs=[pl.BlockSpec(memory_space=pltpu.SMEM),
                  pl.BlockSpec((blk, 2 * DIFF_DK), lambda b, h, i: (b * nq + i, h)),
                  pl.BlockSpec((seq_len, 2 * DIFF_DK), lambda b, h, i: (b, DIFF_HEADS + h)),
                  pl.BlockSpec((seq_len, DIFF_DV), lambda b, h, i: (b, 2 * DIFF_HEADS + h)),
                  pl.BlockSpec((1, DIFF_DK), const), pl.BlockSpec((1, DIFF_DK), const),
                  pl.BlockSpec((1, DIFF_DK), const), pl.BlockSpec((1, DIFF_DK), const),
                  pl.BlockSpec((1, DIFF_DV), const)],
        out_specs=pl.BlockSpec((blk, DIFF_DV), lambda b, h, i: (b * nq + i, h)),
        scratch_shapes=[pltpu.VMEM((2, seq_len, 2 * DIFF_DK), BF16),
                        pltpu.VMEM((seq_len, 2 * DIFF_DV), BF16),
                        pltpu.VMEM((6, blk, 2 * DIFF_DK), BF16),
                        pltpu.VMEM((blk, seq_len), F32),
                        pltpu.VMEM((blk, seq_len), F32),
                        pltpu.VMEM((blk, seq_len), BF16),
                        pltpu.VMEM((blk, seq_len), BF16),
                        pltpu.VMEM((seq_len, 2 * DIFF_DV), BF16)],
        compiler_params=_params("parallel", "parallel", "arbitrary"),
        name="diff_attention",
    )(slopes, qkv, qkv, qkv, vec(lq1), vec(lk1), vec(lq2), vec(lk2),
      sub_norm.astype(F32).reshape(1, DIFF_DV))


def _pad_ev_w_in(w):
    return jnp.pad(w, ((0, 0), (0, EV_PAD_COLS - w.shape[1]))).astype(BF16)


def _trunk(x3, mem, p):
    bsz, seq_len, d = x3.shape
    x = x3.reshape(bsz * seq_len, d)
    kn_all, v_all = mem_kv(mem, p['norm_mem'], p['x_w_kv'], p['x_k_norm'])
    for layer in range(DEPTH):
        if layer % 2 == 0:
            e = layer // 2
            proj = norm_matmul(x, p['norm_mix'][layer], p['ev_w_in'][e], BF16, EV_TN)
            ys = s5_scan(proj[:, :S5_WIDTH], bsz, seq_len, p['s5_ops'][e])
            o_f, o_b = gla_scan(proj, bsz, seq_len, p['gla_w_gate'][e], p['gla_b_gate'][e])
            x = even_out(x, ys, o_f, o_b, proj, p['s5_w_glu'][e], p['gla_norm'][e], p['ev_w_out'][e])
        else:
            o = layer // 2
            lambda_init = 0.8 - 0.6 * math.exp(-0.3 * layer)
            qkv = qkv_project(x, p['norm_mix'][layer], p['od_w_in'][o],
                              p['diff_q_norm'][o], p['diff_k_norm'][o])
            att = diff_attention(qkv, bsz, seq_len, p['diff_lambda_q1'][o], p['diff_lambda_k1'][o],
                                 p['diff_lambda_q2'][o], p['diff_lambda_k2'][o], p['diff_norm'][o],
                                 lambda_init)
            x = matmul_residual(att, p['od_w_out'][o], x, tn=d)
        x = cross_block(x, seq_len, p['norm_cross'][layer], p['x_w_q'][layer], p['x_q_norm'][layer],
                        kn_all[layer], v_all[layer], p['x_w_o'][layer])
        x = mlp_block(x, p['norm_mlp'][layer], p['mlp_w1'][layer], p['mlp_w2'][layer])
    return x.reshape(bsz, seq_len, d)


def kernel(x_prompt, x_sample, mem_prompt, mem_sample, norm_mix, norm_cross, norm_mem, norm_mlp,
           ev_w_in, ev_w_out, s5_lambda_re, s5_lambda_im, s5_log_step, s5_b_re, s5_b_im,
           s5_c_re, s5_c_im, s5_d, s5_w_glu, gla_w_gate, gla_b_gate, gla_norm,
           od_w_in, od_w_out, diff_q_norm, diff_k_norm, diff_lambda_q1, diff_lambda_k1,
           diff_lambda_q2, diff_lambda_k2, diff_norm, x_w_q, x_w_kv, x_w_o, x_q_norm, x_k_norm,
           mlp_w1, mlp_w2):
    n_even = ev_w_in.shape[0]
    bf = lambda w: w.astype(BF16)
    p = dict(
        norm_mix=norm_mix, norm_cross=norm_cross, norm_mem=norm_mem, norm_mlp=norm_mlp,
        ev_w_in=[_pad_ev_w_in(ev_w_in[e]) for e in range(n_even)], ev_w_out=bf(ev_w_out),
        s5_ops=[s5_operators(s5_lambda_re[e], s5_lambda_im[e], s5_log_step[e], s5_b_re[e], s5_b_im[e],
                             s5_c_re[e], s5_c_im[e], s5_d[e]) for e in range(n_even)],
        s5_w_glu=bf(s5_w_glu), gla_w_gate=gla_w_gate, gla_b_gate=gla_b_gate, gla_norm=gla_norm,
        od_w_in=bf(od_w_in), od_w_out=bf(od_w_out), diff_q_norm=diff_q_norm, diff_k_norm=diff_k_norm,
        diff_lambda_q1=diff_lambda_q1, diff_lambda_k1=diff_lambda_k1,
        diff_lambda_q2=diff_lambda_q2, diff_lambda_k2=diff_lambda_k2, diff_norm=diff_norm,
        x_w_q=bf(x_w_q), x_w_kv=bf(x_w_kv), x_w_o=bf(x_w_o), x_q_norm=x_q_norm, x_k_norm=x_k_norm,
        mlp_w1=bf(mlp_w1), mlp_w2=bf(mlp_w2))
    return (_trunk(x_prompt, mem_prompt, p), _trunk(x_sample, mem_sample, p))
```

```python
import functools
import math

import numpy as np
import jax
import jax.numpy as jnp
from jax import lax
from jax.experimental import pallas as pl
from jax.experimental.pallas import tpu as pltpu

F32 = jnp.float32
BF16 = jnp.bfloat16
HIGHEST = lax.Precision.HIGHEST

D_MODEL = 1024
DEPTH = 4
EPS = 1e-6
S5_WIDTH = 512
S5_GROUP = 16
S5_GROUPS = 32
S5_STATE = 64
S5_CHUNK = 64
GLA_HEADS = 4
GLA_DV = 128
GLA_DK = 64
GLA_RANK = 16
GLA_TAU = 16.0
GLA_CHUNK = 64
GLA_QK = GLA_HEADS * GLA_DK
GLA_V = GLA_HEADS * GLA_DV
EV_PAD_COLS = 2304
EV_TN = 768
DIFF_HEADS = 8
DIFF_DK = 64
DIFF_DV = 128
X_HEADS = 4
X_DH = 256
D_FF = 4096

ROW_TILE = 512
VMEM_LIMIT = 48 * 1024 * 1024

NT_DIMS = (((1,), (1,)), ((), ()))
TN_DIMS = (((0,), (0,)), ((), ()))


def _params(*sem, flags=None):
    return pltpu.CompilerParams(dimension_semantics=sem, vmem_limit_bytes=VMEM_LIMIT, flags=flags)


def _rms(x, gain):
    ms = jnp.mean(x * x, axis=-1, keepdims=True)
    return x * lax.rsqrt(ms + EPS) * gain


def _sigmoid(x):
    return 1.0 / (1.0 + jnp.exp(-x))


def _norm_matmul_body(x_ref, g_ref, w_ref, o_ref, *, tn):
    h = _rms(x_ref[...], g_ref[...]).astype(BF16)
    for j in range(w_ref.shape[1] // tn):
        cols = slice(j * tn, (j + 1) * tn)
        o_ref[:, cols] = jnp.dot(h, w_ref[:, cols], preferred_element_type=F32).astype(o_ref.dtype)


def norm_matmul(x, gain, w, out_dtype, tn, tm=ROW_TILE):
    t, d = x.shape
    n = w.shape[1]
    return pl.pallas_call(
        functools.partial(_norm_matmul_body, tn=tn),
        out_shape=jax.ShapeDtypeStruct((t, n), out_dtype),
        grid=(t // tm,),
        in_specs=[pl.BlockSpec((tm, d), lambda i: (i, 0)),
                  pl.BlockSpec((1, d), lambda i: (0, 0)),
                  pl.BlockSpec((d, n), lambda i: (0, 0))],
        out_specs=pl.BlockSpec((tm, n), lambda i: (i, 0)),
        compiler_params=_params("parallel"),
        name="norm_matmul",
    )(x, gain.reshape(1, d), w)


def _matmul_res_body(a_ref, w_ref, r_ref, o_ref):
    o_ref[...] = r_ref[...] + jnp.dot(a_ref[...].astype(BF16), w_ref[...], preferred_element_type=F32)


def matmul_residual(a, w, res, tn=512, tm=ROW_TILE):
    t, k = a.shape
    n = w.shape[1]
    return pl.pallas_call(
        _matmul_res_body,
        out_shape=jax.ShapeDtypeStruct((t, n), F32),
        grid=(t // tm, n // tn),
        in_specs=[pl.BlockSpec((tm, k), lambda i, j: (i, 0)),
                  pl.BlockSpec((k, tn), lambda i, j: (0, j)),
                  pl.BlockSpec((tm, tn), lambda i, j: (i, j))],
        out_specs=pl.BlockSpec((tm, tn), lambda i, j: (i, j)),
        compiler_params=_params("parallel", "arbitrary"),
        name="matmul_residual",
    )(a, w, res)


def _mlp_body(x_ref, g_ref, w1_ref, w2_ref, o_ref, hid_ref, *, tf):
    x = x_ref[...]
    h = _rms(x, g_ref[...]).astype(BF16)
    for f in range(w1_ref.shape[1] // tf):
        cols = slice(f * tf, (f + 1) * tf)
        hid = jnp.dot(h, w1_ref[:, cols], preferred_element_type=F32)
        hid_ref[:, cols] = jnp.square(jnp.maximum(hid, 0.0)).astype(BF16)
    o_ref[...] = x + jnp.dot(hid_ref[...], w2_ref[...], preferred_element_type=F32)


def mlp_block(x, gain, w1, w2, tf=512, tm=ROW_TILE):
    t, d = x.shape
    ff = w1.shape[1]
    resident = lambda shape: pl.BlockSpec(shape, lambda i: (0, 0), pipeline_mode=pl.Buffered(1))
    return pl.pallas_call(
        functools.partial(_mlp_body, tf=tf),
        out_shape=jax.ShapeDtypeStruct((t, d), F32),
        grid=(t // tm,),
        in_specs=[pl.BlockSpec((tm, d), lambda i: (i, 0)),
                  pl.BlockSpec((1, d), lambda i: (0, 0)),
                  resident((d, ff)), resident((ff, d))],
        out_specs=pl.BlockSpec((tm, d), lambda i: (i, 0)),
        scratch_shapes=[pltpu.VMEM((tm, ff), BF16)],
        compiler_params=_params("parallel"),
        name="mlp_block",
    )(x, gain.reshape(1, d), w1, w2)


def _mem_kv_body(m_ref, g_ref, w_ref, kg_ref, k_ref, v_ref):
    h = _rms(m_ref[0], g_ref[0]).astype(BF16)
    kv = jnp.dot(h, w_ref[0], preferred_element_type=F32)
    for hd in range(X_HEADS):
        sl = slice(hd * X_DH, (hd + 1) * X_DH)
        k_ref[0, 0, :, sl] = _rms(kv[:, sl], kg_ref[0]).astype(BF16)
    v_ref[0, 0] = kv[:, D_MODEL:].astype(BF16)


def mem_kv(mem, norm_mem, w_kv, k_norm):
    bm, nm, d = mem.shape
    out = jax.ShapeDtypeStruct((DEPTH, bm, nm, d), BF16)
    return pl.pallas_call(
        _mem_kv_body,
        out_shape=(out, out),
        grid=(DEPTH, bm),
        in_specs=[pl.BlockSpec((1, nm, d), lambda l, b: (b, 0, 0)),
                  pl.BlockSpec((1, 1, d), lambda l, b: (l, 0, 0)),
                  pl.BlockSpec((1, d, 2 * d), lambda l, b: (l, 0, 0)),
                  pl.BlockSpec((1, 1, X_DH), lambda l, b: (l, 0, 0))],
        out_specs=(pl.BlockSpec((1, 1, nm, d), lambda l, b: (l, b, 0, 0)),
                   pl.BlockSpec((1, 1, nm, d), lambda l, b: (l, b, 0, 0))),
        compiler_params=_params("arbitrary", "arbitrary"),
        name="mem_kv",
    )(mem, norm_mem.reshape(DEPTH, 1, d), w_kv, k_norm.reshape(DEPTH, 1, X_DH))


def _cross_body(x_ref, g_ref, wq_ref, qg_ref, k_ref, v_ref, wo_ref, o_ref):
    x = x_ref[...]
    h = _rms(x, g_ref[...]).astype(BF16)
    q = jnp.dot(h, wq_ref[...], preferred_element_type=F32)
    heads = []
    for hd in range(X_HEADS):
        sl = slice(hd * X_DH, (hd + 1) * X_DH)
        qn = _rms(q[:, sl], qg_ref[...]).astype(BF16)
        s = lax.dot_general(qn, k_ref[0, :, sl], NT_DIMS, preferred_element_type=F32)
        p = jnp.exp(s - jnp.max(s, axis=-1, keepdims=True))
        l = jnp.sum(p, axis=-1, keepdims=True)
        oh = jnp.dot(p.astype(BF16), v_ref[0, :, sl], preferred_element_type=F32) / l
        heads.append(oh.astype(BF16))
    o = jnp.concatenate(heads, axis=-1)
    o_ref[...] = x + jnp.dot(o, wo_ref[...], preferred_element_type=F32)


def cross_block(x, seq_len, gain, w_q, q_gain, kn, v, w_o, tm=ROW_TILE):
    t, d = x.shape
    nm = kn.shape[1]
    per_seq = seq_len // tm
    return pl.pallas_call(
        _cross_body,
        out_shape=jax.ShapeDtypeStruct((t, d), F32),
        grid=(t // tm,),
        in_specs=[pl.BlockSpec((tm, d), lambda i: (i, 0)),
                  pl.BlockSpec((1, d), lambda i: (0, 0)),
                  pl.BlockSpec((d, d), lambda i: (0, 0)),
                  pl.BlockSpec((1, X_DH), lambda i: (0, 0)),
                  pl.BlockSpec((1, nm, d), lambda i: (i // per_seq, 0, 0)),
                  pl.BlockSpec((1, nm, d), lambda i: (i // per_seq, 0, 0)),
                  pl.BlockSpec((d, d), lambda i: (0, 0))],
        out_specs=pl.BlockSpec((tm, d), lambda i: (i, 0)),
        compiler_params=_params("parallel"),
        name="cross_block",
    )(x, gain.reshape(1, d), w_q, (q_gain * X_DH ** -0.5).reshape(1, X_DH), kn, v, w_o)


LANES = 128


def _toeplitz_body(k_ref, o_ref):
    lc, gs = S5_CHUNK, S5_GROUP
    kern = k_ref[0]
    n = kern.shape[1]
    per_tile = LANES // gs
    for r in range(per_tile):
        shifted = pltpu.roll(kern, n - gs * r, 1) if r else kern
        for a in range(lc // per_tile):
            s = lc - 1 - (per_tile * a + r)
            o_ref[0, s * gs:(s + 1) * gs, :] = shifted[:, LANES * a:LANES * a + lc * gs].astype(o_ref.dtype)


def toeplitz_expand(kern_flat):
    g, gs, n = kern_flat.shape
    width = S5_CHUNK * S5_GROUP
    return pl.pallas_call(
        _toeplitz_body,
        out_shape=jax.ShapeDtypeStruct((g, width, width), BF16),
        grid=(g,),
        in_specs=[pl.BlockSpec((1, gs, n), lambda i: (i, 0, 0))],
        out_specs=pl.BlockSpec((1, width, width), lambda i: (i, 0, 0)),
        compiler_params=_params("parallel"),
        name="toeplitz_expand",
    )(kern_flat)


def s5_operators(lam_re, lam_im, log_step, b_re, b_im, c_re, c_im, d):
    lc = S5_CHUNK
    lam = lax.complex(lam_re.astype(F32), lam_im.astype(F32))
    step = jnp.exp(log_step.astype(F32))[..., None]
    lam_bar = jnp.exp(lam * step)
    b_bar = ((lam_bar - 1.0) / lam)[..., None] * lax.complex(b_re.astype(F32), b_im.astype(F32))
    c = lax.complex(c_re.astype(F32), c_im.astype(F32))
    pw = jnp.cumprod(jnp.broadcast_to(lam_bar[..., None], lam_bar.shape + (lc,)), axis=-1)
    pw = jnp.concatenate([jnp.ones_like(pw[..., :1]), pw], axis=-1)
    kern = jnp.einsum('zgcp,zgpt,zgpd->zgtcd', c, pw[..., :lc], b_bar, precision=HIGHEST).real
    kf, kb = kern[0], kern[1]
    k0 = kf[:, :1] + kb[:, :1] + (d.astype(F32)[:, :, None] * jnp.eye(S5_GROUP, dtype=F32))[:, None]
    kern_full = jnp.concatenate([kb[:, :0:-1], k0, kf[:, 1:]], axis=1)
    kern_flat = kern_full.transpose(0, 3, 1, 2).reshape(S5_GROUPS, S5_GROUP, (2 * lc - 1) * S5_GROUP)
    toep = toeplitz_expand(jnp.pad(kern_flat, ((0, 0), (0, 0), (0, S5_GROUP))))

    pf = jnp.einsum('gps,gpd->gsdp', pw[0][..., lc - 1::-1], b_bar[0])
    pb = jnp.einsum('gps,gpd->gsdp', pw[1][..., :lc], b_bar[1])
    p_op = jnp.concatenate([pf.real, pb.real, pf.imag, pb.imag], axis=-1)
    p_op = p_op.reshape(S5_GROUPS, lc * S5_GROUP, 4 * S5_STATE)

    qf = jnp.einsum('gcp,gpt->gptc', c[0], pw[0][..., 1:])
    qb = jnp.einsum('gcp,gpt->gptc', c[1], pw[1][..., :0:-1])
    q_op = jnp.concatenate([qf.real, qb.real, -qf.imag, -qb.imag], axis=1)
    q_op = q_op.reshape(S5_GROUPS, 4 * S5_STATE, lc * S5_GROUP)

    a = pw[..., lc]
    coef = jnp.stack([jnp.concatenate([a[0].real, a[1].real], -1),
                      jnp.concatenate([a[0].imag, a[1].imag], -1)], axis=1)
    return toep, _split_bf16(p_op), _split_bf16(q_op), coef


def _split_bf16(x):
    hi = x.astype(BF16)
    lo = (x - hi.astype(F32)).astype(BF16)
    return jnp.stack([hi, lo], axis=1)


SUBLANES = 8


def _s5_group_body(u_ref, t_ref, p_ref, q_ref, c_ref, y_ref,
                   v_re_ref, v_im_ref, xf_re_ref, xf_im_ref, xb_re_ref, xb_im_ref, *, n_chunks, bsz):
    u = u_ref[0]
    half = 2 * S5_STATE
    v = (jnp.dot(u, p_ref[0, 0], preferred_element_type=F32)
         + jnp.dot(u, p_ref[0, 1], preferred_element_type=F32))
    v_re_ref[...] = v[:, :half]
    v_im_ref[...] = v[:, half:]
    coef = c_ref[0]
    a_re, a_im = coef[0:1], coef[1:2]
    fwd = lax.broadcasted_iota(jnp.int32, (bsz, half), 1) < S5_STATE

    def step(i, carry):
        re, im = carry
        rows_f = pl.ds(i, bsz, stride=n_chunks)
        rows_b = pl.ds(n_chunks - 1 - i, bsz, stride=n_chunks)
        xf_re_ref[rows_f, :] = re
        xf_im_ref[rows_f, :] = im
        xb_re_ref[rows_b, :] = re
        xb_im_ref[rows_b, :] = im
        re_next = a_re * re - a_im * im + jnp.where(fwd, v_re_ref[rows_f, :], v_re_ref[rows_b, :])
        im_next = a_re * im + a_im * re + jnp.where(fwd, v_im_ref[rows_f, :], v_im_ref[rows_b, :])
        return re_next, im_next

    zero = jnp.zeros((bsz, half), F32)
    lax.fori_loop(0, n_chunks, step, (zero, zero))
    is_fwd = lax.broadcasted_iota(jnp.int32, xf_re_ref.shape, 1) < S5_STATE
    x = jnp.concatenate([jnp.where(is_fwd, xf_re_ref[...], xb_re_ref[...]),
                         jnp.where(is_fwd, xf_im_ref[...], xb_im_ref[...])], axis=1)
    x_hi = x.astype(BF16)
    x_lo = (x - x_hi.astype(F32)).astype(BF16)
    y = (jnp.dot(u, t_ref[0], preferred_element_type=F32)
         + jnp.dot(x_hi, q_ref[0, 0], preferred_element_type=F32)
         + jnp.dot(x_lo, q_ref[0, 0], preferred_element_type=F32)
         + jnp.dot(x_hi, q_ref[0, 1], preferred_element_type=F32))
    y_ref[0] = y.astype(y_ref.dtype)


def s5_scan(u, bsz, seq_len, ops):
    toep, p_op, q_op, coef = ops
    lc, g, w = S5_CHUNK, S5_GROUPS, S5_CHUNK * S5_GROUP
    n = seq_len // lc
    c = bsz * n
    ns = 4 * S5_STATE
    ug = u.reshape(bsz, n, lc, g, S5_GROUP).transpose(3, 0, 1, 2, 4).reshape(g, c, w)
    group = lambda *shape: pl.BlockSpec((1,) + shape, lambda i: (i,) + (0,) * len(shape))
    y = pl.pallas_call(
        functools.partial(_s5_group_body, n_chunks=n, bsz=bsz),
        out_shape=jax.ShapeDtypeStruct((g, c, w), BF16),
        grid=(g,),
        in_specs=[group(c, w), group(w, w), group(2, w, ns), group(2, ns, w), group(2, ns // 2)],
        out_specs=group(c, w),
        scratch_shapes=[pltpu.VMEM((c, ns // 2), F32)] * 6,
        compiler_params=_params("parallel"),
        name="s5_group",
    )(ug, toep, p_op, q_op, coef)
    return y.reshape(g, bsz, n, lc, S5_GROUP).transpose(1, 2, 3, 0, 4).reshape(bsz * seq_len, S5_WIDTH)


def _gla_direction(q_ref, k_ref, v_ref, g_ref, wg, bg, s_ref, o_ref, forward, chunks):
    cs = GLA_CHUNK
    tb = chunks * cs
    glr = g_ref[...]
    logit = (jnp.dot(glr, wg[0], preferred_element_type=F32)
             + jnp.dot(glr, wg[1], preferred_element_type=F32) + bg)
    g = (jnp.minimum(logit, 0.0) - jnp.log1p(jnp.exp(-jnp.abs(logit)))) / GLA_TAU
    row = lax.broadcasted_iota(jnp.int32, (tb, tb), 0)
    col = lax.broadcasted_iota(jnp.int32, (tb, tb), 1)
    same_chunk = (row // cs) == (col // cs)
    within = same_chunk & ((col <= row) if forward else (col >= row))
    ones = jnp.where(within, 1.0, 0.0).astype(BF16)
    bcum = jnp.zeros_like(g)
    rest = g
    for _ in range(3):
        term = rest.astype(BF16)
        bcum = bcum + jnp.dot(ones, term, preferred_element_type=F32)
        rest = rest - term.astype(F32)

    srow = lax.broadcasted_iota(jnp.int32, (GLA_HEADS * cs, cs), 0) & (cs - 1)
    scol = lax.broadcasted_iota(jnp.int32, (GLA_HEADS * cs, cs), 1)
    keep = (scol <= srow) if forward else (scol >= srow)
    lane = lax.broadcasted_iota(jnp.int32, (1, GLA_QK), 1)
    head_lanes = [((lane >= h * GLA_DK) & (lane < (h + 1) * GLA_DK)).astype(F32) for h in range(GLA_HEADS)]
    state_mask = (lax.broadcasted_iota(jnp.int32, (GLA_V, GLA_QK), 0) // GLA_DV
                  == lax.broadcasted_iota(jnp.int32, (GLA_V, GLA_QK), 1) // GLA_DK)

    i_ref = cs // 2 - 1 if forward else cs // 2
    i_last = cs - 1 if forward else 0
    scale = GLA_DK ** -0.5
    for ci in (range(chunks) if forward else reversed(range(chunks))):
        rows = slice(ci * cs, (ci + 1) * cs)
        b = bcum[rows]
        bref = b[i_ref:i_ref + 1]
        blast = b[i_last:i_last + 1]
        q = q_ref[rows, :].astype(F32) * scale
        k = k_ref[rows, :].astype(F32)
        v = v_ref[rows, :]
        q_rel = q * jnp.exp(b - bref)
        k_rel = (k * jnp.exp(bref - b)).astype(BF16)
        k_out = (k * jnp.exp(blast - b)).astype(BF16)
        q_dec = (q * jnp.exp(b)).astype(BF16)
        decay = jnp.exp(blast)
        q_heads = jnp.concatenate([q_rel * hm for hm in head_lanes], axis=0).astype(BF16)
        s = lax.dot_general(q_heads, k_rel, NT_DIMS, preferred_element_type=F32)
        s = jnp.where(keep, s, 0.0).astype(BF16)
        st = s_ref[...]
        o_inter = lax.dot_general(q_dec, st.astype(BF16), NT_DIMS, preferred_element_type=F32)
        o_intra = [jnp.dot(s[h * cs:(h + 1) * cs], v[:, h * GLA_DV:(h + 1) * GLA_DV],
                           preferred_element_type=F32) for h in range(GLA_HEADS)]
        o_ref[rows, :] = o_inter + jnp.concatenate(o_intra, axis=-1)
        kv = lax.dot_general(v, k_out, TN_DIMS, preferred_element_type=F32)
        s_ref[...] = decay * st + jnp.where(state_mask, kv, 0.0)


def _gla_body(qf_ref, kf_ref, vf_ref, gf_ref, qb_ref, kb_ref, vb_ref, gb_ref, wg_ref, bg_ref,
              of_ref, ob_ref, sf_ref, sb_ref, *, chunks):
    @pl.when(pl.program_id(1) == 0)
    def _():
        sf_ref[...] = jnp.zeros_like(sf_ref)
        sb_ref[...] = jnp.zeros_like(sb_ref)

    _gla_direction(qf_ref, kf_ref, vf_ref, gf_ref, wg_ref[0], bg_ref[0], sf_ref, of_ref, True, chunks)
    _gla_direction(qb_ref, kb_ref, vb_ref, gb_ref, wg_ref[1], bg_ref[1], sb_ref, ob_ref, False, chunks)


def gla_scan(proj, bsz, seq_len, w_gate, b_gate, chunks=4):
    t = proj.shape[0]
    tb = chunks * GLA_CHUNK
    nb = seq_len // tb
    fwd = lambda cb: (lambda b, i: (b * nb + i, cb))
    bwd = lambda cb: (lambda b, i: (b * nb + nb - 1 - i, cb))
    qc, kc, vc, gc = 512 // GLA_QK, 768 // GLA_QK, 1024 // GLA_V, 2048 // 128
    wg = jnp.zeros((2, 128, GLA_QK), F32)
    wg = wg.at[0, :GLA_RANK].set(w_gate[0].astype(F32)).at[1, GLA_RANK:2 * GLA_RANK].set(w_gate[1].astype(F32))
    wg = _split_bf16(wg)
    out = jax.ShapeDtypeStruct((t, GLA_V), F32)
    state = pltpu.VMEM((GLA_V, GLA_QK), F32)
    return pl.pallas_call(
        functools.partial(_gla_body, chunks=chunks),
        out_shape=(out, out),
        grid=(bsz, nb),
        in_specs=[pl.BlockSpec((tb, GLA_QK), fwd(qc)), pl.BlockSpec((tb, GLA_QK), fwd(kc)),
                  pl.BlockSpec((tb, GLA_V), fwd(vc)), pl.BlockSpec((tb, 128), fwd(gc)),
                  pl.BlockSpec((tb, GLA_QK), bwd(qc)), pl.BlockSpec((tb, GLA_QK), bwd(kc)),
                  pl.BlockSpec((tb, GLA_V), bwd(vc)), pl.BlockSpec((tb, 128), bwd(gc)),
                  pl.BlockSpec((2, 2, 128, GLA_QK), lambda b, i: (0, 0, 0, 0)),
                  pl.BlockSpec((2, 1, GLA_QK), lambda b, i: (0, 0, 0))],
        out_specs=(pl.BlockSpec((tb, GLA_V), fwd(0)), pl.BlockSpec((tb, GLA_V), bwd(0))),
        scratch_shapes=[state, state],
        compiler_params=_params("parallel", "arbitrary"),
        name="gla_scan",
    )(proj, proj, proj, proj, proj, proj, proj, proj, wg, b_gate.astype(F32).reshape(2, 1, GLA_QK))


def _even_out_body(x_ref, ys_ref, of_ref, ob_ref, og_ref, wglu_ref, gn_ref, wtop_ref, wbot_ref, o_ref):
    y = ys_ref[...].astype(F32)
    y = 0.5 * y * (1.0 + jnp.tanh(math.sqrt(2.0 / math.pi) * (y + 0.044715 * (y * y * y))))
    gate = jnp.dot(y.astype(BF16), wglu_ref[...], preferred_element_type=F32)
    y = y * _sigmoid(gate)
    o = of_ref[...] + ob_ref[...]
    og = og_ref[...].astype(F32)
    heads = []
    for h in range(GLA_HEADS):
        sl = slice(h * GLA_DV, (h + 1) * GLA_DV)
        heads.append(_rms(o[:, sl], gn_ref[...]))
    o = jnp.concatenate(heads, axis=-1) * (og * _sigmoid(og))
    o_ref[...] = (x_ref[...]
                  + jnp.dot(y.astype(BF16), wtop_ref[...], preferred_element_type=F32)
                  + jnp.dot(o.astype(BF16), wbot_ref[...], preferred_element_type=F32))


def even_out(x, ys, o_f, o_b, proj, w_glu, gla_norm, w_out, tm=ROW_TILE):
    t, d = x.shape
    row = lambda i: (i, 0)
    const = lambda i: (0, 0)
    return pl.pallas_call(
        _even_out_body,
        out_shape=jax.ShapeDtypeStruct((t, d), F32),
        grid=(t // tm,),
        in_specs=[pl.BlockSpec((tm, d), row), pl.BlockSpec((tm, S5_WIDTH), row),
                  pl.BlockSpec((tm, GLA_V), row), pl.BlockSpec((tm, GLA_V), row),
                  pl.BlockSpec((tm, GLA_V), lambda i: (i, 1536 // GLA_V)),
                  pl.BlockSpec((S5_WIDTH, S5_WIDTH), const), pl.BlockSpec((1, GLA_DV), const),
                  pl.BlockSpec((S5_WIDTH, d), const), pl.BlockSpec((GLA_V, d), const)],
        out_specs=pl.BlockSpec((tm, d), row),
        compiler_params=_params("parallel"),
        name="even_out",
    )(x, ys, o_f, o_b, proj, w_glu, gla_norm.astype(F32).reshape(1, GLA_DV),
      w_out[:S5_WIDTH], w_out[S5_WIDTH:])


QK_NORM_WIDTH = 256


def _seg_rms(a, gain, same_seg):
    ssq = jnp.dot((a * a).astype(BF16), same_seg, preferred_element_type=F32)
    return a * lax.rsqrt(ssq * (1.0 / DIFF_DK) + EPS) * gain


def _qkv_body(x_ref, g_ref, w_ref, qk_gain_ref, o_ref):
    d = x_ref.shape[1]
    h = _rms(x_ref[...], g_ref[...]).astype(BF16)
    w = QK_NORM_WIDTH
    same_seg = jnp.where(lax.broadcasted_iota(jnp.int32, (w, w), 0) // DIFF_DK
                         == lax.broadcasted_iota(jnp.int32, (w, w), 1) // DIFF_DK, 1.0, 0.0).astype(BF16)
    for part in range(2):
        acc = jnp.dot(h, w_ref[:, part * d:(part + 1) * d], preferred_element_type=F32)
        gain = qk_gain_ref[part]
        for c in range(d // w):
            sl = slice(c * w, (c + 1) * w)
            o_ref[:, part * d + c * w:part * d + (c + 1) * w] = (
                _seg_rms(acc[:, sl], gain, same_seg).astype(o_ref.dtype))
    o_ref[:, 2 * d:] = jnp.dot(h, w_ref[:, 2 * d:], preferred_element_type=F32).astype(o_ref.dtype)


def qkv_project(x, gain, w, q_norm, k_norm, tm=ROW_TILE):
    t, d = x.shape
    reps = QK_NORM_WIDTH // DIFF_DK
    qg = jnp.tile(q_norm.astype(F32) * DIFF_DK ** -0.5, reps)
    kg = jnp.tile(k_norm.astype(F32), reps)
    qk_gain = jnp.stack([qg, kg]).reshape(2, 1, QK_NORM_WIDTH)
    return pl.pallas_call(
        _qkv_body,
        out_shape=jax.ShapeDtypeStruct((t, 3 * d), BF16),
        grid=(t // tm,),
        in_specs=[pl.BlockSpec((tm, d), lambda i: (i, 0)),
                  pl.BlockSpec((1, d), lambda i: (0, 0)),
                  pl.BlockSpec((d, 3 * d), lambda i: (0, 0)),
                  pl.BlockSpec((2, 1, QK_NORM_WIDTH), lambda i: (0, 0, 0))],
        out_specs=pl.BlockSpec((tm, 3 * d), lambda i: (i, 0)),
        compiler_params=_params("parallel"),
        name="qkv_project",
    )(x, gain.reshape(1, d), w, qk_gain)


POS_SPLIT = 16
POS_SHIFT = 4
AUG_LANE = DIFF_DK
SOFTMAX_GROUP_ELEMS = 32 * 1024
ROW_PARTS = 2


def _pos_terms(shape):
    lane = lax.broadcasted_iota(jnp.int32, shape, 1)
    pos = lax.broadcasted_iota(jnp.int32, shape, 0)
    hi = lax.shift_right_logical(pos, POS_SHIFT).astype(F32)
    lo = (pos & (POS_SPLIT - 1)).astype(F32)
    return lane, hi, lo


def _lane_select(lane, first, values):
    out = 0.0
    for n, val in reversed(list(enumerate(values))):
        out = jnp.where(lane == first + n, val, out)
    return out


def _diff_attn_body(slope_ref, q_ref, k_ref, v_ref, lq1_ref, lk1_ref, lq2_ref, lk2_ref, sub_ref,
                    o_ref, ka_ref, va_ref, qs_ref, s0_ref, s1_ref, p0_ref, p1_ref, vs_ref,
                    *, blk, seq_len, lambda_init):
    h = pl.program_id(1)
    qi = pl.program_id(2)
    slope = slope_ref[h]
    nk = seq_len // blk
    s_refs, p_refs = (s0_ref, s1_ref), (p0_ref, p1_ref)

    @pl.when(qi == 0)
    def _():
        va_ref[:, :DIFF_DV] = v_ref[...]
        va_ref[:, DIFF_DV:] = jnp.ones((seq_len, DIFF_DV), BF16)

        def build(t, carry):
            rows = pl.ds(pl.multiple_of(t * blk, blk), blk)
            kf = k_ref[rows, :].astype(F32)
            lane, hi, lo = _pos_terms(kf.shape)
            aug = _lane_select(lane, AUG_LANE,
                               [-blk * slope, -POS_SPLIT * slope, -slope,
                                (blk * slope) * jnp.asarray(t, F32), (POS_SPLIT * slope) * hi, slope * lo])
            for z, kz in enumerate((kf, pltpu.roll(kf, DIFF_DK, 1))):
                ka_ref[z, rows, :] = jnp.where(lane < DIFF_DK, kz, aug).astype(BF16)
            return carry

        lax.fori_loop(0, nk, build, 0)

    qf = q_ref[...].astype(F32)
    lane, hi, lo = _pos_terms(qf.shape)
    qa = _lane_select(lane, AUG_LANE, [jnp.asarray(qi, F32), hi, lo, 1.0, 1.0, 1.0])
    for z, qz in enumerate((qf, pltpu.roll(qf, DIFF_DK, 1))):
        qs_ref[z] = jnp.where(lane < DIFF_DK, qz, 0.0).astype(BF16)
        qs_ref[2 + z] = jnp.where(lane < DIFF_DK, qz, qa).astype(BF16)
        qs_ref[4 + z] = jnp.where(lane < DIFF_DK, qz, -qa).astype(BF16)

    def key_block(t):
        if t == 0:
            return qi, 0
        j = (t - 1) + jnp.asarray(qi <= t - 1, jnp.int32)
        return j, jnp.where(j < qi, 2, 4)

    def key_rows(t):
        j, _ = key_block(t)
        return pl.ds(pl.multiple_of(j * blk, blk), blk)

    part = blk // ROW_PARTS
    parts = [slice(r * part, (r + 1) * part) for r in range(ROW_PARTS)]
    for z, s_ref in enumerate(s_refs):
        for rows in parts:
            for t in range(nk):
                _, variant = key_block(t)
                s_ref[rows, t * blk:(t + 1) * blk] = lax.dot_general(
                    qs_ref[variant + z, rows, :], ka_ref[z, key_rows(t), :], NT_DIMS,
                    preferred_element_type=F32)

    rg = SOFTMAX_GROUP_ELEMS // seq_len
    rel = (lax.broadcasted_iota(jnp.int32, (rg, blk), 0) - lax.broadcasted_iota(jnp.int32, (rg, blk), 1))
    for t in range(nk):
        vs_ref[t * blk:(t + 1) * blk, :] = va_ref[key_rows(t), :]

    acc = []
    for s_ref, p_ref in zip(s_refs, p_refs):
        for g in range(blk // rg):
            rows = slice(g * rg, (g + 1) * rg)
            s_diag = s_ref[rows, :blk] - slope * jnp.abs(rel + g * rg).astype(F32)
            s_rest = s_ref[rows, blk:]
            m = jnp.maximum(jnp.max(s_diag, axis=-1, keepdims=True), jnp.max(s_rest, axis=-1, keepdims=True))
            p_ref[rows, :blk] = jnp.exp(s_diag - m).astype(BF16)
            p_ref[rows, blk:] = jnp.exp(s_rest - m).astype(BF16)
        acc.append([jnp.dot(p_ref[rows, :], vs_ref[...], preferred_element_type=F32) for rows in parts])

    lam = (jnp.exp(jnp.sum(lq1_ref[...] * lk1_ref[...], axis=-1, keepdims=True))
           - jnp.exp(jnp.sum(lq2_ref[...] * lk2_ref[...], axis=-1, keepdims=True)) + lambda_init)
    for rows, a0, a1 in zip(parts, *acc):
        o = (a0[:, :DIFF_DV] / a0[:, DIFF_DV:DIFF_DV + 1]
             - lam * (a1[:, :DIFF_DV] / a1[:, DIFF_DV:DIFF_DV + 1]))
        o_ref[rows, :] = (_rms(o, sub_ref[...]) * (1.0 - lambda_init)).astype(o_ref.dtype)


def diff_attention(qkv, bsz, seq_len, lq1, lk1, lq2, lk2, sub_norm, lambda_init, blk=512):
    t = qkv.shape[0]
    nq = seq_len // blk
    assert blk // POS_SPLIT <= 256, "hi part of a block position must stay exact in bf16"
    slopes = jnp.asarray(2.0 ** (-8.0 * np.arange(1, DIFF_HEADS + 1, dtype=np.float32) / DIFF_HEADS), F32)
    vec = lambda a: a.astype(F32).reshape(1, DIFF_DK)
    const = lambda b, h, i: (0, 0)
    return pl.pallas_call(
        functools.partial(_diff_attn_body, blk=blk, seq_len=seq_len, lambda_init=lambda_init),
        out_shape=jax.ShapeDtypeStruct((t, DIFF_HEADS * DIFF_DV), BF16),
        grid=(bsz, DIFF_HEADS, nq),
        in_specs=[pl.BlockSpec(memory_space=pltpu.SMEM),
                  pl.BlockSpec((blk, 2 * DIFF_DK), lambda b, h, i: (b * nq + i, h)),
                  pl.BlockSpec((seq_len, 2 * DIFF_DK), lambda b, h, i: (b, DIFF_HEADS + h)),
                  pl.BlockSpec((seq_len, DIFF_DV), lambda b, h, i: (b, 2 * DIFF_HEADS + h)),
                  pl.BlockSpec((1, DIFF_DK), const), pl.BlockSpec((1, DIFF_DK), const),
                  pl.BlockSpec((1, DIFF_DK), const), pl.BlockSpec((1, DIFF_DK), const),
                  pl.BlockSpec((1, DIFF_DV), const)],
        out_specs=pl.BlockSpec((blk, DIFF_DV), lambda b, h, i: (b * nq + i, h)),
        scratch_shapes=[pltpu.VMEM((2, seq_len, 2 * DIFF_DK), BF16),
                        pltpu.VMEM((seq_len, 2 * DIFF_DV), BF16),
                        pltpu.VMEM((6, blk, 2 * DIFF_DK), BF16),
                        pltpu.VMEM((blk, seq_len), F32),
                        pltpu.VMEM((blk, seq_len), F32),
                        pltpu.VMEM((blk, seq_len), BF16),
                        pltpu.VMEM((blk, seq_len), BF16),
                        pltpu.VMEM((seq_len, 2 * DIFF_DV), BF16)],
        compiler_params=_params("parallel", "parallel", "arbitrary"),
        name="diff_attention",
    )(slopes, qkv, qkv, qkv, vec(lq1), vec(lk1), vec(lq2), vec(lk2),
      sub_norm.astype(F32).reshape(1, DIFF_DV))


def _pad_ev_w_in(w):
    return jnp.pad(w, ((0, 0), (0, EV_PAD_COLS - w.shape[1]))).astype(BF16)


def _trunk(x3, mem, p):
    bsz, seq_len, d = x3.shape
    x = x3.reshape(bsz * seq_len, d)
    kn_all, v_all = mem_kv(mem, p['norm_mem'], p['x_w_kv'], p['x_k_norm'])
    for layer in range(DEPTH):
        if layer % 2 == 0:
            e = layer // 2
            proj = norm_matmul(x, p['norm_mix'][layer], p['ev_w_in'][e], BF16, EV_TN)
            ys = s5_scan(proj[:, :S5_WIDTH], bsz, seq_len, p['s5_ops'][e])
            o_f, o_b = gla_scan(proj, bsz, seq_len, p['gla_w_gate'][e], p['gla_b_gate'][e])
            x = even_out(x, ys, o_f, o_b, proj, p['s5_w_glu'][e], p['gla_norm'][e], p['ev_w_out'][e])
        else:
            o = layer // 2
            lambda_init = 0.8 - 0.6 * math.exp(-0.3 * layer)
            qkv = qkv_project(x, p['norm_mix'][layer], p['od_w_in'][o],
                              p['diff_q_norm'][o], p['diff_k_norm'][o])
            att = diff_attention(qkv, bsz, seq_len, p['diff_lambda_q1'][o], p['diff_lambda_k1'][o],
                                 p['diff_lambda_q2'][o], p['diff_lambda_k2'][o], p['diff_norm'][o],
                                 lambda_init)
            x = matmul_residual(att, p['od_w_out'][o], x, tn=d)
        x = cross_block(x, seq_len, p['norm_cross'][layer], p['x_w_q'][layer], p['x_q_norm'][layer],
                        kn_all[layer], v_all[layer], p['x_w_o'][layer])
        x = mlp_block(x, p['norm_mlp'][layer], p['mlp_w1'][layer], p['mlp_w2'][layer])
    return x.reshape(bsz, seq_len, d)


def kernel(x_prompt, x_sample, mem_prompt, mem_sample, norm_mix, norm_cross, norm_mem, norm_mlp,
           ev_w_in, ev_w_out, s5_lambda_re, s5_lambda_im, s5_log_step, s5_b_re, s5_b_im,
           s5_c_re, s5_c_im, s5_d, s5_w_glu, gla_w_gate, gla_b_gate, gla_norm,
           od_w_in, od_w_out, diff_q_norm, diff_k_norm, diff_lambda_q1, diff_lambda_k1,
           diff_lambda_q2, diff_lambda_k2, diff_norm, x_w_q, x_w_kv, x_w_o, x_q_norm, x_k_norm,
           mlp_w1, mlp_w2):
    n_even = ev_w_in.shape[0]
    bf = lambda w: w.astype(BF16)
    p = dict(
        norm_mix=norm_mix, norm_cross=norm_cross, norm_mem=norm_mem, norm_mlp=norm_mlp,
        ev_w_in=[_pad_ev_w_in(ev_w_in[e]) for e in range(n_even)], ev_w_out=bf(ev_w_out),
        s5_ops=[s5_operators(s5_lambda_re[e], s5_lambda_im[e], s5_log_step[e], s5_b_re[e], s5_b_im[e],
                             s5_c_re[e], s5_c_im[e], s5_d[e]) for e in range(n_even)],
        s5_w_glu=bf(s5_w_glu), gla_w_gate=gla_w_gate, gla_b_gate=gla_b_gate, gla_norm=gla_norm,
        od_w_in=bf(od_w_in), od_w_out=bf(od_w_out), diff_q_norm=diff_q_norm, diff_k_norm=diff_k_norm,
        diff_lambda_q1=diff_lambda_q1, diff_lambda_k1=diff_lambda_k1,
        diff_lambda_q2=diff_lambda_q2, diff_lambda_k2=diff_lambda_k2, diff_norm=diff_norm,
        x_w_q=bf(x_w_q), x_w_kv=bf(x_w_kv), x_w_o=bf(x_w_o), x_q_norm=x_q_norm, x_k_norm=x_k_norm,
        mlp_w1=bf(mlp_w1), mlp_w2=bf(mlp_w2))
    return (_trunk(x_prompt, mem_prompt, p), _trunk(x_sample, mem_sample, p))
```

```python
import functools
import math

import numpy as np
import jax
import jax.numpy as jnp
from jax import lax
from jax.experimental import pallas as pl
from jax.experimental.pallas import tpu as pltpu

F32 = jnp.float32
BF16 = jnp.bfloat16
HIGHEST = lax.Precision.HIGHEST

D_MODEL = 1024
DEPTH = 4
EPS = 1e-6
S5_WIDTH = 512
S5_GROUP = 16
S5_GROUPS = 32
S5_STATE = 64
S5_CHUNK = 64
GLA_HEADS = 4
GLA_DV = 128
GLA_DK = 64
GLA_RANK = 16
GLA_TAU = 16.0
GLA_CHUNK = 64
GLA_QK = GLA_HEADS * GLA_DK
GLA_V = GLA_HEADS * GLA_DV
EV_PAD_COLS = 2304
EV_TN = 768
DIFF_HEADS = 8
DIFF_DK = 64
DIFF_DV = 128
X_HEADS = 4
X_DH = 256
D_FF = 4096

ROW_TILE = 512
VMEM_LIMIT = 48 * 1024 * 1024

NT_DIMS = (((1,), (1,)), ((), ()))
TN_DIMS = (((0,), (0,)), ((), ()))


def _params(*sem, flags=None):
    return pltpu.CompilerParams(dimension_semantics=sem, vmem_limit_bytes=VMEM_LIMIT, flags=flags)


def _rms(x, gain):
    ms = jnp.mean(x * x, axis=-1, keepdims=True)
    return x * lax.rsqrt(ms + EPS) * gain


def _sigmoid(x):
    return 1.0 / (1.0 + jnp.exp(-x))


def _norm_matmul_body(x_ref, g_ref, w_ref, o_ref, *, tn):
    h = _rms(x_ref[...], g_ref[...]).astype(BF16)
    for j in range(w_ref.shape[1] // tn):
        cols = slice(j * tn, (j + 1) * tn)
        o_ref[:, cols] = jnp.dot(h, w_ref[:, cols], preferred_element_type=F32).astype(o_ref.dtype)


def norm_matmul(x, gain, w, out_dtype, tn, tm=ROW_TILE):
    t, d = x.shape
    n = w.shape[1]
    return pl.pallas_call(
        functools.partial(_norm_matmul_body, tn=tn),
        out_shape=jax.ShapeDtypeStruct((t, n), out_dtype),
        grid=(t // tm,),
        in_specs=[pl.BlockSpec((tm, d), lambda i: (i, 0)),
                  pl.BlockSpec((1, d), lambda i: (0, 0)),
                  pl.BlockSpec((d, n), lambda i: (0, 0))],
        out_specs=pl.BlockSpec((tm, n), lambda i: (i, 0)),
        compiler_params=_params("parallel"),
        name="norm_matmul",
    )(x, gain.reshape(1, d), w)


def _matmul_res_body(a_ref, w_ref, r_ref, o_ref):
    o_ref[...] = r_ref[...] + jnp.dot(a_ref[...].astype(BF16), w_ref[...], preferred_element_type=F32)


def matmul_residual(a, w, res, tn=512, tm=ROW_TILE):
    t, k = a.shape
    n = w.shape[1]
    return pl.pallas_call(
        _matmul_res_body,
        out_shape=jax.ShapeDtypeStruct((t, n), F32),
        grid=(t // tm, n // tn),
        in_specs=[pl.BlockSpec((tm, k), lambda i, j: (i, 0)),
                  pl.BlockSpec((k, tn), lambda i, j: (0, j)),
                  pl.BlockSpec((tm, tn), lambda i, j: (i, j))],
        out_specs=pl.BlockSpec((tm, tn), lambda i, j: (i, j)),
        compiler_params=_params("parallel", "arbitrary"),
        name="matmul_residual",
    )(a, w, res)


def _mlp_body(x_ref, g_ref, w1_ref, w2_ref, o_ref, hid_ref, *, tf):
    x = x_ref[...]
    h = _rms(x, g_ref[...]).astype(BF16)
    for f in range(w1_ref.shape[1] // tf):
        cols = slice(f * tf, (f + 1) * tf)
        hid = jnp.dot(h, w1_ref[:, cols], preferred_element_type=F32)
        hid_ref[:, cols] = jnp.square(jnp.maximum(hid, 0.0)).astype(BF16)
    o_ref[...] = x + jnp.dot(hid_ref[...], w2_ref[...], preferred_element_type=F32)


def mlp_block(x, gain, w1, w2, tf=512, tm=ROW_TILE):
    t, d = x.shape
    ff = w1.shape[1]
    resident = lambda shape: pl.BlockSpec(shape, lambda i: (0, 0), pipeline_mode=pl.Buffered(1))
    return pl.pallas_call(
        functools.partial(_mlp_body, tf=tf),
        out_shape=jax.ShapeDtypeStruct((t, d), F32),
        grid=(t // tm,),
        in_specs=[pl.BlockSpec((tm, d), lambda i: (i, 0)),
                  pl.BlockSpec((1, d), lambda i: (0, 0)),
                  resident((d, ff)), resident((ff, d))],
        out_specs=pl.BlockSpec((tm, d), lambda i: (i, 0)),
        scratch_shapes=[pltpu.VMEM((tm, ff), BF16)],
        compiler_params=_params("parallel"),
        name="mlp_block",
    )(x, gain.reshape(1, d), w1, w2)


def _mem_kv_body(m_ref, g_ref, w_ref, kg_ref, k_ref, v_ref):
    h = _rms(m_ref[0], g_ref[0]).astype(BF16)
    kv = jnp.dot(h, w_ref[0], preferred_element_type=F32)
    for hd in range(X_HEADS):
        sl = slice(hd * X_DH, (hd + 1) * X_DH)
        k_ref[0, 0, :, sl] = _rms(kv[:, sl], kg_ref[0]).astype(BF16)
    v_ref[0, 0] = kv[:, D_MODEL:].astype(BF16)


def mem_kv(mem, norm_mem, w_kv, k_norm):
    bm, nm, d = mem.shape
    out = jax.ShapeDtypeStruct((DEPTH, bm, nm, d), BF16)
    return pl.pallas_call(
        _mem_kv_body,
        out_shape=(out, out),
        grid=(DEPTH, bm),
        in_specs=[pl.BlockSpec((1, nm, d), lambda l, b: (b, 0, 0)),
                  pl.BlockSpec((1, 1, d), lambda l, b: (l, 0, 0)),
                  pl.BlockSpec((1, d, 2 * d), lambda l, b: (l, 0, 0)),
                  pl.BlockSpec((1, 1, X_DH), lambda l, b: (l, 0, 0))],
        out_specs=(pl.BlockSpec((1, 1, nm, d), lambda l, b: (l, b, 0, 0)),
                   pl.BlockSpec((1, 1, nm, d), lambda l, b: (l, b, 0, 0))),
        compiler_params=_params("arbitrary", "arbitrary"),
        name="mem_kv",
    )(mem, norm_mem.reshape(DEPTH, 1, d), w_kv, k_norm.reshape(DEPTH, 1, X_DH))


def _cross_body(x_ref, g_ref, wq_ref, qg_ref, k_ref, v_ref, wo_ref, o_ref):
    x = x_ref[...]
    h = _rms(x, g_ref[...]).astype(BF16)
    q = jnp.dot(h, wq_ref[...], preferred_element_type=F32)
    heads = []
    for hd in range(X_HEADS):
        sl = slice(hd * X_DH, (hd + 1) * X_DH)
        qn = _rms(q[:, sl], qg_ref[...]).astype(BF16)
        s = lax.dot_general(qn, k_ref[0, :, sl], NT_DIMS, preferred_element_type=F32)
        p = jnp.exp(s - jnp.max(s, axis=-1, keepdims=True))
        l = jnp.sum(p, axis=-1, keepdims=True)
        oh = jnp.dot(p.astype(BF16), v_ref[0, :, sl], preferred_element_type=F32) / l
        heads.append(oh.astype(BF16))
    o = jnp.concatenate(heads, axis=-1)
    o_ref[...] = x + jnp.dot(o, wo_ref[...], preferred_element_type=F32)


def cross_block(x, seq_len, gain, w_q, q_gain, kn, v, w_o, tm=ROW_TILE):
    t, d = x.shape
    nm = kn.shape[1]
    per_seq = seq_len // tm
    return pl.pallas_call(
        _cross_body,
        out_shape=jax.ShapeDtypeStruct((t, d), F32),
        grid=(t // tm,),
        in_specs=[pl.BlockSpec((tm, d), lambda i: (i, 0)),
                  pl.BlockSpec((1, d), lambda i: (0, 0)),
                  pl.BlockSpec((d, d), lambda i: (0, 0)),
                  pl.BlockSpec((1, X_DH), lambda i: (0, 0)),
                  pl.BlockSpec((1, nm, d), lambda i: (i // per_seq, 0, 0)),
                  pl.BlockSpec((1, nm, d), lambda i: (i // per_seq, 0, 0)),
                  pl.BlockSpec((d, d), lambda i: (0, 0))],
        out_specs=pl.BlockSpec((tm, d), lambda i: (i, 0)),
        compiler_params=_params("parallel"),
        name="cross_block",
    )(x, gain.reshape(1, d), w_q, (q_gain * X_DH ** -0.5).reshape(1, X_DH), kn, v, w_o)


LANES = 128


def _toeplitz_body(k_ref, o_ref):
    lc, gs = S5_CHUNK, S5_GROUP
    kern = k_ref[0]
    n = kern.shape[1]
    per_tile = LANES // gs
    for r in range(per_tile):
        shifted = pltpu.roll(kern, n - gs * r, 1) if r else kern
        for a in range(lc // per_tile):
            s = lc - 1 - (per_tile * a + r)
            o_ref[0, s * gs:(s + 1) * gs, :] = shifted[:, LANES * a:LANES * a + lc * gs].astype(o_ref.dtype)


def toeplitz_expand(kern_flat):
    g, gs, n = kern_flat.shape
    width = S5_CHUNK * S5_GROUP
    return pl.pallas_call(
        _toeplitz_body,
        out_shape=jax.ShapeDtypeStruct((g, width, width), BF16),
        grid=(g,),
        in_specs=[pl.BlockSpec((1, gs, n), lambda i: (i, 0, 0))],
        out_specs=pl.BlockSpec((1, width, width), lambda i: (i, 0, 0)),
        compiler_params=_params("parallel"),
        name="toeplitz_expand",
    )(kern_flat)


def s5_operators(lam_re, lam_im, log_step, b_re, b_im, c_re, c_im, d):
    lc = S5_CHUNK
    lam = lax.complex(lam_re.astype(F32), lam_im.astype(F32))
    step = jnp.exp(log_step.astype(F32))[..., None]
    lam_bar = jnp.exp(lam * step)
    b_bar = ((lam_bar - 1.0) / lam)[..., None] * lax.complex(b_re.astype(F32), b_im.astype(F32))
    c = lax.complex(c_re.astype(F32), c_im.astype(F32))
    pw = jnp.cumprod(jnp.broadcast_to(lam_bar[..., None], lam_bar.shape + (lc,)), axis=-1)
    pw = jnp.concatenate([jnp.ones_like(pw[..., :1]), pw], axis=-1)
    kern = jnp.einsum('zgcp,zgpt,zgpd->zgtcd', c, pw[..., :lc], b_bar, precision=HIGHEST).real
    kf, kb = kern[0], kern[1]
    k0 = kf[:, :1] + kb[:, :1] + (d.astype(F32)[:, :, None] * jnp.eye(S5_GROUP, dtype=F32))[:, None]
    kern_full = jnp.concatenate([kb[:, :0:-1], k0, kf[:, 1:]], axis=1)
    kern_flat = kern_full.transpose(0, 3, 1, 2).reshape(S5_GROUPS, S5_GROUP, (2 * lc - 1) * S5_GROUP)
    toep = toeplitz_expand(jnp.pad(kern_flat, ((0, 0), (0, 0), (0, S5_GROUP))))

    pf = jnp.einsum('gps,gpd->gsdp', pw[0][..., lc - 1::-1], b_bar[0])
    pb = jnp.einsum('gps,gpd->gsdp', pw[1][..., :lc], b_bar[1])
    p_op = jnp.concatenate([pf.real, pb.real, pf.imag, pb.imag], axis=-1)
    p_op = p_op.reshape(S5_GROUPS, lc * S5_GROUP, 4 * S5_STATE)

    qf = jnp.einsum('gcp,gpt->gptc', c[0], pw[0][..., 1:])
    qb = jnp.einsum('gcp,gpt->gptc', c[1], pw[1][..., :0:-1])
    q_op = jnp.concatenate([qf.real, qb.real, -qf.imag, -qb.imag], axis=1)
    q_op = q_op.reshape(S5_GROUPS, 4 * S5_STATE, lc * S5_GROUP)

    a = pw[..., lc]
    coef = jnp.stack([jnp.concatenate([a[0].real, a[1].real], -1),
                      jnp.concatenate([a[0].imag, a[1].imag], -1)], axis=1)
    return toep, _split_bf16(p_op), _split_bf16(q_op), coef


def _split_bf16(x):
    hi = x.astype(BF16)
    lo = (x - hi.astype(F32)).astype(BF16)
    return jnp.stack([hi, lo], axis=1)


SUBLANES = 8


def _s5_group_body(u_ref, t_ref, p_ref, q_ref, c_ref, y_ref,
                   v_re_ref, v_im_ref, xf_re_ref, xf_im_ref, xb_re_ref, xb_im_ref, *, n_chunks, bsz):
    u = u_ref[0]
    half = 2 * S5_STATE
    v = (jnp.dot(u, p_ref[0, 0], preferred_element_type=F32)
         + jnp.dot(u, p_ref[0, 1], preferred_element_type=F32))
    v_re_ref[...] = v[:, :half]
    v_im_ref[...] = v[:, half:]
    coef = c_ref[0]
    a_re, a_im = coef[0:1], coef[1:2]
    fwd = lax.broadcasted_iota(jnp.int32, (bsz, half), 1) < S5_STATE

    def step(i, carry):
        re, im = carry
        rows_f = pl.ds(i, bsz, stride=n_chunks)
        rows_b = pl.ds(n_chunks - 1 - i, bsz, stride=n_chunks)
        xf_re_ref[rows_f, :] = re
        xf_im_ref[rows_f, :] = im
        xb_re_ref[rows_b, :] = re
        xb_im_ref[rows_b, :] = im
        re_next = a_re * re - a_im * im + jnp.where(fwd, v_re_ref[rows_f, :], v_re_ref[rows_b, :])
        im_next = a_re * im + a_im * re + jnp.where(fwd, v_im_ref[rows_f, :], v_im_ref[rows_b, :])
        return re_next, im_next

    zero = jnp.zeros((bsz, half), F32)
    lax.fori_loop(0, n_chunks, step, (zero, zero))
    is_fwd = lax.broadcasted_iota(jnp.int32, xf_re_ref.shape, 1) < S5_STATE
    x = jnp.concatenate([jnp.where(is_fwd, xf_re_ref[...], xb_re_ref[...]),
                         jnp.where(is_fwd, xf_im_ref[...], xb_im_ref[...])], axis=1)
    x_hi = x.astype(BF16)
    x_lo = (x - x_hi.astype(F32)).astype(BF16)
    y = (jnp.dot(u, t_ref[0], preferred_element_type=F32)
         + jnp.dot(x_hi, q_ref[0, 0], preferred_element_type=F32)
         + jnp.dot(x_lo, q_ref[0, 0], preferred_element_type=F32)
         + jnp.dot(x_hi, q_ref[0, 1], preferred_element_type=F32))
    y_ref[0] = y.astype(y_ref.dtype)


def s5_scan(u, bsz, seq_len, ops):
    toep, p_op, q_op, coef = ops
    lc, g, w = S5_CHUNK, S5_GROUPS, S5_CHUNK * S5_GROUP
    n = seq_len // lc
    c = bsz * n
    ns = 4 * S5_STATE
    ug = u.reshape(bsz, n, lc, g, S5_GROUP).transpose(3, 0, 1, 2, 4).reshape(g, c, w)
    group = lambda *shape: pl.BlockSpec((1,) + shape, lambda i: (i,) + (0,) * len(shape))
    y = pl.pallas_call(
        functools.partial(_s5_group_body, n_chunks=n, bsz=bsz),
        out_shape=jax.ShapeDtypeStruct((g, c, w), BF16),
        grid=(g,),
        in_specs=[group(c, w), group(w, w), group(2, w, ns), group(2, ns, w), group(2, ns // 2)],
        out_specs=group(c, w),
        scratch_shapes=[pltpu.VMEM((c, ns // 2), F32)] * 6,
        compiler_params=_params("parallel"),
        name="s5_group",
    )(ug, toep, p_op, q_op, coef)
    return y.reshape(g, bsz, n, lc, S5_GROUP).transpose(1, 2, 3, 0, 4).reshape(bsz * seq_len, S5_WIDTH)


def _gla_direction(q_ref, k_ref, v_ref, g_ref, wg, bg, s_ref, o_ref, forward, chunks):
    cs = GLA_CHUNK
    tb = chunks * cs
    glr = g_ref[...]
    logit = (jnp.dot(glr, wg[0], preferred_element_type=F32)
             + jnp.dot(glr, wg[1], preferred_element_type=F32) + bg)
    g = (jnp.minimum(logit, 0.0) - jnp.log1p(jnp.exp(-jnp.abs(logit)))) / GLA_TAU
    row = lax.broadcasted_iota(jnp.int32, (tb, tb), 0)
    col = lax.broadcasted_iota(jnp.int32, (tb, tb), 1)
    same_chunk = (row // cs) == (col // cs)
    within = same_chunk & ((col <= row) if forward else (col >= row))
    ones = jnp.where(within, 1.0, 0.0).astype(BF16)
    bcum = jnp.zeros_like(g)
    rest = g
    for _ in range(3):
        term = rest.astype(BF16)
        bcum = bcum + jnp.dot(ones, term, preferred_element_type=F32)
        rest = rest - term.astype(F32)

    srow = lax.broadcasted_iota(jnp.int32, (GLA_HEADS * cs, cs), 0) & (cs - 1)
    scol = lax.broadcasted_iota(jnp.int32, (GLA_HEADS * cs, cs), 1)
    keep = (scol <= srow) if forward else (scol >= srow)
    lane = lax.broadcasted_iota(jnp.int32, (1, GLA_QK), 1)
    head_lanes = [((lane >= h * GLA_DK) & (lane < (h + 1) * GLA_DK)).astype(F32) for h in range(GLA_HEADS)]
    state_mask = (lax.broadcasted_iota(jnp.int32, (GLA_V, GLA_QK), 0) // GLA_DV
                  == lax.broadcasted_iota(jnp.int32, (GLA_V, GLA_QK), 1) // GLA_DK)

    i_ref = cs // 2 - 1 if forward else cs // 2
    i_last = cs - 1 if forward else 0
    scale = GLA_DK ** -0.5
    for ci in (range(chunks) if forward else reversed(range(chunks))):
        rows = slice(ci * cs, (ci + 1) * cs)
        b = bcum[rows]
        bref = b[i_ref:i_ref + 1]
        blast = b[i_last:i_last + 1]
        q = q_ref[rows, :].astype(F32) * scale
        k = k_ref[rows, :].astype(F32)
        v = v_ref[rows, :]
        q_rel = q * jnp.exp(b - bref)
        k_rel = (k * jnp.exp(bref - b)).astype(BF16)
        k_out = (k * jnp.exp(blast - b)).astype(BF16)
        q_dec = (q * jnp.exp(b)).astype(BF16)
        decay = jnp.exp(blast)
        q_heads = jnp.concatenate([q_rel * hm for hm in head_lanes], axis=0).astype(BF16)
        s = lax.dot_general(q_heads, k_rel, NT_DIMS, preferred_element_type=F32)
        s = jnp.where(keep, s, 0.0).astype(BF16)
        st = s_ref[...]
        o_inter = lax.dot_general(q_dec, st.astype(BF16), NT_DIMS, preferred_element_type=F32)
        o_intra = [jnp.dot(s[h * cs:(h + 1) * cs], v[:, h * GLA_DV:(h + 1) * GLA_DV],
                           preferred_element_type=F32) for h in range(GLA_HEADS)]
        o_ref[rows, :] = o_inter + jnp.concatenate(o_intra, axis=-1)
        kv = lax.dot_general(v, k_out, TN_DIMS, preferred_element_type=F32)
        s_ref[...] = decay * st + jnp.where(state_mask, kv, 0.0)


def _gla_body(qf_ref, kf_ref, vf_ref, gf_ref, qb_ref, kb_ref, vb_ref, gb_ref, wg_ref, bg_ref,
              of_ref, ob_ref, sf_ref, sb_ref, *, chunks):
    @pl.when(pl.program_id(1) == 0)
    def _():
        sf_ref[...] = jnp.zeros_like(sf_ref)
        sb_ref[...] = jnp.zeros_like(sb_ref)

    _gla_direction(qf_ref, kf_ref, vf_ref, gf_ref, wg_ref[0], bg_ref[0], sf_ref, of_ref, True, chunks)
    _gla_direction(qb_ref, kb_ref, vb_ref, gb_ref, wg_ref[1], bg_ref[1], sb_ref, ob_ref, False, chunks)


def gla_scan(proj, bsz, seq_len, w_gate, b_gate, chunks=4):
    t = proj.shape[0]
    tb = chunks * GLA_CHUNK
    nb = seq_len // tb
    fwd = lambda cb: (lambda b, i: (b * nb + i, cb))
    bwd = lambda cb: (lambda b, i: (b * nb + nb - 1 - i, cb))
    qc, kc, vc, gc = 512 // GLA_QK, 768 // GLA_QK, 1024 // GLA_V, 2048 // 128
    wg = jnp.zeros((2, 128, GLA_QK), F32)
    wg = wg.at[0, :GLA_RANK].set(w_gate[0].astype(F32)).at[1, GLA_RANK:2 * GLA_RANK].set(w_gate[1].astype(F32))
    wg = _split_bf16(wg)
    out = jax.ShapeDtypeStruct((t, GLA_V), F32)
    state = pltpu.VMEM((GLA_V, GLA_QK), F32)
    return pl.pallas_call(
        functools.partial(_gla_body, chunks=chunks),
        out_shape=(out, out),
        grid=(bsz, nb),
        in_specs=[pl.BlockSpec((tb, GLA_QK), fwd(qc)), pl.BlockSpec((tb, GLA_QK), fwd(kc)),
                  pl.BlockSpec((tb, GLA_V), fwd(vc)), pl.BlockSpec((tb, 128), fwd(gc)),
                  pl.BlockSpec((tb, GLA_QK), bwd(qc)), pl.BlockSpec((tb, GLA_QK), bwd(kc)),
                  pl.BlockSpec((tb, GLA_V), bwd(vc)), pl.BlockSpec((tb, 128), bwd(gc)),
                  pl.BlockSpec((2, 2, 128, GLA_QK), lambda b, i: (0, 0, 0, 0)),
                  pl.BlockSpec((2, 1, GLA_QK), lambda b, i: (0, 0, 0))],
        out_specs=(pl.BlockSpec((tb, GLA_V), fwd(0)), pl.BlockSpec((tb, GLA_V), bwd(0))),
        scratch_shapes=[state, state],
        compiler_params=_params("parallel", "arbitrary"),
        name="gla_scan",
    )(proj, proj, proj, proj, proj, proj, proj, proj, wg, b_gate.astype(F32).reshape(2, 1, GLA_QK))


def _even_out_body(x_ref, ys_ref, of_ref, ob_ref, og_ref, wglu_ref, gn_ref, wtop_ref, wbot_ref, o_ref):
    y = ys_ref[...].astype(F32)
    y = 0.5 * y * (1.0 + jnp.tanh(math.sqrt(2.0 / math.pi) * (y + 0.044715 * (y * y * y))))
    gate = jnp.dot(y.astype(BF16), wglu_ref[...], preferred_element_type=F32)
    y = y * _sigmoid(gate)
    o = of_ref[...] + ob_ref[...]
    og = og_ref[...].astype(F32)
    heads = []
    for h in range(GLA_HEADS):
        sl = slice(h * GLA_DV, (h + 1) * GLA_DV)
        heads.append(_rms(o[:, sl], gn_ref[...]))
    o = jnp.concatenate(heads, axis=-1) * (og * _sigmoid(og))
    o_ref[...] = (x_ref[...]
                  + jnp.dot(y.astype(BF16), wtop_ref[...], preferred_element_type=F32)
                  + jnp.dot(o.astype(BF16), wbot_ref[...], preferred_element_type=F32))


def even_out(x, ys, o_f, o_b, proj, w_glu, gla_norm, w_out, tm=ROW_TILE):
    t, d = x.shape
    row = lambda i: (i, 0)
    const = lambda i: (0, 0)
    return pl.pallas_call(
        _even_out_body,
        out_shape=jax.ShapeDtypeStruct((t, d), F32),
        grid=(t // tm,),
        in_specs=[pl.BlockSpec((tm, d), row), pl.BlockSpec((tm, S5_WIDTH), row),
                  pl.BlockSpec((tm, GLA_V), row), pl.BlockSpec((tm, GLA_V), row),
                  pl.BlockSpec((tm, GLA_V), lambda i: (i, 1536 // GLA_V)),
                  pl.BlockSpec((S5_WIDTH, S5_WIDTH), const), pl.BlockSpec((1, GLA_DV), const),
                  pl.BlockSpec((S5_WIDTH, d), const), pl.BlockSpec((GLA_V, d), const)],
        out_specs=pl.BlockSpec((tm, d), row),
        compiler_params=_params("parallel"),
        name="even_out",
    )(x, ys, o_f, o_b, proj, w_glu, gla_norm.astype(F32).reshape(1, GLA_DV),
      w_out[:S5_WIDTH], w_out[S5_WIDTH:])


QK_NORM_WIDTH = 256


def _seg_rms(a, gain, same_seg):
    ssq = jnp.dot((a * a).astype(BF16), same_seg, preferred_element_type=F32)
    return a * lax.rsqrt(ssq * (1.0 / DIFF_DK) + EPS) * gain


def _qkv_body(x_ref, g_ref, w_ref, qk_gain_ref, o_ref):
    d = x_ref.shape[1]
    h = _rms(x_ref[...], g_ref[...]).astype(BF16)
    w = QK_NORM_WIDTH
    same_seg = jnp.where(lax.broadcasted_iota(jnp.int32, (w, w), 0) // DIFF_DK
                         == lax.broadcasted_iota(jnp.int32, (w, w), 1) // DIFF_DK, 1.0, 0.0).astype(BF16)
    for part in range(2):
        acc = jnp.dot(h, w_ref[:, part * d:(part + 1) * d], preferred_element_type=F32)
        gain = qk_gain_ref[part]
        for c in range(d // w):
            sl = slice(c * w, (c + 1) * w)
            o_ref[:, part * d + c * w:part * d + (c + 1) * w] = (
                _seg_rms(acc[:, sl], gain, same_seg).astype(o_ref.dtype))
    o_ref[:, 2 * d:] = jnp.dot(h, w_ref[:, 2 * d:], preferred_element_type=F32).astype(o_ref.dtype)


def qkv_project(x, gain, w, q_norm, k_norm, tm=ROW_TILE):
    t, d = x.shape
    reps = QK_NORM_WIDTH // DIFF_DK
    qg = jnp.tile(q_norm.astype(F32) * DIFF_DK ** -0.5, reps)
    kg = jnp.tile(k_norm.astype(F32), reps)
    qk_gain = jnp.stack([qg, kg]).reshape(2, 1, QK_NORM_WIDTH)
    return pl.pallas_call(
        _qkv_body,
        out_shape=jax.ShapeDtypeStruct((t, 3 * d), BF16),
        grid=(t // tm,),
        in_specs=[pl.BlockSpec((tm, d), lambda i: (i, 0)),
                  pl.BlockSpec((1, d), lambda i: (0, 0)),
                  pl.BlockSpec((d, 3 * d), lambda i: (0, 0)),
                  pl.BlockSpec((2, 1, QK_NORM_WIDTH), lambda i: (0, 0, 0))],
        out_specs=pl.BlockSpec((tm, 3 * d), lambda i: (i, 0)),
        compiler_params=_params("parallel"),
        name="qkv_project",
    )(x, gain.reshape(1, d), w, qk_gain)


POS_SPLIT = 16
POS_SHIFT = 4
AUG_LANE = DIFF_DK
SOFTMAX_ROWS = 16
ROW_PARTS = 2


def _pos_terms(shape):
    lane = lax.broadcasted_iota(jnp.int32, shape, 1)
    pos = lax.broadcasted_iota(jnp.int32, shape, 0)
    hi = lax.shift_right_logical(pos, POS_SHIFT).astype(F32)
    lo = (pos & (POS_SPLIT - 1)).astype(F32)
    return lane, hi, lo


def _lane_select(lane, first, values):
    out = 0.0
    for n, val in reversed(list(enumerate(values))):
        out = jnp.where(lane == first + n, val, out)
    return out


def _diff_attn_body(slope_ref, q_ref, k_ref, v_ref, lq1_ref, lk1_ref, lq2_ref, lk2_ref, sub_ref,
                    o_ref, ka_ref, va_ref, qs_ref, s0_ref, s1_ref, p0_ref, p1_ref, vs_ref,
                    *, blk, seq_len, lambda_init):
    h = pl.program_id(1)
    qi = pl.program_id(2)
    slope = slope_ref[h]
    nk = seq_len // blk
    s_refs, p_refs = (s0_ref, s1_ref), (p0_ref, p1_ref)

    @pl.when(qi == 0)
    def _():
        va_ref[:, :DIFF_DV] = v_ref[...]
        va_ref[:, DIFF_DV:] = jnp.ones((seq_len, DIFF_DV), BF16)

        def build(t, carry):
            rows = pl.ds(pl.multiple_of(t * blk, blk), blk)
            kf = k_ref[rows, :].astype(F32)
            lane, hi, lo = _pos_terms(kf.shape)
            aug = _lane_select(lane, AUG_LANE,
                               [-blk * slope, -POS_SPLIT * slope, -slope,
                                (blk * slope) * jnp.asarray(t, F32), (POS_SPLIT * slope) * hi, slope * lo])
            for z, kz in enumerate((kf, pltpu.roll(kf, DIFF_DK, 1))):
                ka_ref[z, rows, :] = jnp.where(lane < DIFF_DK, kz, aug).astype(BF16)
            return carry

        lax.fori_loop(0, nk, build, 0)

    qf = q_ref[...].astype(F32)
    lane, hi, lo = _pos_terms(qf.shape)
    qa = _lane_select(lane, AUG_LANE, [jnp.asarray(qi, F32), hi, lo, 1.0, 1.0, 1.0])
    for z, qz in enumerate((qf, pltpu.roll(qf, DIFF_DK, 1))):
        qs_ref[z] = jnp.where(lane < DIFF_DK, qz, 0.0).astype(BF16)
        qs_ref[2 + z] = jnp.where(lane < DIFF_DK, qz, qa).astype(BF16)
        qs_ref[4 + z] = jnp.where(lane < DIFF_DK, qz, -qa).astype(BF16)

    def key_block(t):
        if t == 0:
            return qi, 0
        j = (t - 1) + jnp.asarray(qi <= t - 1, jnp.int32)
        return j, jnp.where(j < qi, 2, 4)

    def key_rows(t):
        j, _ = key_block(t)
        return pl.ds(pl.multiple_of(j * blk, blk), blk)

    part = blk // ROW_PARTS
    parts = [slice(r * part, (r + 1) * part) for r in range(ROW_PARTS)]
    for z, s_ref in enumerate(s_refs):
        for rows in parts:
            for t in range(nk):
                _, variant = key_block(t)
                s_ref[rows, t * blk:(t + 1) * blk] = lax.dot_general(
                    qs_ref[variant + z, rows, :], ka_ref[z, key_rows(t), :], NT_DIMS,
                    preferred_element_type=F32)

    rg = SOFTMAX_ROWS
    rel = (lax.broadcasted_iota(jnp.int32, (rg, blk), 0) - lax.broadcasted_iota(jnp.int32, (rg, blk), 1))
    for t in range(nk):
        vs_ref[t * blk:(t + 1) * blk, :] = va_ref[key_rows(t), :]

    acc = []
    for s_ref, p_ref in zip(s_refs, p_refs):
        for g in range(blk // rg):
            rows = slice(g * rg, (g + 1) * rg)
            s_diag = s_ref[rows, :blk] - slope * jnp.abs(rel + g * rg).astype(F32)
            top = s_diag
            for t in range(1, nk):
                top = jnp.maximum(top, s_ref[rows, t * blk:(t + 1) * blk])
            m = jnp.max(top, axis=-1, keepdims=True)
            p_ref[rows, :blk] = jnp.exp(s_diag - m).astype(BF16)
            for t in range(1, nk):
                cols = slice(t * blk, (t + 1) * blk)
                p_ref[rows, cols] = jnp.exp(s_ref[rows, cols] - m).astype(BF16)
        acc.append([jnp.dot(p_ref[rows, :], vs_ref[...], preferred_element_type=F32) for rows in parts])

    lam = (jnp.exp(jnp.sum(lq1_ref[...] * lk1_ref[...], axis=-1, keepdims=True))
           - jnp.exp(jnp.sum(lq2_ref[...] * lk2_ref[...], axis=-1, keepdims=True)) + lambda_init)
    for rows, a0, a1 in zip(parts, *acc):
        o = (a0[:, :DIFF_DV] / a0[:, DIFF_DV:DIFF_DV + 1]
             - lam * (a1[:, :DIFF_DV] / a1[:, DIFF_DV:DIFF_DV + 1]))
        o_ref[rows, :] = (_rms(o, sub_ref[...]) * (1.0 - lambda_init)).astype(o_ref.dtype)


def diff_attention(qkv, bsz, seq_len, lq1, lk1, lq2, lk2, sub_norm, lambda_init, blk=512):
    t = qkv.shape[0]
    nq = seq_len // blk
    assert blk // POS_SPLIT <= 256, "hi part of a block position must stay exact in bf16"
    slopes = jnp.asarray(2.0 ** (-8.0 * np.arange(1, DIFF_HEADS + 1, dtype=np.float32) / DIFF_HEADS), F32)
    vec = lambda a: a.astype(F32).reshape(1, DIFF_DK)
    const = lambda b, h, i: (0, 0)
    return pl.pallas_call(
        functools.partial(_diff_attn_body, blk=blk, seq_len=seq_len, lambda_init=lambda_init),
        out_shape=jax.ShapeDtypeStruct((t, DIFF_HEADS * DIFF_DV), BF16),
        grid=(bsz, DIFF_HEADS, nq),
        in_specs=[pl.BlockSpec(memory_space=pltpu.SMEM),
                  pl.BlockSpec((blk, 2 * DIFF_DK), lambda b, h, i: (b * nq + i, h)),
                  pl.BlockSpec((seq_len, 2 * DIFF_DK), lambda b, h, i: (b, DIFF_HEADS + h)),
                  pl.BlockSpec((seq_len, DIFF_DV), lambda b, h, i: (b, 2 * DIFF_HEADS + h)),
                  pl.BlockSpec((1, DIFF_DK), const), pl.BlockSpec((1, DIFF_DK), const),
                  pl.BlockSpec((1, DIFF_DK), const), pl.BlockSpec((1, DIFF_DK), const),
                  pl.BlockSpec((1, DIFF_DV), const)],
        out_specs=pl.BlockSpec((blk, DIFF_DV), lambda b, h, i: (b * nq + i, h)),
        scratch_shapes=[pltpu.VMEM((2, seq_len, 2 * DIFF_DK), BF16),
                        pltpu.VMEM((seq_len, 2 * DIFF_DV), BF16),
                        pltpu.VMEM((6, blk, 2 * DIFF_DK), BF16),
                        pltpu.VMEM((blk, seq_len), F32),
                        pltpu.VMEM((blk, seq_len), F32),
                        pltpu.VMEM((blk, seq_len), BF16),
                        pltpu.VMEM((blk, seq_len), BF16),
                        pltpu.VMEM((seq_len, 2 * DIFF_DV), BF16)],
        compiler_params=_params("parallel", "parallel", "arbitrary"),
        name="diff_attention",
    )(slopes, qkv, qkv, qkv, vec(lq1), vec(lk1), vec(lq2), vec(lk2),
      sub_norm.astype(F32).reshape(1, DIFF_DV))


def _pad_ev_w_in(w):
    return jnp.pad(w, ((0, 0), (0, EV_PAD_COLS - w.shape[1]))).astype(BF16)


def _trunk(x3, mem, p):
    bsz, seq_len, d = x3.shape
    x = x3.reshape(bsz * seq_len, d)
    kn_all, v_all = mem_kv(mem, p['norm_mem'], p['x_w_kv'], p['x_k_norm'])
    for layer in range(DEPTH):
        if layer % 2 == 0:
            e = layer // 2
            proj = norm_matmul(x, p['norm_mix'][layer], p['ev_w_in'][e], BF16, EV_TN)
            ys = s5_scan(proj[:, :S5_WIDTH], bsz, seq_len, p['s5_ops'][e])
            o_f, o_b = gla_scan(proj, bsz, seq_len, p['gla_w_gate'][e], p['gla_b_gate'][e])
            x = even_out(x, ys, o_f, o_b, proj, p['s5_w_glu'][e], p['gla_norm'][e], p['ev_w_out'][e])
        else:
            o = layer // 2
            lambda_init = 0.8 - 0.6 * math.exp(-0.3 * layer)
            qkv = qkv_project(x, p['norm_mix'][layer], p['od_w_in'][o],
                              p['diff_q_norm'][o], p['diff_k_norm'][o])
            att = diff_attention(qkv, bsz, seq_len, p['diff_lambda_q1'][o], p['diff_lambda_k1'][o],
                                 p['diff_lambda_q2'][o], p['diff_lambda_k2'][o], p['diff_norm'][o],
                                 lambda_init)
            x = matmul_residual(att, p['od_w_out'][o], x, tn=d)
        x = cross_block(x, seq_len, p['norm_cross'][layer], p['x_w_q'][layer], p['x_q_norm'][layer],
                        kn_all[layer], v_all[layer], p['x_w_o'][layer])
        x = mlp_block(x, p['norm_mlp'][layer], p['mlp_w1'][layer], p['mlp_w2'][layer])
    return x.reshape(bsz, seq_len, d)


def kernel(x_prompt, x_sample, mem_prompt, mem_sample, norm_mix, norm_cross, norm_mem, norm_mlp,
           ev_w_in, ev_w_out, s5_lambda_re, s5_lambda_im, s5_log_step, s5_b_re, s5_b_im,
           s5_c_re, s5_c_im, s5_d, s5_w_glu, gla_w_gate, gla_b_gate, gla_norm,
           od_w_in, od_w_out, diff_q_norm, diff_k_norm, diff_lambda_q1, diff_lambda_k1,
           diff_lambda_q2, diff_lambda_k2, diff_norm, x_w_q, x_w_kv, x_w_o, x_q_norm, x_k_norm,
           mlp_w1, mlp_w2):
    n_even = ev_w_in.shape[0]
    bf = lambda w: w.astype(BF16)
    p = dict(
        norm_mix=norm_mix, norm_cross=norm_cross, norm_mem=norm_mem, norm_mlp=norm_mlp,
        ev_w_in=[_pad_ev_w_in(ev_w_in[e]) for e in range(n_even)], ev_w_out=bf(ev_w_out),
        s5_ops=[s5_operators(s5_lambda_re[e], s5_lambda_im[e], s5_log_step[e], s5_b_re[e], s5_b_im[e],
                             s5_c_re[e], s5_c_im[e], s5_d[e]) for e in range(n_even)],
        s5_w_glu=bf(s5_w_glu), gla_w_gate=gla_w_gate, gla_b_gate=gla_b_gate, gla_norm=gla_norm,
        od_w_in=bf(od_w_in), od_w_out=bf(od_w_out), diff_q_norm=diff_q_norm, diff_k_norm=diff_k_norm,
        diff_lambda_q1=diff_lambda_q1, diff_lambda_k1=diff_lambda_k1,
        diff_lambda_q2=diff_lambda_q2, diff_lambda_k2=diff_lambda_k2, diff_norm=diff_norm,
        x_w_q=bf(x_w_q), x_w_kv=bf(x_w_kv), x_w_o=bf(x_w_o), x_q_norm=x_q_norm, x_k_norm=x_k_norm,
        mlp_w1=bf(mlp_w1), mlp_w2=bf(mlp_w2))
    return (_trunk(x_prompt, mem_prompt, p), _trunk(x_sample, mem_sample, p))
```

```python
import functools
import math

import numpy as np
import jax
import jax.numpy as jnp
from jax import lax
from jax.experimental import pallas as pl
from jax.experimental.pallas import tpu as pltpu

F32 = jnp.float32
BF16 = jnp.bfloat16
HIGHEST = lax.Precision.HIGHEST

D_MODEL = 1024
DEPTH = 4
EPS = 1e-6
S5_WIDTH = 512
S5_GROUP = 16
S5_GROUPS = 32
S5_STATE = 64
S5_CHUNK = 64
GLA_HEADS = 4
GLA_DV = 128
GLA_DK = 64
GLA_RANK = 16
GLA_TAU = 16.0
GLA_CHUNK = 64
GLA_QK = GLA_HEADS * GLA_DK
GLA_V = GLA_HEADS * GLA_DV
EV_PAD_COLS = 2304
EV_TN = 768
DIFF_HEADS = 8
DIFF_DK = 64
DIFF_DV = 128
X_HEADS = 4
X_DH = 256
D_FF = 4096

ROW_TILE = 512
VMEM_LIMIT = 48 * 1024 * 1024

NT_DIMS = (((1,), (1,)), ((), ()))
TN_DIMS = (((0,), (0,)), ((), ()))


def _params(*sem, flags=None):
    return pltpu.CompilerParams(dimension_semantics=sem, vmem_limit_bytes=VMEM_LIMIT, flags=flags)


def _rms(x, gain):
    ms = jnp.mean(x * x, axis=-1, keepdims=True)
    return x * lax.rsqrt(ms + EPS) * gain


def _sigmoid(x):
    return 1.0 / (1.0 + jnp.exp(-x))


def _norm_matmul_body(x_ref, g_ref, w_ref, o_ref, *, tn):
    h = _rms(x_ref[...], g_ref[...]).astype(BF16)
    for j in range(w_ref.shape[1] // tn):
        cols = slice(j * tn, (j + 1) * tn)
        o_ref[:, cols] = jnp.dot(h, w_ref[:, cols], preferred_element_type=F32).astype(o_ref.dtype)


def norm_matmul(x, gain, w, out_dtype, tn, tm=ROW_TILE):
    t, d = x.shape
    n = w.shape[1]
    return pl.pallas_call(
        functools.partial(_norm_matmul_body, tn=tn),
        out_shape=jax.ShapeDtypeStruct((t, n), out_dtype),
        grid=(t // tm,),
        in_specs=[pl.BlockSpec((tm, d), lambda i: (i, 0)),
                  pl.BlockSpec((1, d), lambda i: (0, 0)),
                  pl.BlockSpec((d, n), lambda i: (0, 0))],
        out_specs=pl.BlockSpec((tm, n), lambda i: (i, 0)),
        compiler_params=_params("parallel"),
        name="norm_matmul",
    )(x, gain.reshape(1, d), w)


def _matmul_res_body(a_ref, w_ref, r_ref, o_ref):
    o_ref[...] = r_ref[...] + jnp.dot(a_ref[...].astype(BF16), w_ref[...], preferred_element_type=F32)


def matmul_residual(a, w, res, tn=512, tm=ROW_TILE):
    t, k = a.shape
    n = w.shape[1]
    return pl.pallas_call(
        _matmul_res_body,
        out_shape=jax.ShapeDtypeStruct((t, n), F32),
        grid=(t // tm, n // tn),
        in_specs=[pl.BlockSpec((tm, k), lambda i, j: (i, 0)),
                  pl.BlockSpec((k, tn), lambda i, j: (0, j)),
                  pl.BlockSpec((tm, tn), lambda i, j: (i, j))],
        out_specs=pl.BlockSpec((tm, tn), lambda i, j: (i, j)),
        compiler_params=_params("parallel", "arbitrary"),
        name="matmul_residual",
    )(a, w, res)


def _mlp_body(x_ref, g_ref, w1_ref, w2_ref, o_ref, hid_ref, *, tf):
    x = x_ref[...]
    h = _rms(x, g_ref[...]).astype(BF16)
    for f in range(w1_ref.shape[1] // tf):
        cols = slice(f * tf, (f + 1) * tf)
        hid = jnp.dot(h, w1_ref[:, cols], preferred_element_type=F32)
        hid_ref[:, cols] = jnp.square(jnp.maximum(hid, 0.0)).astype(BF16)
    o_ref[...] = x + jnp.dot(hid_ref[...], w2_ref[...], preferred_element_type=F32)


def mlp_block(x, gain, w1, w2, tf=512, tm=ROW_TILE):
    t, d = x.shape
    ff = w1.shape[1]
    resident = lambda shape: pl.BlockSpec(shape, lambda i: (0, 0), pipeline_mode=pl.Buffered(1))
    return pl.pallas_call(
        functools.partial(_mlp_body, tf=tf),
        out_shape=jax.ShapeDtypeStruct((t, d), F32),
        grid=(t // tm,),
        in_specs=[pl.BlockSpec((tm, d), lambda i: (i, 0)),
                  pl.BlockSpec((1, d), lambda i: (0, 0)),
                  resident((d, ff)), resident((ff, d))],
        out_specs=pl.BlockSpec((tm, d), lambda i: (i, 0)),
        scratch_shapes=[pltpu.VMEM((tm, ff), BF16)],
        compiler_params=_params("parallel"),
        name="mlp_block",
    )(x, gain.reshape(1, d), w1, w2)


def _mem_kv_body(m_ref, g_ref, w_ref, kg_ref, k_ref, v_ref):
    h = _rms(m_ref[0], g_ref[0]).astype(BF16)
    kv = jnp.dot(h, w_ref[0], preferred_element_type=F32)
    for hd in range(X_HEADS):
        sl = slice(hd * X_DH, (hd + 1) * X_DH)
        k_ref[0, 0, :, sl] = _rms(kv[:, sl], kg_ref[0]).astype(BF16)
    v_ref[0, 0] = kv[:, D_MODEL:].astype(BF16)


def mem_kv(mem, norm_mem, w_kv, k_norm):
    bm, nm, d = mem.shape
    out = jax.ShapeDtypeStruct((DEPTH, bm, nm, d), BF16)
    return pl.pallas_call(
        _mem_kv_body,
        out_shape=(out, out),
        grid=(DEPTH, bm),
        in_specs=[pl.BlockSpec((1, nm, d), lambda l, b: (b, 0, 0)),
                  pl.BlockSpec((1, 1, d), lambda l, b: (l, 0, 0)),
                  pl.BlockSpec((1, d, 2 * d), lambda l, b: (l, 0, 0)),
                  pl.BlockSpec((1, 1, X_DH), lambda l, b: (l, 0, 0))],
        out_specs=(pl.BlockSpec((1, 1, nm, d), lambda l, b: (l, b, 0, 0)),
                   pl.BlockSpec((1, 1, nm, d), lambda l, b: (l, b, 0, 0))),
        compiler_params=_params("arbitrary", "arbitrary"),
        name="mem_kv",
    )(mem, norm_mem.reshape(DEPTH, 1, d), w_kv, k_norm.reshape(DEPTH, 1, X_DH))


def _cross_body(x_ref, g_ref, wq_ref, qg_ref, k_ref, v_ref, wo_ref, o_ref):
    x = x_ref[...]
    h = _rms(x, g_ref[...]).astype(BF16)
    q = jnp.dot(h, wq_ref[...], preferred_element_type=F32)
    heads = []
    for hd in range(X_HEADS):
        sl = slice(hd * X_DH, (hd + 1) * X_DH)
        qn = _rms(q[:, sl], qg_ref[...]).astype(BF16)
        s = lax.dot_general(qn, k_ref[0, 0, :, sl], NT_DIMS, preferred_element_type=F32)
        p = jnp.exp(s - jnp.max(s, axis=-1, keepdims=True))
        l = jnp.sum(p, axis=-1, keepdims=True)
        oh = jnp.dot(p.astype(BF16), v_ref[0, 0, :, sl], preferred_element_type=F32) / l
        heads.append(oh.astype(BF16))
    o = jnp.concatenate(heads, axis=-1)
    o_ref[...] = x + jnp.dot(o, wo_ref[...], preferred_element_type=F32)


def cross_block(x, seq_len, layer, gain, w_q, q_gain, kn, v, w_o, tm=ROW_TILE):
    t, d = x.shape
    nm = kn.shape[2]
    per_seq = seq_len // tm
    return pl.pallas_call(
        _cross_body,
        out_shape=jax.ShapeDtypeStruct((t, d), F32),
        grid=(t // tm,),
        in_specs=[pl.BlockSpec((tm, d), lambda i: (i, 0)),
                  pl.BlockSpec((1, d), lambda i: (0, 0)),
                  pl.BlockSpec((d, d), lambda i: (0, 0)),
                  pl.BlockSpec((1, X_DH), lambda i: (0, 0)),
                  pl.BlockSpec((1, 1, nm, d), lambda i: (layer, i // per_seq, 0, 0)),
                  pl.BlockSpec((1, 1, nm, d), lambda i: (layer, i // per_seq, 0, 0)),
                  pl.BlockSpec((d, d), lambda i: (0, 0))],
        out_specs=pl.BlockSpec((tm, d), lambda i: (i, 0)),
        compiler_params=_params("parallel"),
        name="cross_block",
    )(x, gain.reshape(1, d), w_q, (q_gain * X_DH ** -0.5).reshape(1, X_DH), kn, v, w_o)


LANES = 128


def _toeplitz_body(k_ref, o_ref):
    lc, gs = S5_CHUNK, S5_GROUP
    kern = k_ref[0]
    n = kern.shape[1]
    per_tile = LANES // gs
    for r in range(per_tile):
        shifted = pltpu.roll(kern, n - gs * r, 1) if r else kern
        for a in range(lc // per_tile):
            s = lc - 1 - (per_tile * a + r)
            o_ref[0, s * gs:(s + 1) * gs, :] = shifted[:, LANES * a:LANES * a + lc * gs].astype(o_ref.dtype)


def toeplitz_expand(kern_flat):
    g, gs, n = kern_flat.shape
    width = S5_CHUNK * S5_GROUP
    return pl.pallas_call(
        _toeplitz_body,
        out_shape=jax.ShapeDtypeStruct((g, width, width), BF16),
        grid=(g,),
        in_specs=[pl.BlockSpec((1, gs, n), lambda i: (i, 0, 0))],
        out_specs=pl.BlockSpec((1, width, width), lambda i: (i, 0, 0)),
        compiler_params=_params("parallel"),
        name="toeplitz_expand",
    )(kern_flat)


def s5_operators(lam_re, lam_im, log_step, b_re, b_im, c_re, c_im, d):
    lc = S5_CHUNK
    lam = lax.complex(lam_re.astype(F32), lam_im.astype(F32))
    step = jnp.exp(log_step.astype(F32))[..., None]
    lam_bar = jnp.exp(lam * step)
    b_bar = ((lam_bar - 1.0) / lam)[..., None] * lax.complex(b_re.astype(F32), b_im.astype(F32))
    c = lax.complex(c_re.astype(F32), c_im.astype(F32))
    pw = jnp.ones_like(lam_bar)[..., None]
    stride = lam_bar[..., None]
    while pw.shape[-1] < lc + 1:
        pw = jnp.concatenate([pw, pw * stride], axis=-1)
        stride = stride * stride
    pw = pw[..., :lc + 1]
    kern = jnp.einsum('zgcp,zgpt,zgpd->zgtcd', c, pw[..., :lc], b_bar, precision=HIGHEST).real
    kf, kb = kern[0], kern[1]
    k0 = kf[:, :1] + kb[:, :1] + (d.astype(F32)[:, :, None] * jnp.eye(S5_GROUP, dtype=F32))[:, None]
    kern_full = jnp.concatenate([kb[:, :0:-1], k0, kf[:, 1:]], axis=1)
    kern_flat = kern_full.transpose(0, 3, 1, 2).reshape(S5_GROUPS, S5_GROUP, (2 * lc - 1) * S5_GROUP)
    toep = toeplitz_expand(jnp.pad(kern_flat, ((0, 0), (0, 0), (0, S5_GROUP))))

    pf = jnp.einsum('gps,gpd->gsdp', pw[0][..., lc - 1::-1], b_bar[0])
    pb = jnp.einsum('gps,gpd->gsdp', pw[1][..., :lc], b_bar[1])
    p_op = jnp.concatenate([pf.real, pb.real, pf.imag, pb.imag], axis=-1)
    p_op = p_op.reshape(S5_GROUPS, lc * S5_GROUP, 4 * S5_STATE)

    qf = jnp.einsum('gcp,gpt->gptc', c[0], pw[0][..., 1:])
    qb = jnp.einsum('gcp,gpt->gptc', c[1], pw[1][..., :0:-1])
    q_op = jnp.concatenate([qf.real, qb.real, -qf.imag, -qb.imag], axis=1)
    q_op = q_op.reshape(S5_GROUPS, 4 * S5_STATE, lc * S5_GROUP)

    a = pw[..., lc]
    coef = jnp.stack([jnp.concatenate([a[0].real, a[1].real], -1),
                      jnp.concatenate([a[0].imag, a[1].imag], -1)], axis=1)
    return toep, _split_bf16(p_op), _split_bf16(q_op), coef


def _split_bf16(x):
    hi = x.astype(BF16)
    lo = (x - hi.astype(F32)).astype(BF16)
    return jnp.stack([hi, lo], axis=1)


SUBLANES = 8


def _s5_group_body(u_ref, t_ref, p_ref, q_ref, c_ref, y_ref,
                   v_re_ref, v_im_ref, xf_re_ref, xf_im_ref, xb_re_ref, xb_im_ref, *, n_chunks, bsz):
    u = u_ref[0]
    half = 2 * S5_STATE
    v = (jnp.dot(u, p_ref[0, 0], preferred_element_type=F32)
         + jnp.dot(u, p_ref[0, 1], preferred_element_type=F32))
    v_re_ref[...] = v[:, :half]
    v_im_ref[...] = v[:, half:]
    coef = c_ref[0]
    a_re, a_im = coef[0:1], coef[1:2]
    fwd = lax.broadcasted_iota(jnp.int32, (bsz, half), 1) < S5_STATE

    def step(i, carry):
        re, im = carry
        rows_f = pl.ds(i, bsz, stride=n_chunks)
        rows_b = pl.ds(n_chunks - 1 - i, bsz, stride=n_chunks)
        xf_re_ref[rows_f, :] = re
        xf_im_ref[rows_f, :] = im
        xb_re_ref[rows_b, :] = re
        xb_im_ref[rows_b, :] = im
        re_next = a_re * re - a_im * im + jnp.where(fwd, v_re_ref[rows_f, :], v_re_ref[rows_b, :])
        im_next = a_re * im + a_im * re + jnp.where(fwd, v_im_ref[rows_f, :], v_im_ref[rows_b, :])
        return re_next, im_next

    zero = jnp.zeros((bsz, half), F32)
    lax.fori_loop(0, n_chunks, step, (zero, zero))
    is_fwd = lax.broadcasted_iota(jnp.int32, xf_re_ref.shape, 1) < S5_STATE
    x = jnp.concatenate([jnp.where(is_fwd, xf_re_ref[...], xb_re_ref[...]),
                         jnp.where(is_fwd, xf_im_ref[...], xb_im_ref[...])], axis=1)
    x_hi = x.astype(BF16)
    x_lo = (x - x_hi.astype(F32)).astype(BF16)
    y = (jnp.dot(u, t_ref[0], preferred_element_type=F32)
         + jnp.dot(x_hi, q_ref[0, 0], preferred_element_type=F32)
         + jnp.dot(x_lo, q_ref[0, 0], preferred_element_type=F32)
         + jnp.dot(x_hi, q_ref[0, 1], preferred_element_type=F32))
    y_ref[0] = y.astype(y_ref.dtype)


def s5_scan(u, bsz, seq_len, ops):
    toep, p_op, q_op, coef = ops
    lc, g, w = S5_CHUNK, S5_GROUPS, S5_CHUNK * S5_GROUP
    n = seq_len // lc
    c = bsz * n
    ns = 4 * S5_STATE
    ug = u.reshape(bsz, n, lc, g, S5_GROUP).transpose(3, 0, 1, 2, 4).reshape(g, c, w)
    group = lambda *shape: pl.BlockSpec((1,) + shape, lambda i: (i,) + (0,) * len(shape))
    y = pl.pallas_call(
        functools.partial(_s5_group_body, n_chunks=n, bsz=bsz),
        out_shape=jax.ShapeDtypeStruct((g, c, w), BF16),
        grid=(g,),
        in_specs=[group(c, w), group(w, w), group(2, w, ns), group(2, ns, w), group(2, ns // 2)],
        out_specs=group(c, w),
        scratch_shapes=[pltpu.VMEM((c, ns // 2), F32)] * 6,
        compiler_params=_params("parallel"),
        name="s5_group",
    )(ug, toep, p_op, q_op, coef)
    return y.reshape(g, bsz, n, lc, S5_GROUP).transpose(1, 2, 3, 0, 4).reshape(bsz * seq_len, S5_WIDTH)


def _gla_direction(q_ref, k_ref, v_ref, g_ref, wg, bg, s_ref, o_ref, forward, chunks):
    cs = GLA_CHUNK
    tb = chunks * cs
    glr = g_ref[...]
    logit = (jnp.dot(glr, wg[0], preferred_element_type=F32)
             + jnp.dot(glr, wg[1], preferred_element_type=F32) + bg)
    g = (jnp.minimum(logit, 0.0) - jnp.log1p(jnp.exp(-jnp.abs(logit)))) / GLA_TAU
    row = lax.broadcasted_iota(jnp.int32, (tb, tb), 0)
    col = lax.broadcasted_iota(jnp.int32, (tb, tb), 1)
    same_chunk = (row // cs) == (col // cs)
    within = same_chunk & ((col <= row) if forward else (col >= row))
    ones = jnp.where(within, 1.0, 0.0).astype(BF16)
    bcum = jnp.zeros_like(g)
    rest = g
    for _ in range(3):
        term = rest.astype(BF16)
        bcum = bcum + jnp.dot(ones, term, preferred_element_type=F32)
        rest = rest - term.astype(F32)

    srow = lax.broadcasted_iota(jnp.int32, (GLA_HEADS * cs, cs), 0) & (cs - 1)
    scol = lax.broadcasted_iota(jnp.int32, (GLA_HEADS * cs, cs), 1)
    keep = (scol <= srow) if forward else (scol >= srow)
    lane = lax.broadcasted_iota(jnp.int32, (1, GLA_QK), 1)
    head_lanes = [((lane >= h * GLA_DK) & (lane < (h + 1) * GLA_DK)).astype(F32) for h in range(GLA_HEADS)]
    state_mask = (lax.broadcasted_iota(jnp.int32, (GLA_V, GLA_QK), 0) // GLA_DV
                  == lax.broadcasted_iota(jnp.int32, (GLA_V, GLA_QK), 1) // GLA_DK)

    i_ref = cs // 2 - 1 if forward else cs // 2
    i_last = cs - 1 if forward else 0
    scale = GLA_DK ** -0.5
    for ci in (range(chunks) if forward else reversed(range(chunks))):
        rows = slice(ci * cs, (ci + 1) * cs)
        b = bcum[rows]
        bref = b[i_ref:i_ref + 1]
        blast = b[i_last:i_last + 1]
        q = q_ref[rows, :].astype(F32) * scale
        k = k_ref[rows, :].astype(F32)
        v = v_ref[rows, :]
        q_rel = q * jnp.exp(b - bref)
        k_rel = (k * jnp.exp(bref - b)).astype(BF16)
        k_out = (k * jnp.exp(blast - b)).astype(BF16)
        q_dec = (q * jnp.exp(b)).astype(BF16)
        decay = jnp.exp(blast)
        q_heads = jnp.concatenate([q_rel * hm for hm in head_lanes], axis=0).astype(BF16)
        s = lax.dot_general(q_heads, k_rel, NT_DIMS, preferred_element_type=F32)
        s = jnp.where(keep, s, 0.0).astype(BF16)
        st = s_ref[...]
        o_inter = lax.dot_general(q_dec, st.astype(BF16), NT_DIMS, preferred_element_type=F32)
        o_intra = [jnp.dot(s[h * cs:(h + 1) * cs], v[:, h * GLA_DV:(h + 1) * GLA_DV],
                           preferred_element_type=F32) for h in range(GLA_HEADS)]
        o_ref[rows, :] = o_inter + jnp.concatenate(o_intra, axis=-1)
        kv = lax.dot_general(v, k_out, TN_DIMS, preferred_element_type=F32)
        s_ref[...] = decay * st + jnp.where(state_mask, kv, 0.0)


def _gla_body(qf_ref, kf_ref, vf_ref, gf_ref, qb_ref, kb_ref, vb_ref, gb_ref, wg_ref, bg_ref,
              of_ref, ob_ref, sf_ref, sb_ref, *, chunks):
    @pl.when(pl.program_id(1) == 0)
    def _():
        sf_ref[...] = jnp.zeros_like(sf_ref)
        sb_ref[...] = jnp.zeros_like(sb_ref)

    _gla_direction(qf_ref, kf_ref, vf_ref, gf_ref, wg_ref[0], bg_ref[0], sf_ref, of_ref, True, chunks)
    _gla_direction(qb_ref, kb_ref, vb_ref, gb_ref, wg_ref[1], bg_ref[1], sb_ref, ob_ref, False, chunks)


def gla_scan(proj, bsz, seq_len, w_gate, b_gate, chunks=4):
    t = proj.shape[0]
    tb = chunks * GLA_CHUNK
    nb = seq_len // tb
    fwd = lambda cb: (lambda b, i: (b * nb + i, cb))
    bwd = lambda cb: (lambda b, i: (b * nb + nb - 1 - i, cb))
    qc, kc, vc, gc = 512 // GLA_QK, 768 // GLA_QK, 1024 // GLA_V, 2048 // 128
    wg = jnp.zeros((2, 128, GLA_QK), F32)
    wg = wg.at[0, :GLA_RANK].set(w_gate[0].astype(F32)).at[1, GLA_RANK:2 * GLA_RANK].set(w_gate[1].astype(F32))
    wg = _split_bf16(wg)
    out = jax.ShapeDtypeStruct((t, GLA_V), F32)
    state = pltpu.VMEM((GLA_V, GLA_QK), F32)
    return pl.pallas_call(
        functools.partial(_gla_body, chunks=chunks),
        out_shape=(out, out),
        grid=(bsz, nb),
        in_specs=[pl.BlockSpec((tb, GLA_QK), fwd(qc)), pl.BlockSpec((tb, GLA_QK), fwd(kc)),
                  pl.BlockSpec((tb, GLA_V), fwd(vc)), pl.BlockSpec((tb, 128), fwd(gc)),
                  pl.BlockSpec((tb, GLA_QK), bwd(qc)), pl.BlockSpec((tb, GLA_QK), bwd(kc)),
                  pl.BlockSpec((tb, GLA_V), bwd(vc)), pl.BlockSpec((tb, 128), bwd(gc)),
                  pl.BlockSpec((2, 2, 128, GLA_QK), lambda b, i: (0, 0, 0, 0)),
                  pl.BlockSpec((2, 1, GLA_QK), lambda b, i: (0, 0, 0))],
        out_specs=(pl.BlockSpec((tb, GLA_V), fwd(0)), pl.BlockSpec((tb, GLA_V), bwd(0))),
        scratch_shapes=[state, state],
        compiler_params=_params("parallel", "arbitrary"),
        name="gla_scan",
    )(proj, proj, proj, proj, proj, proj, proj, proj, wg, b_gate.astype(F32).reshape(2, 1, GLA_QK))


def _even_out_body(x_ref, ys_ref, of_ref, ob_ref, og_ref, wglu_ref, gn_ref, wtop_ref, wbot_ref, o_ref):
    y = ys_ref[...].astype(F32)
    y = 0.5 * y * (1.0 + jnp.tanh(math.sqrt(2.0 / math.pi) * (y + 0.044715 * (y * y * y))))
    gate = jnp.dot(y.astype(BF16), wglu_ref[...], preferred_element_type=F32)
    y = y * _sigmoid(gate)
    o = of_ref[...] + ob_ref[...]
    og = og_ref[...].astype(F32)
    heads = []
    for h in range(GLA_HEADS):
        sl = slice(h * GLA_DV, (h + 1) * GLA_DV)
        heads.append(_rms(o[:, sl], gn_ref[...]))
    o = jnp.concatenate(heads, axis=-1) * (og * _sigmoid(og))
    o_ref[...] = (x_ref[...]
                  + jnp.dot(y.astype(BF16), wtop_ref[...], preferred_element_type=F32)
                  + jnp.dot(o.astype(BF16), wbot_ref[...], preferred_element_type=F32))


def even_out(x, ys, o_f, o_b, proj, w_glu, gla_norm, w_out, tm=ROW_TILE):
    t, d = x.shape
    row = lambda i: (i, 0)
    const = lambda i: (0, 0)
    return pl.pallas_call(
        _even_out_body,
        out_shape=jax.ShapeDtypeStruct((t, d), F32),
        grid=(t // tm,),
        in_specs=[pl.BlockSpec((tm, d), row), pl.BlockSpec((tm, S5_WIDTH), row),
                  pl.BlockSpec((tm, GLA_V), row), pl.BlockSpec((tm, GLA_V), row),
                  pl.BlockSpec((tm, GLA_V), lambda i: (i, 1536 // GLA_V)),
                  pl.BlockSpec((S5_WIDTH, S5_WIDTH), const), pl.BlockSpec((1, GLA_DV), const),
                  pl.BlockSpec((S5_WIDTH, d), const), pl.BlockSpec((GLA_V, d), lambda i: (1, 0))],
        out_specs=pl.BlockSpec((tm, d), row),
        compiler_params=_params("parallel"),
        name="even_out",
    )(x, ys, o_f, o_b, proj, w_glu, gla_norm.astype(F32).reshape(1, GLA_DV),
      w_out, w_out)


QK_NORM_WIDTH = 256


def _seg_rms(a, gain, same_seg):
    ssq = jnp.dot((a * a).astype(BF16), same_seg, preferred_element_type=F32)
    return a * lax.rsqrt(ssq * (1.0 / DIFF_DK) + EPS) * gain


def _qkv_body(x_ref, g_ref, w_ref, qk_gain_ref, o_ref):
    d = x_ref.shape[1]
    h = _rms(x_ref[...], g_ref[...]).astype(BF16)
    w = QK_NORM_WIDTH
    same_seg = jnp.where(lax.broadcasted_iota(jnp.int32, (w, w), 0) // DIFF_DK
                         == lax.broadcasted_iota(jnp.int32, (w, w), 1) // DIFF_DK, 1.0, 0.0).astype(BF16)
    for part in range(2):
        acc = jnp.dot(h, w_ref[:, part * d:(part + 1) * d], preferred_element_type=F32)
        gain = qk_gain_ref[part]
        for c in range(d // w):
            sl = slice(c * w, (c + 1) * w)
            o_ref[:, part * d + c * w:part * d + (c + 1) * w] = (
                _seg_rms(acc[:, sl], gain, same_seg).astype(o_ref.dtype))
    o_ref[:, 2 * d:] = jnp.dot(h, w_ref[:, 2 * d:], preferred_element_type=F32).astype(o_ref.dtype)


def qkv_project(x, gain, w, q_norm, k_norm, tm=ROW_TILE):
    t, d = x.shape
    reps = QK_NORM_WIDTH // DIFF_DK
    qg = jnp.tile(q_norm.astype(F32) * DIFF_DK ** -0.5, reps)
    kg = jnp.tile(k_norm.astype(F32), reps)
    qk_gain = jnp.stack([qg, kg]).reshape(2, 1, QK_NORM_WIDTH)
    return pl.pallas_call(
        _qkv_body,
        out_shape=jax.ShapeDtypeStruct((t, 3 * d), BF16),
        grid=(t // tm,),
        in_specs=[pl.BlockSpec((tm, d), lambda i: (i, 0)),
                  pl.BlockSpec((1, d), lambda i: (0, 0)),
                  pl.BlockSpec((d, 3 * d), lambda i: (0, 0)),
                  pl.BlockSpec((2, 1, QK_NORM_WIDTH), lambda i: (0, 0, 0))],
        out_specs=pl.BlockSpec((tm, 3 * d), lambda i: (i, 0)),
        compiler_params=_params("parallel"),
        name="qkv_project",
    )(x, gain.reshape(1, d), w, qk_gain)


POS_SPLIT = 16
POS_SHIFT = 4
AUG_LANE = DIFF_DK
SOFTMAX_ROWS = 16
ROW_PARTS = 2


def _pos_terms(shape):
    lane = lax.broadcasted_iota(jnp.int32, shape, 1)
    pos = lax.broadcasted_iota(jnp.int32, shape, 0)
    hi = lax.shift_right_logical(pos, POS_SHIFT).astype(F32)
    lo = (pos & (POS_SPLIT - 1)).astype(F32)
    return lane, hi, lo


def _lane_select(lane, first, values):
    out = 0.0
    for n, val in reversed(list(enumerate(values))):
        out = jnp.where(lane == first + n, val, out)
    return out


def _diff_attn_body(slope_ref, q_ref, k_ref, v_ref, lq1_ref, lk1_ref, lq2_ref, lk2_ref, sub_ref,
                    o_ref, ka_ref, va_ref, qs_ref, s0_ref, s1_ref, p0_ref, p1_ref, vs_ref,
                    *, blk, seq_len, lambda_init):
    h = pl.program_id(1)
    qi = pl.program_id(2)
    slope = slope_ref[h]
    nk = seq_len // blk
    s_refs, p_refs = (s0_ref, s1_ref), (p0_ref, p1_ref)

    @pl.when(qi == 0)
    def _():
        va_ref[:, :DIFF_DV] = v_ref[...]
        va_ref[:, DIFF_DV:] = jnp.ones((seq_len, DIFF_DV), BF16)

        def build(t, carry):
            rows = pl.ds(pl.multiple_of(t * blk, blk), blk)
            kf = k_ref[rows, :].astype(F32)
            lane, hi, lo = _pos_terms(kf.shape)
            aug = _lane_select(lane, AUG_LANE,
                               [-blk * slope, -POS_SPLIT * slope, -slope,
                                (blk * slope) * jnp.asarray(t, F32), (POS_SPLIT * slope) * hi, slope * lo])
            for z, kz in enumerate((kf, pltpu.roll(kf, DIFF_DK, 1))):
                ka_ref[z, rows, :] = jnp.where(lane < DIFF_DK, kz, aug).astype(BF16)
            return carry

        lax.fori_loop(0, nk, build, 0)

    qf = q_ref[...].astype(F32)
    lane, hi, lo = _pos_terms(qf.shape)
    qa = _lane_select(lane, AUG_LANE, [jnp.asarray(qi, F32), hi, lo, 1.0, 1.0, 1.0])
    for z, qz in enumerate((qf, pltpu.roll(qf, DIFF_DK, 1))):
        qs_ref[z] = jnp.where(lane < DIFF_DK, qz, 0.0).astype(BF16)
        qs_ref[2 + z] = jnp.where(lane < DIFF_DK, qz, qa).astype(BF16)
        qs_ref[4 + z] = jnp.where(lane < DIFF_DK, qz, -qa).astype(BF16)

    def key_block(t):
        if t == 0:
            return qi, 0
        j = (t - 1) + jnp.asarray(qi <= t - 1, jnp.int32)
        return j, jnp.where(j < qi, 2, 4)

    def key_rows(t):
        j, _ = key_block(t)
        return pl.ds(pl.multiple_of(j * blk, blk), blk)

    part = blk // ROW_PARTS
    parts = [slice(r * part, (r + 1) * part) for r in range(ROW_PARTS)]
    for z, s_ref in enumerate(s_refs):
        for rows in parts:
            for t in range(nk):
                _, variant = key_block(t)
                s_ref[rows, t * blk:(t + 1) * blk] = lax.dot_general(
                    qs_ref[variant + z, rows, :], ka_ref[z, key_rows(t), :], NT_DIMS,
                    preferred_element_type=F32)

    rg = SOFTMAX_ROWS
    rel = (lax.broadcasted_iota(jnp.int32, (rg, blk), 0) - lax.broadcasted_iota(jnp.int32, (rg, blk), 1))
    for t in range(nk):
        vs_ref[t * blk:(t + 1) * blk, :] = va_ref[key_rows(t), :]

    acc = []
    for s_ref, p_ref in zip(s_refs, p_refs):
        for g in range(blk // rg):
            rows = slice(g * rg, (g + 1) * rg)
            s_diag = s_ref[rows, :blk] - slope * jnp.abs(rel + g * rg).astype(F32)
            top = s_diag
            for t in range(1, nk):
                top = jnp.maximum(top, s_ref[rows, t * blk:(t + 1) * blk])
            m = jnp.max(top, axis=-1, keepdims=True)
            p_ref[rows, :blk] = jnp.exp(s_diag - m).astype(BF16)
            for t in range(1, nk):
                cols = slice(t * blk, (t + 1) * blk)
                p_ref[rows, cols] = jnp.exp(s_ref[rows, cols] - m).astype(BF16)
        acc.append([jnp.dot(p_ref[rows, :], vs_ref[...], preferred_element_type=F32) for rows in parts])

    lam = (jnp.exp(jnp.sum(lq1_ref[...] * lk1_ref[...], axis=-1, keepdims=True))
           - jnp.exp(jnp.sum(lq2_ref[...] * lk2_ref[...], axis=-1, keepdims=True)) + lambda_init)
    for rows, a0, a1 in zip(parts, *acc):
        o = (a0[:, :DIFF_DV] / a0[:, DIFF_DV:DIFF_DV + 1]
             - lam * (a1[:, :DIFF_DV] / a1[:, DIFF_DV:DIFF_DV + 1]))
        o_ref[rows, :] = (_rms(o, sub_ref[...]) * (1.0 - lambda_init)).astype(o_ref.dtype)


def diff_attention(qkv, bsz, seq_len, lq1, lk1, lq2, lk2, sub_norm, lambda_init, blk=512):
    t = qkv.shape[0]
    nq = seq_len // blk
    assert blk // POS_SPLIT <= 256, "hi part of a block position must stay exact in bf16"
    slopes = jnp.asarray(2.0 ** (-8.0 * np.arange(1, DIFF_HEADS + 1, dtype=np.float32) / DIFF_HEADS), F32)
    vec = lambda a: a.astype(F32).reshape(1, DIFF_DK)
    const = lambda b, h, i: (0, 0)
    return pl.pallas_call(
        functools.partial(_diff_attn_body, blk=blk, seq_len=seq_len, lambda_init=lambda_init),
        out_shape=jax.ShapeDtypeStruct((t, DIFF_HEADS * DIFF_DV), BF16),
        grid=(bsz, DIFF_HEADS, nq),
        in_specs=[pl.BlockSpec(memory_space=pltpu.SMEM),
                  pl.BlockSpec((blk, 2 * DIFF_DK), lambda b, h, i: (b * nq + i, h)),
                  pl.BlockSpec((seq_len, 2 * DIFF_DK), lambda b, h, i: (b, DIFF_HEADS + h)),
                  pl.BlockSpec((seq_len, DIFF_DV), lambda b, h, i: (b, 2 * DIFF_HEADS + h)),
                  pl.BlockSpec((1, DIFF_DK), const), pl.BlockSpec((1, DIFF_DK), const),
                  pl.BlockSpec((1, DIFF_DK), const), pl.BlockSpec((1, DIFF_DK), const),
                  pl.BlockSpec((1, DIFF_DV), const)],
        out_specs=pl.BlockSpec((blk, DIFF_DV), lambda b, h, i: (b * nq + i, h)),
        scratch_shapes=[pltpu.VMEM((2, seq_len, 2 * DIFF_DK), BF16),
                        pltpu.VMEM((seq_len, 2 * DIFF_DV), BF16),
                        pltpu.VMEM((6, blk, 2 * DIFF_DK), BF16),
                        pltpu.VMEM((blk, seq_len), F32),
                        pltpu.VMEM((blk, seq_len), F32),
                        pltpu.VMEM((blk, seq_len), BF16),
                        pltpu.VMEM((blk, seq_len), BF16),
                        pltpu.VMEM((seq_len, 2 * DIFF_DV), BF16)],
        compiler_params=_params("parallel", "parallel", "arbitrary"),
        name="diff_attention",
    )(slopes, qkv, qkv, qkv, vec(lq1), vec(lk1), vec(lq2), vec(lk2),
      sub_norm.astype(F32).reshape(1, DIFF_DV))


def _pad_ev_w_in(w):
    return jnp.pad(w, ((0, 0), (0, EV_PAD_COLS - w.shape[1]))).astype(BF16)


def _trunk(x3, mem, p):
    bsz, seq_len, d = x3.shape
    x = x3.reshape(bsz * seq_len, d)
    kn_all, v_all = mem_kv(mem, p['norm_mem'], p['x_w_kv'], p['x_k_norm'])
    for layer in range(DEPTH):
        if layer % 2 == 0:
            e = layer // 2
            proj = norm_matmul(x, p['norm_mix'][layer], p['ev_w_in'][e], BF16, EV_TN)
            ys = s5_scan(proj[:, :S5_WIDTH], bsz, seq_len, p['s5_ops'][e])
            o_f, o_b = gla_scan(proj, bsz, seq_len, p['gla_w_gate'][e], p['gla_b_gate'][e])
            x = even_out(x, ys, o_f, o_b, proj, p['s5_w_glu'][e], p['gla_norm'][e], p['ev_w_out'][e])
        else:
            o = layer // 2
            lambda_init = 0.8 - 0.6 * math.exp(-0.3 * layer)
            qkv = qkv_project(x, p['norm_mix'][layer], p['od_w_in'][o],
                              p['diff_q_norm'][o], p['diff_k_norm'][o])
            att = diff_attention(qkv, bsz, seq_len, p['diff_lambda_q1'][o], p['diff_lambda_k1'][o],
                                 p['diff_lambda_q2'][o], p['diff_lambda_k2'][o], p['diff_norm'][o],
                                 lambda_init)
            x = matmul_residual(att, p['od_w_out'][o], x, tn=d)
        x = cross_block(x, seq_len, layer, p['norm_cross'][layer], p['x_w_q'][layer], p['x_q_norm'][layer],
                        kn_all, v_all, p['x_w_o'][layer])
        x = mlp_block(x, p['norm_mlp'][layer], p['mlp_w1'][layer], p['mlp_w2'][layer])
    return x.reshape(bsz, seq_len, d)


def kernel(x_prompt, x_sample, mem_prompt, mem_sample, norm_mix, norm_cross, norm_mem, norm_mlp,
           ev_w_in, ev_w_out, s5_lambda_re, s5_lambda_im, s5_log_step, s5_b_re, s5_b_im,
           s5_c_re, s5_c_im, s5_d, s5_w_glu, gla_w_gate, gla_b_gate, gla_norm,
           od_w_in, od_w_out, diff_q_norm, diff_k_norm, diff_lambda_q1, diff_lambda_k1,
           diff_lambda_q2, diff_lambda_k2, diff_norm, x_w_q, x_w_kv, x_w_o, x_q_norm, x_k_norm,
           mlp_w1, mlp_w2):
    n_even = ev_w_in.shape[0]
    bf = lambda w: w.astype(BF16)
    p = dict(
        norm_mix=norm_mix, norm_cross=norm_cross, norm_mem=norm_mem, norm_mlp=norm_mlp,
        ev_w_in=[_pad_ev_w_in(ev_w_in[e]) for e in range(n_even)], ev_w_out=bf(ev_w_out),
        s5_ops=[s5_operators(s5_lambda_re[e], s5_lambda_im[e], s5_log_step[e], s5_b_re[e], s5_b_im[e],
                             s5_c_re[e], s5_c_im[e], s5_d[e]) for e in range(n_even)],
        s5_w_glu=bf(s5_w_glu), gla_w_gate=gla_w_gate, gla_b_gate=gla_b_gate, gla_norm=gla_norm,
        od_w_in=bf(od_w_in), od_w_out=bf(od_w_out), diff_q_norm=diff_q_norm, diff_k_norm=diff_k_norm,
        diff_lambda_q1=diff_lambda_q1, diff_lambda_k1=diff_lambda_k1,
        diff_lambda_q2=diff_lambda_q2, diff_lambda_k2=diff_lambda_k2, diff_norm=diff_norm,
        x_w_q=bf(x_w_q), x_w_kv=bf(x_w_kv), x_w_o=bf(x_w_o), x_q_norm=x_q_norm, x_k_norm=x_k_norm,
        mlp_w1=bf(mlp_w1), mlp_w2=bf(mlp_w2))
    return (_trunk(x_prompt, mem_prompt, p), _trunk(x_sample, mem_sample, p))
```

```python
import functools
import math

import numpy as np
import jax
import jax.numpy as jnp
from jax import lax
from jax.experimental import pallas as pl
from jax.experimental.pallas import tpu as pltpu

F32 = jnp.float32
BF16 = jnp.bfloat16
HIGHEST = lax.Precision.HIGHEST

D_MODEL = 1024
DEPTH = 4
EPS = 1e-6
S5_WIDTH = 512
S5_GROUP = 16
S5_GROUPS = 32
S5_STATE = 64
S5_CHUNK = 64
GLA_HEADS = 4
GLA_DV = 128
GLA_DK = 64
GLA_RANK = 16
GLA_TAU = 16.0
GLA_CHUNK = 64
GLA_QK = GLA_HEADS * GLA_DK
GLA_V = GLA_HEADS * GLA_DV
EV_PAD_COLS = 2304
EV_TN = 768
DIFF_HEADS = 8
DIFF_DK = 64
DIFF_DV = 128
X_HEADS = 4
X_DH = 256
D_FF = 4096

ROW_TILE = 512
VMEM_LIMIT = 48 * 1024 * 1024

NT_DIMS = (((1,), (1,)), ((), ()))
TN_DIMS = (((0,), (0,)), ((), ()))


def _params(*sem, flags=None):
    return pltpu.CompilerParams(dimension_semantics=sem, vmem_limit_bytes=VMEM_LIMIT, flags=flags)


def _rms(x, gain):
    ms = jnp.mean(x * x, axis=-1, keepdims=True)
    return x * lax.rsqrt(ms + EPS) * gain


def _sigmoid(x):
    return 1.0 / (1.0 + jnp.exp(-x))


def _norm_matmul_body(x_ref, g_ref, w_ref, o_ref, *, tn):
    h = _rms(x_ref[...], g_ref[...]).astype(BF16)
    for j in range(w_ref.shape[1] // tn):
        cols = slice(j * tn, (j + 1) * tn)
        o_ref[:, cols] = jnp.dot(h, w_ref[:, cols], preferred_element_type=F32).astype(o_ref.dtype)


def norm_matmul(x, gain, w, out_dtype, tn, tm=ROW_TILE):
    t, d = x.shape
    n = w.shape[1]
    return pl.pallas_call(
        functools.partial(_norm_matmul_body, tn=tn),
        out_shape=jax.ShapeDtypeStruct((t, n), out_dtype),
        grid=(t // tm,),
        in_specs=[pl.BlockSpec((tm, d), lambda i: (i, 0)),
                  pl.BlockSpec((1, d), lambda i: (0, 0)),
                  pl.BlockSpec((d, n), lambda i: (0, 0))],
        out_specs=pl.BlockSpec((tm, n), lambda i: (i, 0)),
        compiler_params=_params("parallel"),
        name="norm_matmul",
    )(x, gain.reshape(1, d), w)


def _matmul_res_body(a_ref, w_ref, r_ref, o_ref):
    o_ref[...] = r_ref[...] + jnp.dot(a_ref[...].astype(BF16), w_ref[...], preferred_element_type=F32)


def matmul_residual(a, w, res, tn=512, tm=ROW_TILE):
    t, k = a.shape
    n = w.shape[1]
    return pl.pallas_call(
        _matmul_res_body,
        out_shape=jax.ShapeDtypeStruct((t, n), F32),
        grid=(t // tm, n // tn),
        in_specs=[pl.BlockSpec((tm, k), lambda i, j: (i, 0)),
                  pl.BlockSpec((k, tn), lambda i, j: (0, j)),
                  pl.BlockSpec((tm, tn), lambda i, j: (i, j))],
        out_specs=pl.BlockSpec((tm, tn), lambda i, j: (i, j)),
        compiler_params=_params("parallel", "arbitrary"),
        name="matmul_residual",
    )(a, w, res)


def _mlp_body(x_ref, g_ref, w1_ref, w2_ref, o_ref, hid_ref, *, tf):
    x = x_ref[...]
    h = _rms(x, g_ref[...]).astype(BF16)
    for f in range(w1_ref.shape[1] // tf):
        cols = slice(f * tf, (f + 1) * tf)
        hid = jnp.dot(h, w1_ref[:, cols], preferred_element_type=F32)
        hid_ref[:, cols] = jnp.square(jnp.maximum(hid, 0.0)).astype(BF16)
    o_ref[...] = x + jnp.dot(hid_ref[...], w2_ref[...], preferred_element_type=F32)


def mlp_block(x, gain, w1, w2, tf=512, tm=ROW_TILE):
    t, d = x.shape
    ff = w1.shape[1]
    resident = lambda shape: pl.BlockSpec(shape, lambda i: (0, 0), pipeline_mode=pl.Buffered(1))
    return pl.pallas_call(
        functools.partial(_mlp_body, tf=tf),
        out_shape=jax.ShapeDtypeStruct((t, d), F32),
        grid=(t // tm,),
        in_specs=[pl.BlockSpec((tm, d), lambda i: (i, 0)),
                  pl.BlockSpec((1, d), lambda i: (0, 0)),
                  resident((d, ff)), resident((ff, d))],
        out_specs=pl.BlockSpec((tm, d), lambda i: (i, 0)),
        scratch_shapes=[pltpu.VMEM((tm, ff), BF16)],
        compiler_params=_params("parallel"),
        name="mlp_block",
    )(x, gain.reshape(1, d), w1, w2)


def _mem_kv_body(m_ref, g_ref, w_ref, kg_ref, k_ref, v_ref):
    h = _rms(m_ref[0], g_ref[0]).astype(BF16)
    kv = jnp.dot(h, w_ref[0], preferred_element_type=F32)
    for hd in range(X_HEADS):
        sl = slice(hd * X_DH, (hd + 1) * X_DH)
        k_ref[0, 0, :, sl] = _rms(kv[:, sl], kg_ref[0]).astype(BF16)
    v_ref[0, 0] = kv[:, D_MODEL:].astype(BF16)


def mem_kv(mem, norm_mem, w_kv, k_norm):
    bm, nm, d = mem.shape
    out = jax.ShapeDtypeStruct((DEPTH, bm, nm, d), BF16)
    return pl.pallas_call(
        _mem_kv_body,
        out_shape=(out, out),
        grid=(DEPTH, bm),
        in_specs=[pl.BlockSpec((1, nm, d), lambda l, b: (b, 0, 0)),
                  pl.BlockSpec((1, 1, d), lambda l, b: (l, 0, 0)),
                  pl.BlockSpec((1, d, 2 * d), lambda l, b: (l, 0, 0)),
                  pl.BlockSpec((1, 1, X_DH), lambda l, b: (l, 0, 0))],
        out_specs=(pl.BlockSpec((1, 1, nm, d), lambda l, b: (l, b, 0, 0)),
                   pl.BlockSpec((1, 1, nm, d), lambda l, b: (l, b, 0, 0))),
        compiler_params=_params("arbitrary", "arbitrary"),
        name="mem_kv",
    )(mem, norm_mem.reshape(DEPTH, 1, d), w_kv, k_norm.reshape(DEPTH, 1, X_DH))


def _cross_body(x_ref, g_ref, wq_ref, qg_ref, k_ref, v_ref, wo_ref, o_ref):
    x = x_ref[...]
    h = _rms(x, g_ref[...]).astype(BF16)
    q = jnp.dot(h, wq_ref[...], preferred_element_type=F32)
    heads = []
    for hd in range(X_HEADS):
        sl = slice(hd * X_DH, (hd + 1) * X_DH)
        qn = _rms(q[:, sl], qg_ref[...]).astype(BF16)
        s = lax.dot_general(qn, k_ref[0, 0, :, sl], NT_DIMS, preferred_element_type=F32)
        p = jnp.exp(s - jnp.max(s, axis=-1, keepdims=True))
        l = jnp.sum(p, axis=-1, keepdims=True)
        oh = jnp.dot(p.astype(BF16), v_ref[0, 0, :, sl], preferred_element_type=F32) / l
        heads.append(oh.astype(BF16))
    o = jnp.concatenate(heads, axis=-1)
    o_ref[...] = x + jnp.dot(o, wo_ref[...], preferred_element_type=F32)


def cross_block(x, seq_len, layer, gain, w_q, q_gain, kn, v, w_o, tm=ROW_TILE):
    t, d = x.shape
    nm = kn.shape[2]
    per_seq = seq_len // tm
    return pl.pallas_call(
        _cross_body,
        out_shape=jax.ShapeDtypeStruct((t, d), F32),
        grid=(t // tm,),
        in_specs=[pl.BlockSpec((tm, d), lambda i: (i, 0)),
                  pl.BlockSpec((1, d), lambda i: (0, 0)),
                  pl.BlockSpec((d, d), lambda i: (0, 0)),
                  pl.BlockSpec((1, X_DH), lambda i: (0, 0)),
                  pl.BlockSpec((1, 1, nm, d), lambda i: (layer, i // per_seq, 0, 0)),
                  pl.BlockSpec((1, 1, nm, d), lambda i: (layer, i // per_seq, 0, 0)),
                  pl.BlockSpec((d, d), lambda i: (0, 0))],
        out_specs=pl.BlockSpec((tm, d), lambda i: (i, 0)),
        compiler_params=_params("parallel"),
        name="cross_block",
    )(x, gain.reshape(1, d), w_q, (q_gain * X_DH ** -0.5).reshape(1, X_DH), kn, v, w_o)


LANES = 128


def _toeplitz_body(k_ref, o_ref):
    lc, gs = S5_CHUNK, S5_GROUP
    kern = k_ref[0]
    n = kern.shape[1]
    per_tile = LANES // gs
    for r in range(per_tile):
        shifted = pltpu.roll(kern, n - gs * r, 1) if r else kern
        for a in range(lc // per_tile):
            s = lc - 1 - (per_tile * a + r)
            o_ref[0, s * gs:(s + 1) * gs, :] = shifted[:, LANES * a:LANES * a + lc * gs].astype(o_ref.dtype)


def toeplitz_expand(kern_flat):
    g, gs, n = kern_flat.shape
    width = S5_CHUNK * S5_GROUP
    return pl.pallas_call(
        _toeplitz_body,
        out_shape=jax.ShapeDtypeStruct((g, width, width), BF16),
        grid=(g,),
        in_specs=[pl.BlockSpec((1, gs, n), lambda i: (i, 0, 0))],
        out_specs=pl.BlockSpec((1, width, width), lambda i: (i, 0, 0)),
        compiler_params=_params("parallel"),
        name="toeplitz_expand",
    )(kern_flat)


def s5_operators(lam_re, lam_im, log_step, b_re, b_im, c_re, c_im, d):
    lc = S5_CHUNK
    lam = lax.complex(lam_re.astype(F32), lam_im.astype(F32))
    step = jnp.exp(log_step.astype(F32))[..., None]
    lam_bar = jnp.exp(lam * step)
    b_bar = ((lam_bar - 1.0) / lam)[..., None] * lax.complex(b_re.astype(F32), b_im.astype(F32))
    c = lax.complex(c_re.astype(F32), c_im.astype(F32))
    pw = jnp.ones_like(lam_bar)[..., None]
    stride = lam_bar[..., None]
    while pw.shape[-1] < lc + 1:
        pw = jnp.concatenate([pw, pw * stride], axis=-1)
        stride = stride * stride
    pw = pw[..., :lc + 1]
    kern = jnp.einsum('zgcp,zgpt,zgpd->zgtcd', c, pw[..., :lc], b_bar, precision=HIGHEST).real
    kf, kb = kern[0], kern[1]
    k0 = kf[:, :1] + kb[:, :1] + (d.astype(F32)[:, :, None] * jnp.eye(S5_GROUP, dtype=F32))[:, None]
    kern_full = jnp.concatenate([kb[:, :0:-1], k0, kf[:, 1:]], axis=1)
    kern_flat = kern_full.transpose(0, 3, 1, 2).reshape(S5_GROUPS, S5_GROUP, (2 * lc - 1) * S5_GROUP)
    toep = toeplitz_expand(jnp.pad(kern_flat, ((0, 0), (0, 0), (0, S5_GROUP))))

    def outer(x, y):
        return x[0] * y[0] - x[1] * y[1], x[0] * y[1] + x[1] * y[0]

    parts = lambda z: (z.real, z.imag)
    s_first = lambda z: z.transpose(0, 2, 1)[:, :, None, :]
    b_last = lambda z: z.transpose(0, 2, 1)[:, None, :, :]
    pf = outer([s_first(v) for v in parts(pw[0][..., lc - 1::-1])], [b_last(v) for v in parts(b_bar[0])])
    pb = outer([s_first(v) for v in parts(pw[1][..., :lc])], [b_last(v) for v in parts(b_bar[1])])
    p_op = jnp.concatenate([pf[0], pb[0], pf[1], pb[1]], axis=-1)
    p_op = p_op.reshape(S5_GROUPS, lc * S5_GROUP, 4 * S5_STATE)

    c_last = lambda z: z.transpose(0, 2, 1)[:, :, None, :]
    t_mid = lambda z: z[..., None]
    qf = outer([c_last(v) for v in parts(c[0])], [t_mid(v) for v in parts(pw[0][..., 1:])])
    qb = outer([c_last(v) for v in parts(c[1])], [t_mid(v) for v in parts(pw[1][..., :0:-1])])
    q_op = jnp.concatenate([qf[0], qb[0], -qf[1], -qb[1]], axis=1)
    q_op = q_op.reshape(S5_GROUPS, 4 * S5_STATE, lc * S5_GROUP)

    a = pw[..., lc]
    coef = jnp.stack([jnp.concatenate([a[0].real, a[1].real], -1),
                      jnp.concatenate([a[0].imag, a[1].imag], -1)], axis=1)
    return toep, _split_bf16(p_op), _split_bf16(q_op), coef


def _split_bf16(x):
    hi = x.astype(BF16)
    lo = (x - hi.astype(F32)).astype(BF16)
    return jnp.stack([hi, lo], axis=1)


SUBLANES = 8


def _s5_group_body(u_ref, t_ref, p_ref, q_ref, c_ref, y_ref,
                   v_re_ref, v_im_ref, xf_re_ref, xf_im_ref, xb_re_ref, xb_im_ref, *, n_chunks, bsz):
    u = u_ref[0]
    half = 2 * S5_STATE
    v = (jnp.dot(u, p_ref[0, 0], preferred_element_type=F32)
         + jnp.dot(u, p_ref[0, 1], preferred_element_type=F32))
    v_re_ref[...] = v[:, :half]
    v_im_ref[...] = v[:, half:]
    coef = c_ref[0]
    a_re, a_im = coef[0:1], coef[1:2]
    fwd = lax.broadcasted_iota(jnp.int32, (bsz, half), 1) < S5_STATE

    def step(i, carry):
        re, im = carry
        rows_f = pl.ds(i, bsz, stride=n_chunks)
        rows_b = pl.ds(n_chunks - 1 - i, bsz, stride=n_chunks)
        xf_re_ref[rows_f, :] = re
        xf_im_ref[rows_f, :] = im
        xb_re_ref[rows_b, :] = re
        xb_im_ref[rows_b, :] = im
        re_next = a_re * re - a_im * im + jnp.where(fwd, v_re_ref[rows_f, :], v_re_ref[rows_b, :])
        im_next = a_re * im + a_im * re + jnp.where(fwd, v_im_ref[rows_f, :], v_im_ref[rows_b, :])
        return re_next, im_next

    zero = jnp.zeros((bsz, half), F32)
    lax.fori_loop(0, n_chunks, step, (zero, zero))
    is_fwd = lax.broadcasted_iota(jnp.int32, xf_re_ref.shape, 1) < S5_STATE
    x = jnp.concatenate([jnp.where(is_fwd, xf_re_ref[...], xb_re_ref[...]),
                         jnp.where(is_fwd, xf_im_ref[...], xb_im_ref[...])], axis=1)
    x_hi = x.astype(BF16)
    x_lo = (x - x_hi.astype(F32)).astype(BF16)
    y = (jnp.dot(u, t_ref[0], preferred_element_type=F32)
         + jnp.dot(x_hi, q_ref[0, 0], preferred_element_type=F32)
         + jnp.dot(x_lo, q_ref[0, 0], preferred_element_type=F32)
         + jnp.dot(x_hi, q_ref[0, 1], preferred_element_type=F32))
    y_ref[0] = y.astype(y_ref.dtype)


def s5_scan(u, bsz, seq_len, ops):
    toep, p_op, q_op, coef = ops
    lc, g, w = S5_CHUNK, S5_GROUPS, S5_CHUNK * S5_GROUP
    n = seq_len // lc
    c = bsz * n
    ns = 4 * S5_STATE
    ug = u.reshape(bsz, n, lc, g, S5_GROUP).transpose(3, 0, 1, 2, 4).reshape(g, c, w)
    group = lambda *shape: pl.BlockSpec((1,) + shape, lambda i: (i,) + (0,) * len(shape))
    y = pl.pallas_call(
        functools.partial(_s5_group_body, n_chunks=n, bsz=bsz),
        out_shape=jax.ShapeDtypeStruct((g, c, w), BF16),
        grid=(g,),
        in_specs=[group(c, w), group(w, w), group(2, w, ns), group(2, ns, w), group(2, ns // 2)],
        out_specs=group(c, w),
        scratch_shapes=[pltpu.VMEM((c, ns // 2), F32)] * 6,
        compiler_params=_params("parallel"),
        name="s5_group",
    )(ug, toep, p_op, q_op, coef)
    return y.reshape(g, bsz, n, lc, S5_GROUP).transpose(1, 2, 3, 0, 4).reshape(bsz * seq_len, S5_WIDTH)


def _gla_direction(q_ref, k_ref, v_ref, g_ref, wg, bg, s_ref, o_ref, forward, chunks):
    cs = GLA_CHUNK
    tb = chunks * cs
    glr = g_ref[...]
    logit = (jnp.dot(glr, wg[0], preferred_element_type=F32)
             + jnp.dot(glr, wg[1], preferred_element_type=F32) + bg)
    g = (jnp.minimum(logit, 0.0) - jnp.log1p(jnp.exp(-jnp.abs(logit)))) / GLA_TAU
    row = lax.broadcasted_iota(jnp.int32, (tb, tb), 0)
    col = lax.broadcasted_iota(jnp.int32, (tb, tb), 1)
    same_chunk = (row // cs) == (col // cs)
    within = same_chunk & ((col <= row) if forward else (col >= row))
    ones = jnp.where(within, 1.0, 0.0).astype(BF16)
    bcum = jnp.zeros_like(g)
    rest = g
    for _ in range(3):
        term = rest.astype(BF16)
        bcum = bcum + jnp.dot(ones, term, preferred_element_type=F32)
        rest = rest - term.astype(F32)

    srow = lax.broadcasted_iota(jnp.int32, (GLA_HEADS * cs, cs), 0) & (cs - 1)
    scol = lax.broadcasted_iota(jnp.int32, (GLA_HEADS * cs, cs), 1)
    keep = (scol <= srow) if forward else (scol >= srow)
    lane = lax.broadcasted_iota(jnp.int32, (1, GLA_QK), 1)
    head_lanes = [((lane >= h * GLA_DK) & (lane < (h + 1) * GLA_DK)).astype(F32) for h in range(GLA_HEADS)]
    state_mask = (lax.broadcasted_iota(jnp.int32, (GLA_V, GLA_QK), 0) // GLA_DV
                  == lax.broadcasted_iota(jnp.int32, (GLA_V, GLA_QK), 1) // GLA_DK)

    i_ref = cs // 2 - 1 if forward else cs // 2
    i_last = cs - 1 if forward else 0
    scale = GLA_DK ** -0.5
    for ci in (range(chunks) if forward else reversed(range(chunks))):
        rows = slice(ci * cs, (ci + 1) * cs)
        b = bcum[rows]
        bref = b[i_ref:i_ref + 1]
        blast = b[i_last:i_last + 1]
        q = q_ref[rows, :].astype(F32) * scale
        k = k_ref[rows, :].astype(F32)
        v = v_ref[rows, :]
        q_rel = q * jnp.exp(b - bref)
        k_rel = (k * jnp.exp(bref - b)).astype(BF16)
        k_out = (k * jnp.exp(blast - b)).astype(BF16)
        q_dec = (q * jnp.exp(b)).astype(BF16)
        decay = jnp.exp(blast)
        q_heads = jnp.concatenate([q_rel * hm for hm in head_lanes], axis=0).astype(BF16)
        s = lax.dot_general(q_heads, k_rel, NT_DIMS, preferred_element_type=F32)
        s = jnp.where(keep, s, 0.0).astype(BF16)
        st = s_ref[...]
        o_inter = lax.dot_general(q_dec, st.astype(BF16), NT_DIMS, preferred_element_type=F32)
        o_intra = [jnp.dot(s[h * cs:(h + 1) * cs], v[:, h * GLA_DV:(h + 1) * GLA_DV],
                           preferred_element_type=F32) for h in range(GLA_HEADS)]
        o_ref[rows, :] = o_inter + jnp.concatenate(o_intra, axis=-1)
        kv = lax.dot_general(v, k_out, TN_DIMS, preferred_element_type=F32)
        s_ref[...] = decay * st + jnp.where(state_mask, kv, 0.0)


def _gla_body(qf_ref, kf_ref, vf_ref, gf_ref, qb_ref, kb_ref, vb_ref, gb_ref, wg_ref, bg_ref,
              of_ref, ob_ref, sf_ref, sb_ref, *, chunks):
    @pl.when(pl.program_id(1) == 0)
    def _():
        sf_ref[...] = jnp.zeros_like(sf_ref)
        sb_ref[...] = jnp.zeros_like(sb_ref)

    _gla_direction(qf_ref, kf_ref, vf_ref, gf_ref, wg_ref[0], bg_ref[0], sf_ref, of_ref, True, chunks)
    _gla_direction(qb_ref, kb_ref, vb_ref, gb_ref, wg_ref[1], bg_ref[1], sb_ref, ob_ref, False, chunks)


def gla_scan(proj, bsz, seq_len, w_gate, b_gate, chunks=4):
    t = proj.shape[0]
    tb = chunks * GLA_CHUNK
    nb = seq_len // tb
    fwd = lambda cb: (lambda b, i: (b * nb + i, cb))
    bwd = lambda cb: (lambda b, i: (b * nb + nb - 1 - i, cb))
    qc, kc, vc, gc = 512 // GLA_QK, 768 // GLA_QK, 1024 // GLA_V, 2048 // 128
    wg = jnp.zeros((2, 128, GLA_QK), F32)
    wg = wg.at[0, :GLA_RANK].set(w_gate[0].astype(F32)).at[1, GLA_RANK:2 * GLA_RANK].set(w_gate[1].astype(F32))
    wg = _split_bf16(wg)
    out = jax.ShapeDtypeStruct((t, GLA_V), F32)
    state = pltpu.VMEM((GLA_V, GLA_QK), F32)
    return pl.pallas_call(
        functools.partial(_gla_body, chunks=chunks),
        out_shape=(out, out),
        grid=(bsz, nb),
        in_specs=[pl.BlockSpec((tb, GLA_QK), fwd(qc)), pl.BlockSpec((tb, GLA_QK), fwd(kc)),
                  pl.BlockSpec((tb, GLA_V), fwd(vc)), pl.BlockSpec((tb, 128), fwd(gc)),
                  pl.BlockSpec((tb, GLA_QK), bwd(qc)), pl.BlockSpec((tb, GLA_QK), bwd(kc)),
                  pl.BlockSpec((tb, GLA_V), bwd(vc)), pl.BlockSpec((tb, 128), bwd(gc)),
                  pl.BlockSpec((2, 2, 128, GLA_QK), lambda b, i: (0, 0, 0, 0)),
                  pl.BlockSpec((2, 1, GLA_QK), lambda b, i: (0, 0, 0))],
        out_specs=(pl.BlockSpec((tb, GLA_V), fwd(0)), pl.BlockSpec((tb, GLA_V), bwd(0))),
        scratch_shapes=[state, state],
        compiler_params=_params("parallel", "arbitrary"),
        name="gla_scan",
    )(proj, proj, proj, proj, proj, proj, proj, proj, wg, b_gate.astype(F32).reshape(2, 1, GLA_QK))


def _even_out_body(x_ref, ys_ref, of_ref, ob_ref, og_ref, wglu_ref, gn_ref, wtop_ref, wbot_ref, o_ref):
    y = ys_ref[...].astype(F32)
    y = 0.5 * y * (1.0 + jnp.tanh(math.sqrt(2.0 / math.pi) * (y + 0.044715 * (y * y * y))))
    gate = jnp.dot(y.astype(BF16), wglu_ref[...], preferred_element_type=F32)
    y = y * _sigmoid(gate)
    o = of_ref[...] + ob_ref[...]
    og = og_ref[...].astype(F32)
    heads = []
    for h in range(GLA_HEADS):
        sl = slice(h * GLA_DV, (h + 1) * GLA_DV)
        heads.append(_rms(o[:, sl], gn_ref[...]))
    o = jnp.concatenate(heads, axis=-1) * (og * _sigmoid(og))
    o_ref[...] = (x_ref[...]
                  + jnp.dot(y.astype(BF16), wtop_ref[...], preferred_element_type=F32)
                  + jnp.dot(o.astype(BF16), wbot_ref[...], preferred_element_type=F32))


def even_out(x, ys, o_f, o_b, proj, w_glu, gla_norm, w_out, tm=ROW_TILE):
    t, d = x.shape
    row = lambda i: (i, 0)
    const = lambda i: (0, 0)
    return pl.pallas_call(
        _even_out_body,
        out_shape=jax.ShapeDtypeStruct((t, d), F32),
        grid=(t // tm,),
        in_specs=[pl.BlockSpec((tm, d), row), pl.BlockSpec((tm, S5_WIDTH), row),
                  pl.BlockSpec((tm, GLA_V), row), pl.BlockSpec((tm, GLA_V), row),
                  pl.BlockSpec((tm, GLA_V), lambda i: (i, 1536 // GLA_V)),
                  pl.BlockSpec((S5_WIDTH, S5_WIDTH), const), pl.BlockSpec((1, GLA_DV), const),
                  pl.BlockSpec((S5_WIDTH, d), const), pl.BlockSpec((GLA_V, d), lambda i: (1, 0))],
        out_specs=pl.BlockSpec((tm, d), row),
        compiler_params=_params("parallel"),
        name="even_out",
    )(x, ys, o_f, o_b, proj, w_glu, gla_norm.astype(F32).reshape(1, GLA_DV),
      w_out, w_out)


QK_NORM_WIDTH = 256


def _seg_rms(a, gain, same_seg):
    ssq = jnp.dot((a * a).astype(BF16), same_seg, preferred_element_type=F32)
    return a * lax.rsqrt(ssq * (1.0 / DIFF_DK) + EPS) * gain


def _qkv_body(x_ref, g_ref, w_ref, qk_gain_ref, o_ref):
    d = x_ref.shape[1]
    h = _rms(x_ref[...], g_ref[...]).astype(BF16)
    w = QK_NORM_WIDTH
    same_seg = jnp.where(lax.broadcasted_iota(jnp.int32, (w, w), 0) // DIFF_DK
                         == lax.broadcasted_iota(jnp.int32, (w, w), 1) // DIFF_DK, 1.0, 0.0).astype(BF16)
    for part in range(2):
        acc = jnp.dot(h, w_ref[:, part * d:(part + 1) * d], preferred_element_type=F32)
        gain = qk_gain_ref[part]
        for c in range(d // w):
            sl = slice(c * w, (c + 1) * w)
            o_ref[:, part * d + c * w:part * d + (c + 1) * w] = (
                _seg_rms(acc[:, sl], gain, same_seg).astype(o_ref.dtype))
    o_ref[:, 2 * d:] = jnp.dot(h, w_ref[:, 2 * d:], preferred_element_type=F32).astype(o_ref.dtype)


def qkv_project(x, gain, w, q_norm, k_norm, tm=ROW_TILE):
    t, d = x.shape
    reps = QK_NORM_WIDTH // DIFF_DK
    qg = jnp.tile(q_norm.astype(F32) * DIFF_DK ** -0.5, reps)
    kg = jnp.tile(k_norm.astype(F32), reps)
    qk_gain = jnp.stack([qg, kg]).reshape(2, 1, QK_NORM_WIDTH)
    return pl.pallas_call(
        _qkv_body,
        out_shape=jax.ShapeDtypeStruct((t, 3 * d), BF16),
        grid=(t // tm,),
        in_specs=[pl.BlockSpec((tm, d), lambda i: (i, 0)),
                  pl.BlockSpec((1, d), lambda i: (0, 0)),
                  pl.BlockSpec((d, 3 * d), lambda i: (0, 0)),
                  pl.BlockSpec((2, 1, QK_NORM_WIDTH), lambda i: (0, 0, 0))],
        out_specs=pl.BlockSpec((tm, 3 * d), lambda i: (i, 0)),
        compiler_params=_params("parallel"),
        name="qkv_project",
    )(x, gain.reshape(1, d), w, qk_gain)


POS_SPLIT = 16
POS_SHIFT = 4
AUG_LANE = DIFF_DK
SOFTMAX_ROWS = 16
ROW_PARTS = 2


def _pos_terms(shape):
    lane = lax.broadcasted_iota(jnp.int32, shape, 1)
    pos = lax.broadcasted_iota(jnp.int32, shape, 0)
    hi = lax.shift_right_logical(pos, POS_SHIFT).astype(F32)
    lo = (pos & (POS_SPLIT - 1)).astype(F32)
    return lane, hi, lo


def _lane_select(lane, first, values):
    out = 0.0
    for n, val in reversed(list(enumerate(values))):
        out = jnp.where(lane == first + n, val, out)
    return out


def _diff_attn_body(slope_ref, q_ref, k_ref, v_ref, lq1_ref, lk1_ref, lq2_ref, lk2_ref, sub_ref,
                    o_ref, ka_ref, va_ref, qs_ref, s0_ref, s1_ref, p0_ref, p1_ref, vs_ref,
                    *, blk, seq_len, lambda_init):
    h = pl.program_id(1)
    qi = pl.program_id(2)
    slope = slope_ref[h]
    nk = seq_len // blk
    s_refs, p_refs = (s0_ref, s1_ref), (p0_ref, p1_ref)

    @pl.when(qi == 0)
    def _():
        va_ref[:, :DIFF_DV] = v_ref[...]
        va_ref[:, DIFF_DV:] = jnp.ones((seq_len, DIFF_DV), BF16)

        def build(t, carry):
            rows = pl.ds(pl.multiple_of(t * blk, blk), blk)
            kf = k_ref[rows, :].astype(F32)
            lane, hi, lo = _pos_terms(kf.shape)
            aug = _lane_select(lane, AUG_LANE,
                               [-blk * slope, -POS_SPLIT * slope, -slope,
                                (blk * slope) * jnp.asarray(t, F32), (POS_SPLIT * slope) * hi, slope * lo])
            for z, kz in enumerate((kf, pltpu.roll(kf, DIFF_DK, 1))):
                ka_ref[z, rows, :] = jnp.where(lane < DIFF_DK, kz, aug).astype(BF16)
            return carry

        lax.fori_loop(0, nk, build, 0)

    qf = q_ref[...].astype(F32)
    lane, hi, lo = _pos_terms(qf.shape)
    qa = _lane_select(lane, AUG_LANE, [jnp.asarray(qi, F32), hi, lo, 1.0, 1.0, 1.0])
    for z, qz in enumerate((qf, pltpu.roll(qf, DIFF_DK, 1))):
        qs_ref[z] = jnp.where(lane < DIFF_DK, qz, 0.0).astype(BF16)
        qs_ref[2 + z] = jnp.where(lane < DIFF_DK, qz, qa).astype(BF16)
        qs_ref[4 + z] = jnp.where(lane < DIFF_DK, qz, -qa).astype(BF16)

    def key_block(t):
        if t == 0:
            return qi, 0
        j = (t - 1) + jnp.asarray(qi <= t - 1, jnp.int32)
        return j, jnp.where(j < qi, 2, 4)

    def key_rows(t):
        j, _ = key_block(t)
        return pl.ds(pl.multiple_of(j * blk, blk), blk)

    part = blk // ROW_PARTS
    parts = [slice(r * part, (r + 1) * part) for r in range(ROW_PARTS)]
    for z, s_ref in enumerate(s_refs):
        for rows in parts:
            for t in range(nk):
                _, variant = key_block(t)
                s_ref[rows, t * blk:(t + 1) * blk] = lax.dot_general(
                    qs_ref[variant + z, rows, :], ka_ref[z, key_rows(t), :], NT_DIMS,
                    preferred_element_type=F32)

    rg = SOFTMAX_ROWS
    rel = (lax.broadcasted_iota(jnp.int32, (rg, blk), 0) - lax.broadcasted_iota(jnp.int32, (rg, blk), 1))
    for t in range(nk):
        vs_ref[t * blk:(t + 1) * blk, :] = va_ref[key_rows(t), :]

    acc = []
    for s_ref, p_ref in zip(s_refs, p_refs):
        for g in range(blk // rg):
            rows = slice(g * rg, (g + 1) * rg)
            s_diag = s_ref[rows, :blk] - slope * jnp.abs(rel + g * rg).astype(F32)
            top = s_diag
            for t in range(1, nk):
                top = jnp.maximum(top, s_ref[rows, t * blk:(t + 1) * blk])
            m = jnp.max(top, axis=-1, keepdims=True)
            p_ref[rows, :blk] = jnp.exp(s_diag - m).astype(BF16)
            for t in range(1, nk):
                cols = slice(t * blk, (t + 1) * blk)
                p_ref[rows, cols] = jnp.exp(s_ref[rows, cols] - m).astype(BF16)
        acc.append([jnp.dot(p_ref[rows, :], vs_ref[...], preferred_element_type=F32) for rows in parts])

    lam = (jnp.exp(jnp.sum(lq1_ref[...] * lk1_ref[...], axis=-1, keepdims=True))
           - jnp.exp(jnp.sum(lq2_ref[...] * lk2_ref[...], axis=-1, keepdims=True)) + lambda_init)
    for rows, a0, a1 in zip(parts, *acc):
        o = (a0[:, :DIFF_DV] / a0[:, DIFF_DV:DIFF_DV + 1]
             - lam * (a1[:, :DIFF_DV] / a1[:, DIFF_DV:DIFF_DV + 1]))
        o_ref[rows, :] = (_rms(o, sub_ref[...]) * (1.0 - lambda_init)).astype(o_ref.dtype)


def diff_attention(qkv, bsz, seq_len, lq1, lk1, lq2, lk2, sub_norm, lambda_init, blk=512):
    t = qkv.shape[0]
    nq = seq_len // blk
    assert blk // POS_SPLIT <= 256, "hi part of a block position must stay exact in bf16"
    slopes = jnp.asarray(2.0 ** (-8.0 * np.arange(1, DIFF_HEADS + 1, dtype=np.float32) / DIFF_HEADS), F32)
    vec = lambda a: a.astype(F32).reshape(1, DIFF_DK)
    const = lambda b, h, i: (0, 0)
    return pl.pallas_call(
        functools.partial(_diff_attn_body, blk=blk, seq_len=seq_len, lambda_init=lambda_init),
        out_shape=jax.ShapeDtypeStruct((t, DIFF_HEADS * DIFF_DV), BF16),
        grid=(bsz, DIFF_HEADS, nq),
        in_specs=[pl.BlockSpec(memory_space=pltpu.SMEM),
                  pl.BlockSpec((blk, 2 * DIFF_DK), lambda b, h, i: (b * nq + i, h)),
                  pl.BlockSpec((seq_len, 2 * DIFF_DK), lambda b, h, i: (b, DIFF_HEADS + h)),
                  pl.BlockSpec((seq_len, DIFF_DV), lambda b, h, i: (b, 2 * DIFF_HEADS + h)),
                  pl.BlockSpec((1, DIFF_DK), const), pl.BlockSpec((1, DIFF_DK), const),
                  pl.BlockSpec((1, DIFF_DK), const), pl.BlockSpec((1, DIFF_DK), const),
                  pl.BlockSpec((1, DIFF_DV), const)],
        out_specs=pl.BlockSpec((blk, DIFF_DV), lambda b, h, i: (b * nq + i, h)),
        scratch_shapes=[pltpu.VMEM((2, seq_len, 2 * DIFF_DK), BF16),
                        pltpu.VMEM((seq_len, 2 * DIFF_DV), BF16),
                        pltpu.VMEM((6, blk, 2 * DIFF_DK), BF16),
                        pltpu.VMEM((blk, seq_len), F32),
                        pltpu.VMEM((blk, seq_len), F32),
                        pltpu.VMEM((blk, seq_len), BF16),
                        pltpu.VMEM((blk, seq_len), BF16),
                        pltpu.VMEM((seq_len, 2 * DIFF_DV), BF16)],
        compiler_params=_params("parallel", "parallel", "arbitrary"),
        name="diff_attention",
    )(slopes, qkv, qkv, qkv, vec(lq1), vec(lk1), vec(lq2), vec(lk2),
      sub_norm.astype(F32).reshape(1, DIFF_DV))


def _pad_ev_w_in(w):
    return jnp.pad(w, ((0, 0), (0, EV_PAD_COLS - w.shape[1]))).astype(BF16)


def _trunk(x3, mem, p):
    bsz, seq_len, d = x3.shape
    x = x3.reshape(bsz * seq_len, d)
    kn_all, v_all = mem_kv(mem, p['norm_mem'], p['x_w_kv'], p['x_k_norm'])
    for layer in range(DEPTH):
        if layer % 2 == 0:
            e = layer // 2
            proj = norm_matmul(x, p['norm_mix'][layer], p['ev_w_in'][e], BF16, EV_TN)
            ys = s5_scan(proj[:, :S5_WIDTH], bsz, seq_len, p['s5_ops'][e])
            o_f, o_b = gla_scan(proj, bsz, seq_len, p['gla_w_gate'][e], p['gla_b_gate'][e])
            x = even_out(x, ys, o_f, o_b, proj, p['s5_w_glu'][e], p['gla_norm'][e], p['ev_w_out'][e])
        else:
            o = layer // 2
            lambda_init = 0.8 - 0.6 * math.exp(-0.3 * layer)
            qkv = qkv_project(x, p['norm_mix'][layer], p['od_w_in'][o],
                              p['diff_q_norm'][o], p['diff_k_norm'][o])
            att = diff_attention(qkv, bsz, seq_len, p['diff_lambda_q1'][o], p['diff_lambda_k1'][o],
                                 p['diff_lambda_q2'][o], p['diff_lambda_k2'][o], p['diff_norm'][o],
                                 lambda_init)
            x = matmul_residual(att, p['od_w_out'][o], x, tn=d)
        x = cross_block(x, seq_len, layer, p['norm_cross'][layer], p['x_w_q'][layer], p['x_q_norm'][layer],
                        kn_all, v_all, p['x_w_o'][layer])
        x = mlp_block(x, p['norm_mlp'][layer], p['mlp_w1'][layer], p['mlp_w2'][layer])
    return x.reshape(bsz, seq_len, d)


def kernel(x_prompt, x_sample, mem_prompt, mem_sample, norm_mix, norm_cross, norm_mem, norm_mlp,
           ev_w_in, ev_w_out, s5_lambda_re, s5_lambda_im, s5_log_step, s5_b_re, s5_b_im,
           s5_c_re, s5_c_im, s5_d, s5_w_glu, gla_w_gate, gla_b_gate, gla_norm,
           od_w_in, od_w_out, diff_q_norm, diff_k_norm, diff_lambda_q1, diff_lambda_k1,
           diff_lambda_q2, diff_lambda_k2, diff_norm, x_w_q, x_w_kv, x_w_o, x_q_norm, x_k_norm,
           mlp_w1, mlp_w2):
    n_even = ev_w_in.shape[0]
    bf = lambda w: w.astype(BF16)
    p = dict(
        norm_mix=norm_mix, norm_cross=norm_cross, norm_mem=norm_mem, norm_mlp=norm_mlp,
        ev_w_in=[_pad_ev_w_in(ev_w_in[e]) for e in range(n_even)], ev_w_out=bf(ev_w_out),
        s5_ops=[s5_operators(s5_lambda_re[e], s5_lambda_im[e], s5_log_step[e], s5_b_re[e], s5_b_im[e],
                             s5_c_re[e], s5_c_im[e], s5_d[e]) for e in range(n_even)],
        s5_w_glu=bf(s5_w_glu), gla_w_gate=gla_w_gate, gla_b_gate=gla_b_gate, gla_norm=gla_norm,
        od_w_in=bf(od_w_in), od_w_out=bf(od_w_out), diff_q_norm=diff_q_norm, diff_k_norm=diff_k_norm,
        diff_lambda_q1=diff_lambda_q1, diff_lambda_k1=diff_lambda_k1,
        diff_lambda_q2=diff_lambda_q2, diff_lambda_k2=diff_lambda_k2, diff_norm=diff_norm,
        x_w_q=bf(x_w_q), x_w_kv=bf(x_w_kv), x_w_o=bf(x_w_o), x_q_norm=x_q_norm, x_k_norm=x_k_norm,
        mlp_w1=bf(mlp_w1), mlp_w2=bf(mlp_w2))
    return (_trunk(x_prompt, mem_prompt, p), _trunk(x_sample, mem_sample, p))
```

```python
import functools
import math

import numpy as np
import jax
import jax.numpy as jnp
from jax import lax
from jax.experimental import pallas as pl
from jax.experimental.pallas import tpu as pltpu

F32 = jnp.float32
BF16 = jnp.bfloat16
HIGHEST = lax.Precision.HIGHEST

D_MODEL = 1024
DEPTH = 4
EPS = 1e-6
S5_WIDTH = 512
S5_GROUP = 16
S5_GROUPS = 32
S5_STATE = 64
S5_CHUNK = 64
GLA_HEADS = 4
GLA_DV = 128
GLA_DK = 64
GLA_RANK = 16
GLA_TAU = 16.0
GLA_CHUNK = 64
GLA_QK = GLA_HEADS * GLA_DK
GLA_V = GLA_HEADS * GLA_DV
EV_PAD_COLS = 2304
EV_TN = 768
DIFF_HEADS = 8
DIFF_DK = 64
DIFF_DV = 128
X_HEADS = 4
X_DH = 256
D_FF = 4096

ROW_TILE = 512
VMEM_LIMIT = 48 * 1024 * 1024

NT_DIMS = (((1,), (1,)), ((), ()))
TN_DIMS = (((0,), (0,)), ((), ()))


def _params(*sem, flags=None):
    return pltpu.CompilerParams(dimension_semantics=sem, vmem_limit_bytes=VMEM_LIMIT, flags=flags)


def _rms(x, gain):
    ms = jnp.mean(x * x, axis=-1, keepdims=True)
    return x * lax.rsqrt(ms + EPS) * gain


def _sigmoid(x):
    return 1.0 / (1.0 + jnp.exp(-x))


def _norm_matmul_body(x_ref, g_ref, w_ref, o_ref, *, tn):
    h = _rms(x_ref[...], g_ref[...]).astype(BF16)
    for j in range(w_ref.shape[1] // tn):
        cols = slice(j * tn, (j + 1) * tn)
        o_ref[:, cols] = jnp.dot(h, w_ref[:, cols], preferred_element_type=F32).astype(o_ref.dtype)


def norm_matmul(x, gain, w, out_dtype, tn, tm=ROW_TILE):
    t, d = x.shape
    n = w.shape[1]
    return pl.pallas_call(
        functools.partial(_norm_matmul_body, tn=tn),
        out_shape=jax.ShapeDtypeStruct((t, n), out_dtype),
        grid=(t // tm,),
        in_specs=[pl.BlockSpec((tm, d), lambda i: (i, 0)),
                  pl.BlockSpec((1, d), lambda i: (0, 0)),
                  pl.BlockSpec((d, n), lambda i: (0, 0))],
        out_specs=pl.BlockSpec((tm, n), lambda i: (i, 0)),
        compiler_params=_params("parallel"),
        name="norm_matmul",
    )(x, gain.reshape(1, d), w)


def _even_project_body(x_ref, g_ref, w_ref, wt_ref, o_ref, ot_ref, *, tn):
    h = _rms(x_ref[...], g_ref[...]).astype(BF16)
    for j in range(w_ref.shape[1] // tn):
        cols = slice(j * tn, (j + 1) * tn)
        o_ref[:, cols] = jnp.dot(h, w_ref[:, cols], preferred_element_type=F32).astype(o_ref.dtype)
    ot_ref[...] = lax.dot_general(wt_ref[...], h, NT_DIMS, preferred_element_type=F32).astype(ot_ref.dtype)


def even_project(x, gain, w, w_t, tn, tm=ROW_TILE):
    t, d = x.shape
    n = w.shape[1]
    m = w_t.shape[0]
    return pl.pallas_call(
        functools.partial(_even_project_body, tn=tn),
        out_shape=(jax.ShapeDtypeStruct((t, n), BF16), jax.ShapeDtypeStruct((m, t), BF16)),
        grid=(t // tm,),
        in_specs=[pl.BlockSpec((tm, d), lambda i: (i, 0)),
                  pl.BlockSpec((1, d), lambda i: (0, 0)),
                  pl.BlockSpec((d, n), lambda i: (0, 0)),
                  pl.BlockSpec((m, d), lambda i: (0, 0))],
        out_specs=(pl.BlockSpec((tm, n), lambda i: (i, 0)), pl.BlockSpec((m, tm), lambda i: (0, i))),
        compiler_params=_params("parallel"),
        name="even_project",
    )(x, gain.reshape(1, d), w, w_t)


def _matmul_res_body(a_ref, w_ref, r_ref, o_ref):
    o_ref[...] = r_ref[...] + jnp.dot(a_ref[...].astype(BF16), w_ref[...], preferred_element_type=F32)


def matmul_residual(a, w, res, tn=512, tm=ROW_TILE):
    t, k = a.shape
    n = w.shape[1]
    return pl.pallas_call(
        _matmul_res_body,
        out_shape=jax.ShapeDtypeStruct((t, n), F32),
        grid=(t // tm, n // tn),
        in_specs=[pl.BlockSpec((tm, k), lambda i, j: (i, 0)),
                  pl.BlockSpec((k, tn), lambda i, j: (0, j)),
                  pl.BlockSpec((tm, tn), lambda i, j: (i, j))],
        out_specs=pl.BlockSpec((tm, tn), lambda i, j: (i, j)),
        compiler_params=_params("parallel", "arbitrary"),
        name="matmul_residual",
    )(a, w, res)


def _mlp_body(x_ref, g_ref, w1_ref, w2_ref, o_ref, hid_ref, *, tf):
    x = x_ref[...]
    h = _rms(x, g_ref[...]).astype(BF16)
    for f in range(w1_ref.shape[1] // tf):
        cols = slice(f * tf, (f + 1) * tf)
        hid = jnp.dot(h, w1_ref[:, cols], preferred_element_type=F32)
        hid_ref[:, cols] = jnp.square(jnp.maximum(hid, 0.0)).astype(BF16)
    o_ref[...] = x + jnp.dot(hid_ref[...], w2_ref[...], preferred_element_type=F32)


def mlp_block(x, gain, w1, w2, tf=512, tm=ROW_TILE):
    t, d = x.shape
    ff = w1.shape[1]
    resident = lambda shape: pl.BlockSpec(shape, lambda i: (0, 0), pipeline_mode=pl.Buffered(1))
    return pl.pallas_call(
        functools.partial(_mlp_body, tf=tf),
        out_shape=jax.ShapeDtypeStruct((t, d), F32),
        grid=(t // tm,),
        in_specs=[pl.BlockSpec((tm, d), lambda i: (i, 0)),
                  pl.BlockSpec((1, d), lambda i: (0, 0)),
                  resident((d, ff)), resident((ff, d))],
        out_specs=pl.BlockSpec((tm, d), lambda i: (i, 0)),
        scratch_shapes=[pltpu.VMEM((tm, ff), BF16)],
        compiler_params=_params("parallel"),
        name="mlp_block",
    )(x, gain.reshape(1, d), w1, w2)


def _mem_kv_body(m_ref, g_ref, w_ref, kg_ref, k_ref, v_ref):
    h = _rms(m_ref[0], g_ref[0]).astype(BF16)
    kv = jnp.dot(h, w_ref[0], preferred_element_type=F32)
    for hd in range(X_HEADS):
        sl = slice(hd * X_DH, (hd + 1) * X_DH)
        k_ref[0, 0, :, sl] = _rms(kv[:, sl], kg_ref[0]).astype(BF16)
    v_ref[0, 0] = kv[:, D_MODEL:].astype(BF16)


def mem_kv(mem, norm_mem, w_kv, k_norm):
    bm, nm, d = mem.shape
    out = jax.ShapeDtypeStruct((DEPTH, bm, nm, d), BF16)
    return pl.pallas_call(
        _mem_kv_body,
        out_shape=(out, out),
        grid=(DEPTH, bm),
        in_specs=[pl.BlockSpec((1, nm, d), lambda l, b: (b, 0, 0)),
                  pl.BlockSpec((1, 1, d), lambda l, b: (l, 0, 0)),
                  pl.BlockSpec((1, d, 2 * d), lambda l, b: (l, 0, 0)),
                  pl.BlockSpec((1, 1, X_DH), lambda l, b: (l, 0, 0))],
        out_specs=(pl.BlockSpec((1, 1, nm, d), lambda l, b: (l, b, 0, 0)),
                   pl.BlockSpec((1, 1, nm, d), lambda l, b: (l, b, 0, 0))),
        compiler_params=_params("arbitrary", "arbitrary"),
        name="mem_kv",
    )(mem, norm_mem.reshape(DEPTH, 1, d), w_kv, k_norm.reshape(DEPTH, 1, X_DH))


def _cross_body(x_ref, g_ref, wq_ref, qg_ref, k_ref, v_ref, wo_ref, o_ref):
    x = x_ref[...]
    h = _rms(x, g_ref[...]).astype(BF16)
    q = jnp.dot(h, wq_ref[...], preferred_element_type=F32)
    heads = []
    for hd in range(X_HEADS):
        sl = slice(hd * X_DH, (hd + 1) * X_DH)
        qn = _rms(q[:, sl], qg_ref[...]).astype(BF16)
        s = lax.dot_general(qn, k_ref[0, 0, :, sl], NT_DIMS, preferred_element_type=F32)
        p = jnp.exp(s - jnp.max(s, axis=-1, keepdims=True))
        l = jnp.sum(p, axis=-1, keepdims=True)
        oh = jnp.dot(p.astype(BF16), v_ref[0, 0, :, sl], preferred_element_type=F32) / l
        heads.append(oh.astype(BF16))
    o = jnp.concatenate(heads, axis=-1)
    o_ref[...] = x + jnp.dot(o, wo_ref[...], preferred_element_type=F32)


def cross_block(x, seq_len, layer, gain, w_q, q_gain, kn, v, w_o, tm=ROW_TILE):
    t, d = x.shape
    nm = kn.shape[2]
    per_seq = seq_len // tm
    return pl.pallas_call(
        _cross_body,
        out_shape=jax.ShapeDtypeStruct((t, d), F32),
        grid=(t // tm,),
        in_specs=[pl.BlockSpec((tm, d), lambda i: (i, 0)),
                  pl.BlockSpec((1, d), lambda i: (0, 0)),
                  pl.BlockSpec((d, d), lambda i: (0, 0)),
                  pl.BlockSpec((1, X_DH), lambda i: (0, 0)),
                  pl.BlockSpec((1, 1, nm, d), lambda i: (layer, i // per_seq, 0, 0)),
                  pl.BlockSpec((1, 1, nm, d), lambda i: (layer, i // per_seq, 0, 0)),
                  pl.BlockSpec((d, d), lambda i: (0, 0))],
        out_specs=pl.BlockSpec((tm, d), lambda i: (i, 0)),
        compiler_params=_params("parallel"),
        name="cross_block",
    )(x, gain.reshape(1, d), w_q, (q_gain * X_DH ** -0.5).reshape(1, X_DH), kn, v, w_o)


LANES = 128


def _toeplitz_body(k_ref, o_ref):
    lc, gs = S5_CHUNK, S5_GROUP
    n = k_ref.shape[3]
    lane = lax.broadcasted_iota(jnp.int32, (lc, n), 1)
    for cp in range(gs):
        for pair in range(gs * lc // n):
            c = pair * (n // lc)
            lo = pltpu.roll(jnp.broadcast_to(k_ref[0, cp, c:c + 1, :], (lc, n)), lc + 1, 1,
                            stride=1, stride_axis=0)
            hi = pltpu.roll(jnp.broadcast_to(k_ref[0, cp, c + 1:c + 2, :], (lc, n)), 1, 1,
                            stride=1, stride_axis=0)
            o_ref[0, cp, :, pair * n:(pair + 1) * n] = jnp.where(lane < lc, lo, hi).astype(o_ref.dtype)


def toeplitz_expand(kern):
    g, gs, _, n = kern.shape
    lc = S5_CHUNK
    assert n == 2 * lc == LANES
    return pl.pallas_call(
        _toeplitz_body,
        out_shape=jax.ShapeDtypeStruct((g, gs, lc, gs * lc), BF16),
        grid=(g,),
        in_specs=[pl.BlockSpec((1, gs, gs, n), lambda i: (i, 0, 0, 0))],
        out_specs=pl.BlockSpec((1, gs, lc, gs * lc), lambda i: (i, 0, 0, 0)),
        compiler_params=_params("parallel"),
        name="toeplitz_expand",
    )(kern)


def s5_operators(lam_re, lam_im, log_step, b_re, b_im, c_re, c_im, d):
    lc = S5_CHUNK
    lam = lax.complex(lam_re.astype(F32), lam_im.astype(F32))
    step = jnp.exp(log_step.astype(F32))[..., None]
    lam_bar = jnp.exp(lam * step)
    b_bar = ((lam_bar - 1.0) / lam)[..., None] * lax.complex(b_re.astype(F32), b_im.astype(F32))
    c = lax.complex(c_re.astype(F32), c_im.astype(F32))
    pw = jnp.ones_like(lam_bar)[..., None]
    stride = lam_bar[..., None]
    while pw.shape[-1] < lc + 1:
        pw = jnp.concatenate([pw, pw * stride], axis=-1)
        stride = stride * stride
    pw = pw[..., :lc + 1]
    kern = jnp.einsum('zgcp,zgpt,zgpd->zgtcd', c, pw[..., :lc], b_bar, precision=HIGHEST).real
    kf, kb = kern[0], kern[1]
    k0 = kf[:, :1] + kb[:, :1] + (d.astype(F32)[:, :, None] * jnp.eye(S5_GROUP, dtype=F32))[:, None]
    kern_full = jnp.concatenate([kb[:, :0:-1], k0, kf[:, 1:]], axis=1)
    toep = toeplitz_expand(jnp.pad(kern_full.transpose(0, 3, 2, 1), ((0, 0), (0, 0), (0, 0), (0, 1))))

    def outer(x, y):
        return x[0] * y[0] - x[1] * y[1], x[0] * y[1] + x[1] * y[0]

    parts = lambda z: (z.real, z.imag)
    s_mid = lambda z: z.transpose(0, 2, 1)[:, None, :, :]
    b_first = lambda z: z.transpose(0, 2, 1)[:, :, None, :]
    pf = outer([s_mid(v) for v in parts(pw[0][..., lc - 1::-1])], [b_first(v) for v in parts(b_bar[0])])
    pb = outer([s_mid(v) for v in parts(pw[1][..., :lc])], [b_first(v) for v in parts(b_bar[1])])
    p_op = jnp.concatenate([pf[0], pb[0], pf[1], pb[1]], axis=-1)

    c_mid = lambda z: z.transpose(0, 2, 1)[:, :, :, None]
    t_last = lambda z: z[:, :, None, :]
    qf = outer([c_mid(v) for v in parts(c[0])], [t_last(v) for v in parts(pw[0][..., 1:])])
    qb = outer([c_mid(v) for v in parts(c[1])], [t_last(v) for v in parts(pw[1][..., :0:-1])])
    q_op = jnp.concatenate([qf[0], qb[0], -qf[1], -qb[1]], axis=1)
    q_op = q_op.reshape(S5_GROUPS, 4 * S5_STATE, S5_GROUP * lc)

    a = pw[..., lc]
    coef = jnp.stack([jnp.concatenate([a[0].real, a[1].real], -1),
                      jnp.concatenate([a[0].imag, a[1].imag], -1)], axis=1)
    return toep, _split_bf16(p_op), _split_bf16(q_op), coef


def _split_bf16(x):
    hi = x.astype(BF16)
    lo = (x - hi.astype(F32)).astype(BF16)
    return jnp.stack([hi, lo], axis=1)


CH_PACK = 4


def _s5_group_body(u_ref, t_ref, p_ref, q_ref, c_ref, y_ref,
                   v_re_ref, v_im_ref, xf_re_ref, xf_im_ref, xb_re_ref, xb_im_ref, *, n_chunks, bsz):
    half = 2 * S5_STATE
    lc = S5_CHUNK
    packs = S5_GROUP // CH_PACK
    u = [jnp.concatenate([u_ref[j * CH_PACK + i] for i in range(CH_PACK)], axis=-1) for j in range(packs)]
    packed = lambda op_ref, *lead: [op_ref[lead + (slice(j * CH_PACK, (j + 1) * CH_PACK),)]
                                    .reshape(CH_PACK * lc, -1) for j in range(packs)]
    v = None
    for term in range(2):
        for uj, pj in zip(u, packed(p_ref, 0, term)):
            part = jnp.dot(uj, pj, preferred_element_type=F32)
            v = part if v is None else v + part
    v_re_ref[...] = v[:, :half]
    v_im_ref[...] = v[:, half:]
    coef = c_ref[0]
    a_re, a_im = coef[0:1], coef[1:2]
    fwd = lax.broadcasted_iota(jnp.int32, (bsz, half), 1) < S5_STATE

    def step(i, carry):
        re, im = carry
        rows_f = pl.ds(i, bsz, stride=n_chunks)
        rows_b = pl.ds(n_chunks - 1 - i, bsz, stride=n_chunks)
        xf_re_ref[rows_f, :] = re
        xf_im_ref[rows_f, :] = im
        xb_re_ref[rows_b, :] = re
        xb_im_ref[rows_b, :] = im
        re_next = a_re * re - a_im * im + jnp.where(fwd, v_re_ref[rows_f, :], v_re_ref[rows_b, :])
        im_next = a_re * im + a_im * re + jnp.where(fwd, v_im_ref[rows_f, :], v_im_ref[rows_b, :])
        return re_next, im_next

    zero = jnp.zeros((bsz, half), F32)
    lax.fori_loop(0, n_chunks, step, (zero, zero))
    is_fwd = lax.broadcasted_iota(jnp.int32, xf_re_ref.shape, 1) < S5_STATE
    x = jnp.concatenate([jnp.where(is_fwd, xf_re_ref[...], xb_re_ref[...]),
                         jnp.where(is_fwd, xf_im_ref[...], xb_im_ref[...])], axis=1)
    x_hi = x.astype(BF16)
    x_lo = (x - x_hi.astype(F32)).astype(BF16)
    y = (jnp.dot(x_hi, q_ref[0, 0], preferred_element_type=F32)
         + jnp.dot(x_lo, q_ref[0, 0], preferred_element_type=F32)
         + jnp.dot(x_hi, q_ref[0, 1], preferred_element_type=F32))
    for uj, tj in zip(u, packed(t_ref, 0)):
        y = y + jnp.dot(uj, tj, preferred_element_type=F32)
    for c in range(S5_GROUP):
        y_ref[c] = y[:, c * lc:(c + 1) * lc].astype(y_ref.dtype)


def s5_scan(u_t, bsz, seq_len, ops):
    toep, p_op, q_op, coef = ops
    lc, g, gs, w = S5_CHUNK, S5_GROUPS, S5_GROUP, S5_CHUNK * S5_GROUP
    n = seq_len // lc
    c = bsz * n
    ns = 4 * S5_STATE
    group = lambda *shape: pl.BlockSpec((1,) + shape, lambda i: (i,) + (0,) * len(shape))
    channels = pl.BlockSpec((gs, c, lc), lambda i: (i, 0, 0))
    y = pl.pallas_call(
        functools.partial(_s5_group_body, n_chunks=n, bsz=bsz),
        out_shape=jax.ShapeDtypeStruct((g * gs, c, lc), BF16),
        grid=(g,),
        in_specs=[channels, group(gs, lc, w), group(2, gs, lc, ns), group(2, ns, w), group(2, ns // 2)],
        out_specs=channels,
        scratch_shapes=[pltpu.VMEM((c, ns // 2), F32)] * 6,
        compiler_params=_params("parallel"),
        name="s5_group",
    )(u_t.reshape(g * gs, c, lc), toep, p_op, q_op, coef)
    return y.reshape(g * gs, bsz * seq_len)


def _gla_direction(q_ref, k_ref, v_ref, g_ref, wg, bg, s_ref, o_ref, forward, chunks):
    cs = GLA_CHUNK
    tb = chunks * cs
    glr = g_ref[...]
    logit = (jnp.dot(glr, wg[0], preferred_element_type=F32)
             + jnp.dot(glr, wg[1], preferred_element_type=F32) + bg)
    g = (jnp.minimum(logit, 0.0) - jnp.log1p(jnp.exp(-jnp.abs(logit)))) / GLA_TAU
    row = lax.broadcasted_iota(jnp.int32, (tb, tb), 0)
    col = lax.broadcasted_iota(jnp.int32, (tb, tb), 1)
    same_chunk = (row // cs) == (col // cs)
    within = same_chunk & ((col <= row) if forward else (col >= row))
    ones = jnp.where(within, 1.0, 0.0).astype(BF16)
    bcum = jnp.zeros_like(g)
    rest = g
    for _ in range(3):
        term = rest.astype(BF16)
        bcum = bcum + jnp.dot(ones, term, preferred_element_type=F32)
        rest = rest - term.astype(F32)

    srow = lax.broadcasted_iota(jnp.int32, (GLA_HEADS * cs, cs), 0) & (cs - 1)
    scol = lax.broadcasted_iota(jnp.int32, (GLA_HEADS * cs, cs), 1)
    keep = (scol <= srow) if forward else (scol >= srow)
    lane = lax.broadcasted_iota(jnp.int32, (1, GLA_QK), 1)
    head_lanes = [((lane >= h * GLA_DK) & (lane < (h + 1) * GLA_DK)).astype(F32) for h in range(GLA_HEADS)]
    state_mask = (lax.broadcasted_iota(jnp.int32, (GLA_V, GLA_QK), 0) // GLA_DV
                  == lax.broadcasted_iota(jnp.int32, (GLA_V, GLA_QK), 1) // GLA_DK)

    i_ref = cs // 2 - 1 if forward else cs // 2
    i_last = cs - 1 if forward else 0
    scale = GLA_DK ** -0.5
    for ci in (range(chunks) if forward else reversed(range(chunks))):
        rows = slice(ci * cs, (ci + 1) * cs)
        b = bcum[rows]
        bref = b[i_ref:i_ref + 1]
        blast = b[i_last:i_last + 1]
        q = q_ref[rows, :].astype(F32) * scale
        k = k_ref[rows, :].astype(F32)
        v = v_ref[rows, :]
        q_rel = q * jnp.exp(b - bref)
        k_rel = (k * jnp.exp(bref - b)).astype(BF16)
        k_out = (k * jnp.exp(blast - b)).astype(BF16)
        q_dec = (q * jnp.exp(b)).astype(BF16)
        decay = jnp.exp(blast)
        q_heads = jnp.concatenate([q_rel * hm for hm in head_lanes], axis=0).astype(BF16)
        s = lax.dot_general(q_heads, k_rel, NT_DIMS, preferred_element_type=F32)
        s = jnp.where(keep, s, 0.0).astype(BF16)
        st = s_ref[...]
        o_inter = lax.dot_general(q_dec, st.astype(BF16), NT_DIMS, preferred_element_type=F32)
        o_intra = [jnp.dot(s[h * cs:(h + 1) * cs], v[:, h * GLA_DV:(h + 1) * GLA_DV],
                           preferred_element_type=F32) for h in range(GLA_HEADS)]
        o_ref[rows, :] = o_inter + jnp.concatenate(o_intra, axis=-1)
        kv = lax.dot_general(v, k_out, TN_DIMS, preferred_element_type=F32)
        s_ref[...] = decay * st + jnp.where(state_mask, kv, 0.0)


def _gla_body(qf_ref, kf_ref, vf_ref, gf_ref, qb_ref, kb_ref, vb_ref, gb_ref, wg_ref, bg_ref,
              of_ref, ob_ref, sf_ref, sb_ref, *, chunks):
    @pl.when(pl.program_id(1) == 0)
    def _():
        sf_ref[...] = jnp.zeros_like(sf_ref)
        sb_ref[...] = jnp.zeros_like(sb_ref)

    _gla_direction(qf_ref, kf_ref, vf_ref, gf_ref, wg_ref[0], bg_ref[0], sf_ref, of_ref, True, chunks)
    _gla_direction(qb_ref, kb_ref, vb_ref, gb_ref, wg_ref[1], bg_ref[1], sb_ref, ob_ref, False, chunks)


def gla_scan(proj, bsz, seq_len, w_gate, b_gate, chunks=4):
    t = proj.shape[0]
    tb = chunks * GLA_CHUNK
    nb = seq_len // tb
    fwd = lambda cb: (lambda b, i: (b * nb + i, cb))
    bwd = lambda cb: (lambda b, i: (b * nb + nb - 1 - i, cb))
    qc, kc, vc, gc = 512 // GLA_QK, 768 // GLA_QK, 1024 // GLA_V, 2048 // 128
    wg = jnp.zeros((2, 128, GLA_QK), F32)
    wg = wg.at[0, :GLA_RANK].set(w_gate[0].astype(F32)).at[1, GLA_RANK:2 * GLA_RANK].set(w_gate[1].astype(F32))
    wg = _split_bf16(wg)
    out = jax.ShapeDtypeStruct((t, GLA_V), F32)
    state = pltpu.VMEM((GLA_V, GLA_QK), F32)
    return pl.pallas_call(
        functools.partial(_gla_body, chunks=chunks),
        out_shape=(out, out),
        grid=(bsz, nb),
        in_specs=[pl.BlockSpec((tb, GLA_QK), fwd(qc)), pl.BlockSpec((tb, GLA_QK), fwd(kc)),
                  pl.BlockSpec((tb, GLA_V), fwd(vc)), pl.BlockSpec((tb, 128), fwd(gc)),
                  pl.BlockSpec((tb, GLA_QK), bwd(qc)), pl.BlockSpec((tb, GLA_QK), bwd(kc)),
                  pl.BlockSpec((tb, GLA_V), bwd(vc)), pl.BlockSpec((tb, 128), bwd(gc)),
                  pl.BlockSpec((2, 2, 128, GLA_QK), lambda b, i: (0, 0, 0, 0)),
                  pl.BlockSpec((2, 1, GLA_QK), lambda b, i: (0, 0, 0))],
        out_specs=(pl.BlockSpec((tb, GLA_V), fwd(0)), pl.BlockSpec((tb, GLA_V), bwd(0))),
        scratch_shapes=[state, state],
        compiler_params=_params("parallel", "arbitrary"),
        name="gla_scan",
    )(proj, proj, proj, proj, proj, proj, proj, proj, wg, b_gate.astype(F32).reshape(2, 1, GLA_QK))


def _even_out_body(x_ref, ys_ref, of_ref, ob_ref, og_ref, wglu_t_ref, gn_ref, wtop_ref, wbot_ref, o_ref):
    y = ys_ref[...].astype(F32)
    y = 0.5 * y * (1.0 + jnp.tanh(math.sqrt(2.0 / math.pi) * (y + 0.044715 * (y * y * y))))
    gate = jnp.dot(wglu_t_ref[...], y.astype(BF16), preferred_element_type=F32)
    y = y * _sigmoid(gate)
    o = of_ref[...] + ob_ref[...]
    og = og_ref[...].astype(F32)
    heads = []
    for h in range(GLA_HEADS):
        sl = slice(h * GLA_DV, (h + 1) * GLA_DV)
        heads.append(_rms(o[:, sl], gn_ref[...]))
    o = jnp.concatenate(heads, axis=-1) * (og * _sigmoid(og))
    o_ref[...] = (x_ref[...]
                  + lax.dot_general(y.astype(BF16), wtop_ref[...], TN_DIMS, preferred_element_type=F32)
                  + jnp.dot(o.astype(BF16), wbot_ref[...], preferred_element_type=F32))


def even_out(x, ys_t, o_f, o_b, proj, w_glu_t, gla_norm, w_out, tm=ROW_TILE):
    t, d = x.shape
    row = lambda i: (i, 0)
    const = lambda i: (0, 0)
    return pl.pallas_call(
        _even_out_body,
        out_shape=jax.ShapeDtypeStruct((t, d), F32),
        grid=(t // tm,),
        in_specs=[pl.BlockSpec((tm, d), row), pl.BlockSpec((S5_WIDTH, tm), lambda i: (0, i)),
                  pl.BlockSpec((tm, GLA_V), row), pl.BlockSpec((tm, GLA_V), row),
                  pl.BlockSpec((tm, GLA_V), lambda i: (i, 1536 // GLA_V)),
                  pl.BlockSpec((S5_WIDTH, S5_WIDTH), const), pl.BlockSpec((1, GLA_DV), const),
                  pl.BlockSpec((S5_WIDTH, d), const), pl.BlockSpec((GLA_V, d), lambda i: (1, 0))],
        out_specs=pl.BlockSpec((tm, d), row),
        compiler_params=_params("parallel"),
        name="even_out",
    )(x, ys_t, o_f, o_b, proj, w_glu_t, gla_norm.astype(F32).reshape(1, GLA_DV),
      w_out, w_out)


QK_NORM_WIDTH = 256


def _seg_rms(a, gain, same_seg):
    ssq = jnp.dot((a * a).astype(BF16), same_seg, preferred_element_type=F32)
    return a * lax.rsqrt(ssq * (1.0 / DIFF_DK) + EPS) * gain


def _qkv_body(x_ref, g_ref, w_ref, qk_gain_ref, o_ref):
    d = x_ref.shape[1]
    h = _rms(x_ref[...], g_ref[...]).astype(BF16)
    w = QK_NORM_WIDTH
    same_seg = jnp.where(lax.broadcasted_iota(jnp.int32, (w, w), 0) // DIFF_DK
                         == lax.broadcasted_iota(jnp.int32, (w, w), 1) // DIFF_DK, 1.0, 0.0).astype(BF16)
    for part in range(2):
        acc = jnp.dot(h, w_ref[:, part * d:(part + 1) * d], preferred_element_type=F32)
        gain = qk_gain_ref[part]
        for c in range(d // w):
            sl = slice(c * w, (c + 1) * w)
            o_ref[:, part * d + c * w:part * d + (c + 1) * w] = (
                _seg_rms(acc[:, sl], gain, same_seg).astype(o_ref.dtype))
    o_ref[:, 2 * d:] = jnp.dot(h, w_ref[:, 2 * d:], preferred_element_type=F32).astype(o_ref.dtype)


def qkv_project(x, gain, w, q_norm, k_norm, tm=ROW_TILE):
    t, d = x.shape
    reps = QK_NORM_WIDTH // DIFF_DK
    qg = jnp.tile(q_norm.astype(F32) * DIFF_DK ** -0.5, reps)
    kg = jnp.tile(k_norm.astype(F32), reps)
    qk_gain = jnp.stack([qg, kg]).reshape(2, 1, QK_NORM_WIDTH)
    return pl.pallas_call(
        _qkv_body,
        out_shape=jax.ShapeDtypeStruct((t, 3 * d), BF16),
        grid=(t // tm,),
        in_specs=[pl.BlockSpec((tm, d), lambda i: (i, 0)),
                  pl.BlockSpec((1, d), lambda i: (0, 0)),
                  pl.BlockSpec((d, 3 * d), lambda i: (0, 0)),
                  pl.BlockSpec((2, 1, QK_NORM_WIDTH), lambda i: (0, 0, 0))],
        out_specs=pl.BlockSpec((tm, 3 * d), lambda i: (i, 0)),
        compiler_params=_params("parallel"),
        name="qkv_project",
    )(x, gain.reshape(1, d), w, qk_gain)


POS_SPLIT = 16
POS_SHIFT = 4
AUG_LANE = DIFF_DK
SOFTMAX_ROWS = 16
ROW_PARTS = 2


def _pos_terms(shape):
    lane = lax.broadcasted_iota(jnp.int32, shape, 1)
    pos = lax.broadcasted_iota(jnp.int32, shape, 0)
    hi = lax.shift_right_logical(pos, POS_SHIFT).astype(F32)
    lo = (pos & (POS_SPLIT - 1)).astype(F32)
    return lane, hi, lo


def _lane_select(lane, first, values):
    out = 0.0
    for n, val in reversed(list(enumerate(values))):
        out = jnp.where(lane == first + n, val, out)
    return out


def _diff_attn_body(slope_ref, q_ref, k_ref, v_ref, lq1_ref, lk1_ref, lq2_ref, lk2_ref, sub_ref,
                    o_ref, ka_ref, va_ref, qs_ref, s0_ref, s1_ref, p0_ref, p1_ref, vs_ref,
                    *, blk, seq_len, lambda_init):
    h = pl.program_id(1)
    qi = pl.program_id(2)
    slope = slope_ref[h]
    nk = seq_len // blk
    s_refs, p_refs = (s0_ref, s1_ref), (p0_ref, p1_ref)

    @pl.when(qi == 0)
    def _():
        va_ref[:, :DIFF_DV] = v_ref[...]
        va_ref[:, DIFF_DV:] = jnp.ones((seq_len, DIFF_DV), BF16)

        def build(t, carry):
            rows = pl.ds(pl.multiple_of(t * blk, blk), blk)
            kf = k_ref[rows, :].astype(F32)
            lane, hi, lo = _pos_terms(kf.shape)
            aug = _lane_select(lane, AUG_LANE,
                               [-blk * slope, -POS_SPLIT * slope, -slope,
                                (blk * slope) * jnp.asarray(t, F32), (POS_SPLIT * slope) * hi, slope * lo])
            for z, kz in enumerate((kf, pltpu.roll(kf, DIFF_DK, 1))):
                ka_ref[z, rows, :] = jnp.where(lane < DIFF_DK, kz, aug).astype(BF16)
            return carry

        lax.fori_loop(0, nk, build, 0)

    qf = q_ref[...].astype(F32)
    lane, hi, lo = _pos_terms(qf.shape)
    qa = _lane_select(lane, AUG_LANE, [jnp.asarray(qi, F32), hi, lo, 1.0, 1.0, 1.0])
    for z, qz in enumerate((qf, pltpu.roll(qf, DIFF_DK, 1))):
        qs_ref[z] = jnp.where(lane < DIFF_DK, qz, 0.0).astype(BF16)
        qs_ref[2 + z] = jnp.where(lane < DIFF_DK, qz, qa).astype(BF16)
        qs_ref[4 + z] = jnp.where(lane < DIFF_DK, qz, -qa).astype(BF16)

    def key_block(t):
        if t == 0:
            return qi, 0
        j = (t - 1) + jnp.asarray(qi <= t - 1, jnp.int32)
        return j, jnp.where(j < qi, 2, 4)

    def key_rows(t):
        j, _ = key_block(t)
        return pl.ds(pl.multiple_of(j * blk, blk), blk)

    part = blk // ROW_PARTS
    parts = [slice(r * part, (r + 1) * part) for r in range(ROW_PARTS)]
    for z, s_ref in enumerate(s_refs):
        for rows in parts:
            for t in range(nk):
                _, variant = key_block(t)
                s_ref[rows, t * blk:(t + 1) * blk] = lax.dot_general(
                    qs_ref[variant + z, rows, :], ka_ref[z, key_rows(t), :], NT_DIMS,
                    preferred_element_type=F32)

    rg = SOFTMAX_ROWS
    rel = (lax.broadcasted_iota(jnp.int32, (rg, blk), 0) - lax.broadcasted_iota(jnp.int32, (rg, blk), 1))
    for t in range(nk):
        vs_ref[t * blk:(t + 1) * blk, :] = va_ref[key_rows(t), :]

    acc = []
    for s_ref, p_ref in zip(s_refs, p_refs):
        for g in range(blk // rg):
            rows = slice(g * rg, (g + 1) * rg)
            s_diag = s_ref[rows, :blk] - slope * jnp.abs(rel + g * rg).astype(F32)
            top = s_diag
            for t in range(1, nk):
                top = jnp.maximum(top, s_ref[rows, t * blk:(t + 1) * blk])
            m = jnp.max(top, axis=-1, keepdims=True)
            p_ref[rows, :blk] = jnp.exp(s_diag - m).astype(BF16)
            for t in range(1, nk):
                cols = slice(t * blk, (t + 1) * blk)
                p_ref[rows, cols] = jnp.exp(s_ref[rows, cols] - m).astype(BF16)
        acc.append([jnp.dot(p_ref[rows, :], vs_ref[...], preferred_element_type=F32) for rows in parts])

    lam = (jnp.exp(jnp.sum(lq1_ref[...] * lk1_ref[...], axis=-1, keepdims=True))
           - jnp.exp(jnp.sum(lq2_ref[...] * lk2_ref[...], axis=-1, keepdims=True)) + lambda_init)
    for rows, a0, a1 in zip(parts, *acc):
        o = (a0[:, :DIFF_DV] / a0[:, DIFF_DV:DIFF_DV + 1]
             - lam * (a1[:, :DIFF_DV] / a1[:, DIFF_DV:DIFF_DV + 1]))
        o_ref[rows, :] = (_rms(o, sub_ref[...]) * (1.0 - lambda_init)).astype(o_ref.dtype)


def diff_attention(qkv, bsz, seq_len, lq1, lk1, lq2, lk2, sub_norm, lambda_init, blk=512):
    t = qkv.shape[0]
    nq = seq_len // blk
    assert blk // POS_SPLIT <= 256, "hi part of a block position must stay exact in bf16"
    slopes = jnp.asarray(2.0 ** (-8.0 * np.arange(1, DIFF_HEADS + 1, dtype=np.float32) / DIFF_HEADS), F32)
    vec = lambda a: a.astype(F32).reshape(1, DIFF_DK)
    const = lambda b, h, i: (0, 0)
    return pl.pallas_call(
        functools.partial(_diff_attn_body, blk=blk, seq_len=seq_len, lambda_init=lambda_init),
        out_shape=jax.ShapeDtypeStruct((t, DIFF_HEADS * DIFF_DV), BF16),
        grid=(bsz, DIFF_HEADS, nq),
        in_specs=[pl.BlockSpec(memory_space=pltpu.SMEM),
                  pl.BlockSpec((blk, 2 * DIFF_DK), lambda b, h, i: (b * nq + i, h)),
                  pl.BlockSpec((seq_len, 2 * DIFF_DK), lambda b, h, i: (b, DIFF_HEADS + h)),
                  pl.BlockSpec((seq_len, DIFF_DV), lambda b, h, i: (b, 2 * DIFF_HEADS + h)),
                  pl.BlockSpec((1, DIFF_DK), const), pl.BlockSpec((1, DIFF_DK), const),
                  pl.BlockSpec((1, DIFF_DK), const), pl.BlockSpec((1, DIFF_DK), const),
                  pl.BlockSpec((1, DIFF_DV), const)],
        out_specs=pl.BlockSpec((blk, DIFF_DV), lambda b, h, i: (b * nq + i, h)),
        scratch_shapes=[pltpu.VMEM((2, seq_len, 2 * DIFF_DK), BF16),
                        pltpu.VMEM((seq_len, 2 * DIFF_DV), BF16),
                        pltpu.VMEM((6, blk, 2 * DIFF_DK), BF16),
                        pltpu.VMEM((blk, seq_len), F32),
                        pltpu.VMEM((blk, seq_len), F32),
                        pltpu.VMEM((blk, seq_len), BF16),
                        pltpu.VMEM((blk, seq_len), BF16),
                        pltpu.VMEM((seq_len, 2 * DIFF_DV), BF16)],
        compiler_params=_params("parallel", "parallel", "arbitrary"),
        name="diff_attention",
    )(slopes, qkv, qkv, qkv, vec(lq1), vec(lk1), vec(lq2), vec(lk2),
      sub_norm.astype(F32).reshape(1, DIFF_DV))


def _pad_ev_w_in(w):
    return jnp.pad(w, ((0, 0), (0, EV_PAD_COLS - w.shape[1]))).astype(BF16)


def _trunk(x3, mem, p):
    bsz, seq_len, d = x3.shape
    x = x3.reshape(bsz * seq_len, d)
    kn_all, v_all = mem_kv(mem, p['norm_mem'], p['x_w_kv'], p['x_k_norm'])
    for layer in range(DEPTH):
        if layer % 2 == 0:
            e = layer // 2
            proj, u_t = even_project(x, p['norm_mix'][layer], p['ev_w_in'][e], p['ev_w_u_t'][e], EV_TN)
            ys_t = s5_scan(u_t, bsz, seq_len, p['s5_ops'][e])
            o_f, o_b = gla_scan(proj, bsz, seq_len, p['gla_w_gate'][e], p['gla_b_gate'][e])
            x = even_out(x, ys_t, o_f, o_b, proj, p['s5_w_glu_t'][e], p['gla_norm'][e], p['ev_w_out'][e])
        else:
            o = layer // 2
            lambda_init = 0.8 - 0.6 * math.exp(-0.3 * layer)
            qkv = qkv_project(x, p['norm_mix'][layer], p['od_w_in'][o],
                              p['diff_q_norm'][o], p['diff_k_norm'][o])
            att = diff_attention(qkv, bsz, seq_len, p['diff_lambda_q1'][o], p['diff_lambda_k1'][o],
                                 p['diff_lambda_q2'][o], p['diff_lambda_k2'][o], p['diff_norm'][o],
                                 lambda_init)
            x = matmul_residual(att, p['od_w_out'][o], x, tn=d)
        x = cross_block(x, seq_len, layer, p['norm_cross'][layer], p['x_w_q'][layer], p['x_q_norm'][layer],
                        kn_all, v_all, p['x_w_o'][layer])
        x = mlp_block(x, p['norm_mlp'][layer], p['mlp_w1'][layer], p['mlp_w2'][layer])
    return x.reshape(bsz, seq_len, d)


def kernel(x_prompt, x_sample, mem_prompt, mem_sample, norm_mix, norm_cross, norm_mem, norm_mlp,
           ev_w_in, ev_w_out, s5_lambda_re, s5_lambda_im, s5_log_step, s5_b_re, s5_b_im,
           s5_c_re, s5_c_im, s5_d, s5_w_glu, gla_w_gate, gla_b_gate, gla_norm,
           od_w_in, od_w_out, diff_q_norm, diff_k_norm, diff_lambda_q1, diff_lambda_k1,
           diff_lambda_q2, diff_lambda_k2, diff_norm, x_w_q, x_w_kv, x_w_o, x_q_norm, x_k_norm,
           mlp_w1, mlp_w2):
    n_even = ev_w_in.shape[0]
    bf = lambda w: w.astype(BF16)
    p = dict(
        norm_mix=norm_mix, norm_cross=norm_cross, norm_mem=norm_mem, norm_mlp=norm_mlp,
        ev_w_in=[_pad_ev_w_in(ev_w_in[e]) for e in range(n_even)], ev_w_out=bf(ev_w_out),
        s5_ops=[s5_operators(s5_lambda_re[e], s5_lambda_im[e], s5_log_step[e], s5_b_re[e], s5_b_im[e],
                             s5_c_re[e], s5_c_im[e], s5_d[e]) for e in range(n_even)],
        s5_w_glu_t=bf(jnp.swapaxes(s5_w_glu, 1, 2)),
        ev_w_u_t=bf(jnp.swapaxes(ev_w_in[:, :, :S5_WIDTH], 1, 2)), gla_w_gate=gla_w_gate, gla_b_gate=gla_b_gate, gla_norm=gla_norm,
        od_w_in=bf(od_w_in), od_w_out=bf(od_w_out), diff_q_norm=diff_q_norm, diff_k_norm=diff_k_norm,
        diff_lambda_q1=diff_lambda_q1, diff_lambda_k1=diff_lambda_k1,
        diff_lambda_q2=diff_lambda_q2, diff_lambda_k2=diff_lambda_k2, diff_norm=diff_norm,
        x_w_q=bf(x_w_q), x_w_kv=bf(x_w_kv), x_w_o=bf(x_w_o), x_q_norm=x_q_norm, x_k_norm=x_k_norm,
        mlp_w1=bf(mlp_w1), mlp_w2=bf(mlp_w2))
    return (_trunk(x_prompt, mem_prompt, p), _trunk(x_sample, mem_sample, p))
```

```python
import functools
import math

import numpy as np
import jax
import jax.numpy as jnp
from jax import lax
from jax.experimental import pallas as pl
from jax.experimental.pallas import tpu as pltpu

F32 = jnp.float32
BF16 = jnp.bfloat16
HIGHEST = lax.Precision.HIGHEST

D_MODEL = 1024
DEPTH = 4
EPS = 1e-6
S5_WIDTH = 512
S5_GROUP = 16
S5_GROUPS = 32
S5_STATE = 64
S5_CHUNK = 64
GLA_HEADS = 4
GLA_DV = 128
GLA_DK = 64
GLA_RANK = 16
GLA_TAU = 16.0
GLA_CHUNK = 64
GLA_QK = GLA_HEADS * GLA_DK
GLA_V = GLA_HEADS * GLA_DV
EV_Q, EV_K, EV_VAL, EV_OG, EV_GLR = 0, 256, 512, 1024, 1536
EV_PAD_COLS = 1664
EV_TN = 1664
DIFF_HEADS = 8
DIFF_DK = 64
DIFF_DV = 128
X_HEADS = 4
X_DH = 256
D_FF = 4096

ROW_TILE = 512
VMEM_LIMIT = 48 * 1024 * 1024

NT_DIMS = (((1,), (1,)), ((), ()))
TN_DIMS = (((0,), (0,)), ((), ()))


def _params(*sem, flags=None):
    return pltpu.CompilerParams(dimension_semantics=sem, vmem_limit_bytes=VMEM_LIMIT, flags=flags)


def _rms(x, gain):
    ms = jnp.mean(x * x, axis=-1, keepdims=True)
    return x * lax.rsqrt(ms + EPS) * gain


def _sigmoid(x):
    return 1.0 / (1.0 + jnp.exp(-x))


def _norm_matmul_body(x_ref, g_ref, w_ref, o_ref, *, tn):
    h = _rms(x_ref[...], g_ref[...]).astype(BF16)
    for j in range(w_ref.shape[1] // tn):
        cols = slice(j * tn, (j + 1) * tn)
        o_ref[:, cols] = jnp.dot(h, w_ref[:, cols], preferred_element_type=F32).astype(o_ref.dtype)


def norm_matmul(x, gain, w, out_dtype, tn, tm=ROW_TILE):
    t, d = x.shape
    n = w.shape[1]
    return pl.pallas_call(
        functools.partial(_norm_matmul_body, tn=tn),
        out_shape=jax.ShapeDtypeStruct((t, n), out_dtype),
        grid=(t // tm,),
        in_specs=[pl.BlockSpec((tm, d), lambda i: (i, 0)),
                  pl.BlockSpec((1, d), lambda i: (0, 0)),
                  pl.BlockSpec((d, n), lambda i: (0, 0))],
        out_specs=pl.BlockSpec((tm, n), lambda i: (i, 0)),
        compiler_params=_params("parallel"),
        name="norm_matmul",
    )(x, gain.reshape(1, d), w)


def _even_project_body(x_ref, g_ref, w_ref, wt_ref, o_ref, ot_ref, *, tn):
    h = _rms(x_ref[...], g_ref[...]).astype(BF16)
    for j in range(w_ref.shape[1] // tn):
        cols = slice(j * tn, (j + 1) * tn)
        o_ref[:, cols] = jnp.dot(h, w_ref[:, cols], preferred_element_type=F32).astype(o_ref.dtype)
    ot_ref[...] = lax.dot_general(wt_ref[...], h, NT_DIMS, preferred_element_type=F32).astype(ot_ref.dtype)


def even_project(x, gain, w, w_t, tn, tm=ROW_TILE):
    t, d = x.shape
    n = w.shape[1]
    m = w_t.shape[0]
    return pl.pallas_call(
        functools.partial(_even_project_body, tn=tn),
        out_shape=(jax.ShapeDtypeStruct((t, n), BF16), jax.ShapeDtypeStruct((m, t), BF16)),
        grid=(t // tm,),
        in_specs=[pl.BlockSpec((tm, d), lambda i: (i, 0)),
                  pl.BlockSpec((1, d), lambda i: (0, 0)),
                  pl.BlockSpec((d, n), lambda i: (0, 0)),
                  pl.BlockSpec((m, d), lambda i: (0, 0))],
        out_specs=(pl.BlockSpec((tm, n), lambda i: (i, 0)), pl.BlockSpec((m, tm), lambda i: (0, i))),
        compiler_params=_params("parallel"),
        name="even_project",
    )(x, gain.reshape(1, d), w, w_t)


def _mlp_body(x_ref, g_ref, w1_ref, w2_ref, o_ref, hid_ref, *, tf):
    x = x_ref[...]
    h = _rms(x, g_ref[...]).astype(BF16)
    for f in range(w1_ref.shape[1] // tf):
        cols = slice(f * tf, (f + 1) * tf)
        hid = jnp.dot(h, w1_ref[:, cols], preferred_element_type=F32)
        hid_ref[:, cols] = jnp.square(jnp.maximum(hid, 0.0)).astype(BF16)
    o_ref[...] = x + jnp.dot(hid_ref[...], w2_ref[...], preferred_element_type=F32)


def mlp_block(x, gain, w1, w2, tf=512, tm=ROW_TILE):
    t, d = x.shape
    ff = w1.shape[1]
    resident = lambda shape: pl.BlockSpec(shape, lambda i: (0, 0), pipeline_mode=pl.Buffered(1))
    return pl.pallas_call(
        functools.partial(_mlp_body, tf=tf),
        out_shape=jax.ShapeDtypeStruct((t, d), F32),
        grid=(t // tm,),
        in_specs=[pl.BlockSpec((tm, d), lambda i: (i, 0)),
                  pl.BlockSpec((1, d), lambda i: (0, 0)),
                  resident((d, ff)), resident((ff, d))],
        out_specs=pl.BlockSpec((tm, d), lambda i: (i, 0)),
        scratch_shapes=[pltpu.VMEM((tm, ff), BF16)],
        compiler_params=_params("parallel"),
        name="mlp_block",
    )(x, gain.reshape(1, d), w1, w2)


def _mem_kv_body(m_ref, g_ref, w_ref, kg_ref, k_ref, v_ref):
    h = _rms(m_ref[0], g_ref[0]).astype(BF16)
    kv = jnp.dot(h, w_ref[0], preferred_element_type=F32)
    for hd in range(X_HEADS):
        sl = slice(hd * X_DH, (hd + 1) * X_DH)
        k_ref[0, 0, :, sl] = _rms(kv[:, sl], kg_ref[0]).astype(BF16)
    v_ref[0, 0] = kv[:, D_MODEL:].astype(BF16)


def mem_kv(mem, norm_mem, w_kv, k_norm):
    bm, nm, d = mem.shape
    out = jax.ShapeDtypeStruct((DEPTH, bm, nm, d), BF16)
    return pl.pallas_call(
        _mem_kv_body,
        out_shape=(out, out),
        grid=(DEPTH, bm),
        in_specs=[pl.BlockSpec((1, nm, d), lambda l, b: (b, 0, 0)),
                  pl.BlockSpec((1, 1, d), lambda l, b: (l, 0, 0)),
                  pl.BlockSpec((1, d, 2 * d), lambda l, b: (l, 0, 0)),
                  pl.BlockSpec((1, 1, X_DH), lambda l, b: (l, 0, 0))],
        out_specs=(pl.BlockSpec((1, 1, nm, d), lambda l, b: (l, b, 0, 0)),
                   pl.BlockSpec((1, 1, nm, d), lambda l, b: (l, b, 0, 0))),
        compiler_params=_params("arbitrary", "arbitrary"),
        name="mem_kv",
    )(mem, norm_mem.reshape(DEPTH, 1, d), w_kv, k_norm.reshape(DEPTH, 1, X_DH))


def _cross_body(x_ref, g_ref, wq_ref, qg_ref, k_ref, v_ref, wo_ref, *rest):
    o_ref = rest[-1]
    x = x_ref[...]
    if len(rest) == 3:
        x = x + jnp.dot(rest[0][...], rest[1][...], preferred_element_type=F32)
    h = _rms(x, g_ref[...]).astype(BF16)
    q = jnp.dot(h, wq_ref[...], preferred_element_type=F32)
    heads = []
    for hd in range(X_HEADS):
        sl = slice(hd * X_DH, (hd + 1) * X_DH)
        qn = _rms(q[:, sl], qg_ref[...]).astype(BF16)
        s = lax.dot_general(qn, k_ref[0, 0, :, sl], NT_DIMS, preferred_element_type=F32)
        p = jnp.exp(s - jnp.max(s, axis=-1, keepdims=True))
        l = jnp.sum(p, axis=-1, keepdims=True)
        oh = jnp.dot(p.astype(BF16), v_ref[0, 0, :, sl], preferred_element_type=F32) / l
        heads.append(oh.astype(BF16))
    o = jnp.concatenate(heads, axis=-1)
    o_ref[...] = x + jnp.dot(o, wo_ref[...], preferred_element_type=F32)


def cross_block(x, seq_len, layer, gain, w_q, q_gain, kn, v, w_o, mixer_out=None, tm=ROW_TILE):
    t, d = x.shape
    nm = kn.shape[2]
    per_seq = seq_len // tm
    operands = [x, gain.reshape(1, d), w_q, (q_gain * X_DH ** -0.5).reshape(1, X_DH), kn, v, w_o]
    in_specs = [pl.BlockSpec((tm, d), lambda i: (i, 0)),
                pl.BlockSpec((1, d), lambda i: (0, 0)),
                pl.BlockSpec((d, d), lambda i: (0, 0)),
                pl.BlockSpec((1, X_DH), lambda i: (0, 0)),
                pl.BlockSpec((1, 1, nm, d), lambda i: (layer, i // per_seq, 0, 0)),
                pl.BlockSpec((1, 1, nm, d), lambda i: (layer, i // per_seq, 0, 0)),
                pl.BlockSpec((d, d), lambda i: (0, 0))]
    if mixer_out is not None:
        a, wa = mixer_out
        operands += [a, wa]
        in_specs += [pl.BlockSpec((tm, a.shape[1]), lambda i: (i, 0)),
                     pl.BlockSpec(wa.shape, lambda i: (0, 0))]
    return pl.pallas_call(
        _cross_body,
        out_shape=jax.ShapeDtypeStruct((t, d), F32),
        grid=(t // tm,),
        in_specs=in_specs,
        out_specs=pl.BlockSpec((tm, d), lambda i: (i, 0)),
        compiler_params=_params("parallel"),
        name="cross_block",
    )(*operands)


LANES = 128


def _toeplitz_body(k_ref, o_ref):
    lc, gs = S5_CHUNK, S5_GROUP
    n = k_ref.shape[3]
    lane = lax.broadcasted_iota(jnp.int32, (lc, n), 1)
    for cp in range(gs):
        for pair in range(gs * lc // n):
            c = pair * (n // lc)
            lo = pltpu.roll(jnp.broadcast_to(k_ref[0, cp, c:c + 1, :], (lc, n)), lc + 1, 1,
                            stride=1, stride_axis=0)
            hi = pltpu.roll(jnp.broadcast_to(k_ref[0, cp, c + 1:c + 2, :], (lc, n)), 1, 1,
                            stride=1, stride_axis=0)
            o_ref[0, cp, :, pair * n:(pair + 1) * n] = jnp.where(lane < lc, lo, hi).astype(o_ref.dtype)


def toeplitz_expand(kern):
    g, gs, _, n = kern.shape
    lc = S5_CHUNK
    assert n == 2 * lc == LANES
    return pl.pallas_call(
        _toeplitz_body,
        out_shape=jax.ShapeDtypeStruct((g, gs, lc, gs * lc), BF16),
        grid=(g,),
        in_specs=[pl.BlockSpec((1, gs, gs, n), lambda i: (i, 0, 0, 0))],
        out_specs=pl.BlockSpec((1, gs, lc, gs * lc), lambda i: (i, 0, 0, 0)),
        compiler_params=_params("parallel"),
        name="toeplitz_expand",
    )(kern)


def s5_operators(lam_re, lam_im, log_step, b_re, b_im, c_re, c_im, d):
    lc = S5_CHUNK
    lam = lax.complex(lam_re.astype(F32), lam_im.astype(F32))
    step = jnp.exp(log_step.astype(F32))[..., None]
    lam_bar = jnp.exp(lam * step)
    b_bar = ((lam_bar - 1.0) / lam)[..., None] * lax.complex(b_re.astype(F32), b_im.astype(F32))
    c = lax.complex(c_re.astype(F32), c_im.astype(F32))
    pw = jnp.ones_like(lam_bar)[..., None]
    stride = lam_bar[..., None]
    while pw.shape[-1] < lc + 1:
        pw = jnp.concatenate([pw, pw * stride], axis=-1)
        stride = stride * stride
    pw = pw[..., :lc + 1]
    kern = jnp.einsum('zgcp,zgpt,zgpd->zgtcd', c, pw[..., :lc], b_bar, precision=HIGHEST).real
    kf, kb = kern[0], kern[1]
    k0 = kf[:, :1] + kb[:, :1] + (d.astype(F32)[:, :, None] * jnp.eye(S5_GROUP, dtype=F32))[:, None]
    kern_full = jnp.concatenate([kb[:, :0:-1], k0, kf[:, 1:]], axis=1)
    toep = toeplitz_expand(jnp.pad(kern_full.transpose(0, 3, 2, 1), ((0, 0), (0, 0), (0, 0), (0, 1))))

    def outer(x, y):
        return x[0] * y[0] - x[1] * y[1], x[0] * y[1] + x[1] * y[0]

    parts = lambda z: (z.real, z.imag)
    s_mid = lambda z: z.transpose(0, 2, 1)[:, None, :, :]
    b_first = lambda z: z.transpose(0, 2, 1)[:, :, None, :]
    pf = outer([s_mid(v) for v in parts(pw[0][..., lc - 1::-1])], [b_first(v) for v in parts(b_bar[0])])
    pb = outer([s_mid(v) for v in parts(pw[1][..., :lc])], [b_first(v) for v in parts(b_bar[1])])
    p_op = jnp.concatenate([pf[0], pb[0], pf[1], pb[1]], axis=-1)

    c_mid = lambda z: z.transpose(0, 2, 1)[:, :, :, None]
    t_last = lambda z: z[:, :, None, :]
    qf = outer([c_mid(v) for v in parts(c[0])], [t_last(v) for v in parts(pw[0][..., 1:])])
    qb = outer([c_mid(v) for v in parts(c[1])], [t_last(v) for v in parts(pw[1][..., :0:-1])])
    q_op = jnp.concatenate([qf[0], qb[0], -qf[1], -qb[1]], axis=1)
    q_op = q_op.reshape(S5_GROUPS, 4 * S5_STATE, S5_GROUP * lc)

    a = pw[..., lc]
    coef = jnp.stack([jnp.concatenate([a[0].real, a[1].real], -1),
                      jnp.concatenate([a[0].imag, a[1].imag], -1)], axis=1)
    return toep, _split_bf16(p_op), _split_bf16(q_op), coef


def _split_bf16(x):
    hi = x.astype(BF16)
    lo = (x - hi.astype(F32)).astype(BF16)
    return jnp.stack([hi, lo], axis=1)


CH_PACK = 4


def _s5_group_body(u_ref, t_ref, p_ref, q_ref, c_ref, y_ref,
                   v_re_ref, v_im_ref, xf_re_ref, xf_im_ref, xb_re_ref, xb_im_ref, *, n_chunks, bsz):
    half = 2 * S5_STATE
    lc = S5_CHUNK
    packs = S5_GROUP // CH_PACK
    u = [jnp.concatenate([u_ref[j * CH_PACK + i] for i in range(CH_PACK)], axis=-1) for j in range(packs)]
    packed = lambda op_ref, *lead: [op_ref[lead + (slice(j * CH_PACK, (j + 1) * CH_PACK),)]
                                    .reshape(CH_PACK * lc, -1) for j in range(packs)]
    v = None
    for term in range(2):
        for uj, pj in zip(u, packed(p_ref, 0, term)):
            part = jnp.dot(uj, pj, preferred_element_type=F32)
            v = part if v is None else v + part
    v_re_ref[...] = v[:, :half]
    v_im_ref[...] = v[:, half:]
    coef = c_ref[0]
    a_re, a_im = coef[0:1], coef[1:2]
    fwd = lax.broadcasted_iota(jnp.int32, (bsz, half), 1) < S5_STATE

    def step(i, carry):
        re, im = carry
        rows_f = pl.ds(i, bsz, stride=n_chunks)
        rows_b = pl.ds(n_chunks - 1 - i, bsz, stride=n_chunks)
        xf_re_ref[rows_f, :] = re
        xf_im_ref[rows_f, :] = im
        xb_re_ref[rows_b, :] = re
        xb_im_ref[rows_b, :] = im
        re_next = a_re * re - a_im * im + jnp.where(fwd, v_re_ref[rows_f, :], v_re_ref[rows_b, :])
        im_next = a_re * im + a_im * re + jnp.where(fwd, v_im_ref[rows_f, :], v_im_ref[rows_b, :])
        return re_next, im_next

    zero = jnp.zeros((bsz, half), F32)
    lax.fori_loop(0, n_chunks, step, (zero, zero))
    is_fwd = lax.broadcasted_iota(jnp.int32, xf_re_ref.shape, 1) < S5_STATE
    x = jnp.concatenate([jnp.where(is_fwd, xf_re_ref[...], xb_re_ref[...]),
                         jnp.where(is_fwd, xf_im_ref[...], xb_im_ref[...])], axis=1)
    x_hi = x.astype(BF16)
    x_lo = (x - x_hi.astype(F32)).astype(BF16)
    y = (jnp.dot(x_hi, q_ref[0, 0], preferred_element_type=F32)
         + jnp.dot(x_lo, q_ref[0, 0], preferred_element_type=F32)
         + jnp.dot(x_hi, q_ref[0, 1], preferred_element_type=F32))
    for uj, tj in zip(u, packed(t_ref, 0)):
        y = y + jnp.dot(uj, tj, preferred_element_type=F32)
    for c in range(S5_GROUP):
        y_ref[c] = y[:, c * lc:(c + 1) * lc].astype(y_ref.dtype)


def s5_scan(u_t, bsz, seq_len, ops):
    toep, p_op, q_op, coef = ops
    lc, g, gs, w = S5_CHUNK, S5_GROUPS, S5_GROUP, S5_CHUNK * S5_GROUP
    n = seq_len // lc
    c = bsz * n
    ns = 4 * S5_STATE
    group = lambda *shape: pl.BlockSpec((1,) + shape, lambda i: (i,) + (0,) * len(shape))
    channels = pl.BlockSpec((gs, c, lc), lambda i: (i, 0, 0))
    y = pl.pallas_call(
        functools.partial(_s5_group_body, n_chunks=n, bsz=bsz),
        out_shape=jax.ShapeDtypeStruct((g * gs, c, lc), BF16),
        grid=(g,),
        in_specs=[channels, group(gs, lc, w), group(2, gs, lc, ns), group(2, ns, w), group(2, ns // 2)],
        out_specs=channels,
        scratch_shapes=[pltpu.VMEM((c, ns // 2), F32)] * 6,
        compiler_params=_params("parallel"),
        name="s5_group",
    )(u_t.reshape(g * gs, c, lc), toep, p_op, q_op, coef)
    return y.reshape(g * gs, bsz * seq_len)


def _gla_direction(q_ref, k_ref, v_ref, g_ref, wg, bg, s_ref, o_ref, forward, chunks):
    cs = GLA_CHUNK
    tb = chunks * cs
    glr = g_ref[...]
    logit = (jnp.dot(glr, wg[0], preferred_element_type=F32)
             + jnp.dot(glr, wg[1], preferred_element_type=F32) + bg)
    g = (jnp.minimum(logit, 0.0) - jnp.log1p(jnp.exp(-jnp.abs(logit)))) / GLA_TAU
    row = lax.broadcasted_iota(jnp.int32, (tb, tb), 0)
    col = lax.broadcasted_iota(jnp.int32, (tb, tb), 1)
    same_chunk = (row // cs) == (col // cs)
    within = same_chunk & ((col <= row) if forward else (col >= row))
    ones = jnp.where(within, 1.0, 0.0).astype(BF16)
    bcum = jnp.zeros_like(g)
    rest = g
    for _ in range(3):
        term = rest.astype(BF16)
        bcum = bcum + jnp.dot(ones, term, preferred_element_type=F32)
        rest = rest - term.astype(F32)

    srow = lax.broadcasted_iota(jnp.int32, (GLA_HEADS * cs, cs), 0) & (cs - 1)
    scol = lax.broadcasted_iota(jnp.int32, (GLA_HEADS * cs, cs), 1)
    keep = (scol <= srow) if forward else (scol >= srow)
    lane = lax.broadcasted_iota(jnp.int32, (1, GLA_QK), 1)
    head_lanes = [((lane >= h * GLA_DK) & (lane < (h + 1) * GLA_DK)).astype(F32) for h in range(GLA_HEADS)]
    state_mask = (lax.broadcasted_iota(jnp.int32, (GLA_V, GLA_QK), 0) // GLA_DV
                  == lax.broadcasted_iota(jnp.int32, (GLA_V, GLA_QK), 1) // GLA_DK)

    i_ref = cs // 2 - 1 if forward else cs // 2
    i_last = cs - 1 if forward else 0
    scale = GLA_DK ** -0.5
    for ci in (range(chunks) if forward else reversed(range(chunks))):
        rows = slice(ci * cs, (ci + 1) * cs)
        b = bcum[rows]
        bref = b[i_ref:i_ref + 1]
        blast = b[i_last:i_last + 1]
        q = q_ref[rows, :].astype(F32) * scale
        k = k_ref[rows, :].astype(F32)
        v = v_ref[rows, :]
        q_rel = q * jnp.exp(b - bref)
        k_rel = (k * jnp.exp(bref - b)).astype(BF16)
        k_out = (k * jnp.exp(blast - b)).astype(BF16)
        q_dec = (q * jnp.exp(b)).astype(BF16)
        decay = jnp.exp(blast)
        q_heads = jnp.concatenate([q_rel * hm for hm in head_lanes], axis=0).astype(BF16)
        s = lax.dot_general(q_heads, k_rel, NT_DIMS, preferred_element_type=F32)
        s = jnp.where(keep, s, 0.0).astype(BF16)
        st = s_ref[...]
        o_inter = lax.dot_general(q_dec, st.astype(BF16), NT_DIMS, preferred_element_type=F32)
        o_intra = [jnp.dot(s[h * cs:(h + 1) * cs], v[:, h * GLA_DV:(h + 1) * GLA_DV],
                           preferred_element_type=F32) for h in range(GLA_HEADS)]
        o_ref[rows, :] = o_inter + jnp.concatenate(o_intra, axis=-1)
        kv = lax.dot_general(v, k_out, TN_DIMS, preferred_element_type=F32)
        s_ref[...] = decay * st + jnp.where(state_mask, kv, 0.0)


def _gla_body(qf_ref, kf_ref, vf_ref, gf_ref, qb_ref, kb_ref, vb_ref, gb_ref, wg_ref, bg_ref,
              of_ref, ob_ref, sf_ref, sb_ref, *, chunks):
    @pl.when(pl.program_id(1) == 0)
    def _():
        sf_ref[...] = jnp.zeros_like(sf_ref)
        sb_ref[...] = jnp.zeros_like(sb_ref)

    _gla_direction(qf_ref, kf_ref, vf_ref, gf_ref, wg_ref[0], bg_ref[0], sf_ref, of_ref, True, chunks)
    _gla_direction(qb_ref, kb_ref, vb_ref, gb_ref, wg_ref[1], bg_ref[1], sb_ref, ob_ref, False, chunks)


def gla_scan(proj, bsz, seq_len, w_gate, b_gate, chunks=4):
    t = proj.shape[0]
    tb = chunks * GLA_CHUNK
    nb = seq_len // tb
    fwd = lambda cb: (lambda b, i: (b * nb + i, cb))
    bwd = lambda cb: (lambda b, i: (b * nb + nb - 1 - i, cb))
    qc, kc, vc, gc = EV_Q // GLA_QK, EV_K // GLA_QK, EV_VAL // GLA_V, EV_GLR // 128
    wg = jnp.zeros((2, 128, GLA_QK), F32)
    wg = wg.at[0, :GLA_RANK].set(w_gate[0].astype(F32)).at[1, GLA_RANK:2 * GLA_RANK].set(w_gate[1].astype(F32))
    wg = _split_bf16(wg)
    out = jax.ShapeDtypeStruct((t, GLA_V), F32)
    state = pltpu.VMEM((GLA_V, GLA_QK), F32)
    return pl.pallas_call(
        functools.partial(_gla_body, chunks=chunks),
        out_shape=(out, out),
        grid=(bsz, nb),
        in_specs=[pl.BlockSpec((tb, GLA_QK), fwd(qc)), pl.BlockSpec((tb, GLA_QK), fwd(kc)),
                  pl.BlockSpec((tb, GLA_V), fwd(vc)), pl.BlockSpec((tb, 128), fwd(gc)),
                  pl.BlockSpec((tb, GLA_QK), bwd(qc)), pl.BlockSpec((tb, GLA_QK), bwd(kc)),
                  pl.BlockSpec((tb, GLA_V), bwd(vc)), pl.BlockSpec((tb, 128), bwd(gc)),
                  pl.BlockSpec((2, 2, 128, GLA_QK), lambda b, i: (0, 0, 0, 0)),
                  pl.BlockSpec((2, 1, GLA_QK), lambda b, i: (0, 0, 0))],
        out_specs=(pl.BlockSpec((tb, GLA_V), fwd(0)), pl.BlockSpec((tb, GLA_V), bwd(0))),
        scratch_shapes=[state, state],
        compiler_params=_params("parallel", "arbitrary"),
        name="gla_scan",
    )(proj, proj, proj, proj, proj, proj, proj, proj, wg, b_gate.astype(F32).reshape(2, 1, GLA_QK))


def _even_out_body(x_ref, ys_ref, of_ref, ob_ref, og_ref, wglu_t_ref, gn_ref, wtop_ref, wbot_ref, o_ref):
    y = ys_ref[...].astype(F32)
    y = 0.5 * y * (1.0 + jnp.tanh(math.sqrt(2.0 / math.pi) * (y + 0.044715 * (y * y * y))))
    gate = jnp.dot(wglu_t_ref[...], y.astype(BF16), preferred_element_type=F32)
    y = y * _sigmoid(gate)
    o = of_ref[...] + ob_ref[...]
    og = og_ref[...].astype(F32)
    heads = []
    for h in range(GLA_HEADS):
        sl = slice(h * GLA_DV, (h + 1) * GLA_DV)
        heads.append(_rms(o[:, sl], gn_ref[...]))
    o = jnp.concatenate(heads, axis=-1) * (og * _sigmoid(og))
    o_ref[...] = (x_ref[...]
                  + lax.dot_general(y.astype(BF16), wtop_ref[...], TN_DIMS, preferred_element_type=F32)
                  + jnp.dot(o.astype(BF16), wbot_ref[...], preferred_element_type=F32))


def even_out(x, ys_t, o_f, o_b, proj, w_glu_t, gla_norm, w_out, tm=ROW_TILE):
    t, d = x.shape
    row = lambda i: (i, 0)
    const = lambda i: (0, 0)
    return pl.pallas_call(
        _even_out_body,
        out_shape=jax.ShapeDtypeStruct((t, d), F32),
        grid=(t // tm,),
        in_specs=[pl.BlockSpec((tm, d), row), pl.BlockSpec((S5_WIDTH, tm), lambda i: (0, i)),
                  pl.BlockSpec((tm, GLA_V), row), pl.BlockSpec((tm, GLA_V), row),
                  pl.BlockSpec((tm, GLA_V), lambda i: (i, EV_OG // GLA_V)),
                  pl.BlockSpec((S5_WIDTH, S5_WIDTH), const), pl.BlockSpec((1, GLA_DV), const),
                  pl.BlockSpec((S5_WIDTH, d), const), pl.BlockSpec((GLA_V, d), lambda i: (1, 0))],
        out_specs=pl.BlockSpec((tm, d), row),
        compiler_params=_params("parallel"),
        name="even_out",
    )(x, ys_t, o_f, o_b, proj, w_glu_t, gla_norm.astype(F32).reshape(1, GLA_DV),
      w_out, w_out)


QK_NORM_WIDTH = 256


def _seg_rms(a, gain, same_seg):
    ssq = jnp.dot((a * a).astype(BF16), same_seg, preferred_element_type=F32)
    return a * lax.rsqrt(ssq * (1.0 / DIFF_DK) + EPS) * gain


def _qkv_body(x_ref, g_ref, w_ref, qk_gain_ref, o_ref):
    d = x_ref.shape[1]
    h = _rms(x_ref[...], g_ref[...]).astype(BF16)
    w = QK_NORM_WIDTH
    same_seg = jnp.where(lax.broadcasted_iota(jnp.int32, (w, w), 0) // DIFF_DK
                         == lax.broadcasted_iota(jnp.int32, (w, w), 1) // DIFF_DK, 1.0, 0.0).astype(BF16)
    for part in range(2):
        acc = jnp.dot(h, w_ref[:, part * d:(part + 1) * d], preferred_element_type=F32)
        gain = qk_gain_ref[part]
        for c in range(d // w):
            sl = slice(c * w, (c + 1) * w)
            o_ref[:, part * d + c * w:part * d + (c + 1) * w] = (
                _seg_rms(acc[:, sl], gain, same_seg).astype(o_ref.dtype))
    o_ref[:, 2 * d:] = jnp.dot(h, w_ref[:, 2 * d:], preferred_element_type=F32).astype(o_ref.dtype)


def qkv_project(x, gain, w, q_norm, k_norm, tm=ROW_TILE):
    t, d = x.shape
    reps = QK_NORM_WIDTH // DIFF_DK
    qg = jnp.tile(q_norm.astype(F32) * DIFF_DK ** -0.5, reps)
    kg = jnp.tile(k_norm.astype(F32), reps)
    qk_gain = jnp.stack([qg, kg]).reshape(2, 1, QK_NORM_WIDTH)
    return pl.pallas_call(
        _qkv_body,
        out_shape=jax.ShapeDtypeStruct((t, 3 * d), BF16),
        grid=(t // tm,),
        in_specs=[pl.BlockSpec((tm, d), lambda i: (i, 0)),
                  pl.BlockSpec((1, d), lambda i: (0, 0)),
                  pl.BlockSpec((d, 3 * d), lambda i: (0, 0)),
                  pl.BlockSpec((2, 1, QK_NORM_WIDTH), lambda i: (0, 0, 0))],
        out_specs=pl.BlockSpec((tm, 3 * d), lambda i: (i, 0)),
        compiler_params=_params("parallel"),
        name="qkv_project",
    )(x, gain.reshape(1, d), w, qk_gain)


POS_SPLIT = 16
POS_SHIFT = 4
AUG_LANE = DIFF_DK
SOFTMAX_ROWS = 16
ROW_PARTS = 2


def _pos_terms(shape):
    lane = lax.broadcasted_iota(jnp.int32, shape, 1)
    pos = lax.broadcasted_iota(jnp.int32, shape, 0)
    hi = lax.shift_right_logical(pos, POS_SHIFT).astype(F32)
    lo = (pos & (POS_SPLIT - 1)).astype(F32)
    return lane, hi, lo


def _lane_select(lane, first, values):
    out = 0.0
    for n, val in reversed(list(enumerate(values))):
        out = jnp.where(lane == first + n, val, out)
    return out


def _diff_attn_body(slope_ref, q_ref, k_ref, v_ref, lq1_ref, lk1_ref, lq2_ref, lk2_ref, sub_ref,
                    o_ref, ka_ref, va_ref, qs_ref, s0_ref, s1_ref, p0_ref, p1_ref, vs_ref,
                    *, blk, seq_len, lambda_init):
    h = pl.program_id(1)
    qi = pl.program_id(2)
    slope = slope_ref[h]
    nk = seq_len // blk
    s_refs, p_refs = (s0_ref, s1_ref), (p0_ref, p1_ref)

    @pl.when(qi == 0)
    def _():
        va_ref[:, :DIFF_DV] = v_ref[...]
        va_ref[:, DIFF_DV:] = jnp.ones((seq_len, DIFF_DV), BF16)

        def build(t, carry):
            rows = pl.ds(pl.multiple_of(t * blk, blk), blk)
            kf = k_ref[rows, :].astype(F32)
            lane, hi, lo = _pos_terms(kf.shape)
            aug = _lane_select(lane, AUG_LANE,
                               [-blk * slope, -POS_SPLIT * slope, -slope,
                                (blk * slope) * jnp.asarray(t, F32), (POS_SPLIT * slope) * hi, slope * lo])
            for z, kz in enumerate((kf, pltpu.roll(kf, DIFF_DK, 1))):
                ka_ref[z, rows, :] = jnp.where(lane < DIFF_DK, kz, aug).astype(BF16)
            return carry

        lax.fori_loop(0, nk, build, 0)

    qf = q_ref[...].astype(F32)
    lane, hi, lo = _pos_terms(qf.shape)
    qa = _lane_select(lane, AUG_LANE, [jnp.asarray(qi, F32), hi, lo, 1.0, 1.0, 1.0])
    for z, qz in enumerate((qf, pltpu.roll(qf, DIFF_DK, 1))):
        qs_ref[z] = jnp.where(lane < DIFF_DK, qz, 0.0).astype(BF16)
        qs_ref[2 + z] = jnp.where(lane < DIFF_DK, qz, qa).astype(BF16)
        qs_ref[4 + z] = jnp.where(lane < DIFF_DK, qz, -qa).astype(BF16)

    def key_block(t):
        if t == 0:
            return qi, 0
        j = (t - 1) + jnp.asarray(qi <= t - 1, jnp.int32)
        return j, jnp.where(j < qi, 2, 4)

    def key_rows(t):
        j, _ = key_block(t)
        return pl.ds(pl.multiple_of(j * blk, blk), blk)

    part = blk // ROW_PARTS
    parts = [slice(r * part, (r + 1) * part) for r in range(ROW_PARTS)]
    for z, s_ref in enumerate(s_refs):
        for rows in parts:
            for t in range(nk):
                _, variant = key_block(t)
                s_ref[rows, t * blk:(t + 1) * blk] = lax.dot_general(
                    qs_ref[variant + z, rows, :], ka_ref[z, key_rows(t), :], NT_DIMS,
                    preferred_element_type=F32)

    rg = SOFTMAX_ROWS
    rel = (lax.broadcasted_iota(jnp.int32, (rg, blk), 0) - lax.broadcasted_iota(jnp.int32, (rg, blk), 1))
    for t in range(nk):
        vs_ref[t * blk:(t + 1) * blk, :] = va_ref[key_rows(t), :]

    acc = []
    for s_ref, p_ref in zip(s_refs, p_refs):
        for g in range(blk // rg):
            rows = slice(g * rg, (g + 1) * rg)
            s_diag = s_ref[rows, :blk] - slope * jnp.abs(rel + g * rg).astype(F32)
            top = s_diag
            for t in range(1, nk):
                top = jnp.maximum(top, s_ref[rows, t * blk:(t + 1) * blk])
            m = jnp.max(top, axis=-1, keepdims=True)
            p_ref[rows, :blk] = jnp.exp(s_diag - m).astype(BF16)
            for t in range(1, nk):
                cols = slice(t * blk, (t + 1) * blk)
                p_ref[rows, cols] = jnp.exp(s_ref[rows, cols] - m).astype(BF16)
        acc.append([jnp.dot(p_ref[rows, :], vs_ref[...], preferred_element_type=F32) for rows in parts])

    lam = (jnp.exp(jnp.sum(lq1_ref[...] * lk1_ref[...], axis=-1, keepdims=True))
           - jnp.exp(jnp.sum(lq2_ref[...] * lk2_ref[...], axis=-1, keepdims=True)) + lambda_init)
    for rows, a0, a1 in zip(parts, *acc):
        o = (a0[:, :DIFF_DV] / a0[:, DIFF_DV:DIFF_DV + 1]
             - lam * (a1[:, :DIFF_DV] / a1[:, DIFF_DV:DIFF_DV + 1]))
        o_ref[rows, :] = (_rms(o, sub_ref[...]) * (1.0 - lambda_init)).astype(o_ref.dtype)


def diff_attention(qkv, bsz, seq_len, lq1, lk1, lq2, lk2, sub_norm, lambda_init, blk=512):
    t = qkv.shape[0]
    nq = seq_len // blk
    assert blk // POS_SPLIT <= 256, "hi part of a block position must stay exact in bf16"
    slopes = jnp.asarray(2.0 ** (-8.0 * np.arange(1, DIFF_HEADS + 1, dtype=np.float32) / DIFF_HEADS), F32)
    vec = lambda a: a.astype(F32).reshape(1, DIFF_DK)
    const = lambda b, h, i: (0, 0)
    return pl.pallas_call(
        functools.partial(_diff_attn_body, blk=blk, seq_len=seq_len, lambda_init=lambda_init),
        out_shape=jax.ShapeDtypeStruct((t, DIFF_HEADS * DIFF_DV), BF16),
        grid=(bsz, DIFF_HEADS, nq),
        in_specs=[pl.BlockSpec(memory_space=pltpu.SMEM),
                  pl.BlockSpec((blk, 2 * DIFF_DK), lambda b, h, i: (b * nq + i, h)),
                  pl.BlockSpec((seq_len, 2 * DIFF_DK), lambda b, h, i: (b, DIFF_HEADS + h)),
                  pl.BlockSpec((seq_len, DIFF_DV), lambda b, h, i: (b, 2 * DIFF_HEADS + h)),
                  pl.BlockSpec((1, DIFF_DK), const), pl.BlockSpec((1, DIFF_DK), const),
                  pl.BlockSpec((1, DIFF_DK), const), pl.BlockSpec((1, DIFF_DK), const),
                  pl.BlockSpec((1, DIFF_DV), const)],
        out_specs=pl.BlockSpec((blk, DIFF_DV), lambda b, h, i: (b * nq + i, h)),
        scratch_shapes=[pltpu.VMEM((2, seq_len, 2 * DIFF_DK), BF16),
                        pltpu.VMEM((seq_len, 2 * DIFF_DV), BF16),
                        pltpu.VMEM((6, blk, 2 * DIFF_DK), BF16),
                        pltpu.VMEM((blk, seq_len), F32),
                        pltpu.VMEM((blk, seq_len), F32),
                        pltpu.VMEM((blk, seq_len), BF16),
                        pltpu.VMEM((blk, seq_len), BF16),
                        pltpu.VMEM((seq_len, 2 * DIFF_DV), BF16)],
        compiler_params=_params("parallel", "parallel", "arbitrary"),
        name="diff_attention",
    )(slopes, qkv, qkv, qkv, vec(lq1), vec(lk1), vec(lq2), vec(lk2),
      sub_norm.astype(F32).reshape(1, DIFF_DV))


def _pad_ev_w_in(w):
    w = w[:, S5_WIDTH:]
    return jnp.pad(w, ((0, 0), (0, EV_PAD_COLS - w.shape[1]))).astype(BF16)


def _trunk(x3, mem, p):
    bsz, seq_len, d = x3.shape
    x = x3.reshape(bsz * seq_len, d)
    kn_all, v_all = mem_kv(mem, p['norm_mem'], p['x_w_kv'], p['x_k_norm'])
    for layer in range(DEPTH):
        if layer % 2 == 0:
            e = layer // 2
            proj, u_t = even_project(x, p['norm_mix'][layer], p['ev_w_in'][e], p['ev_w_u_t'][e], EV_TN)
            ys_t = s5_scan(u_t, bsz, seq_len, p['s5_ops'][e])
            o_f, o_b = gla_scan(proj, bsz, seq_len, p['gla_w_gate'][e], p['gla_b_gate'][e])
            x = even_out(x, ys_t, o_f, o_b, proj, p['s5_w_glu_t'][e], p['gla_norm'][e], p['ev_w_out'][e])
            mixer_out = None
        else:
            o = layer // 2
            lambda_init = 0.8 - 0.6 * math.exp(-0.3 * layer)
            qkv = qkv_project(x, p['norm_mix'][layer], p['od_w_in'][o],
                              p['diff_q_norm'][o], p['diff_k_norm'][o])
            att = diff_attention(qkv, bsz, seq_len, p['diff_lambda_q1'][o], p['diff_lambda_k1'][o],
                                 p['diff_lambda_q2'][o], p['diff_lambda_k2'][o], p['diff_norm'][o],
                                 lambda_init)
            mixer_out = (att, p['od_w_out'][o])
        x = cross_block(x, seq_len, layer, p['norm_cross'][layer], p['x_w_q'][layer], p['x_q_norm'][layer],
                        kn_all, v_all, p['x_w_o'][layer], mixer_out)
        x = mlp_block(x, p['norm_mlp'][layer], p['mlp_w1'][layer], p['mlp_w2'][layer])
    return x.reshape(bsz, seq_len, d)


def kernel(x_prompt, x_sample, mem_prompt, mem_sample, norm_mix, norm_cross, norm_mem, norm_mlp,
           ev_w_in, ev_w_out, s5_lambda_re, s5_lambda_im, s5_log_step, s5_b_re, s5_b_im,
           s5_c_re, s5_c_im, s5_d, s5_w_glu, gla_w_gate, gla_b_gate, gla_norm,
           od_w_in, od_w_out, diff_q_norm, diff_k_norm, diff_lambda_q1, diff_lambda_k1,
           diff_lambda_q2, diff_lambda_k2, diff_norm, x_w_q, x_w_kv, x_w_o, x_q_norm, x_k_norm,
           mlp_w1, mlp_w2):
    n_even = ev_w_in.shape[0]
    bf = lambda w: w.astype(BF16)
    p = dict(
        norm_mix=norm_mix, norm_cross=norm_cross, norm_mem=norm_mem, norm_mlp=norm_mlp,
        ev_w_in=[_pad_ev_w_in(ev_w_in[e]) for e in range(n_even)], ev_w_out=bf(ev_w_out),
        s5_ops=[s5_operators(s5_lambda_re[e], s5_lambda_im[e], s5_log_step[e], s5_b_re[e], s5_b_im[e],
                             s5_c_re[e], s5_c_im[e], s5_d[e]) for e in range(n_even)],
        s5_w_glu_t=bf(jnp.swapaxes(s5_w_glu, 1, 2)),
        ev_w_u_t=bf(jnp.swapaxes(ev_w_in[:, :, :S5_WIDTH], 1, 2)), gla_w_gate=gla_w_gate, gla_b_gate=gla_b_gate, gla_norm=gla_norm,
        od_w_in=bf(od_w_in), od_w_out=bf(od_w_out), diff_q_norm=diff_q_norm, diff_k_norm=diff_k_norm,
        diff_lambda_q1=diff_lambda_q1, diff_lambda_k1=diff_lambda_k1,
        diff_lambda_q2=diff_lambda_q2, diff_lambda_k2=diff_lambda_k2, diff_norm=diff_norm,
        x_w_q=bf(x_w_q), x_w_kv=bf(x_w_kv), x_w_o=bf(x_w_o), x_q_norm=x_q_norm, x_k_norm=x_k_norm,
        mlp_w1=bf(mlp_w1), mlp_w2=bf(mlp_w2))
    return (_trunk(x_prompt, mem_prompt, p), _trunk(x_sample, mem_sample, p))
```

```python
import functools
import math

import numpy as np
import jax
import jax.numpy as jnp
from jax import lax
from jax.experimental import pallas as pl
from jax.experimental.pallas import tpu as pltpu

F32 = jnp.float32
BF16 = jnp.bfloat16
HIGHEST = lax.Precision.HIGHEST

D_MODEL = 1024
DEPTH = 4
EPS = 1e-6
S5_WIDTH = 512
S5_GROUP = 16
S5_GROUPS = 32
S5_STATE = 64
S5_CHUNK = 64
GLA_HEADS = 4
GLA_DV = 128
GLA_DK = 64
GLA_RANK = 16
GLA_TAU = 16.0
GLA_CHUNK = 64
GLA_QK = GLA_HEADS * GLA_DK
GLA_V = GLA_HEADS * GLA_DV
EV_Q, EV_K, EV_VAL, EV_OG, EV_GLR = 0, 256, 512, 1024, 1536
EV_PAD_COLS = 1664
EV_TN = 1664
DIFF_HEADS = 8
DIFF_DK = 64
DIFF_DV = 128
X_HEADS = 4
X_DH = 256
D_FF = 4096

ROW_TILE = 512
VMEM_LIMIT = 48 * 1024 * 1024

NT_DIMS = (((1,), (1,)), ((), ()))
TN_DIMS = (((0,), (0,)), ((), ()))


def _params(*sem, flags=None):
    return pltpu.CompilerParams(dimension_semantics=sem, vmem_limit_bytes=VMEM_LIMIT, flags=flags)


def _rms(x, gain):
    ms = jnp.mean(x * x, axis=-1, keepdims=True)
    return x * lax.rsqrt(ms + EPS) * gain


def _sigmoid(x):
    return 1.0 / (1.0 + jnp.exp(-x))


def _even_project_body(x_ref, g_ref, w_ref, wt_ref, o_ref, ot_ref, *, tn):
    h = _rms(x_ref[...], g_ref[...]).astype(BF16)
    for j in range(w_ref.shape[1] // tn):
        cols = slice(j * tn, (j + 1) * tn)
        o_ref[:, cols] = jnp.dot(h, w_ref[:, cols], preferred_element_type=F32).astype(o_ref.dtype)
    ot_ref[...] = lax.dot_general(wt_ref[...], h, NT_DIMS, preferred_element_type=F32).astype(ot_ref.dtype)


def even_project(x, gain, w, w_t, tn, tm=ROW_TILE):
    t, d = x.shape
    n = w.shape[1]
    m = w_t.shape[0]
    return pl.pallas_call(
        functools.partial(_even_project_body, tn=tn),
        out_shape=(jax.ShapeDtypeStruct((t, n), BF16), jax.ShapeDtypeStruct((m, t), BF16)),
        grid=(t // tm,),
        in_specs=[pl.BlockSpec((tm, d), lambda i: (i, 0)),
                  pl.BlockSpec((1, d), lambda i: (0, 0)),
                  pl.BlockSpec((d, n), lambda i: (0, 0)),
                  pl.BlockSpec((m, d), lambda i: (0, 0))],
        out_specs=(pl.BlockSpec((tm, n), lambda i: (i, 0)), pl.BlockSpec((m, tm), lambda i: (0, i))),
        compiler_params=_params("parallel"),
        name="even_project",
    )(x, gain.reshape(1, d), w, w_t)


def _mlp_body(x_ref, g_ref, w1_ref, w2_ref, o_ref, hid_ref, *, tf):
    x = x_ref[...]
    h = _rms(x, g_ref[...]).astype(BF16)
    for f in range(w1_ref.shape[1] // tf):
        cols = slice(f * tf, (f + 1) * tf)
        hid = jnp.dot(h, w1_ref[:, cols], preferred_element_type=F32)
        hid_ref[:, cols] = jnp.square(jnp.maximum(hid, 0.0)).astype(BF16)
    o_ref[...] = x + jnp.dot(hid_ref[...], w2_ref[...], preferred_element_type=F32)


def mlp_block(x, gain, w1, w2, tf=512, tm=ROW_TILE):
    t, d = x.shape
    ff = w1.shape[1]
    resident = lambda shape: pl.BlockSpec(shape, lambda i: (0, 0), pipeline_mode=pl.Buffered(1))
    return pl.pallas_call(
        functools.partial(_mlp_body, tf=tf),
        out_shape=jax.ShapeDtypeStruct((t, d), F32),
        grid=(t // tm,),
        in_specs=[pl.BlockSpec((tm, d), lambda i: (i, 0)),
                  pl.BlockSpec((1, d), lambda i: (0, 0)),
                  resident((d, ff)), resident((ff, d))],
        out_specs=pl.BlockSpec((tm, d), lambda i: (i, 0)),
        scratch_shapes=[pltpu.VMEM((tm, ff), BF16)],
        compiler_params=_params("parallel"),
        name="mlp_block",
    )(x, gain.reshape(1, d), w1, w2)


def _mem_kv_body(m_ref, g_ref, w_ref, kg_ref, k_ref, v_ref):
    h = _rms(m_ref[0], g_ref[0]).astype(BF16)
    kv = jnp.dot(h, w_ref[0], preferred_element_type=F32)
    for hd in range(X_HEADS):
        sl = slice(hd * X_DH, (hd + 1) * X_DH)
        k_ref[0, 0, :, sl] = _rms(kv[:, sl], kg_ref[0]).astype(BF16)
    v_ref[0, 0] = kv[:, D_MODEL:].astype(BF16)


def mem_kv(mem, norm_mem, w_kv, k_norm):
    bm, nm, d = mem.shape
    out = jax.ShapeDtypeStruct((DEPTH, bm, nm, d), BF16)
    return pl.pallas_call(
        _mem_kv_body,
        out_shape=(out, out),
        grid=(DEPTH, bm),
        in_specs=[pl.BlockSpec((1, nm, d), lambda l, b: (b, 0, 0)),
                  pl.BlockSpec((1, 1, d), lambda l, b: (l, 0, 0)),
                  pl.BlockSpec((1, d, 2 * d), lambda l, b: (l, 0, 0)),
                  pl.BlockSpec((1, 1, X_DH), lambda l, b: (l, 0, 0))],
        out_specs=(pl.BlockSpec((1, 1, nm, d), lambda l, b: (l, b, 0, 0)),
                   pl.BlockSpec((1, 1, nm, d), lambda l, b: (l, b, 0, 0))),
        compiler_params=_params("arbitrary", "arbitrary"),
        name="mem_kv",
    )(mem, norm_mem.reshape(DEPTH, 1, d), w_kv, k_norm.reshape(DEPTH, 1, X_DH))


def _cross_body(x_ref, g_ref, wq_ref, qg_ref, k_ref, v_ref, wo_ref, *rest):
    o_ref = rest[-1]
    x = x_ref[...]
    if len(rest) == 3:
        x = x + jnp.dot(rest[0][...], rest[1][...], preferred_element_type=F32)
    h = _rms(x, g_ref[...]).astype(BF16)
    q = jnp.dot(h, wq_ref[...], preferred_element_type=F32)
    heads = []
    for hd in range(X_HEADS):
        sl = slice(hd * X_DH, (hd + 1) * X_DH)
        qn = _rms(q[:, sl], qg_ref[...]).astype(BF16)
        s = lax.dot_general(qn, k_ref[0, 0, :, sl], NT_DIMS, preferred_element_type=F32)
        p = jnp.exp(s - jnp.max(s, axis=-1, keepdims=True))
        l = jnp.sum(p, axis=-1, keepdims=True)
        oh = jnp.dot(p.astype(BF16), v_ref[0, 0, :, sl], preferred_element_type=F32) / l
        heads.append(oh.astype(BF16))
    o = jnp.concatenate(heads, axis=-1)
    o_ref[...] = x + jnp.dot(o, wo_ref[...], preferred_element_type=F32)


def cross_block(x, seq_len, layer, gain, w_q, q_gain, kn, v, w_o, mixer_out=None, tm=ROW_TILE):
    t, d = x.shape
    nm = kn.shape[2]
    per_seq = seq_len // tm
    operands = [x, gain.reshape(1, d), w_q, (q_gain * X_DH ** -0.5).reshape(1, X_DH), kn, v, w_o]
    in_specs = [pl.BlockSpec((tm, d), lambda i: (i, 0)),
                pl.BlockSpec((1, d), lambda i: (0, 0)),
                pl.BlockSpec((d, d), lambda i: (0, 0)),
                pl.BlockSpec((1, X_DH), lambda i: (0, 0)),
                pl.BlockSpec((1, 1, nm, d), lambda i: (layer, i // per_seq, 0, 0)),
                pl.BlockSpec((1, 1, nm, d), lambda i: (layer, i // per_seq, 0, 0)),
                pl.BlockSpec((d, d), lambda i: (0, 0))]
    if mixer_out is not None:
        a, wa = mixer_out
        operands += [a, wa]
        in_specs += [pl.BlockSpec((tm, a.shape[1]), lambda i: (i, 0)),
                     pl.BlockSpec(wa.shape, lambda i: (0, 0))]
    return pl.pallas_call(
        _cross_body,
        out_shape=jax.ShapeDtypeStruct((t, d), F32),
        grid=(t // tm,),
        in_specs=in_specs,
        out_specs=pl.BlockSpec((tm, d), lambda i: (i, 0)),
        compiler_params=_params("parallel"),
        name="cross_block",
    )(*operands)


LANES = 128


def _toeplitz_body(k_ref, o_ref):
    lc, gs = S5_CHUNK, S5_GROUP
    n = k_ref.shape[3]
    lane = lax.broadcasted_iota(jnp.int32, (lc, n), 1)
    for cp in range(gs):
        for pair in range(gs * lc // n):
            c = pair * (n // lc)
            lo = pltpu.roll(jnp.broadcast_to(k_ref[0, cp, c:c + 1, :], (lc, n)), lc + 1, 1,
                            stride=1, stride_axis=0)
            hi = pltpu.roll(jnp.broadcast_to(k_ref[0, cp, c + 1:c + 2, :], (lc, n)), 1, 1,
                            stride=1, stride_axis=0)
            o_ref[0, cp, :, pair * n:(pair + 1) * n] = jnp.where(lane < lc, lo, hi).astype(o_ref.dtype)


def toeplitz_expand(kern):
    g, gs, _, n = kern.shape
    lc = S5_CHUNK
    assert n == 2 * lc == LANES
    return pl.pallas_call(
        _toeplitz_body,
        out_shape=jax.ShapeDtypeStruct((g, gs, lc, gs * lc), BF16),
        grid=(g,),
        in_specs=[pl.BlockSpec((1, gs, gs, n), lambda i: (i, 0, 0, 0))],
        out_specs=pl.BlockSpec((1, gs, lc, gs * lc), lambda i: (i, 0, 0, 0)),
        compiler_params=_params("parallel"),
        name="toeplitz_expand",
    )(kern)


def s5_operators(lam_re, lam_im, log_step, b_re, b_im, c_re, c_im, d):
    lc = S5_CHUNK
    lam = lax.complex(lam_re.astype(F32), lam_im.astype(F32))
    step = jnp.exp(log_step.astype(F32))[..., None]
    lam_bar = jnp.exp(lam * step)
    b_bar = ((lam_bar - 1.0) / lam)[..., None] * lax.complex(b_re.astype(F32), b_im.astype(F32))
    c = lax.complex(c_re.astype(F32), c_im.astype(F32))
    pw = jnp.ones_like(lam_bar)[..., None]
    stride = lam_bar[..., None]
    while pw.shape[-1] < lc + 1:
        pw = jnp.concatenate([pw, pw * stride], axis=-1)
        stride = stride * stride
    pw = pw[..., :lc + 1]
    kern = jnp.einsum('zgcp,zgpt,zgpd->zgtcd', c, pw[..., :lc], b_bar, precision=HIGHEST).real
    kf, kb = kern[0], kern[1]
    k0 = kf[:, :1] + kb[:, :1] + (d.astype(F32)[:, :, None] * jnp.eye(S5_GROUP, dtype=F32))[:, None]
    kern_full = jnp.concatenate([kb[:, :0:-1], k0, kf[:, 1:]], axis=1)
    toep = toeplitz_expand(jnp.pad(kern_full.transpose(0, 3, 2, 1), ((0, 0), (0, 0), (0, 0), (0, 1))))

    def outer(x, y):
        return x[0] * y[0] - x[1] * y[1], x[0] * y[1] + x[1] * y[0]

    parts = lambda z: (z.real, z.imag)
    s_mid = lambda z: z.transpose(0, 2, 1)[:, None, :, :]
    b_first = lambda z: z.transpose(0, 2, 1)[:, :, None, :]
    pf = outer([s_mid(v) for v in parts(pw[0][..., lc - 1::-1])], [b_first(v) for v in parts(b_bar[0])])
    pb = outer([s_mid(v) for v in parts(pw[1][..., :lc])], [b_first(v) for v in parts(b_bar[1])])
    p_op = jnp.concatenate([pf[0], pb[0], pf[1], pb[1]], axis=-1)

    c_mid = lambda z: z.transpose(0, 2, 1)[:, :, :, None]
    t_last = lambda z: z[:, :, None, :]
    qf = outer([c_mid(v) for v in parts(c[0])], [t_last(v) for v in parts(pw[0][..., 1:])])
    qb = outer([c_mid(v) for v in parts(c[1])], [t_last(v) for v in parts(pw[1][..., :0:-1])])
    q_op = jnp.concatenate([qf[0], qb[0], -qf[1], -qb[1]], axis=1)
    q_op = q_op.reshape(S5_GROUPS, 4 * S5_STATE, S5_GROUP * lc)

    a = pw[..., lc]
    coef = jnp.stack([jnp.concatenate([a[0].real, a[1].real], -1),
                      jnp.concatenate([a[0].imag, a[1].imag], -1)], axis=1)
    return toep, _split_bf16(p_op), _split_bf16(q_op), coef


def _split_bf16(x):
    hi = x.astype(BF16)
    lo = (x - hi.astype(F32)).astype(BF16)
    return jnp.stack([hi, lo], axis=1)


CH_PACK = 4


def _s5_group_body(u_ref, t_ref, p_ref, q_ref, c_ref, y_ref,
                   v_re_ref, v_im_ref, xf_re_ref, xf_im_ref, xb_re_ref, xb_im_ref, *, n_chunks, bsz):
    half = 2 * S5_STATE
    lc = S5_CHUNK
    packs = S5_GROUP // CH_PACK
    u = [jnp.concatenate([u_ref[j * CH_PACK + i] for i in range(CH_PACK)], axis=-1) for j in range(packs)]
    packed = lambda op_ref, *lead: [op_ref[lead + (slice(j * CH_PACK, (j + 1) * CH_PACK),)]
                                    .reshape(CH_PACK * lc, -1) for j in range(packs)]
    v = None
    for term in range(2):
        for uj, pj in zip(u, packed(p_ref, 0, term)):
            part = jnp.dot(uj, pj, preferred_element_type=F32)
            v = part if v is None else v + part
    v_re_ref[...] = v[:, :half]
    v_im_ref[...] = v[:, half:]
    coef = c_ref[0]
    a_re, a_im = coef[0:1], coef[1:2]
    fwd = lax.broadcasted_iota(jnp.int32, (bsz, half), 1) < S5_STATE

    def step(i, carry):
        re, im = carry
        rows_f = pl.ds(i, bsz, stride=n_chunks)
        rows_b = pl.ds(n_chunks - 1 - i, bsz, stride=n_chunks)
        xf_re_ref[rows_f, :] = re
        xf_im_ref[rows_f, :] = im
        xb_re_ref[rows_b, :] = re
        xb_im_ref[rows_b, :] = im
        re_next = a_re * re - a_im * im + jnp.where(fwd, v_re_ref[rows_f, :], v_re_ref[rows_b, :])
        im_next = a_re * im + a_im * re + jnp.where(fwd, v_im_ref[rows_f, :], v_im_ref[rows_b, :])
        return re_next, im_next

    zero = jnp.zeros((bsz, half), F32)
    lax.fori_loop(0, n_chunks, step, (zero, zero))
    is_fwd = lax.broadcasted_iota(jnp.int32, xf_re_ref.shape, 1) < S5_STATE
    x = jnp.concatenate([jnp.where(is_fwd, xf_re_ref[...], xb_re_ref[...]),
                         jnp.where(is_fwd, xf_im_ref[...], xb_im_ref[...])], axis=1)
    x_hi = x.astype(BF16)
    x_lo = (x - x_hi.astype(F32)).astype(BF16)
    y = (jnp.dot(x_hi, q_ref[0, 0], preferred_element_type=F32)
         + jnp.dot(x_lo, q_ref[0, 0], preferred_element_type=F32)
         + jnp.dot(x_hi, q_ref[0, 1], preferred_element_type=F32))
    for uj, tj in zip(u, packed(t_ref, 0)):
        y = y + jnp.dot(uj, tj, preferred_element_type=F32)
    for c in range(S5_GROUP):
        y_ref[c] = y[:, c * lc:(c + 1) * lc].astype(y_ref.dtype)


def s5_scan(u_t, bsz, seq_len, ops):
    toep, p_op, q_op, coef = ops
    lc, g, gs, w = S5_CHUNK, S5_GROUPS, S5_GROUP, S5_CHUNK * S5_GROUP
    n = seq_len // lc
    c = bsz * n
    ns = 4 * S5_STATE
    group = lambda *shape: pl.BlockSpec((1,) + shape, lambda i: (i,) + (0,) * len(shape))
    channels = pl.BlockSpec((gs, c, lc), lambda i: (i, 0, 0))
    y = pl.pallas_call(
        functools.partial(_s5_group_body, n_chunks=n, bsz=bsz),
        out_shape=jax.ShapeDtypeStruct((g * gs, c, lc), BF16),
        grid=(g,),
        in_specs=[channels, group(gs, lc, w), group(2, gs, lc, ns), group(2, ns, w), group(2, ns // 2)],
        out_specs=channels,
        scratch_shapes=[pltpu.VMEM((c, ns // 2), F32)] * 6,
        compiler_params=_params("parallel"),
        name="s5_group",
    )(u_t.reshape(g * gs, c, lc), toep, p_op, q_op, coef)
    return y.reshape(g * gs, bsz * seq_len)


def _gla_direction(q_ref, k_ref, v_ref, g_ref, wg, bg, s_ref, o_ref, forward, chunks):
    cs = GLA_CHUNK
    tb = chunks * cs
    glr = g_ref[...]
    logit = (jnp.dot(glr, wg[0], preferred_element_type=F32)
             + jnp.dot(glr, wg[1], preferred_element_type=F32) + bg)
    g = (jnp.minimum(logit, 0.0) - jnp.log1p(jnp.exp(-jnp.abs(logit)))) / GLA_TAU
    row = lax.broadcasted_iota(jnp.int32, (tb, tb), 0)
    col = lax.broadcasted_iota(jnp.int32, (tb, tb), 1)
    same_chunk = (row // cs) == (col // cs)
    within = same_chunk & ((col <= row) if forward else (col >= row))
    ones = jnp.where(within, 1.0, 0.0).astype(BF16)
    bcum = jnp.zeros_like(g)
    rest = g
    for _ in range(3):
        term = rest.astype(BF16)
        bcum = bcum + jnp.dot(ones, term, preferred_element_type=F32)
        rest = rest - term.astype(F32)

    srow = lax.broadcasted_iota(jnp.int32, (GLA_HEADS * cs, cs), 0) & (cs - 1)
    scol = lax.broadcasted_iota(jnp.int32, (GLA_HEADS * cs, cs), 1)
    keep = (scol <= srow) if forward else (scol >= srow)
    lane = lax.broadcasted_iota(jnp.int32, (1, GLA_QK), 1)
    head_lanes = [((lane >= h * GLA_DK) & (lane < (h + 1) * GLA_DK)).astype(F32) for h in range(GLA_HEADS)]
    state_mask = (lax.broadcasted_iota(jnp.int32, (GLA_V, GLA_QK), 0) // GLA_DV
                  == lax.broadcasted_iota(jnp.int32, (GLA_V, GLA_QK), 1) // GLA_DK)

    i_ref = cs // 2 - 1 if forward else cs // 2
    i_last = cs - 1 if forward else 0
    scale = GLA_DK ** -0.5
    for ci in (range(chunks) if forward else reversed(range(chunks))):
        rows = slice(ci * cs, (ci + 1) * cs)
        b = bcum[rows]
        bref = b[i_ref:i_ref + 1]
        blast = b[i_last:i_last + 1]
        q = q_ref[rows, :].astype(F32) * scale
        k = k_ref[rows, :].astype(F32)
        v = v_ref[rows, :]
        q_rel = q * jnp.exp(b - bref)
        k_rel = (k * jnp.exp(bref - b)).astype(BF16)
        k_out = (k * jnp.exp(blast - b)).astype(BF16)
        q_dec = (q * jnp.exp(b)).astype(BF16)
        decay = jnp.exp(blast)
        q_heads = jnp.concatenate([q_rel * hm for hm in head_lanes], axis=0).astype(BF16)
        s = lax.dot_general(q_heads, k_rel, NT_DIMS, preferred_element_type=F32)
        s = jnp.where(keep, s, 0.0).astype(BF16)
        st = s_ref[...]
        o_inter = lax.dot_general(q_dec, st.astype(BF16), NT_DIMS, preferred_element_type=F32)
        o_intra = [jnp.dot(s[h * cs:(h + 1) * cs], v[:, h * GLA_DV:(h + 1) * GLA_DV],
                           preferred_element_type=F32) for h in range(GLA_HEADS)]
        o_ref[rows, :] = o_inter + jnp.concatenate(o_intra, axis=-1)
        kv = lax.dot_general(v, k_out, TN_DIMS, preferred_element_type=F32)
        s_ref[...] = decay * st + jnp.where(state_mask, kv, 0.0)


def _gla_body(qf_ref, kf_ref, vf_ref, gf_ref, qb_ref, kb_ref, vb_ref, gb_ref, wg_ref, bg_ref,
              of_ref, ob_ref, sf_ref, sb_ref, *, chunks):
    @pl.when(pl.program_id(1) == 0)
    def _():
        sf_ref[...] = jnp.zeros_like(sf_ref)
        sb_ref[...] = jnp.zeros_like(sb_ref)

    _gla_direction(qf_ref, kf_ref, vf_ref, gf_ref, wg_ref[0], bg_ref[0], sf_ref, of_ref, True, chunks)
    _gla_direction(qb_ref, kb_ref, vb_ref, gb_ref, wg_ref[1], bg_ref[1], sb_ref, ob_ref, False, chunks)


def gla_scan(proj, bsz, seq_len, w_gate, b_gate, chunks=4):
    t = proj.shape[0]
    tb = chunks * GLA_CHUNK
    nb = seq_len // tb
    fwd = lambda cb: (lambda b, i: (b * nb + i, cb))
    bwd = lambda cb: (lambda b, i: (b * nb + nb - 1 - i, cb))
    qc, kc, vc, gc = EV_Q // GLA_QK, EV_K // GLA_QK, EV_VAL // GLA_V, EV_GLR // 128
    wg = jnp.zeros((2, 128, GLA_QK), F32)
    wg = wg.at[0, :GLA_RANK].set(w_gate[0].astype(F32)).at[1, GLA_RANK:2 * GLA_RANK].set(w_gate[1].astype(F32))
    wg = _split_bf16(wg)
    out = jax.ShapeDtypeStruct((t, GLA_V), F32)
    state = pltpu.VMEM((GLA_V, GLA_QK), F32)
    return pl.pallas_call(
        functools.partial(_gla_body, chunks=chunks),
        out_shape=(out, out),
        grid=(bsz, nb),
        in_specs=[pl.BlockSpec((tb, GLA_QK), fwd(qc)), pl.BlockSpec((tb, GLA_QK), fwd(kc)),
                  pl.BlockSpec((tb, GLA_V), fwd(vc)), pl.BlockSpec((tb, 128), fwd(gc)),
                  pl.BlockSpec((tb, GLA_QK), bwd(qc)), pl.BlockSpec((tb, GLA_QK), bwd(kc)),
                  pl.BlockSpec((tb, GLA_V), bwd(vc)), pl.BlockSpec((tb, 128), bwd(gc)),
                  pl.BlockSpec((2, 2, 128, GLA_QK), lambda b, i: (0, 0, 0, 0)),
                  pl.BlockSpec((2, 1, GLA_QK), lambda b, i: (0, 0, 0))],
        out_specs=(pl.BlockSpec((tb, GLA_V), fwd(0)), pl.BlockSpec((tb, GLA_V), bwd(0))),
        scratch_shapes=[state, state],
        compiler_params=_params("parallel", "arbitrary"),
        name="gla_scan",
    )(proj, proj, proj, proj, proj, proj, proj, proj, wg, b_gate.astype(F32).reshape(2, 1, GLA_QK))


def _even_out_body(x_ref, ys_ref, of_ref, ob_ref, og_ref, wglu_t_ref, gn_ref, wtop_ref, wbot_ref, o_ref):
    y = ys_ref[...].astype(F32)
    y = 0.5 * y * (1.0 + jnp.tanh(math.sqrt(2.0 / math.pi) * (y + 0.044715 * (y * y * y))))
    gate = jnp.dot(wglu_t_ref[...], y.astype(BF16), preferred_element_type=F32)
    y = y * _sigmoid(gate)
    o = of_ref[...] + ob_ref[...]
    og = og_ref[...].astype(F32)
    heads = []
    for h in range(GLA_HEADS):
        sl = slice(h * GLA_DV, (h + 1) * GLA_DV)
        heads.append(_rms(o[:, sl], gn_ref[...]))
    o = jnp.concatenate(heads, axis=-1) * (og * _sigmoid(og))
    o_ref[...] = (x_ref[...]
                  + lax.dot_general(y.astype(BF16), wtop_ref[...], TN_DIMS, preferred_element_type=F32)
                  + jnp.dot(o.astype(BF16), wbot_ref[...], preferred_element_type=F32))


def even_out(x, ys_t, o_f, o_b, proj, w_glu_t, gla_norm, w_out, tm=ROW_TILE):
    t, d = x.shape
    row = lambda i: (i, 0)
    const = lambda i: (0, 0)
    return pl.pallas_call(
        _even_out_body,
        out_shape=jax.ShapeDtypeStruct((t, d), F32),
        grid=(t // tm,),
        in_specs=[pl.BlockSpec((tm, d), row), pl.BlockSpec((S5_WIDTH, tm), lambda i: (0, i)),
                  pl.BlockSpec((tm, GLA_V), row), pl.BlockSpec((tm, GLA_V), row),
                  pl.BlockSpec((tm, GLA_V), lambda i: (i, EV_OG // GLA_V)),
                  pl.BlockSpec((S5_WIDTH, S5_WIDTH), const), pl.BlockSpec((1, GLA_DV), const),
                  pl.BlockSpec((S5_WIDTH, d), const), pl.BlockSpec((GLA_V, d), lambda i: (1, 0))],
        out_specs=pl.BlockSpec((tm, d), row),
        compiler_params=_params("parallel"),
        name="even_out",
    )(x, ys_t, o_f, o_b, proj, w_glu_t, gla_norm.astype(F32).reshape(1, GLA_DV),
      w_out, w_out)


QK_NORM_WIDTH = 256


def _seg_rms(a, gain, same_seg):
    ssq = jnp.dot((a * a).astype(BF16), same_seg, preferred_element_type=F32)
    return a * lax.rsqrt(ssq * (1.0 / DIFF_DK) + EPS) * gain


def _qkv_body(x_ref, g_ref, w_ref, qk_gain_ref, o_ref):
    d = x_ref.shape[1]
    h = _rms(x_ref[...], g_ref[...]).astype(BF16)
    w = QK_NORM_WIDTH
    same_seg = jnp.where(lax.broadcasted_iota(jnp.int32, (w, w), 0) // DIFF_DK
                         == lax.broadcasted_iota(jnp.int32, (w, w), 1) // DIFF_DK, 1.0, 0.0).astype(BF16)
    for part in range(2):
        acc = jnp.dot(h, w_ref[:, part * d:(part + 1) * d], preferred_element_type=F32)
        gain = qk_gain_ref[part]
        for c in range(d // w):
            sl = slice(c * w, (c + 1) * w)
            o_ref[:, part * d + c * w:part * d + (c + 1) * w] = (
                _seg_rms(acc[:, sl], gain, same_seg).astype(o_ref.dtype))
    o_ref[:, 2 * d:] = jnp.dot(h, w_ref[:, 2 * d:], preferred_element_type=F32).astype(o_ref.dtype)


def qkv_project(x, gain, w, q_norm, k_norm, tm=ROW_TILE):
    t, d = x.shape
    reps = QK_NORM_WIDTH // DIFF_DK
    qg = jnp.tile(q_norm.astype(F32) * DIFF_DK ** -0.5, reps)
    kg = jnp.tile(k_norm.astype(F32), reps)
    qk_gain = jnp.stack([qg, kg]).reshape(2, 1, QK_NORM_WIDTH)
    return pl.pallas_call(
        _qkv_body,
        out_shape=jax.ShapeDtypeStruct((t, 3 * d), BF16),
        grid=(t // tm,),
        in_specs=[pl.BlockSpec((tm, d), lambda i: (i, 0)),
                  pl.BlockSpec((1, d), lambda i: (0, 0)),
                  pl.BlockSpec((d, 3 * d), lambda i: (0, 0)),
                  pl.BlockSpec((2, 1, QK_NORM_WIDTH), lambda i: (0, 0, 0))],
        out_specs=pl.BlockSpec((tm, 3 * d), lambda i: (i, 0)),
        compiler_params=_params("parallel"),
        name="qkv_project",
    )(x, gain.reshape(1, d), w, qk_gain)


POS_SPLIT = 16
POS_SHIFT = 4
AUG_LANE = DIFF_DK
SOFTMAX_ROWS = 16
ROW_PARTS = 2


def _pos_terms(shape):
    lane = lax.broadcasted_iota(jnp.int32, shape, 1)
    pos = lax.broadcasted_iota(jnp.int32, shape, 0)
    hi = lax.shift_right_logical(pos, POS_SHIFT).astype(F32)
    lo = (pos & (POS_SPLIT - 1)).astype(F32)
    return lane, hi, lo


def _lane_select(lane, first, values):
    out = 0.0
    for n, val in reversed(list(enumerate(values))):
        out = jnp.where(lane == first + n, val, out)
    return out


def _diff_attn_body(slope_ref, q_ref, k_ref, v_ref, lq1_ref, lk1_ref, lq2_ref, lk2_ref, sub_ref,
                    o_ref, ka_ref, va_ref, qs_ref, s0_ref, s1_ref, p0_ref, p1_ref, vs_ref,
                    *, blk, seq_len, lambda_init):
    h = pl.program_id(1)
    qi = pl.program_id(2)
    slope = slope_ref[h]
    nk = seq_len // blk
    s_refs, p_refs = (s0_ref, s1_ref), (p0_ref, p1_ref)

    @pl.when(qi == 0)
    def _():
        va_ref[:, :DIFF_DV] = v_ref[...]
        va_ref[:, DIFF_DV:] = jnp.ones((seq_len, DIFF_DV), BF16)

        def build(t, carry):
            rows = pl.ds(pl.multiple_of(t * blk, blk), blk)
            kf = k_ref[rows, :].astype(F32)
            lane, hi, lo = _pos_terms(kf.shape)
            aug = _lane_select(lane, AUG_LANE,
                               [-blk * slope, -POS_SPLIT * slope, -slope,
                                (blk * slope) * jnp.asarray(t, F32), (POS_SPLIT * slope) * hi, slope * lo])
            for z, kz in enumerate((kf, pltpu.roll(kf, DIFF_DK, 1))):
                ka_ref[z, rows, :] = jnp.where(lane < DIFF_DK, kz, aug).astype(BF16)
            return carry

        lax.fori_loop(0, nk, build, 0)

    qf = q_ref[...].astype(F32)
    lane, hi, lo = _pos_terms(qf.shape)
    qa = _lane_select(lane, AUG_LANE, [jnp.asarray(qi, F32), hi, lo, 1.0, 1.0, 1.0])
    for z, qz in enumerate((qf, pltpu.roll(qf, DIFF_DK, 1))):
        qs_ref[z] = jnp.where(lane < DIFF_DK, qz, 0.0).astype(BF16)
        qs_ref[2 + z] = jnp.where(lane < DIFF_DK, qz, qa).astype(BF16)
        qs_ref[4 + z] = jnp.where(lane < DIFF_DK, qz, -qa).astype(BF16)

    def key_block(t):
        if t == 0:
            return qi, 0
        j = (t - 1) + jnp.asarray(qi <= t - 1, jnp.int32)
        return j, jnp.where(j < qi, 2, 4)

    def key_rows(t):
        j, _ = key_block(t)
        return pl.ds(pl.multiple_of(j * blk, blk), blk)

    part = blk // ROW_PARTS
    parts = [slice(r * part, (r + 1) * part) for r in range(ROW_PARTS)]
    for z, s_ref in enumerate(s_refs):
        for rows in parts:
            for t in range(nk):
                _, variant = key_block(t)
                s_ref[rows, t * blk:(t + 1) * blk] = lax.dot_general(
                    qs_ref[variant + z, rows, :], ka_ref[z, key_rows(t), :], NT_DIMS,
                    preferred_element_type=F32)

    rg = SOFTMAX_ROWS
    rel = (lax.broadcasted_iota(jnp.int32, (rg, blk), 0) - lax.broadcasted_iota(jnp.int32, (rg, blk), 1))
    for t in range(nk):
        vs_ref[t * blk:(t + 1) * blk, :] = va_ref[key_rows(t), :]

    acc = []
    for s_ref, p_ref in zip(s_refs, p_refs):
        for g in range(blk // rg):
            rows = slice(g * rg, (g + 1) * rg)
            s_diag = s_ref[rows, :blk] - slope * jnp.abs(rel + g * rg).astype(F32)
            top = s_diag
            for t in range(1, nk):
                top = jnp.maximum(top, s_ref[rows, t * blk:(t + 1) * blk])
            m = jnp.max(top, axis=-1, keepdims=True)
            p_ref[rows, :blk] = jnp.exp(s_diag - m).astype(BF16)
            for t in range(1, nk):
                cols = slice(t * blk, (t + 1) * blk)
                p_ref[rows, cols] = jnp.exp(s_ref[rows, cols] - m).astype(BF16)
        acc.append([jnp.dot(p_ref[rows, :], vs_ref[...], preferred_element_type=F32) for rows in parts])

    lam = (jnp.exp(jnp.sum(lq1_ref[...] * lk1_ref[...], axis=-1, keepdims=True))
           - jnp.exp(jnp.sum(lq2_ref[...] * lk2_ref[...], axis=-1, keepdims=True)) + lambda_init)
    for rows, a0, a1 in zip(parts, *acc):
        o = (a0[:, :DIFF_DV] / a0[:, DIFF_DV:DIFF_DV + 1]
             - lam * (a1[:, :DIFF_DV] / a1[:, DIFF_DV:DIFF_DV + 1]))
        o_ref[rows, :] = (_rms(o, sub_ref[...]) * (1.0 - lambda_init)).astype(o_ref.dtype)


def diff_attention(qkv, bsz, seq_len, lq1, lk1, lq2, lk2, sub_norm, lambda_init, blk=512):
    t = qkv.shape[0]
    nq = seq_len // blk
    assert blk // POS_SPLIT <= 256, "hi part of a block position must stay exact in bf16"
    slopes = jnp.asarray(2.0 ** (-8.0 * np.arange(1, DIFF_HEADS + 1, dtype=np.float32) / DIFF_HEADS), F32)
    vec = lambda a: a.astype(F32).reshape(1, DIFF_DK)
    const = lambda b, h, i: (0, 0)
    return pl.pallas_call(
        functools.partial(_diff_attn_body, blk=blk, seq_len=seq_len, lambda_init=lambda_init),
        out_shape=jax.ShapeDtypeStruct((t, DIFF_HEADS * DIFF_DV), BF16),
        grid=(bsz, DIFF_HEADS, nq),
        in_specs=[pl.BlockSpec(memory_space=pltpu.SMEM),
                  pl.BlockSpec((blk, 2 * DIFF_DK), lambda b, h, i: (b * nq + i, h)),
                  pl.BlockSpec((seq_len, 2 * DIFF_DK), lambda b, h, i: (b, DIFF_HEADS + h)),
                  pl.BlockSpec((seq_len, DIFF_DV), lambda b, h, i: (b, 2 * DIFF_HEADS + h)),
                  pl.BlockSpec((1, DIFF_DK), const), pl.BlockSpec((1, DIFF_DK), const),
                  pl.BlockSpec((1, DIFF_DK), const), pl.BlockSpec((1, DIFF_DK), const),
                  pl.BlockSpec((1, DIFF_DV), const)],
        out_specs=pl.BlockSpec((blk, DIFF_DV), lambda b, h, i: (b * nq + i, h)),
        scratch_shapes=[pltpu.VMEM((2, seq_len, 2 * DIFF_DK), BF16),
                        pltpu.VMEM((seq_len, 2 * DIFF_DV), BF16),
                        pltpu.VMEM((6, blk, 2 * DIFF_DK), BF16),
                        pltpu.VMEM((blk, seq_len), F32),
                        pltpu.VMEM((blk, seq_len), F32),
                        pltpu.VMEM((blk, seq_len), BF16),
                        pltpu.VMEM((blk, seq_len), BF16),
                        pltpu.VMEM((seq_len, 2 * DIFF_DV), BF16)],
        compiler_params=_params("parallel", "parallel", "arbitrary"),
        name="diff_attention",
    )(slopes, qkv, qkv, qkv, vec(lq1), vec(lk1), vec(lq2), vec(lk2),
      sub_norm.astype(F32).reshape(1, DIFF_DV))


def _pad_ev_w_in(w):
    w = w[:, S5_WIDTH:]
    return jnp.pad(w, ((0, 0), (0, EV_PAD_COLS - w.shape[1]))).astype(BF16)


def _trunk(x3, mem, p):
    bsz, seq_len, d = x3.shape
    x = x3.reshape(bsz * seq_len, d)
    kn_all, v_all = mem_kv(mem, p['norm_mem'], p['x_w_kv'], p['x_k_norm'])
    for layer in range(DEPTH):
        if layer % 2 == 0:
            e = layer // 2
            proj, u_t = even_project(x, p['norm_mix'][layer], p['ev_w_in'][e], p['ev_w_u_t'][e], EV_TN)
            ys_t = s5_scan(u_t, bsz, seq_len, p['s5_ops'][e])
            o_f, o_b = gla_scan(proj, bsz, seq_len, p['gla_w_gate'][e], p['gla_b_gate'][e])
            x = even_out(x, ys_t, o_f, o_b, proj, p['s5_w_glu_t'][e], p['gla_norm'][e], p['ev_w_out'][e])
            mixer_out = None
        else:
            o = layer // 2
            lambda_init = 0.8 - 0.6 * math.exp(-0.3 * layer)
            qkv = qkv_project(x, p['norm_mix'][layer], p['od_w_in'][o],
                              p['diff_q_norm'][o], p['diff_k_norm'][o])
            att = diff_attention(qkv, bsz, seq_len, p['diff_lambda_q1'][o], p['diff_lambda_k1'][o],
                                 p['diff_lambda_q2'][o], p['diff_lambda_k2'][o], p['diff_norm'][o],
                                 lambda_init)
            mixer_out = (att, p['od_w_out'][o])
        x = cross_block(x, seq_len, layer, p['norm_cross'][layer], p['x_w_q'][layer], p['x_q_norm'][layer],
                        kn_all, v_all, p['x_w_o'][layer], mixer_out)
        x = mlp_block(x, p['norm_mlp'][layer], p['mlp_w1'][layer], p['mlp_w2'][layer])
    return x.reshape(bsz, seq_len, d)


def kernel(x_prompt, x_sample, mem_prompt, mem_sample, norm_mix, norm_cross, norm_mem, norm_mlp,
           ev_w_in, ev_w_out, s5_lambda_re, s5_lambda_im, s5_log_step, s5_b_re, s5_b_im,
           s5_c_re, s5_c_im, s5_d, s5_w_glu, gla_w_gate, gla_b_gate, gla_norm,
           od_w_in, od_w_out, diff_q_norm, diff_k_norm, diff_lambda_q1, diff_lambda_k1,
           diff_lambda_q2, diff_lambda_k2, diff_norm, x_w_q, x_w_kv, x_w_o, x_q_norm, x_k_norm,
           mlp_w1, mlp_w2):
    n_even = ev_w_in.shape[0]
    bf = lambda w: w.astype(BF16)
    p = dict(
        norm_mix=norm_mix, norm_cross=norm_cross, norm_mem=norm_mem, norm_mlp=norm_mlp,
        ev_w_in=[_pad_ev_w_in(ev_w_in[e]) for e in range(n_even)], ev_w_out=bf(ev_w_out),
        s5_ops=[s5_operators(s5_lambda_re[e], s5_lambda_im[e], s5_log_step[e], s5_b_re[e], s5_b_im[e],
                             s5_c_re[e], s5_c_im[e], s5_d[e]) for e in range(n_even)],
        s5_w_glu_t=bf(jnp.swapaxes(s5_w_glu, 1, 2)),
        ev_w_u_t=bf(jnp.swapaxes(ev_w_in[:, :, :S5_WIDTH], 1, 2)), gla_w_gate=gla_w_gate, gla_b_gate=gla_b_gate, gla_norm=gla_norm,
        od_w_in=bf(od_w_in), od_w_out=bf(od_w_out), diff_q_norm=diff_q_norm, diff_k_norm=diff_k_norm,
        diff_lambda_q1=diff_lambda_q1, diff_lambda_k1=diff_lambda_k1,
        diff_lambda_q2=diff_lambda_q2, diff_lambda_k2=diff_lambda_k2, diff_norm=diff_norm,
        x_w_q=bf(x_w_q), x_w_kv=bf(x_w_kv), x_w_o=bf(x_w_o), x_q_norm=x_q_norm, x_k_norm=x_k_norm,
        mlp_w1=bf(mlp_w1), mlp_w2=bf(mlp_w2))
    return (_trunk(x_prompt, mem_prompt, p), _trunk(x_sample, mem_sample, p))
```

```python
import functools
import math

import numpy as np
import jax
import jax.numpy as jnp
from jax import lax
from jax.experimental import pallas as pl
from jax.experimental.pallas import tpu as pltpu

F32 = jnp.float32
BF16 = jnp.bfloat16
HIGHEST = lax.Precision.HIGHEST

D_MODEL = 1024
DEPTH = 4
EPS = 1e-6
S5_WIDTH = 512
S5_GROUP = 16
S5_GROUPS = 32
S5_STATE = 64
S5_CHUNK = 64
GLA_HEADS = 4
GLA_DV = 128
GLA_DK = 64
GLA_RANK = 16
GLA_TAU = 16.0
GLA_CHUNK = 64
GLA_QK = GLA_HEADS * GLA_DK
GLA_V = GLA_HEADS * GLA_DV
EV_Q, EV_K, EV_VAL, EV_OG, EV_GLR = 0, 256, 512, 1024, 1536
EV_PAD_COLS = 1664
EV_TN = 1664
DIFF_HEADS = 8
DIFF_DK = 64
DIFF_DV = 128
X_HEADS = 4
X_DH = 256
D_FF = 4096

ROW_TILE = 512
VMEM_LIMIT = 48 * 1024 * 1024

NT_DIMS = (((1,), (1,)), ((), ()))
TN_DIMS = (((0,), (0,)), ((), ()))


def _params(*sem, flags=None):
    return pltpu.CompilerParams(dimension_semantics=sem, vmem_limit_bytes=VMEM_LIMIT, flags=flags)


def _rms(x, gain):
    ms = jnp.mean(x * x, axis=-1, keepdims=True)
    return x * lax.rsqrt(ms + EPS) * gain


def _sigmoid(x):
    return 1.0 / (1.0 + jnp.exp(-x))


def _even_project_body(x_ref, g_ref, w_ref, wt_ref, o_ref, ot_ref, *, tn):
    h = _rms(x_ref[...], g_ref[...]).astype(BF16)
    for j in range(w_ref.shape[1] // tn):
        cols = slice(j * tn, (j + 1) * tn)
        o_ref[:, cols] = jnp.dot(h, w_ref[:, cols], preferred_element_type=F32).astype(o_ref.dtype)
    ot_ref[...] = lax.dot_general(wt_ref[...], h, NT_DIMS, preferred_element_type=F32).astype(ot_ref.dtype)


def even_project(x, gain, w, w_t, tn, tm=ROW_TILE):
    t, d = x.shape
    n = w.shape[1]
    m = w_t.shape[0]
    return pl.pallas_call(
        functools.partial(_even_project_body, tn=tn),
        out_shape=(jax.ShapeDtypeStruct((t, n), BF16), jax.ShapeDtypeStruct((m, t), BF16)),
        grid=(t // tm,),
        in_specs=[pl.BlockSpec((tm, d), lambda i: (i, 0)),
                  pl.BlockSpec((1, d), lambda i: (0, 0)),
                  pl.BlockSpec((d, n), lambda i: (0, 0)),
                  pl.BlockSpec((m, d), lambda i: (0, 0))],
        out_specs=(pl.BlockSpec((tm, n), lambda i: (i, 0)), pl.BlockSpec((m, tm), lambda i: (0, i))),
        compiler_params=_params("parallel"),
        name="even_project",
    )(x, gain.reshape(1, d), w, w_t)


def _mlp_body(x_ref, g_ref, w1_ref, w2_ref, o_ref, hid_ref, *, tf):
    x = x_ref[...]
    h = _rms(x, g_ref[...]).astype(BF16)
    for f in range(w1_ref.shape[2] // tf):
        cols = slice(f * tf, (f + 1) * tf)
        hid = jnp.dot(h, w1_ref[0, :, cols], preferred_element_type=F32)
        hid_ref[:, cols] = jnp.square(jnp.maximum(hid, 0.0)).astype(BF16)
    o_ref[...] = x + jnp.dot(hid_ref[...], w2_ref[0], preferred_element_type=F32)


def mlp_block(x, gain, w1, w2, layer, tf=512, tm=ROW_TILE):
    t, d = x.shape
    ff = w1.shape[2]
    resident = lambda shape: pl.BlockSpec((1,) + shape, lambda i: (layer, 0, 0), pipeline_mode=pl.Buffered(1))
    return pl.pallas_call(
        functools.partial(_mlp_body, tf=tf),
        out_shape=jax.ShapeDtypeStruct((t, d), F32),
        grid=(t // tm,),
        in_specs=[pl.BlockSpec((tm, d), lambda i: (i, 0)),
                  pl.BlockSpec((1, d), lambda i: (0, 0)),
                  resident((d, ff)), resident((ff, d))],
        out_specs=pl.BlockSpec((tm, d), lambda i: (i, 0)),
        scratch_shapes=[pltpu.VMEM((tm, ff), BF16)],
        compiler_params=_params("parallel"),
        name="mlp_block",
    )(x, gain.reshape(1, d), w1, w2)


def _mem_kv_body(m_ref, g_ref, w_ref, kg_ref, k_ref, v_ref):
    h = _rms(m_ref[0], g_ref[0]).astype(BF16)
    kv = jnp.dot(h, w_ref[0], preferred_element_type=F32)
    for hd in range(X_HEADS):
        sl = slice(hd * X_DH, (hd + 1) * X_DH)
        k_ref[0, 0, :, sl] = _rms(kv[:, sl], kg_ref[0]).astype(BF16)
    v_ref[0, 0] = kv[:, D_MODEL:].astype(BF16)


def mem_kv(mem, norm_mem, w_kv, k_norm):
    bm, nm, d = mem.shape
    out = jax.ShapeDtypeStruct((DEPTH, bm, nm, d), BF16)
    return pl.pallas_call(
        _mem_kv_body,
        out_shape=(out, out),
        grid=(DEPTH, bm),
        in_specs=[pl.BlockSpec((1, nm, d), lambda l, b: (b, 0, 0)),
                  pl.BlockSpec((1, 1, d), lambda l, b: (l, 0, 0)),
                  pl.BlockSpec((1, d, 2 * d), lambda l, b: (l, 0, 0)),
                  pl.BlockSpec((1, 1, X_DH), lambda l, b: (l, 0, 0))],
        out_specs=(pl.BlockSpec((1, 1, nm, d), lambda l, b: (l, b, 0, 0)),
                   pl.BlockSpec((1, 1, nm, d), lambda l, b: (l, b, 0, 0))),
        compiler_params=_params("arbitrary", "arbitrary"),
        name="mem_kv",
    )(mem, norm_mem.reshape(DEPTH, 1, d), w_kv, k_norm.reshape(DEPTH, 1, X_DH))


def _cross_body(x_ref, g_ref, wq_ref, qg_ref, k_ref, v_ref, wo_ref, *rest):
    o_ref = rest[-1]
    x = x_ref[...]
    if len(rest) == 3:
        x = x + jnp.dot(rest[0][...], rest[1][...], preferred_element_type=F32)
    h = _rms(x, g_ref[...]).astype(BF16)
    q = jnp.dot(h, wq_ref[...], preferred_element_type=F32)
    heads = []
    for hd in range(X_HEADS):
        sl = slice(hd * X_DH, (hd + 1) * X_DH)
        qn = _rms(q[:, sl], qg_ref[...]).astype(BF16)
        s = lax.dot_general(qn, k_ref[0, 0, :, sl], NT_DIMS, preferred_element_type=F32)
        p = jnp.exp(s - jnp.max(s, axis=-1, keepdims=True))
        l = jnp.sum(p, axis=-1, keepdims=True)
        oh = jnp.dot(p.astype(BF16), v_ref[0, 0, :, sl], preferred_element_type=F32) / l
        heads.append(oh.astype(BF16))
    o = jnp.concatenate(heads, axis=-1)
    o_ref[...] = x + jnp.dot(o, wo_ref[...], preferred_element_type=F32)


def cross_block(x, seq_len, layer, gain, w_q, q_gain, kn, v, w_o, mixer_out=None, tm=ROW_TILE):
    t, d = x.shape
    nm = kn.shape[2]
    per_seq = seq_len // tm
    operands = [x, gain.reshape(1, d), w_q, (q_gain * X_DH ** -0.5).reshape(1, X_DH), kn, v, w_o]
    in_specs = [pl.BlockSpec((tm, d), lambda i: (i, 0)),
                pl.BlockSpec((1, d), lambda i: (0, 0)),
                pl.BlockSpec((d, d), lambda i: (0, 0)),
                pl.BlockSpec((1, X_DH), lambda i: (0, 0)),
                pl.BlockSpec((1, 1, nm, d), lambda i: (layer, i // per_seq, 0, 0)),
                pl.BlockSpec((1, 1, nm, d), lambda i: (layer, i // per_seq, 0, 0)),
                pl.BlockSpec((d, d), lambda i: (0, 0))]
    if mixer_out is not None:
        a, wa = mixer_out
        operands += [a, wa]
        in_specs += [pl.BlockSpec((tm, a.shape[1]), lambda i: (i, 0)),
                     pl.BlockSpec(wa.shape, lambda i: (0, 0))]
    return pl.pallas_call(
        _cross_body,
        out_shape=jax.ShapeDtypeStruct((t, d), F32),
        grid=(t // tm,),
        in_specs=in_specs,
        out_specs=pl.BlockSpec((tm, d), lambda i: (i, 0)),
        compiler_params=_params("parallel"),
        name="cross_block",
    )(*operands)


LANES = 128


def _toeplitz_body(k_ref, o_ref):
    lc, gs = S5_CHUNK, S5_GROUP
    n = k_ref.shape[3]
    lane = lax.broadcasted_iota(jnp.int32, (lc, n), 1)
    for cp in range(gs):
        for pair in range(gs * lc // n):
            c = pair * (n // lc)
            lo = pltpu.roll(jnp.broadcast_to(k_ref[0, cp, c:c + 1, :], (lc, n)), lc + 1, 1,
                            stride=1, stride_axis=0)
            hi = pltpu.roll(jnp.broadcast_to(k_ref[0, cp, c + 1:c + 2, :], (lc, n)), 1, 1,
                            stride=1, stride_axis=0)
            o_ref[0, cp, :, pair * n:(pair + 1) * n] = jnp.where(lane < lc, lo, hi).astype(o_ref.dtype)


def toeplitz_expand(kern):
    g, gs, _, n = kern.shape
    lc = S5_CHUNK
    assert n == 2 * lc == LANES
    return pl.pallas_call(
        _toeplitz_body,
        out_shape=jax.ShapeDtypeStruct((g, gs, lc, gs * lc), BF16),
        grid=(g,),
        in_specs=[pl.BlockSpec((1, gs, gs, n), lambda i: (i, 0, 0, 0))],
        out_specs=pl.BlockSpec((1, gs, lc, gs * lc), lambda i: (i, 0, 0, 0)),
        compiler_params=_params("parallel"),
        name="toeplitz_expand",
    )(kern)


def s5_operators(lam_re, lam_im, log_step, b_re, b_im, c_re, c_im, d):
    lc = S5_CHUNK
    lam = lax.complex(lam_re.astype(F32), lam_im.astype(F32))
    step = jnp.exp(log_step.astype(F32))[..., None]
    lam_bar = jnp.exp(lam * step)
    b_bar = ((lam_bar - 1.0) / lam)[..., None] * lax.complex(b_re.astype(F32), b_im.astype(F32))
    c = lax.complex(c_re.astype(F32), c_im.astype(F32))
    pw = jnp.ones_like(lam_bar)[..., None]
    stride = lam_bar[..., None]
    while pw.shape[-1] < lc + 1:
        pw = jnp.concatenate([pw, pw * stride], axis=-1)
        stride = stride * stride
    pw = pw[..., :lc + 1]
    kern = jnp.einsum('zgcp,zgpt,zgpd->zgtcd', c, pw[..., :lc], b_bar, precision=HIGHEST).real
    kf, kb = kern[0], kern[1]
    k0 = kf[:, :1] + kb[:, :1] + (d.astype(F32)[:, :, None] * jnp.eye(S5_GROUP, dtype=F32))[:, None]
    kern_full = jnp.concatenate([kb[:, :0:-1], k0, kf[:, 1:]], axis=1)
    toep = toeplitz_expand(jnp.pad(kern_full.transpose(0, 3, 2, 1), ((0, 0), (0, 0), (0, 0), (0, 1))))

    def outer(x, y):
        return x[0] * y[0] - x[1] * y[1], x[0] * y[1] + x[1] * y[0]

    parts = lambda z: (z.real, z.imag)
    s_mid = lambda z: z.transpose(0, 2, 1)[:, None, :, :]
    b_first = lambda z: z.transpose(0, 2, 1)[:, :, None, :]
    pf = outer([s_mid(v) for v in parts(pw[0][..., lc - 1::-1])], [b_first(v) for v in parts(b_bar[0])])
    pb = outer([s_mid(v) for v in parts(pw[1][..., :lc])], [b_first(v) for v in parts(b_bar[1])])
    p_op = jnp.concatenate([pf[0], pb[0], pf[1], pb[1]], axis=-1)

    c_mid = lambda z: z.transpose(0, 2, 1)[:, :, :, None]
    t_last = lambda z: z[:, :, None, :]
    qf = outer([c_mid(v) for v in parts(c[0])], [t_last(v) for v in parts(pw[0][..., 1:])])
    qb = outer([c_mid(v) for v in parts(c[1])], [t_last(v) for v in parts(pw[1][..., :0:-1])])
    q_op = jnp.concatenate([qf[0], qb[0], -qf[1], -qb[1]], axis=1)
    q_op = q_op.reshape(S5_GROUPS, 4 * S5_STATE, S5_GROUP * lc)

    a = pw[..., lc]
    coef = jnp.stack([jnp.concatenate([a[0].real, a[1].real], -1),
                      jnp.concatenate([a[0].imag, a[1].imag], -1)], axis=1)
    return toep, _split_bf16(p_op), _split_bf16(q_op), coef


def _split_bf16(x):
    hi = x.astype(BF16)
    lo = (x - hi.astype(F32)).astype(BF16)
    return jnp.stack([hi, lo], axis=1)


CH_PACK = 4


def _s5_group_body(u_ref, t_ref, p_ref, q_ref, c_ref, y_ref,
                   v_re_ref, v_im_ref, xf_re_ref, xf_im_ref, xb_re_ref, xb_im_ref, *, n_chunks, bsz):
    half = 2 * S5_STATE
    lc = S5_CHUNK
    packs = S5_GROUP // CH_PACK
    u = [jnp.concatenate([u_ref[j * CH_PACK + i] for i in range(CH_PACK)], axis=-1) for j in range(packs)]
    packed = lambda op_ref, *lead: [op_ref[lead + (slice(j * CH_PACK, (j + 1) * CH_PACK),)]
                                    .reshape(CH_PACK * lc, -1) for j in range(packs)]
    v = None
    for term in range(2):
        for uj, pj in zip(u, packed(p_ref, 0, term)):
            part = jnp.dot(uj, pj, preferred_element_type=F32)
            v = part if v is None else v + part
    v_re_ref[...] = v[:, :half]
    v_im_ref[...] = v[:, half:]
    coef = c_ref[0]
    a_re, a_im = coef[0:1], coef[1:2]
    fwd = lax.broadcasted_iota(jnp.int32, (bsz, half), 1) < S5_STATE

    def step(i, carry):
        re, im = carry
        rows_f = pl.ds(i, bsz, stride=n_chunks)
        rows_b = pl.ds(n_chunks - 1 - i, bsz, stride=n_chunks)
        xf_re_ref[rows_f, :] = re
        xf_im_ref[rows_f, :] = im
        xb_re_ref[rows_b, :] = re
        xb_im_ref[rows_b, :] = im
        re_next = a_re * re - a_im * im + jnp.where(fwd, v_re_ref[rows_f, :], v_re_ref[rows_b, :])
        im_next = a_re * im + a_im * re + jnp.where(fwd, v_im_ref[rows_f, :], v_im_ref[rows_b, :])
        return re_next, im_next

    zero = jnp.zeros((bsz, half), F32)
    lax.fori_loop(0, n_chunks, step, (zero, zero))
    is_fwd = lax.broadcasted_iota(jnp.int32, xf_re_ref.shape, 1) < S5_STATE
    x = jnp.concatenate([jnp.where(is_fwd, xf_re_ref[...], xb_re_ref[...]),
                         jnp.where(is_fwd, xf_im_ref[...], xb_im_ref[...])], axis=1)
    x_hi = x.astype(BF16)
    x_lo = (x - x_hi.astype(F32)).astype(BF16)
    y = (jnp.dot(x_hi, q_ref[0, 0], preferred_element_type=F32)
         + jnp.dot(x_lo, q_ref[0, 0], preferred_element_type=F32)
         + jnp.dot(x_hi, q_ref[0, 1], preferred_element_type=F32))
    for uj, tj in zip(u, packed(t_ref, 0)):
        y = y + jnp.dot(uj, tj, preferred_element_type=F32)
    for c in range(S5_GROUP):
        y_ref[c] = y[:, c * lc:(c + 1) * lc].astype(y_ref.dtype)


def s5_scan(u_t, bsz, seq_len, ops):
    toep, p_op, q_op, coef = ops
    lc, g, gs, w = S5_CHUNK, S5_GROUPS, S5_GROUP, S5_CHUNK * S5_GROUP
    n = seq_len // lc
    c = bsz * n
    ns = 4 * S5_STATE
    group = lambda *shape: pl.BlockSpec((1,) + shape, lambda i: (i,) + (0,) * len(shape))
    channels = pl.BlockSpec((gs, c, lc), lambda i: (i, 0, 0))
    y = pl.pallas_call(
        functools.partial(_s5_group_body, n_chunks=n, bsz=bsz),
        out_shape=jax.ShapeDtypeStruct((g * gs, c, lc), BF16),
        grid=(g,),
        in_specs=[channels, group(gs, lc, w), group(2, gs, lc, ns), group(2, ns, w), group(2, ns // 2)],
        out_specs=channels,
        scratch_shapes=[pltpu.VMEM((c, ns // 2), F32)] * 6,
        compiler_params=_params("parallel"),
        name="s5_group",
    )(u_t.reshape(g * gs, c, lc), toep, p_op, q_op, coef)
    return y.reshape(g * gs, bsz * seq_len)


def _gla_direction(q_ref, k_ref, v_ref, g_ref, wg, bg, s_ref, o_ref, forward, chunks):
    cs = GLA_CHUNK
    tb = chunks * cs
    glr = g_ref[...]
    logit = (jnp.dot(glr, wg[0], preferred_element_type=F32)
             + jnp.dot(glr, wg[1], preferred_element_type=F32) + bg)
    g = (jnp.minimum(logit, 0.0) - jnp.log1p(jnp.exp(-jnp.abs(logit)))) / GLA_TAU
    row = lax.broadcasted_iota(jnp.int32, (tb, tb), 0)
    col = lax.broadcasted_iota(jnp.int32, (tb, tb), 1)
    same_chunk = (row // cs) == (col // cs)
    within = same_chunk & ((col <= row) if forward else (col >= row))
    ones = jnp.where(within, 1.0, 0.0).astype(BF16)
    bcum = jnp.zeros_like(g)
    rest = g
    for _ in range(3):
        term = rest.astype(BF16)
        bcum = bcum + jnp.dot(ones, term, preferred_element_type=F32)
        rest = rest - term.astype(F32)

    srow = lax.broadcasted_iota(jnp.int32, (GLA_HEADS * cs, cs), 0) & (cs - 1)
    scol = lax.broadcasted_iota(jnp.int32, (GLA_HEADS * cs, cs), 1)
    keep = (scol <= srow) if forward else (scol >= srow)
    lane = lax.broadcasted_iota(jnp.int32, (1, GLA_QK), 1)
    head_lanes = [((lane >= h * GLA_DK) & (lane < (h + 1) * GLA_DK)).astype(F32) for h in range(GLA_HEADS)]
    state_mask = (lax.broadcasted_iota(jnp.int32, (GLA_V, GLA_QK), 0) // GLA_DV
                  == lax.broadcasted_iota(jnp.int32, (GLA_V, GLA_QK), 1) // GLA_DK)

    i_ref = cs // 2 - 1 if forward else cs // 2
    i_last = cs - 1 if forward else 0
    scale = GLA_DK ** -0.5
    for ci in (range(chunks) if forward else reversed(range(chunks))):
        rows = slice(ci * cs, (ci + 1) * cs)
        b = bcum[rows]
        bref = b[i_ref:i_ref + 1]
        blast = b[i_last:i_last + 1]
        q = q_ref[rows, :].astype(F32) * scale
        k = k_ref[rows, :].astype(F32)
        v = v_ref[rows, :]
        q_rel = q * jnp.exp(b - bref)
        k_rel = (k * jnp.exp(bref - b)).astype(BF16)
        k_out = (k * jnp.exp(blast - b)).astype(BF16)
        q_dec = (q * jnp.exp(b)).astype(BF16)
        decay = jnp.exp(blast)
        q_heads = jnp.concatenate([q_rel * hm for hm in head_lanes], axis=0).astype(BF16)
        s = lax.dot_general(q_heads, k_rel, NT_DIMS, preferred_element_type=F32)
        s = jnp.where(keep, s, 0.0).astype(BF16)
        st = s_ref[...]
        o_inter = lax.dot_general(q_dec, st.astype(BF16), NT_DIMS, preferred_element_type=F32)
        o_intra = [jnp.dot(s[h * cs:(h + 1) * cs], v[:, h * GLA_DV:(h + 1) * GLA_DV],
                           preferred_element_type=F32) for h in range(GLA_HEADS)]
        o_ref[rows, :] = o_inter + jnp.concatenate(o_intra, axis=-1)
        kv = lax.dot_general(v, k_out, TN_DIMS, preferred_element_type=F32)
        s_ref[...] = decay * st + jnp.where(state_mask, kv, 0.0)


def _gla_body(qf_ref, kf_ref, vf_ref, gf_ref, qb_ref, kb_ref, vb_ref, gb_ref, wg_ref, bg_ref,
              of_ref, ob_ref, sf_ref, sb_ref, *, chunks):
    @pl.when(pl.program_id(1) == 0)
    def _():
        sf_ref[...] = jnp.zeros_like(sf_ref)
        sb_ref[...] = jnp.zeros_like(sb_ref)

    _gla_direction(qf_ref, kf_ref, vf_ref, gf_ref, wg_ref[0], bg_ref[0], sf_ref, of_ref, True, chunks)
    _gla_direction(qb_ref, kb_ref, vb_ref, gb_ref, wg_ref[1], bg_ref[1], sb_ref, ob_ref, False, chunks)


def gla_scan(proj, bsz, seq_len, w_gate, b_gate, chunks=4):
    t = proj.shape[0]
    tb = chunks * GLA_CHUNK
    nb = seq_len // tb
    fwd = lambda cb: (lambda b, i: (b * nb + i, cb))
    bwd = lambda cb: (lambda b, i: (b * nb + nb - 1 - i, cb))
    qc, kc, vc, gc = EV_Q // GLA_QK, EV_K // GLA_QK, EV_VAL // GLA_V, EV_GLR // 128
    wg = jnp.zeros((2, 128, GLA_QK), F32)
    wg = wg.at[0, :GLA_RANK].set(w_gate[0].astype(F32)).at[1, GLA_RANK:2 * GLA_RANK].set(w_gate[1].astype(F32))
    wg = _split_bf16(wg)
    out = jax.ShapeDtypeStruct((t, GLA_V), F32)
    state = pltpu.VMEM((GLA_V, GLA_QK), F32)
    return pl.pallas_call(
        functools.partial(_gla_body, chunks=chunks),
        out_shape=(out, out),
        grid=(bsz, nb),
        in_specs=[pl.BlockSpec((tb, GLA_QK), fwd(qc)), pl.BlockSpec((tb, GLA_QK), fwd(kc)),
                  pl.BlockSpec((tb, GLA_V), fwd(vc)), pl.BlockSpec((tb, 128), fwd(gc)),
                  pl.BlockSpec((tb, GLA_QK), bwd(qc)), pl.BlockSpec((tb, GLA_QK), bwd(kc)),
                  pl.BlockSpec((tb, GLA_V), bwd(vc)), pl.BlockSpec((tb, 128), bwd(gc)),
                  pl.BlockSpec((2, 2, 128, GLA_QK), lambda b, i: (0, 0, 0, 0)),
                  pl.BlockSpec((2, 1, GLA_QK), lambda b, i: (0, 0, 0))],
        out_specs=(pl.BlockSpec((tb, GLA_V), fwd(0)), pl.BlockSpec((tb, GLA_V), bwd(0))),
        scratch_shapes=[state, state],
        compiler_params=_params("parallel", "arbitrary"),
        name="gla_scan",
    )(proj, proj, proj, proj, proj, proj, proj, proj, wg, b_gate.astype(F32).reshape(2, 1, GLA_QK))


def _even_out_body(x_ref, ys_ref, of_ref, ob_ref, og_ref, wglu_t_ref, gn_ref, wtop_ref, wbot_ref, o_ref):
    y = ys_ref[...].astype(F32)
    y = 0.5 * y * (1.0 + jnp.tanh(math.sqrt(2.0 / math.pi) * (y + 0.044715 * (y * y * y))))
    gate = jnp.dot(wglu_t_ref[...], y.astype(BF16), preferred_element_type=F32)
    y = y * _sigmoid(gate)
    o = of_ref[...] + ob_ref[...]
    og = og_ref[...].astype(F32)
    heads = []
    for h in range(GLA_HEADS):
        sl = slice(h * GLA_DV, (h + 1) * GLA_DV)
        heads.append(_rms(o[:, sl], gn_ref[...]))
    o = jnp.concatenate(heads, axis=-1) * (og * _sigmoid(og))
    o_ref[...] = (x_ref[...]
                  + lax.dot_general(y.astype(BF16), wtop_ref[...], TN_DIMS, preferred_element_type=F32)
                  + jnp.dot(o.astype(BF16), wbot_ref[...], preferred_element_type=F32))


def even_out(x, ys_t, o_f, o_b, proj, w_glu_t, gla_norm, w_out, tm=ROW_TILE):
    t, d = x.shape
    row = lambda i: (i, 0)
    const = lambda i: (0, 0)
    return pl.pallas_call(
        _even_out_body,
        out_shape=jax.ShapeDtypeStruct((t, d), F32),
        grid=(t // tm,),
        in_specs=[pl.BlockSpec((tm, d), row), pl.BlockSpec((S5_WIDTH, tm), lambda i: (0, i)),
                  pl.BlockSpec((tm, GLA_V), row), pl.BlockSpec((tm, GLA_V), row),
                  pl.BlockSpec((tm, GLA_V), lambda i: (i, EV_OG // GLA_V)),
                  pl.BlockSpec((S5_WIDTH, S5_WIDTH), const), pl.BlockSpec((1, GLA_DV), const),
                  pl.BlockSpec((S5_WIDTH, d), const), pl.BlockSpec((GLA_V, d), lambda i: (1, 0))],
        out_specs=pl.BlockSpec((tm, d), row),
        compiler_params=_params("parallel"),
        name="even_out",
    )(x, ys_t, o_f, o_b, proj, w_glu_t, gla_norm.astype(F32).reshape(1, GLA_DV),
      w_out, w_out)


QK_NORM_WIDTH = 256


def _seg_rms(a, gain, same_seg):
    ssq = jnp.dot((a * a).astype(BF16), same_seg, preferred_element_type=F32)
    return a * lax.rsqrt(ssq * (1.0 / DIFF_DK) + EPS) * gain


def _qkv_body(x_ref, g_ref, w_ref, qk_gain_ref, o_ref):
    d = x_ref.shape[1]
    h = _rms(x_ref[...], g_ref[...]).astype(BF16)
    w = QK_NORM_WIDTH
    same_seg = jnp.where(lax.broadcasted_iota(jnp.int32, (w, w), 0) // DIFF_DK
                         == lax.broadcasted_iota(jnp.int32, (w, w), 1) // DIFF_DK, 1.0, 0.0).astype(BF16)
    for part in range(2):
        acc = jnp.dot(h, w_ref[:, part * d:(part + 1) * d], preferred_element_type=F32)
        gain = qk_gain_ref[part]
        for c in range(d // w):
            sl = slice(c * w, (c + 1) * w)
            o_ref[:, part * d + c * w:part * d + (c + 1) * w] = (
                _seg_rms(acc[:, sl], gain, same_seg).astype(o_ref.dtype))
    o_ref[:, 2 * d:] = jnp.dot(h, w_ref[:, 2 * d:], preferred_element_type=F32).astype(o_ref.dtype)


def qkv_project(x, gain, w, q_norm, k_norm, tm=ROW_TILE):
    t, d = x.shape
    reps = QK_NORM_WIDTH // DIFF_DK
    qg = jnp.tile(q_norm.astype(F32) * DIFF_DK ** -0.5, reps)
    kg = jnp.tile(k_norm.astype(F32), reps)
    qk_gain = jnp.stack([qg, kg]).reshape(2, 1, QK_NORM_WIDTH)
    return pl.pallas_call(
        _qkv_body,
        out_shape=jax.ShapeDtypeStruct((t, 3 * d), BF16),
        grid=(t // tm,),
        in_specs=[pl.BlockSpec((tm, d), lambda i: (i, 0)),
                  pl.BlockSpec((1, d), lambda i: (0, 0)),
                  pl.BlockSpec((d, 3 * d), lambda i: (0, 0)),
                  pl.BlockSpec((2, 1, QK_NORM_WIDTH), lambda i: (0, 0, 0))],
        out_specs=pl.BlockSpec((tm, 3 * d), lambda i: (i, 0)),
        compiler_params=_params("parallel"),
        name="qkv_project",
    )(x, gain.reshape(1, d), w, qk_gain)


POS_SPLIT = 16
POS_SHIFT = 4
AUG_LANE = DIFF_DK
SOFTMAX_ROWS = 16
ROW_PARTS = 2


def _pos_terms(shape):
    lane = lax.broadcasted_iota(jnp.int32, shape, 1)
    pos = lax.broadcasted_iota(jnp.int32, shape, 0)
    hi = lax.shift_right_logical(pos, POS_SHIFT).astype(F32)
    lo = (pos & (POS_SPLIT - 1)).astype(F32)
    return lane, hi, lo


def _lane_select(lane, first, values):
    out = 0.0
    for n, val in reversed(list(enumerate(values))):
        out = jnp.where(lane == first + n, val, out)
    return out


def _diff_attn_body(slope_ref, q_ref, k_ref, v_ref, lq1_ref, lk1_ref, lq2_ref, lk2_ref, sub_ref,
                    o_ref, ka_ref, va_ref, qs_ref, s0_ref, s1_ref, p0_ref, p1_ref, vs_ref,
                    *, blk, seq_len, lambda_init):
    h = pl.program_id(1)
    qi = pl.program_id(2)
    slope = slope_ref[h]
    nk = seq_len // blk
    s_refs, p_refs = (s0_ref, s1_ref), (p0_ref, p1_ref)

    @pl.when(qi == 0)
    def _():
        va_ref[:, :DIFF_DV] = v_ref[...]
        va_ref[:, DIFF_DV:] = jnp.ones((seq_len, DIFF_DV), BF16)

        def build(t, carry):
            rows = pl.ds(pl.multiple_of(t * blk, blk), blk)
            kf = k_ref[rows, :].astype(F32)
            lane, hi, lo = _pos_terms(kf.shape)
            aug = _lane_select(lane, AUG_LANE,
                               [-blk * slope, -POS_SPLIT * slope, -slope,
                                (blk * slope) * jnp.asarray(t, F32), (POS_SPLIT * slope) * hi, slope * lo])
            for z, kz in enumerate((kf, pltpu.roll(kf, DIFF_DK, 1))):
                ka_ref[z, rows, :] = jnp.where(lane < DIFF_DK, kz, aug).astype(BF16)
            return carry

        lax.fori_loop(0, nk, build, 0)

    qf = q_ref[...].astype(F32)
    lane, hi, lo = _pos_terms(qf.shape)
    qa = _lane_select(lane, AUG_LANE, [jnp.asarray(qi, F32), hi, lo, 1.0, 1.0, 1.0])
    for z, qz in enumerate((qf, pltpu.roll(qf, DIFF_DK, 1))):
        qs_ref[z] = jnp.where(lane < DIFF_DK, qz, 0.0).astype(BF16)
        qs_ref[2 + z] = jnp.where(lane < DIFF_DK, qz, qa).astype(BF16)
        qs_ref[4 + z] = jnp.where(lane < DIFF_DK, qz, -qa).astype(BF16)

    def key_block(t):
        if t == 0:
            return qi, 0
        j = (t - 1) + jnp.asarray(qi <= t - 1, jnp.int32)
        return j, jnp.where(j < qi, 2, 4)

    def key_rows(t):
        j, _ = key_block(t)
        return pl.ds(pl.multiple_of(j * blk, blk), blk)

    part = blk // ROW_PARTS
    parts = [slice(r * part, (r + 1) * part) for r in range(ROW_PARTS)]
    for z, s_ref in enumerate(s_refs):
        for rows in parts:
            for t in range(nk):
                _, variant = key_block(t)
                s_ref[rows, t * blk:(t + 1) * blk] = lax.dot_general(
                    qs_ref[variant + z, rows, :], ka_ref[z, key_rows(t), :], NT_DIMS,
                    preferred_element_type=F32)

    rg = SOFTMAX_ROWS
    rel = (lax.broadcasted_iota(jnp.int32, (rg, blk), 0) - lax.broadcasted_iota(jnp.int32, (rg, blk), 1))
    for t in range(nk):
        vs_ref[t * blk:(t + 1) * blk, :] = va_ref[key_rows(t), :]

    acc = []
    for s_ref, p_ref in zip(s_refs, p_refs):
        for g in range(blk // rg):
            rows = slice(g * rg, (g + 1) * rg)
            s_diag = s_ref[rows, :blk] - slope * jnp.abs(rel + g * rg).astype(F32)
            top = s_diag
            for t in range(1, nk):
                top = jnp.maximum(top, s_ref[rows, t * blk:(t + 1) * blk])
            m = jnp.max(top, axis=-1, keepdims=True)
            p_ref[rows, :blk] = jnp.exp(s_diag - m).astype(BF16)
            for t in range(1, nk):
                cols = slice(t * blk, (t + 1) * blk)
                p_ref[rows, cols] = jnp.exp(s_ref[rows, cols] - m).astype(BF16)
        acc.append([jnp.dot(p_ref[rows, :], vs_ref[...], preferred_element_type=F32) for rows in parts])

    lam = (jnp.exp(jnp.sum(lq1_ref[...] * lk1_ref[...], axis=-1, keepdims=True))
           - jnp.exp(jnp.sum(lq2_ref[...] * lk2_ref[...], axis=-1, keepdims=True)) + lambda_init)
    for rows, a0, a1 in zip(parts, *acc):
        o = (a0[:, :DIFF_DV] / a0[:, DIFF_DV:DIFF_DV + 1]
             - lam * (a1[:, :DIFF_DV] / a1[:, DIFF_DV:DIFF_DV + 1]))
        o_ref[rows, :] = (_rms(o, sub_ref[...]) * (1.0 - lambda_init)).astype(o_ref.dtype)


def diff_attention(qkv, bsz, seq_len, lq1, lk1, lq2, lk2, sub_norm, lambda_init, blk=512):
    t = qkv.shape[0]
    nq = seq_len // blk
    assert blk // POS_SPLIT <= 256, "hi part of a block position must stay exact in bf16"
    slopes = jnp.asarray(2.0 ** (-8.0 * np.arange(1, DIFF_HEADS + 1, dtype=np.float32) / DIFF_HEADS), F32)
    vec = lambda a: a.astype(F32).reshape(1, DIFF_DK)
    const = lambda b, h, i: (0, 0)
    return pl.pallas_call(
        functools.partial(_diff_attn_body, blk=blk, seq_len=seq_len, lambda_init=lambda_init),
        out_shape=jax.ShapeDtypeStruct((t, DIFF_HEADS * DIFF_DV), BF16),
        grid=(bsz, DIFF_HEADS, nq),
        in_specs=[pl.BlockSpec(memory_space=pltpu.SMEM),
                  pl.BlockSpec((blk, 2 * DIFF_DK), lambda b, h, i: (b * nq + i, h)),
                  pl.BlockSpec((seq_len, 2 * DIFF_DK), lambda b, h, i: (b, DIFF_HEADS + h)),
                  pl.BlockSpec((seq_len, DIFF_DV), lambda b, h, i: (b, 2 * DIFF_HEADS + h)),
                  pl.BlockSpec((1, DIFF_DK), const), pl.BlockSpec((1, DIFF_DK), const),
                  pl.BlockSpec((1, DIFF_DK), const), pl.BlockSpec((1, DIFF_DK), const),
                  pl.BlockSpec((1, DIFF_DV), const)],
        out_specs=pl.BlockSpec((blk, DIFF_DV), lambda b, h, i: (b * nq + i, h)),
        scratch_shapes=[pltpu.VMEM((2, seq_len, 2 * DIFF_DK), BF16),
                        pltpu.VMEM((seq_len, 2 * DIFF_DV), BF16),
                        pltpu.VMEM((6, blk, 2 * DIFF_DK), BF16),
                        pltpu.VMEM((blk, seq_len), F32),
                        pltpu.VMEM((blk, seq_len), F32),
                        pltpu.VMEM((blk, seq_len), BF16),
                        pltpu.VMEM((blk, seq_len), BF16),
                        pltpu.VMEM((seq_len, 2 * DIFF_DV), BF16)],
        compiler_params=_params("parallel", "parallel", "arbitrary"),
        name="diff_attention",
    )(slopes, qkv, qkv, qkv, vec(lq1), vec(lk1), vec(lq2), vec(lk2),
      sub_norm.astype(F32).reshape(1, DIFF_DV))


def _pad_ev_w_in(w):
    w = w[:, S5_WIDTH:]
    return jnp.pad(w, ((0, 0), (0, EV_PAD_COLS - w.shape[1]))).astype(BF16)


def _trunk(x3, mem, p):
    bsz, seq_len, d = x3.shape
    x = x3.reshape(bsz * seq_len, d)
    kn_all, v_all = mem_kv(mem, p['norm_mem'], p['x_w_kv'], p['x_k_norm'])
    for layer in range(DEPTH):
        if layer % 2 == 0:
            e = layer // 2
            proj, u_t = even_project(x, p['norm_mix'][layer], p['ev_w_in'][e], p['ev_w_u_t'][e], EV_TN)
            ys_t = s5_scan(u_t, bsz, seq_len, p['s5_ops'][e])
            o_f, o_b = gla_scan(proj, bsz, seq_len, p['gla_w_gate'][e], p['gla_b_gate'][e])
            x = even_out(x, ys_t, o_f, o_b, proj, p['s5_w_glu_t'][e], p['gla_norm'][e], p['ev_w_out'][e])
            mixer_out = None
        else:
            o = layer // 2
            lambda_init = 0.8 - 0.6 * math.exp(-0.3 * layer)
            qkv = qkv_project(x, p['norm_mix'][layer], p['od_w_in'][o],
                              p['diff_q_norm'][o], p['diff_k_norm'][o])
            att = diff_attention(qkv, bsz, seq_len, p['diff_lambda_q1'][o], p['diff_lambda_k1'][o],
                                 p['diff_lambda_q2'][o], p['diff_lambda_k2'][o], p['diff_norm'][o],
                                 lambda_init)
            mixer_out = (att, p['od_w_out'][o])
        x = cross_block(x, seq_len, layer, p['norm_cross'][layer], p['x_w_q'][layer], p['x_q_norm'][layer],
                        kn_all, v_all, p['x_w_o'][layer], mixer_out)
        x = mlp_block(x, p['norm_mlp'][layer], p['mlp_w1'], p['mlp_w2'], layer)
    return x.reshape(bsz, seq_len, d)


def kernel(x_prompt, x_sample, mem_prompt, mem_sample, norm_mix, norm_cross, norm_mem, norm_mlp,
           ev_w_in, ev_w_out, s5_lambda_re, s5_lambda_im, s5_log_step, s5_b_re, s5_b_im,
           s5_c_re, s5_c_im, s5_d, s5_w_glu, gla_w_gate, gla_b_gate, gla_norm,
           od_w_in, od_w_out, diff_q_norm, diff_k_norm, diff_lambda_q1, diff_lambda_k1,
           diff_lambda_q2, diff_lambda_k2, diff_norm, x_w_q, x_w_kv, x_w_o, x_q_norm, x_k_norm,
           mlp_w1, mlp_w2):
    n_even = ev_w_in.shape[0]
    bf = lambda w: w.astype(BF16)
    p = dict(
        norm_mix=norm_mix, norm_cross=norm_cross, norm_mem=norm_mem, norm_mlp=norm_mlp,
        ev_w_in=[_pad_ev_w_in(ev_w_in[e]) for e in range(n_even)], ev_w_out=bf(ev_w_out),
        s5_ops=[s5_operators(s5_lambda_re[e], s5_lambda_im[e], s5_log_step[e], s5_b_re[e], s5_b_im[e],
                             s5_c_re[e], s5_c_im[e], s5_d[e]) for e in range(n_even)],
        s5_w_glu_t=bf(jnp.swapaxes(s5_w_glu, 1, 2)),
        ev_w_u_t=bf(jnp.swapaxes(ev_w_in[:, :, :S5_WIDTH], 1, 2)), gla_w_gate=gla_w_gate, gla_b_gate=gla_b_gate, gla_norm=gla_norm,
        od_w_in=bf(od_w_in), od_w_out=bf(od_w_out), diff_q_norm=diff_q_norm, diff_k_norm=diff_k_norm,
        diff_lambda_q1=diff_lambda_q1, diff_lambda_k1=diff_lambda_k1,
        diff_lambda_q2=diff_lambda_q2, diff_lambda_k2=diff_lambda_k2, diff_norm=diff_norm,
        x_w_q=bf(x_w_q), x_w_kv=bf(x_w_kv), x_w_o=bf(x_w_o), x_q_norm=x_q_norm, x_k_norm=x_k_norm,
        mlp_w1=bf(mlp_w1), mlp_w2=bf(mlp_w2))
    return (_trunk(x_prompt, mem_prompt, p), _trunk(x_sample, mem_sample, p))
```
